```python
import math
import jax, jax.numpy as jnp
from jax import lax
import numpy as np

D_MODEL = 2048
BATCH = 2
SEQ = 4096
DEPTH = 2
DEC_BATCH = 32
DEC_SEQ = 1
PAST_LEN = 16384
PAGE_SIZE = 128

WINDOW = 128
H_A = 16
KV_A = 2
HD_A = 64
G_A = H_A // KV_A
N_BUCKETS = 32
MAX_DIST = 128
CHUNK_B = 128
GB = 16
CG_B = 64
W_B = GB * CG_B
D_INNER = 1024
P_C = 64
H_C = D_INNER // P_C
G_C = 2
R_C = H_C // G_C
N_C = 128
CONV_W = 4
CONV_DIM = D_INNER + 2 * G_C * N_C
SSD_CHUNK = 128
N_BRANCH = 3
BR_W = 1024
N_MEM = 256
XH = 4
XHD = 128
N_EXPERTS = 32
TOP_K = 4
D_FF = D_MODEL
SWIGLU_ALPHA = 1.702
SWIGLU_LIMIT = 7.0
MOE_BLOCK = 128
MOE_BLOCK_SMALL = 8
DN_ALPHA = (2 * DEPTH) ** 0.25
DN_BETA = (8 * DEPTH) ** -0.25
LN_EPS = 1e-5
RMS_EPS = 1e-5

kernel_name = 'hybrid_swa_gmlp_ssd_moe_decoder_step'


def layer_norm(x, g, b):
    xf = x.astype(jnp.float32)
    mu = jnp.mean(xf, axis=-1, keepdims=True)
    var = jnp.mean(jnp.square(xf - mu), axis=-1, keepdims=True)
    return ((xf - mu) * lax.rsqrt(var + LN_EPS) * g + b).astype(x.dtype)


def t5_bucket(dist):
    n = jnp.maximum(dist, 0)
    exact = N_BUCKETS // 2
    nf = jnp.maximum(n, 1).astype(jnp.float32)
    large = exact + (jnp.log(nf / exact) / math.log(MAX_DIST / exact) * (N_BUCKETS - exact)).astype(jnp.int32)
    return jnp.where(n < exact, n, jnp.minimum(large, N_BUCKETS - 1))


def t5_bias(dist, rel_bias):
    b = jnp.moveaxis(rel_bias[t5_bucket(dist)], -1, 0)
    return b.reshape((KV_A, G_A) + dist.shape).astype(jnp.float32)


def sink_attention(q, k, v, bias, mask, sinks):
    s = jnp.einsum('...qhgd,...shd->...hgqs', q, k).astype(jnp.float32) * HD_A ** -0.5 + bias
    s = jnp.where(mask, s, -jnp.inf)
    sk = sinks.astype(jnp.float32).reshape(KV_A, G_A, 1, 1)
    m = jnp.maximum(jnp.max(s, axis=-1, keepdims=True), sk)
    pr = jnp.exp(s - m)
    den = jnp.sum(pr, axis=-1, keepdims=True) + jnp.exp(sk - m)
    return jnp.einsum('...hgqs,...shd->...qhgd', (pr / den).astype(v.dtype), v)


def swa_prompt(q, k, v, sinks, rel_bias):
    bsz, L = q.shape[:2]
    nb = L // WINDOW
    qb = q.reshape(bsz, nb, WINDOW, KV_A, G_A, HD_A)

    def band(t):
        tb = t.reshape(bsz, nb, WINDOW, KV_A, HD_A)
        prev = jnp.pad(tb[:, :-1], ((0, 0), (1, 0), (0, 0), (0, 0), (0, 0)))
        return jnp.concatenate([prev, tb], axis=2)

    i = jnp.arange(WINDOW)[:, None]
    j = jnp.arange(2 * WINDOW)[None, :]
    dist = i + WINDOW - j
    kpos = (jnp.arange(nb) * WINDOW - WINDOW)[:, None, None] + j
    mask = ((dist >= 0) & (dist < WINDOW) & (kpos >= 0))[:, None, None]
    out = sink_attention(qb, band(k), band(v), t5_bias(dist, rel_bias), mask, sinks)
    return out.reshape(bsz, L, H_A * HD_A)


def swa_decode(q, k, v, k_buf, v_buf, sinks, rel_bias):
    bsz, L = q.shape[:2]
    wb = k_buf.shape[1]
    kk = jnp.concatenate([k_buf, k], axis=1)
    vv = jnp.concatenate([v_buf, v], axis=1)
    q_pos = PAST_LEN + jnp.arange(L)
    k_pos = PAST_LEN - wb + jnp.arange(wb + L)
    dist = q_pos[:, None] - k_pos[None, :]
    mask = (dist >= 0) & (dist < WINDOW)
    out = sink_attention(q, kk, vv, t5_bias(dist, rel_bias), mask, sinks)
    return out.reshape(bsz, L, H_A * HD_A), kk[:, L:], vv[:, L:]


def gmlp_mix(vg, ws, bs):
    bsz, L = vg.shape[:2]
    nc = -(-L // CHUNK_B)
    pad = nc * CHUNK_B - L
    vp = jnp.pad(vg, ((0, 0), (0, pad), (0, 0), (0, 0))).reshape(bsz, nc, CHUNK_B, GB, CG_B)
    w = ws * jnp.tril(jnp.ones((CHUNK_B, CHUNK_B), ws.dtype))
    out = jnp.einsum('gts,bcsgd->bctgd', w, vp) + bs.T[:, :, None]
    return out.reshape(bsz, nc * CHUNK_B, GB, CG_B)[:, :L]


def causal_dwconv(xpad, w, bias):
    L = xpad.shape[1] - (CONV_W - 1)
    y = bias
    for t in range(CONV_W):
        y = y + xpad[:, t:t + L] * w[t]
    return y


def ssd_scan(x, dt, a, bm, cm, h0):
    bsz, L = x.shape[:2]
    cl = min(SSD_CHUNK, L)
    nc = -(-L // cl)
    pad = nc * cl - L

    def chunks(t):
        t = jnp.pad(t.astype(jnp.float32), [(0, 0), (0, pad)] + [(0, 0)] * (t.ndim - 2))
        return t.reshape((bsz, nc, cl) + t.shape[2:])

    x, dt, bm, cm = chunks(x), chunks(dt), chunks(bm), chunks(cm)
    acs = jnp.cumsum(dt * a, axis=2)
    xdt = x * dt[..., None]
    causal = jnp.tril(jnp.ones((cl, cl), bool))[:, :, None, None]
    seg = acs[:, :, :, None] - acs[:, :, None, :]
    decay = jnp.exp(jnp.where(causal, seg, -jnp.inf))
    cb = jnp.einsum('bclgn,bcsgn->bclsg', cm, bm)
    y_diag = jnp.einsum('bclsgr,bcsgrp->bclgrp', cb[..., None] * decay, xdt)
    end_decay = jnp.exp(acs[:, :, -1:] - acs)
    states = jnp.einsum('bclgn,bclgrp->bcgrpn', bm, xdt * end_decay[..., None])
    chunk_decay = jnp.exp(acs[:, :, -1])

    def carry(h, inp):
        st, dec = inp
        return h * dec[..., None, None] + st, h

    h_last, h_in = lax.scan(carry, h0, (jnp.moveaxis(states, 1, 0), jnp.moveaxis(chunk_decay, 1, 0)))
    h_in = jnp.moveaxis(h_in, 0, 1)
    y_off = jnp.einsum('bclgn,bcgrpn->bclgrp', cm, h_in) * jnp.exp(acs)[..., None]
    y = (y_diag + y_off).reshape((bsz, nc * cl) + x.shape[3:])[:, :L]
    return y, h_last


def token_mixers(x, p, rel_bias, win_buf, win_k=None, win_v=None, conv_st=None, ssm_st=None):
    bsz, L, _ = x.shape
    prompt = win_k is None
    f32 = jnp.float32
    sizes = [H_A * HD_A, KV_A * HD_A, KV_A * HD_A, W_B, W_B, D_INNER, CONV_DIM, H_C]
    q, k, v, u, gv, z, xbc, dt_raw = jnp.split(x @ p['w_in'], np.cumsum(sizes)[:-1].tolist(), axis=-1)

    q = q.reshape(bsz, L, KV_A, G_A, HD_A)
    k = k.reshape(bsz, L, KV_A, HD_A)
    v = v.reshape(bsz, L, KV_A, HD_A)
    if prompt:
        a_out = swa_prompt(q, k, v, p['sinks'], rel_bias)
        new_wk, new_wv = k[:, L - win_buf:], v[:, L - win_buf:]
    else:
        a_out, new_wk, new_wv = swa_decode(q, k, v, win_k, win_v, p['sinks'], rel_bias)

    u = jax.nn.gelu(u, approximate=False)
    gv = layer_norm(jax.nn.gelu(gv, approximate=False), p['gmlp_ln_g'], p['gmlp_ln_b'])
    b_out = u * gmlp_mix(gv.reshape(bsz, L, GB, CG_B), p['gmlp_ws'], p['gmlp_bs']).reshape(bsz, L, W_B)

    hist = jnp.zeros((bsz, CONV_W - 1, CONV_DIM), xbc.dtype) if prompt else conv_st
    xpad = jnp.concatenate([hist, xbc], axis=1)
    new_conv = xpad[:, xpad.shape[1] - (CONV_W - 1):]
    xbc_c = jax.nn.silu(causal_dwconv(xpad, p['conv_w'], p['conv_b']))
    xs, bm, cm = jnp.split(xbc_c, [D_INNER, D_INNER + G_C * N_C], axis=-1)
    xs = xs.reshape(bsz, L, G_C, R_C, P_C)
    bm = bm.reshape(bsz, L, G_C, N_C)
    cm = cm.reshape(bsz, L, G_C, N_C)
    dt = jax.nn.softplus((dt_raw + p['dt_bias']).astype(f32)).reshape(bsz, L, G_C, R_C)
    a = -jnp.exp(p['a_log'].astype(f32)).reshape(G_C, R_C)
    if prompt:
        h0 = jnp.zeros((bsz, G_C, R_C, P_C, N_C), f32)
    else:
        h0 = ssm_st.astype(f32).reshape(bsz, G_C, R_C, P_C, N_C)
    y, h_last = ssd_scan(xs, dt, a, bm, cm, h0)
    y = y + p['d_skip'].astype(f32).reshape(G_C, R_C, 1) * xs.astype(f32)
    y = (y.reshape(bsz, L, D_INNER) * jax.nn.silu(z.astype(f32))).reshape(bsz, L, G_C, D_INNER // G_C)
    y = y * lax.rsqrt(jnp.mean(y * y, axis=-1, keepdims=True) + RMS_EPS)
    c_out = (y.reshape(bsz, L, D_INNER) * p['ssm_norm_g']).astype(x.dtype)
    new_ssm = h_last.reshape(bsz, H_C, P_C, N_C).astype(x.dtype)

    branches = jnp.stack([a_out, b_out, c_out], axis=2)
    proj = jnp.einsum('blkc,kcd->blkd', branches, p['w_branch'])
    gates = jax.nn.sigmoid(x @ p['w_gate'] + p['b_gate']).reshape(bsz, L, N_BRANCH, D_MODEL)
    mixed = jnp.sum(gates * proj, axis=2)
    return mixed @ p['w_o'], (new_wk, new_wv, new_conv, new_ssm, gv)


def mem_attention(x, mk, mv, w_xq, w_xo):
    bsz, L, _ = x.shape
    q = (x @ w_xq).reshape(bsz, L, XH, XHD)
    s = jnp.einsum('blhd,bmhd->bhlm', q, mk).astype(jnp.float32) * XHD ** -0.5
    w = jax.nn.softmax(s, axis=-1).astype(mv.dtype)
    o = jnp.einsum('bhlm,bmhd->blhd', w, mv).reshape(bsz, L, XH * XHD)
    return o @ w_xo


def moe_ffn(x, w_r, b_r, w1, b1, w2, b2):
    T = x.shape[0]
    logits = (x @ w_r + b_r).astype(jnp.float32)
    top_v, top_i = lax.top_k(logits, TOP_K)
    gate = jax.nn.softmax(top_v, axis=-1)
    n_assign = T * TOP_K
    bt = MOE_BLOCK if n_assign >= N_EXPERTS * MOE_BLOCK else MOE_BLOCK_SMALL
    n_blocks = -(-(n_assign + N_EXPERTS * (bt - 1)) // bt)
    flat_e = top_i.reshape(-1)
    order = jnp.argsort(flat_e)
    e_sorted = flat_e[order]
    tok_sorted = order // TOP_K
    g_sorted = gate.reshape(-1)[order]
    counts = jnp.bincount(flat_e, length=N_EXPERTS)
    padded = (counts + bt - 1) // bt * bt
    p_end = jnp.cumsum(padded)
    u_start = jnp.cumsum(counts) - counts
    dest = (p_end - padded)[e_sorted] + jnp.arange(n_assign) - u_start[e_sorted]
    rows = jnp.zeros((n_blocks * bt, x.shape[1]), x.dtype).at[dest].set(x[tok_sorted])
    block_e = jnp.minimum(jnp.searchsorted(p_end, jnp.arange(n_blocks) * bt, side='right'), N_EXPERTS - 1)

    def expert_block(args):
        xb, e = args
        hdn = xb @ w1[e] + b1[e]
        glu = jnp.minimum(hdn[:, 0::2], SWIGLU_LIMIT)
        lin = jnp.clip(hdn[:, 1::2], -SWIGLU_LIMIT, SWIGLU_LIMIT)
        return (glu * jax.nn.sigmoid(SWIGLU_ALPHA * glu) * (lin + 1.0)) @ w2[e] + b2[e]

    out = lax.map(expert_block, (rows.reshape(n_blocks, bt, -1), block_e)).reshape(n_blocks * bt, -1)
    y = jax.ops.segment_sum(out[dest].astype(jnp.float32) * g_sorted[:, None], tok_sorted, num_segments=T)
    return y.astype(x.dtype)


def decoder_layer(x, mk, mv, p, rel_bias, win_buf, win_k=None, win_v=None, conv_st=None, ssm_st=None):
    mix, states = token_mixers(x, p, rel_bias, win_buf, win_k, win_v, conv_st, ssm_st)
    x = layer_norm(DN_ALPHA * x + mix, p['ln1_g'], p['ln1_b'])
    x = layer_norm(DN_ALPHA * x + mem_attention(x, mk, mv, p['w_xq'], p['w_xo']), p['ln2_g'], p['ln2_b'])
    bsz, L, _ = x.shape
    ff = moe_ffn(x.reshape(bsz * L, D_MODEL), p['w_router'], p['b_router'], p['w_e1'], p['b_e1'],
                 p['w_e2'], p['b_e2']).reshape(bsz, L, D_MODEL)
    x = layer_norm(DN_ALPHA * x + ff, p['ln3_g'], p['ln3_b'])
    return x, states


def setup_inputs(seed: int = 0) -> dict:
    key = jax.random.key(seed)
    ks = iter(jax.random.split(key, 64))

    def nrm(shape, scale=1.0):
        return jax.random.normal(next(ks), shape, jnp.float32) * scale

    def near_one(shape):
        return 1.0 + nrm(shape, 0.01)

    win_buf = min(WINDOW, PAST_LEN)
    in_dim = 2 * W_B + H_A * HD_A + 2 * KV_A * HD_A + D_INNER + CONV_DIM + H_C
    dt0 = jnp.exp(jax.random.uniform(next(ks), (DEPTH, H_C), jnp.float32, math.log(1e-3), math.log(1e-1)))
    a0 = jax.random.uniform(next(ks), (DEPTH, H_C), jnp.float32, 1.0, 16.0)
    return {
        'x_prompt': nrm((BATCH, SEQ, D_MODEL)),
        'x_sample': nrm((DEC_BATCH, DEC_SEQ, D_MODEL)),
        'mem_prompt': nrm((BATCH, N_MEM, D_MODEL)),
        'cache_win_k': nrm((DEPTH, DEC_BATCH, win_buf, KV_A, HD_A)),
        'cache_win_v': nrm((DEPTH, DEC_BATCH, win_buf, KV_A, HD_A)),
        'state_conv': nrm((DEPTH, DEC_BATCH, CONV_W - 1, CONV_DIM)),
        'state_ssm': nrm((DEPTH, DEC_BATCH, H_C, P_C, N_C), 0.3),
        'cache_mem_k': nrm((DEPTH, DEC_BATCH, N_MEM, XH, XHD)),
        'cache_mem_v': nrm((DEPTH, DEC_BATCH, N_MEM, XH, XHD)),
        'w_in': nrm((DEPTH, D_MODEL, in_dim), D_MODEL ** -0.5),
        'rel_bias': nrm((N_BUCKETS, H_A), 0.3),
        'sinks': nrm((DEPTH, H_A), 0.5),
        'gmlp_ln_g': near_one((DEPTH, W_B)),
        'gmlp_ln_b': nrm((DEPTH, W_B), 0.01),
        'gmlp_ws': nrm((DEPTH, GB, CHUNK_B, CHUNK_B), CHUNK_B ** -0.5),
        'gmlp_bs': near_one((DEPTH, GB, CHUNK_B)),
        'conv_w': nrm((DEPTH, CONV_W, CONV_DIM), CONV_W ** -0.5),
        'conv_b': nrm((DEPTH, CONV_DIM), 0.02),
        'dt_bias': dt0 + jnp.log(-jnp.expm1(-dt0)),
        'a_log': jnp.log(a0),
        'd_skip': near_one((DEPTH, H_C)),
        'ssm_norm_g': near_one((DEPTH, D_INNER)),
        'w_branch': nrm((DEPTH, N_BRANCH, BR_W, D_MODEL), BR_W ** -0.5),
        'w_gate': nrm((DEPTH, D_MODEL, N_BRANCH * D_MODEL), D_MODEL ** -0.5),
        'b_gate': nrm((DEPTH, N_BRANCH * D_MODEL), 0.02),
        'w_o': nrm((DEPTH, D_MODEL, D_MODEL), D_MODEL ** -0.5 * DN_BETA),
        'ln1_g': near_one((DEPTH, D_MODEL)),
        'ln1_b': nrm((DEPTH, D_MODEL), 0.01),
        'w_xq': nrm((DEPTH, D_MODEL, XH * XHD), D_MODEL ** -0.5),
        'w_xk': nrm((DEPTH, D_MODEL, XH * XHD), D_MODEL ** -0.5),
        'w_xv': nrm((DEPTH, D_MODEL, XH * XHD), D_MODEL ** -0.5),
        'w_xo': nrm((DEPTH, XH * XHD, D_MODEL), (XH * XHD) ** -0.5 * DN_BETA),
        'ln2_g': near_one((DEPTH, D_MODEL)),
        'ln2_b': nrm((DEPTH, D_MODEL), 0.01),
        'w_router': nrm((DEPTH, D_MODEL, N_EXPERTS), D_MODEL ** -0.5),
        'b_router': nrm((DEPTH, N_EXPERTS), 0.01),
        'w_e1': nrm((DEPTH, N_EXPERTS, D_MODEL, 2 * D_FF), D_MODEL ** -0.5),
        'b_e1': nrm((DEPTH, N_EXPERTS, 2 * D_FF), 0.02),
        'w_e2': nrm((DEPTH, N_EXPERTS, D_FF, D_MODEL), D_FF ** -0.5 * DN_BETA),
        'b_e2': nrm((DEPTH, N_EXPERTS, D_MODEL), 0.02),
        'ln3_g': near_one((DEPTH, D_MODEL)),
        'ln3_b': nrm((DEPTH, D_MODEL), 0.01),
    }


def reference(x_prompt, x_sample, mem_prompt, cache_win_k, cache_win_v, state_conv, state_ssm,
              cache_mem_k, cache_mem_v, w_in, rel_bias, sinks, gmlp_ln_g, gmlp_ln_b, gmlp_ws, gmlp_bs,
              conv_w, conv_b, dt_bias, a_log, d_skip, ssm_norm_g, w_branch, w_gate, b_gate, w_o,
              ln1_g, ln1_b, w_xq, w_xk, w_xv, w_xo, ln2_g, ln2_b, w_router, b_router,
              w_e1, b_e1, w_e2, b_e2, ln3_g, ln3_b):
    win_buf = cache_win_k.shape[2]
    n_prompt = mem_prompt.shape[0]
    hp, hs = x_prompt, x_sample
    wk_p, wv_p, cv_p, ssm_p, mk_ps, mv_ps = [], [], [], [], [], []
    wk_s, wv_s, cv_s, ssm_s, gv_s = [], [], [], [], []
    for l in range(DEPTH):
        p = dict(w_in=w_in[l], sinks=sinks[l], gmlp_ln_g=gmlp_ln_g[l], gmlp_ln_b=gmlp_ln_b[l],
                 gmlp_ws=gmlp_ws[l], gmlp_bs=gmlp_bs[l], conv_w=conv_w[l], conv_b=conv_b[l],
                 dt_bias=dt_bias[l], a_log=a_log[l], d_skip=d_skip[l], ssm_norm_g=ssm_norm_g[l],
                 w_branch=w_branch[l], w_gate=w_gate[l], b_gate=b_gate[l], w_o=w_o[l],
                 ln1_g=ln1_g[l], ln1_b=ln1_b[l], w_xq=w_xq[l], w_xo=w_xo[l], ln2_g=ln2_g[l], ln2_b=ln2_b[l],
                 w_router=w_router[l], b_router=b_router[l], w_e1=w_e1[l], b_e1=b_e1[l],
                 w_e2=w_e2[l], b_e2=b_e2[l], ln3_g=ln3_g[l], ln3_b=ln3_b[l])
        mk = (mem_prompt @ w_xk[l]).reshape(n_prompt, N_MEM, XH, XHD)
        mv = (mem_prompt @ w_xv[l]).reshape(n_prompt, N_MEM, XH, XHD)
        hp, st_p = decoder_layer(hp, mk, mv, p, rel_bias, win_buf)
        wk_p.append(st_p[0]); wv_p.append(st_p[1]); cv_p.append(st_p[2]); ssm_p.append(st_p[3])
        mk_ps.append(mk); mv_ps.append(mv)
        hs, st_s = decoder_layer(hs, cache_mem_k[l], cache_mem_v[l], p, rel_bias, win_buf,
                                 cache_win_k[l], cache_win_v[l], state_conv[l], state_ssm[l])
        wk_s.append(st_s[0]); wv_s.append(st_s[1]); cv_s.append(st_s[2]); ssm_s.append(st_s[3])
        gv_s.append(st_s[4])
    return (hp, hs,
            jnp.stack(wk_p), jnp.stack(wv_p), jnp.stack(cv_p), jnp.stack(ssm_p),
            jnp.stack(mk_ps), jnp.stack(mv_ps),
            jnp.stack(wk_s), jnp.stack(wv_s), jnp.stack(cv_s), jnp.stack(ssm_s), jnp.stack(gv_s))
```

```python
import functools
import math

import numpy as np
import jax
import jax.numpy as jnp
from jax import lax
from jax.experimental import pallas as pl
from jax.experimental.pallas import tpu as pltpu

D_MODEL = 2048
DEPTH = 2
PAST_LEN = 16384
WINDOW = 128
H_A = 16
KV_A = 2
HD_A = 64
G_A = H_A // KV_A
N_BUCKETS = 32
MAX_DIST = 128
CHUNK_B = 128
GB = 16
CG_B = 64
W_B = GB * CG_B
D_INNER = 1024
P_C = 64
H_C = D_INNER // P_C
G_C = 2
R_C = H_C // G_C
N_C = 128
CONV_W = 4
CONV_DIM = D_INNER + 2 * G_C * N_C
SSD_CHUNK = 128
N_BRANCH = 3
BR_W = 1024
N_MEM = 256
XH = 4
XHD = 128
N_EXPERTS = 32
TOP_K = 4
D_FF = D_MODEL
SWIGLU_ALPHA = 1.702
SWIGLU_LIMIT = 7.0
DN_ALPHA = (2 * DEPTH) ** 0.25
LN_EPS = 1e-5
RMS_EPS = 1e-5

IN_SIZES = [H_A * HD_A, KV_A * HD_A, KV_A * HD_A, W_B, W_B, D_INNER, CONV_DIM, H_C]
IN_DIM = sum(IN_SIZES)
IN_DIM_PAD = 6144

VMEM_LIMIT = 56 * 1024 * 1024
MOE_BM = 512
MOE_FC = 512

F32 = jnp.float32
BF16 = jnp.bfloat16


def _cparams(*sem):
    return pltpu.CompilerParams(dimension_semantics=sem, vmem_limit_bytes=VMEM_LIMIT)


def _mm_kernel(x_ref, w_ref, o_ref):
    o_ref[...] = jnp.dot(x_ref[...], w_ref[...], preferred_element_type=F32).astype(o_ref.dtype)


def _matmul(x, w, tm, tn, out_dtype=F32):
    M, K = x.shape
    N = w.shape[1]
    assert M % tm == 0 and N % tn == 0
    return pl.pallas_call(
        _mm_kernel,
        grid=(N // tn, M // tm),
        in_specs=[pl.BlockSpec((tm, K), lambda j, i: (i, 0)),
                  pl.BlockSpec((K, tn), lambda j, i: (0, j))],
        out_specs=pl.BlockSpec((tm, tn), lambda j, i: (i, j)),
        out_shape=jax.ShapeDtypeStruct((M, N), out_dtype),
        compiler_params=_cparams("parallel", "parallel"),
        name="dense_matmul",
    )(x, w)


def _gate_merge_kernel(x_ref, a_ref, b_ref, c_ref, wg0_ref, wg1_ref, wg2_ref,
                       bg0_ref, bg1_ref, bg2_ref, wp_ref, o_ref):
    x = x_ref[...]
    acc = None
    for k, (br_ref, wg_ref, bg_ref) in enumerate(
            ((a_ref, wg0_ref, bg0_ref), (b_ref, wg1_ref, bg1_ref), (c_ref, wg2_ref, bg2_ref))):
        z = jnp.dot(x, wg_ref[...], preferred_element_type=F32) + bg_ref[...]
        gate = 1.0 / (1.0 + jnp.exp(-z))
        proj = jnp.dot(br_ref[...], wp_ref[k], preferred_element_type=F32)
        acc = gate * proj if acc is None else acc + gate * proj
    o_ref[...] = acc.astype(o_ref.dtype)


def _gate_merge(x, a, b, c, w_gate, b_gate, w_branch, tm, tn):
    M = x.shape[0]
    nt = D_MODEL // tn
    row = lambda j, i: (i, 0)
    in_specs = [pl.BlockSpec((tm, D_MODEL), row)] + [pl.BlockSpec((tm, BR_W), row)] * 3
    in_specs += [pl.BlockSpec((D_MODEL, tn), functools.partial(lambda j, i, k: (0, k * nt + j), k=k))
                 for k in range(N_BRANCH)]
    in_specs += [pl.BlockSpec((1, tn), functools.partial(lambda j, i, k: (0, k * nt + j), k=k))
                 for k in range(N_BRANCH)]
    in_specs += [pl.BlockSpec((N_BRANCH, BR_W, tn), lambda j, i: (0, 0, j))]
    return pl.pallas_call(
        _gate_merge_kernel,
        grid=(nt, M // tm),
        in_specs=in_specs,
        out_specs=pl.BlockSpec((tm, tn), lambda j, i: (i, j)),
        out_shape=jax.ShapeDtypeStruct((M, D_MODEL), BF16),
        compiler_params=_cparams("parallel", "parallel"),
        name="gate_merge",
    )(x, a, b, c, w_gate, w_gate, w_gate, b_gate, b_gate, b_gate, w_branch)


def _layer_norm_rows(y, g, b):
    mu = jnp.mean(y, axis=-1, keepdims=True)
    yc = y - mu
    var = jnp.mean(yc * yc, axis=-1, keepdims=True)
    return yc * lax.rsqrt(var + LN_EPS) * g + b


def _mm_res_ln_kernel(a_ref, w_ref, res_ref, g_ref, b_ref, o_ref):
    y = jnp.dot(a_ref[...], w_ref[...], preferred_element_type=F32) + DN_ALPHA * res_ref[...]
    o_ref[...] = _layer_norm_rows(y, g_ref[...], b_ref[...])


def _mm_res_ln(a, w, res, g, b, tm):
    M, K = a.shape
    row = lambda i: (i, 0)
    fixed = lambda i: (0, 0)
    return pl.pallas_call(
        _mm_res_ln_kernel,
        grid=(M // tm,),
        in_specs=[pl.BlockSpec((tm, K), row), pl.BlockSpec((K, D_MODEL), fixed),
                  pl.BlockSpec((tm, D_MODEL), row), pl.BlockSpec((1, D_MODEL), fixed),
                  pl.BlockSpec((1, D_MODEL), fixed)],
        out_specs=pl.BlockSpec((tm, D_MODEL), row),
        out_shape=jax.ShapeDtypeStruct((M, D_MODEL), F32),
        compiler_params=_cparams("parallel"),
        name="matmul_residual_layernorm",
    )(a, w, res, g.reshape(1, -1), b.reshape(1, -1))


def _res_ln_kernel(y_ref, res_ref, g_ref, b_ref, o_ref):
    y = y_ref[...] + DN_ALPHA * res_ref[...]
    o_ref[...] = _layer_norm_rows(y, g_ref[...], b_ref[...])


def _res_ln(y, res, g, b, tm):
    M = y.shape[0]
    row = lambda i: (i, 0)
    fixed = lambda i: (0, 0)
    return pl.pallas_call(
        _res_ln_kernel,
        grid=(M // tm,),
        in_specs=[pl.BlockSpec((tm, D_MODEL), row), pl.BlockSpec((tm, D_MODEL), row),
                  pl.BlockSpec((1, D_MODEL), fixed), pl.BlockSpec((1, D_MODEL), fixed)],
        out_specs=pl.BlockSpec((tm, D_MODEL), row),
        out_shape=jax.ShapeDtypeStruct((M, D_MODEL), F32),
        compiler_params=_cparams("parallel"),
        name="residual_layernorm",
    )(y, res, g.reshape(1, -1), b.reshape(1, -1))


def _router_kernel(x_ref, w_ref, b_ref, o_ref):
    acc = jnp.dot(x_ref[...].astype(BF16), w_ref[...].astype(BF16), preferred_element_type=F32)
    o_ref[...] = acc + b_ref[...]


def _router(x, w_pad, b_pad, tm):
    M = x.shape[0]
    NP = w_pad.shape[1]
    return pl.pallas_call(
        _router_kernel,
        grid=(M // tm,),
        in_specs=[pl.BlockSpec((tm, D_MODEL), lambda i: (i, 0)),
                  pl.BlockSpec((D_MODEL, NP), lambda i: (0, 0)),
                  pl.BlockSpec((1, NP), lambda i: (0, 0))],
        out_specs=pl.BlockSpec((tm, NP), lambda i: (i, 0)),
        out_shape=jax.ShapeDtypeStruct((M, NP), F32),
        compiler_params=_cparams("parallel"),
        name="router_logits",
    )(x, w_pad, b_pad)


def _moe_kernel(be_ref, bx_ref, bv_ref, x_ref, w1g_ref, w1l_ref, b1g_ref, b1l_ref, w2_ref, b2_ref, o_ref):
    i = pl.program_id(0)
    f = pl.program_id(1)

    @pl.when(bv_ref[i] == 1)
    def _():
        x = x_ref[...]
        hg = jnp.dot(x, w1g_ref[0], preferred_element_type=F32) + b1g_ref[0]
        hl = jnp.dot(x, w1l_ref[0], preferred_element_type=F32) + b1l_ref[0]
        glu = jnp.minimum(hg, SWIGLU_LIMIT)
        lin = jnp.clip(hl, -SWIGLU_LIMIT, SWIGLU_LIMIT)
        act = glu * (1.0 / (1.0 + jnp.exp(-SWIGLU_ALPHA * glu))) * (lin + 1.0)
        contrib = jnp.dot(act.astype(BF16), w2_ref[0], preferred_element_type=F32)

        @pl.when(f == 0)
        def _():
            o_ref[...] = contrib + b2_ref[0]

        @pl.when(f != 0)
        def _():
            o_ref[...] += contrib


def _moe_ffn_blocks(rows, block_e, block_x, block_v, w1g, w1l, b1g, b1l, w2, b2):
    nb = rows.shape[0] // MOE_BM
    nf = D_FF // MOE_FC
    last_f = nf - 1

    def fsel(f, bv, i):
        return jnp.where(bv[i] == 1, f, last_f)

    grid_spec = pltpu.PrefetchScalarGridSpec(
        num_scalar_prefetch=3,
        grid=(nb, nf),
        in_specs=[
            pl.BlockSpec((MOE_BM, D_MODEL), lambda i, f, be, bx, bv: (bx[i], 0)),
            pl.BlockSpec((1, D_MODEL, MOE_FC), lambda i, f, be, bx, bv: (be[i], 0, fsel(f, bv, i))),
            pl.BlockSpec((1, D_MODEL, MOE_FC), lambda i, f, be, bx, bv: (be[i], 0, fsel(f, bv, i))),
            pl.BlockSpec((1, 1, MOE_FC), lambda i, f, be, bx, bv: (be[i], 0, fsel(f, bv, i))),
            pl.BlockSpec((1, 1, MOE_FC), lambda i, f, be, bx, bv: (be[i], 0, fsel(f, bv, i))),
            pl.BlockSpec((1, MOE_FC, D_MODEL), lambda i, f, be, bx, bv: (be[i], fsel(f, bv, i), 0)),
            pl.BlockSpec((1, 1, D_MODEL), lambda i, f, be, bx, bv: (be[i], 0, 0)),
        ],
        out_specs=pl.BlockSpec((MOE_BM, D_MODEL), lambda i, f, be, bx, bv: (bx[i], 0)),
    )
    return pl.pallas_call(
        _moe_kernel,
        grid_spec=grid_spec,
        out_shape=jax.ShapeDtypeStruct(rows.shape, F32),
        compiler_params=_cparams("arbitrary", "arbitrary"),
        name="moe_expert_ffn",
    )(block_e, block_x, block_v, rows, w1g, w1l, b1g, b1l, w2, b2)


def _moe(x2_p, x2_s, lw):
    x2 = jnp.concatenate([x2_p, x2_s], axis=0)
    T = x2.shape[0]
    logits = jnp.concatenate([
        _router(x2_p, lw['w_router'], lw['b_router'], 512),
        _router(x2_s, lw['w_router'], lw['b_router'], x2_s.shape[0])], axis=0)[:, :N_EXPERTS]
    top_v, top_i = lax.top_k(logits, TOP_K)
    gate = jax.nn.softmax(top_v, axis=-1)
    n_assign = T * TOP_K
    bm = MOE_BM
    nb = -(-(n_assign + N_EXPERTS * (bm - 1)) // bm)
    flat_e = top_i.reshape(-1)
    order = jnp.argsort(flat_e)
    e_sorted = flat_e[order]
    tok_sorted = order // TOP_K
    counts = jnp.bincount(flat_e, length=N_EXPERTS)
    padded = (counts + bm - 1) // bm * bm
    p_end = jnp.cumsum(padded)
    u_start = jnp.cumsum(counts) - counts
    dest = (p_end - padded)[e_sorted] + jnp.arange(n_assign) - u_start[e_sorted]
    src = jnp.zeros((nb * bm,), jnp.int32).at[dest].set(tok_sorted.astype(jnp.int32))
    rows = x2.astype(BF16)[src]
    n_valid = (p_end[-1] // bm).astype(jnp.int32)
    blk = jnp.arange(nb, dtype=jnp.int32)
    block_v = (blk < n_valid).astype(jnp.int32)
    block_x = jnp.minimum(blk, n_valid - 1)
    block_e = jnp.minimum(jnp.searchsorted(p_end, block_x * bm, side='right'), N_EXPERTS - 1).astype(jnp.int32)
    out_rows = _moe_ffn_blocks(rows, block_e, block_x, block_v,
                               lw['w1g'], lw['w1l'], lw['b1g'], lw['b1l'], lw['w2'], lw['b2'])
    pos = jnp.zeros((n_assign,), jnp.int32).at[order].set(dest.astype(jnp.int32)).reshape(T, TOP_K)
    return jnp.sum(out_rows[pos] * gate[..., None], axis=1)


def _layer_norm(x, g, b):
    mu = jnp.mean(x, axis=-1, keepdims=True)
    var = jnp.mean(jnp.square(x - mu), axis=-1, keepdims=True)
    return (x - mu) * lax.rsqrt(var + LN_EPS) * g + b


def _t5_bucket(dist):
    n = jnp.maximum(dist, 0)
    exact = N_BUCKETS // 2
    nf = jnp.maximum(n, 1).astype(F32)
    large = exact + (jnp.log(nf / exact) / math.log(MAX_DIST / exact) * (N_BUCKETS - exact)).astype(jnp.int32)
    return jnp.where(n < exact, n, jnp.minimum(large, N_BUCKETS - 1))


def _t5_bias(dist, rel_bias):
    b = jnp.moveaxis(rel_bias[_t5_bucket(dist)], -1, 0)
    return b.reshape((KV_A, G_A) + dist.shape).astype(F32)


def _sink_attention(q, k, v, bias, mask, sinks):
    s = jnp.einsum('...qhgd,...shd->...hgqs', q, k).astype(F32) * HD_A ** -0.5 + bias
    s = jnp.where(mask, s, -jnp.inf)
    sk = sinks.astype(F32).reshape(KV_A, G_A, 1, 1)
    m = jnp.maximum(jnp.max(s, axis=-1, keepdims=True), sk)
    pr = jnp.exp(s - m)
    den = jnp.sum(pr, axis=-1, keepdims=True) + jnp.exp(sk - m)
    return jnp.einsum('...hgqs,...shd->...qhgd', (pr / den).astype(v.dtype), v)


def _swa_prompt(q, k, v, sinks, rel_bias):
    bsz, L = q.shape[:2]
    nb = L // WINDOW
    qb = q.reshape(bsz, nb, WINDOW, KV_A, G_A, HD_A)

    def band(t):
        tb = t.reshape(bsz, nb, WINDOW, KV_A, HD_A)
        prev = jnp.pad(tb[:, :-1], ((0, 0), (1, 0), (0, 0), (0, 0), (0, 0)))
        return jnp.concatenate([prev, tb], axis=2)

    i = jnp.arange(WINDOW)[:, None]
    j = jnp.arange(2 * WINDOW)[None, :]
    dist = i + WINDOW - j
    kpos = (jnp.arange(nb) * WINDOW - WINDOW)[:, None, None] + j
    mask = ((dist >= 0) & (dist < WINDOW) & (kpos >= 0))[:, None, None]
    out = _sink_attention(qb, band(k), band(v), _t5_bias(dist, rel_bias), mask, sinks)
    return out.reshape(bsz, L, H_A * HD_A)


def _swa_decode(q, k, v, k_buf, v_buf, sinks, rel_bias):
    bsz, L = q.shape[:2]
    wb = k_buf.shape[1]
    kk = jnp.concatenate([k_buf, k], axis=1)
    vv = jnp.concatenate([v_buf, v], axis=1)
    q_pos = PAST_LEN + jnp.arange(L)
    k_pos = PAST_LEN - wb + jnp.arange(wb + L)
    dist = q_pos[:, None] - k_pos[None, :]
    mask = (dist >= 0) & (dist < WINDOW)
    out = _sink_attention(q, kk, vv, _t5_bias(dist, rel_bias), mask, sinks)
    return out.reshape(bsz, L, H_A * HD_A), kk[:, L:], vv[:, L:]


def _gmlp_mix(vg, ws, bs):
    bsz, L = vg.shape[:2]
    nc = -(-L // CHUNK_B)
    pad = nc * CHUNK_B - L
    vp = jnp.pad(vg, ((0, 0), (0, pad), (0, 0), (0, 0))).reshape(bsz, nc, CHUNK_B, GB, CG_B)
    w = ws * jnp.tril(jnp.ones((CHUNK_B, CHUNK_B), ws.dtype))
    out = jnp.einsum('gts,bcsgd->bctgd', w, vp) + bs.T[:, :, None]
    return out.reshape(bsz, nc * CHUNK_B, GB, CG_B)[:, :L]


def _causal_dwconv(xpad, w, bias):
    L = xpad.shape[1] - (CONV_W - 1)
    y = bias
    for t in range(CONV_W):
        y = y + xpad[:, t:t + L] * w[t]
    return y


def _ssd_scan(x, dt, a, bm, cm, h0):
    bsz, L = x.shape[:2]
    cl = min(SSD_CHUNK, L)
    nc = -(-L // cl)
    pad = nc * cl - L

    def chunks(t):
        t = jnp.pad(t.astype(F32), [(0, 0), (0, pad)] + [(0, 0)] * (t.ndim - 2))
        return t.reshape((bsz, nc, cl) + t.shape[2:])

    x, dt, bm, cm = chunks(x), chunks(dt), chunks(bm), chunks(cm)
    acs = jnp.cumsum(dt * a, axis=2)
    xdt = x * dt[..., None]
    causal = jnp.tril(jnp.ones((cl, cl), bool))[:, :, None, None]
    seg = acs[:, :, :, None] - acs[:, :, None, :]
    decay = jnp.exp(jnp.where(causal, seg, -jnp.inf))
    cb = jnp.einsum('bclgn,bcsgn->bclsg', cm, bm)
    y_diag = jnp.einsum('bclsgr,bcsgrp->bclgrp', cb[..., None] * decay, xdt)
    end_decay = jnp.exp(acs[:, :, -1:] - acs)
    states = jnp.einsum('bclgn,bclgrp->bcgrpn', bm, xdt * end_decay[..., None])
    chunk_decay = jnp.exp(acs[:, :, -1])

    def carry(h, inp):
        st, dec = inp
        return h * dec[..., None, None] + st, h

    h_last, h_in = lax.scan(carry, h0, (jnp.moveaxis(states, 1, 0), jnp.moveaxis(chunk_decay, 1, 0)))
    h_in = jnp.moveaxis(h_in, 0, 1)
    y_off = jnp.einsum('bclgn,bcgrpn->bclgrp', cm, h_in) * jnp.exp(acs)[..., None]
    y = (y_diag + y_off).reshape((bsz, nc * cl) + x.shape[3:])[:, :L]
    return y, h_last


def _mixer_cores(proj, p, rel_bias, win_buf, win_k=None, win_v=None, conv_st=None, ssm_st=None):
    bsz, L, _ = proj.shape
    prompt = win_k is None
    q, k, v, u, gv, z, xbc, dt_raw = jnp.split(proj, np.cumsum(IN_SIZES)[:-1].tolist(), axis=-1)
    q = q.reshape(bsz, L, KV_A, G_A, HD_A)
    k = k.reshape(bsz, L, KV_A, HD_A)
    v = v.reshape(bsz, L, KV_A, HD_A)
    if prompt:
        a_out = _swa_prompt(q, k, v, p['sinks'], rel_bias)
        new_wk, new_wv = k[:, L - win_buf:], v[:, L - win_buf:]
    else:
        a_out, new_wk, new_wv = _swa_decode(q, k, v, win_k, win_v, p['sinks'], rel_bias)

    u = jax.nn.gelu(u, approximate=False)
    gv = _layer_norm(jax.nn.gelu(gv, approximate=False), p['gmlp_ln_g'], p['gmlp_ln_b'])
    b_out = u * _gmlp_mix(gv.reshape(bsz, L, GB, CG_B), p['gmlp_ws'], p['gmlp_bs']).reshape(bsz, L, W_B)

    hist = jnp.zeros((bsz, CONV_W - 1, CONV_DIM), xbc.dtype) if prompt else conv_st
    xpad = jnp.concatenate([hist, xbc], axis=1)
    new_conv = xpad[:, xpad.shape[1] - (CONV_W - 1):]
    xbc_c = jax.nn.silu(_causal_dwconv(xpad, p['conv_w'], p['conv_b']))
    xs, bm, cm = jnp.split(xbc_c, [D_INNER, D_INNER + G_C * N_C], axis=-1)
    xs = xs.reshape(bsz, L, G_C, R_C, P_C)
    bm = bm.reshape(bsz, L, G_C, N_C)
    cm = cm.reshape(bsz, L, G_C, N_C)
    dt = jax.nn.softplus((dt_raw + p['dt_bias']).astype(F32)).reshape(bsz, L, G_C, R_C)
    a = -jnp.exp(p['a_log'].astype(F32)).reshape(G_C, R_C)
    if prompt:
        h0 = jnp.zeros((bsz, G_C, R_C, P_C, N_C), F32)
    else:
        h0 = ssm_st.astype(F32).reshape(bsz, G_C, R_C, P_C, N_C)
    y, h_last = _ssd_scan(xs, dt, a, bm, cm, h0)
    y = y + p['d_skip'].astype(F32).reshape(G_C, R_C, 1) * xs.astype(F32)
    y = (y.reshape(bsz, L, D_INNER) * jax.nn.silu(z.astype(F32))).reshape(bsz, L, G_C, D_INNER // G_C)
    y = y * lax.rsqrt(jnp.mean(y * y, axis=-1, keepdims=True) + RMS_EPS)
    c_out = y.reshape(bsz, L, D_INNER) * p['ssm_norm_g']
    new_ssm = h_last.reshape(bsz, H_C, P_C, N_C)
    return a_out, b_out, c_out, (new_wk, new_wv, new_conv, new_ssm, gv)


def _mem_attention_core(q, mk, mv):
    bsz, L, _ = q.shape
    q = q.reshape(bsz, L, XH, XHD)
    s = jnp.einsum('blhd,bmhd->bhlm', q, mk).astype(F32) * XHD ** -0.5
    w = jax.nn.softmax(s, axis=-1)
    return jnp.einsum('bhlm,bmhd->blhd', w, mv).reshape(bsz, L, XH * XHD)


def _layer_group(x, mk, mv, lw, p, rel_bias, win_buf, tm, caches=None):
    bsz, L, _ = x.shape
    T = bsz * L
    xf = x.reshape(T, D_MODEL)
    xb = xf.astype(BF16)
    proj = _matmul(xb, lw['w_in'], tm, 512)[:, :IN_DIM].reshape(bsz, L, IN_DIM)
    if caches is None:
        a_out, b_out, c_out, states = _mixer_cores(proj, p, rel_bias, win_buf)
    else:
        a_out, b_out, c_out, states = _mixer_cores(proj, p, rel_bias, win_buf, *caches)
    br = [t.reshape(T, BR_W).astype(BF16) for t in (a_out, b_out, c_out)]
    tmg = min(tm, 512)
    mixed = _gate_merge(xb, br[0], br[1], br[2], lw['w_gate'], lw['b_gate'], lw['w_branch'], tmg, 512)
    x1 = _mm_res_ln(mixed, lw['w_o'], xf, p['ln1_g'], p['ln1_b'], tmg)
    q = _matmul(x1.astype(BF16), lw['w_xq'], tmg, XH * XHD)
    o = _mem_attention_core(q.reshape(bsz, L, XH * XHD), mk, mv).reshape(T, XH * XHD)
    x2 = _mm_res_ln(o.astype(BF16), lw['w_xo'], x1, p['ln2_g'], p['ln2_b'], tmg)
    return x2, states


def kernel(x_prompt, x_sample, mem_prompt, cache_win_k, cache_win_v, state_conv, state_ssm, cache_mem_k, cache_mem_v, w_in, rel_bias, sinks, gmlp_ln_g, gmlp_ln_b, gmlp_ws, gmlp_bs, conv_w, conv_b, dt_bias, a_log, d_skip, ssm_norm_g, w_branch, w_gate, b_gate, w_o, ln1_g, ln1_b, w_xq, w_xk, w_xv, w_xo, ln2_g, ln2_b, w_router, b_router, w_e1, b_e1, w_e2, b_e2, ln3_g, ln3_b):
    win_buf = cache_win_k.shape[2]
    n_prompt, n_mem = mem_prompt.shape[0], mem_prompt.shape[1]
    bp, lp = x_prompt.shape[:2]
    bs_, ls = x_sample.shape[:2]
    tp, ts = bp * lp, bs_ * ls
    hp, hs = x_prompt, x_sample
    wk_p, wv_p, cv_p, ssm_p, mk_ps, mv_ps = [], [], [], [], [], []
    wk_s, wv_s, cv_s, ssm_s, gv_s = [], [], [], [], []
    mem_b = mem_prompt.reshape(n_prompt * n_mem, D_MODEL).astype(BF16)
    for l in range(DEPTH):
        p = dict(sinks=sinks[l], gmlp_ln_g=gmlp_ln_g[l], gmlp_ln_b=gmlp_ln_b[l],
                 gmlp_ws=gmlp_ws[l], gmlp_bs=gmlp_bs[l], conv_w=conv_w[l], conv_b=conv_b[l],
                 dt_bias=dt_bias[l], a_log=a_log[l], d_skip=d_skip[l], ssm_norm_g=ssm_norm_g[l],
                 ln1_g=ln1_g[l], ln1_b=ln1_b[l], ln2_g=ln2_g[l], ln2_b=ln2_b[l])
        lw = dict(
            w_in=jnp.pad(w_in[l].astype(BF16), ((0, 0), (0, IN_DIM_PAD - IN_DIM))),
            w_gate=w_gate[l].astype(BF16),
            b_gate=b_gate[l].reshape(1, -1),
            w_branch=w_branch[l].astype(BF16),
            w_o=w_o[l].astype(BF16),
            w_xq=w_xq[l].astype(BF16),
            w_xo=w_xo[l].astype(BF16),
            w_router=jnp.pad(w_router[l], ((0, 0), (0, 128 - N_EXPERTS))),
            b_router=jnp.pad(b_router[l], (0, 128 - N_EXPERTS)).reshape(1, -1),
            w1g=w_e1[l][:, :, 0::2].astype(BF16),
            w1l=w_e1[l][:, :, 1::2].astype(BF16),
            b1g=b_e1[l][:, 0::2].reshape(N_EXPERTS, 1, D_FF),
            b1l=b_e1[l][:, 1::2].reshape(N_EXPERTS, 1, D_FF),
            w2=w_e2[l].astype(BF16),
            b2=b_e2[l].reshape(N_EXPERTS, 1, D_MODEL),
        )
        w_kv = jnp.concatenate([w_xk[l], w_xv[l]], axis=1).astype(BF16)
        mkv = _matmul(mem_b, w_kv, n_mem, XH * XHD)
        mk = mkv[:, :XH * XHD].reshape(n_prompt, n_mem, XH, XHD)
        mv = mkv[:, XH * XHD:].reshape(n_prompt, n_mem, XH, XHD)
        x2_p, st_p = _layer_group(hp, mk, mv, lw, p, rel_bias, win_buf, 1024)
        x2_s, st_s = _layer_group(hs, cache_mem_k[l], cache_mem_v[l], lw, p, rel_bias, win_buf, ts,
                                  caches=(cache_win_k[l], cache_win_v[l], state_conv[l], state_ssm[l]))
        ff = _moe(x2_p, x2_s, lw)
        hp = _res_ln(ff[:tp], x2_p, ln3_g[l], ln3_b[l], 512).reshape(bp, lp, D_MODEL)
        hs = _res_ln(ff[tp:], x2_s, ln3_g[l], ln3_b[l], ts).reshape(bs_, ls, D_MODEL)
        wk_p.append(st_p[0]); wv_p.append(st_p[1]); cv_p.append(st_p[2]); ssm_p.append(st_p[3])
        mk_ps.append(mk); mv_ps.append(mv)
        wk_s.append(st_s[0]); wv_s.append(st_s[1]); cv_s.append(st_s[2]); ssm_s.append(st_s[3])
        gv_s.append(st_s[4])
    return (hp, hs,
            jnp.stack(wk_p), jnp.stack(wv_p), jnp.stack(cv_p), jnp.stack(ssm_p),
            jnp.stack(mk_ps), jnp.stack(mv_ps),
            jnp.stack(wk_s), jnp.stack(wv_s), jnp.stack(cv_s), jnp.stack(ssm_s), jnp.stack(gv_s))
```

```python
import functools
import math

import numpy as np
import jax
import jax.numpy as jnp
from jax import lax
from jax.experimental import pallas as pl
from jax.experimental.pallas import tpu as pltpu

D_MODEL = 2048
DEPTH = 2
PAST_LEN = 16384
WINDOW = 128
H_A = 16
KV_A = 2
HD_A = 64
G_A = H_A // KV_A
N_BUCKETS = 32
MAX_DIST = 128
CHUNK_B = 128
GB = 16
CG_B = 64
W_B = GB * CG_B
D_INNER = 1024
P_C = 64
H_C = D_INNER // P_C
G_C = 2
R_C = H_C // G_C
N_C = 128
CONV_W = 4
CONV_DIM = D_INNER + 2 * G_C * N_C
SSD_CHUNK = 128
N_BRANCH = 3
BR_W = 1024
N_MEM = 256
XH = 4
XHD = 128
N_EXPERTS = 32
TOP_K = 4
D_FF = D_MODEL
SWIGLU_ALPHA = 1.702
SWIGLU_LIMIT = 7.0
DN_ALPHA = (2 * DEPTH) ** 0.25
LN_EPS = 1e-5
RMS_EPS = 1e-5

IN_SIZES = [H_A * HD_A, KV_A * HD_A, KV_A * HD_A, W_B, W_B, D_INNER, CONV_DIM, H_C]
IN_DIM = sum(IN_SIZES)
IN_DIM_PAD = 6144

VMEM_LIMIT = 56 * 1024 * 1024
MOE_BM = 512
MOE_FC = 256

F32 = jnp.float32
BF16 = jnp.bfloat16


def _cparams(*sem):
    return pltpu.CompilerParams(dimension_semantics=sem, vmem_limit_bytes=VMEM_LIMIT)


def _mm_kernel(x_ref, w_ref, o_ref):
    o_ref[...] = jnp.dot(x_ref[...], w_ref[...], preferred_element_type=F32).astype(o_ref.dtype)


def _matmul(x, w, tm, tn, out_dtype=F32):
    M, K = x.shape
    N = w.shape[1]
    assert M % tm == 0 and N % tn == 0
    return pl.pallas_call(
        _mm_kernel,
        grid=(N // tn, M // tm),
        in_specs=[pl.BlockSpec((tm, K), lambda j, i: (i, 0)),
                  pl.BlockSpec((K, tn), lambda j, i: (0, j))],
        out_specs=pl.BlockSpec((tm, tn), lambda j, i: (i, j)),
        out_shape=jax.ShapeDtypeStruct((M, N), out_dtype),
        compiler_params=_cparams("parallel", "parallel"),
        name="dense_matmul",
    )(x, w)


def _gate_merge_kernel(x_ref, a_ref, b_ref, c_ref, wg0_ref, wg1_ref, wg2_ref,
                       bg0_ref, bg1_ref, bg2_ref, wp_ref, o_ref):
    x = x_ref[...]
    acc = None
    for k, (br_ref, wg_ref, bg_ref) in enumerate(
            ((a_ref, wg0_ref, bg0_ref), (b_ref, wg1_ref, bg1_ref), (c_ref, wg2_ref, bg2_ref))):
        z = jnp.dot(x, wg_ref[...], preferred_element_type=F32) + bg_ref[...]
        gate = 1.0 / (1.0 + jnp.exp(-z))
        proj = jnp.dot(br_ref[...], wp_ref[k], preferred_element_type=F32)
        acc = gate * proj if acc is None else acc + gate * proj
    o_ref[...] = acc.astype(o_ref.dtype)


def _gate_merge(x, a, b, c, w_gate, b_gate, w_branch, tm, tn):
    M = x.shape[0]
    nt = D_MODEL // tn
    row = lambda j, i: (i, 0)
    in_specs = [pl.BlockSpec((tm, D_MODEL), row)] + [pl.BlockSpec((tm, BR_W), row)] * 3
    in_specs += [pl.BlockSpec((D_MODEL, tn), functools.partial(lambda j, i, k: (0, k * nt + j), k=k))
                 for k in range(N_BRANCH)]
    in_specs += [pl.BlockSpec((1, tn), functools.partial(lambda j, i, k: (0, k * nt + j), k=k))
                 for k in range(N_BRANCH)]
    in_specs += [pl.BlockSpec((N_BRANCH, BR_W, tn), lambda j, i: (0, 0, j))]
    return pl.pallas_call(
        _gate_merge_kernel,
        grid=(nt, M // tm),
        in_specs=in_specs,
        out_specs=pl.BlockSpec((tm, tn), lambda j, i: (i, j)),
        out_shape=jax.ShapeDtypeStruct((M, D_MODEL), BF16),
        compiler_params=_cparams("parallel", "parallel"),
        name="gate_merge",
    )(x, a, b, c, w_gate, w_gate, w_gate, b_gate, b_gate, b_gate, w_branch)


def _layer_norm_rows(y, g, b):
    mu = jnp.mean(y, axis=-1, keepdims=True)
    yc = y - mu
    var = jnp.mean(yc * yc, axis=-1, keepdims=True)
    return yc * lax.rsqrt(var + LN_EPS) * g + b


def _mm_res_ln_kernel(a_ref, w_ref, res_ref, g_ref, b_ref, o_ref):
    y = jnp.dot(a_ref[...], w_ref[...], preferred_element_type=F32) + DN_ALPHA * res_ref[...]
    o_ref[...] = _layer_norm_rows(y, g_ref[...], b_ref[...])


def _mm_res_ln(a, w, res, g, b, tm):
    M, K = a.shape
    row = lambda i: (i, 0)
    fixed = lambda i: (0, 0)
    return pl.pallas_call(
        _mm_res_ln_kernel,
        grid=(M // tm,),
        in_specs=[pl.BlockSpec((tm, K), row), pl.BlockSpec((K, D_MODEL), fixed),
                  pl.BlockSpec((tm, D_MODEL), row), pl.BlockSpec((1, D_MODEL), fixed),
                  pl.BlockSpec((1, D_MODEL), fixed)],
        out_specs=pl.BlockSpec((tm, D_MODEL), row),
        out_shape=jax.ShapeDtypeStruct((M, D_MODEL), F32),
        compiler_params=_cparams("parallel"),
        name="matmul_residual_layernorm",
    )(a, w, res, g.reshape(1, -1), b.reshape(1, -1))


def _router_kernel(x_ref, w_ref, b_ref, o_ref):
    acc = jnp.dot(x_ref[...].astype(BF16), w_ref[...].astype(BF16), preferred_element_type=F32)
    o_ref[...] = acc + b_ref[...]


def _router(x, w_pad, b_pad, tm):
    M = x.shape[0]
    NP = w_pad.shape[1]
    return pl.pallas_call(
        _router_kernel,
        grid=(M // tm,),
        in_specs=[pl.BlockSpec((tm, D_MODEL), lambda i: (i, 0)),
                  pl.BlockSpec((D_MODEL, NP), lambda i: (0, 0)),
                  pl.BlockSpec((1, NP), lambda i: (0, 0))],
        out_specs=pl.BlockSpec((tm, NP), lambda i: (i, 0)),
        out_shape=jax.ShapeDtypeStruct((M, NP), F32),
        compiler_params=_cparams("parallel"),
        name="router_logits",
    )(x, w_pad, b_pad)


def _bf16_row_interleave(a, b):
    a32 = lax.bitcast_convert_type(a.astype(BF16).astype(F32), jnp.uint32)
    b32 = lax.bitcast_convert_type(b.astype(BF16).astype(F32), jnp.uint32)
    word = (a32 >> 16) | (b32 & jnp.uint32(0xFFFF0000))
    return pltpu.bitcast(word, BF16)


def _moe_kernel(be_ref, bx_ref, bv_ref, x_ref, w1_ref, b1_ref, w2_ref, b2_ref, o_ref):
    i = pl.program_id(0)
    f = pl.program_id(1)
    fc = w2_ref.shape[1] // 2
    bm = x_ref.shape[0]

    @pl.when(bv_ref[i] == 1)
    def _():
        x = x_ref[...]
        w1 = w1_ref[0].astype(BF16)
        h = jnp.dot(x, w1, preferred_element_type=F32) + b1_ref[0]
        lane = lax.broadcasted_iota(jnp.int32, (bm, 128), 1)
        even = (lane & 1) == 0
        acts = []
        for s in range(2 * fc // 128):
            a = h[:, 128 * s:128 * (s + 1)]
            b = h[:, 2 * fc + 128 * s:2 * fc + 128 * (s + 1)]
            glu = jnp.where(even, a, pltpu.roll(b, 1, axis=1))
            lin = jnp.where(even, pltpu.roll(a, 127, axis=1), b)
            glu = jnp.minimum(glu, SWIGLU_LIMIT)
            lin = jnp.clip(lin, -SWIGLU_LIMIT, SWIGLU_LIMIT)
            act = glu * (1.0 / (1.0 + jnp.exp(-SWIGLU_ALPHA * glu))) * (lin + 1.0)
            acts.append(act.astype(BF16))
        act = jnp.concatenate(acts, axis=1) if len(acts) > 1 else acts[0]
        w2 = w2_ref[0]
        w2q = _bf16_row_interleave(w2[:fc], w2[fc:])
        contrib = jnp.dot(act, w2q, preferred_element_type=F32)

        @pl.when(f == 0)
        def _():
            o_ref[...] = contrib + b2_ref[0]

        @pl.when(f != 0)
        def _():
            o_ref[...] += contrib


def _moe_ffn_blocks(rows, block_e, block_x, block_v, w1, b1, w2, b2, bm, fc):
    d_model = rows.shape[1]
    d_ff = w2.shape[1]
    nb = rows.shape[0] // bm
    nf = d_ff // (2 * fc)
    last_f = nf - 1

    def fsel(f, bv, i):
        return jnp.where(bv[i] == 1, f, last_f)

    grid_spec = pltpu.PrefetchScalarGridSpec(
        num_scalar_prefetch=3,
        grid=(nb, nf),
        in_specs=[
            pl.BlockSpec((bm, d_model), lambda i, f, be, bx, bv: (bx[i], 0)),
            pl.BlockSpec((1, d_model, 4 * fc), lambda i, f, be, bx, bv: (be[i], 0, fsel(f, bv, i))),
            pl.BlockSpec((1, 1, 4 * fc), lambda i, f, be, bx, bv: (be[i], 0, fsel(f, bv, i))),
            pl.BlockSpec((1, 2 * fc, d_model), lambda i, f, be, bx, bv: (be[i], fsel(f, bv, i), 0)),
            pl.BlockSpec((1, 1, d_model), lambda i, f, be, bx, bv: (be[i], 0, 0)),
        ],
        out_specs=pl.BlockSpec((bm, d_model), lambda i, f, be, bx, bv: (bx[i], 0)),
    )
    return pl.pallas_call(
        _moe_kernel,
        grid_spec=grid_spec,
        out_shape=jax.ShapeDtypeStruct(rows.shape, F32),
        compiler_params=_cparams("arbitrary", "arbitrary"),
        name="moe_expert_ffn",
    )(block_e, block_x, block_v, rows, w1, b1, w2, b2)


def _combine_ln_kernel(g_ref, gate_ref, res_ref, lg_ref, lb_ref, o_ref):
    d = res_ref.shape[1]
    gate = gate_ref[...]
    y = DN_ALPHA * res_ref[...]
    ff = None
    for k in range(TOP_K):
        term = gate[:, k:k + 1] * g_ref[:, k * d:(k + 1) * d]
        ff = term if ff is None else ff + term
    o_ref[...] = _layer_norm_rows(y + ff, lg_ref[...], lb_ref[...])


def _combine_ln(gathered, gate, res, g, b, tm):
    T, d = res.shape
    row = lambda i: (i, 0)
    fixed = lambda i: (0, 0)
    return pl.pallas_call(
        _combine_ln_kernel,
        grid=(T // tm,),
        in_specs=[pl.BlockSpec((tm, TOP_K * d), row), pl.BlockSpec((tm, TOP_K), row),
                  pl.BlockSpec((tm, d), row), pl.BlockSpec((1, d), fixed), pl.BlockSpec((1, d), fixed)],
        out_specs=pl.BlockSpec((tm, d), row),
        out_shape=jax.ShapeDtypeStruct((T, d), F32),
        compiler_params=_cparams("parallel"),
        name="moe_combine_layernorm",
    )(gathered, gate, res, g.reshape(1, -1), b.reshape(1, -1))


def _moe_route(logits, bm):
    T, n_exp = logits.shape
    top_v, top_i = lax.top_k(logits, TOP_K)
    gate = jax.nn.softmax(top_v, axis=-1)
    n_assign = T * TOP_K
    nb = -(-(n_assign + n_exp * (bm - 1)) // bm)
    flat_e = top_i.reshape(-1)
    order = jnp.argsort(flat_e)
    e_sorted = flat_e[order]
    tok_sorted = (order // TOP_K).astype(jnp.int32)
    counts = jnp.bincount(flat_e, length=n_exp)
    padded = (counts + bm - 1) // bm * bm
    p_end = jnp.cumsum(padded)
    u_start = jnp.cumsum(counts) - counts
    dest = ((p_end - padded)[e_sorted] + jnp.arange(n_assign) - u_start[e_sorted]).astype(jnp.int32)
    src = jnp.zeros((nb * bm,), jnp.int32).at[dest].set(tok_sorted)
    pos = jnp.zeros((n_assign,), jnp.int32).at[order].set(dest)
    n_valid = (p_end[-1] // bm).astype(jnp.int32)
    blk = jnp.arange(nb, dtype=jnp.int32)
    block_v = (blk < n_valid).astype(jnp.int32)
    block_x = jnp.minimum(blk, n_valid - 1)
    block_e = jnp.minimum(jnp.searchsorted(p_end, block_x * bm, side='right'), n_exp - 1).astype(jnp.int32)
    return gate, src, pos, block_e, block_x, block_v


def _moe_ln(x2_p, x2_s, lw, ln_g, ln_b):
    tp, ts = x2_p.shape[0], x2_s.shape[0]
    x2 = jnp.concatenate([x2_p, x2_s], axis=0)
    logits = jnp.concatenate([
        _router(x2_p, lw['w_router'], lw['b_router'], 512),
        _router(x2_s, lw['w_router'], lw['b_router'], ts)], axis=0)[:, :N_EXPERTS]
    gate, src, pos, block_e, block_x, block_v = _moe_route(logits, MOE_BM)
    rows = jnp.take(x2.astype(BF16), src, axis=0)
    out_rows = _moe_ffn_blocks(rows, block_e, block_x, block_v,
                               lw['w_e1'], lw['b_e1'], lw['w_e2'], lw['b_e2'], MOE_BM, MOE_FC)
    g_p = jnp.take(out_rows, pos[:tp * TOP_K], axis=0).reshape(tp, TOP_K * D_MODEL)
    g_s = jnp.take(out_rows, pos[tp * TOP_K:], axis=0).reshape(ts, TOP_K * D_MODEL)
    hp = _combine_ln(g_p, gate[:tp], x2_p, ln_g, ln_b, 256)
    hs = _combine_ln(g_s, gate[tp:], x2_s, ln_g, ln_b, ts)
    return hp, hs


def _layer_norm(x, g, b):
    mu = jnp.mean(x, axis=-1, keepdims=True)
    var = jnp.mean(jnp.square(x - mu), axis=-1, keepdims=True)
    return (x - mu) * lax.rsqrt(var + LN_EPS) * g + b


def _t5_bucket(dist):
    n = jnp.maximum(dist, 0)
    exact = N_BUCKETS // 2
    nf = jnp.maximum(n, 1).astype(F32)
    large = exact + (jnp.log(nf / exact) / math.log(MAX_DIST / exact) * (N_BUCKETS - exact)).astype(jnp.int32)
    return jnp.where(n < exact, n, jnp.minimum(large, N_BUCKETS - 1))


def _t5_bias(dist, rel_bias):
    b = jnp.moveaxis(rel_bias[_t5_bucket(dist)], -1, 0)
    return b.reshape((KV_A, G_A) + dist.shape).astype(F32)


def _sink_attention(q, k, v, bias, mask, sinks):
    s = jnp.einsum('...qhgd,...shd->...hgqs', q, k).astype(F32) * HD_A ** -0.5 + bias
    s = jnp.where(mask, s, -jnp.inf)
    sk = sinks.astype(F32).reshape(KV_A, G_A, 1, 1)
    m = jnp.maximum(jnp.max(s, axis=-1, keepdims=True), sk)
    pr = jnp.exp(s - m)
    den = jnp.sum(pr, axis=-1, keepdims=True) + jnp.exp(sk - m)
    return jnp.einsum('...hgqs,...shd->...qhgd', (pr / den).astype(v.dtype), v)


def _swa_prompt(q, k, v, sinks, rel_bias):
    bsz, L = q.shape[:2]
    nb = L // WINDOW
    qb = q.reshape(bsz, nb, WINDOW, KV_A, G_A, HD_A)

    def band(t):
        tb = t.reshape(bsz, nb, WINDOW, KV_A, HD_A)
        prev = jnp.pad(tb[:, :-1], ((0, 0), (1, 0), (0, 0), (0, 0), (0, 0)))
        return jnp.concatenate([prev, tb], axis=2)

    i = jnp.arange(WINDOW)[:, None]
    j = jnp.arange(2 * WINDOW)[None, :]
    dist = i + WINDOW - j
    kpos = (jnp.arange(nb) * WINDOW - WINDOW)[:, None, None] + j
    mask = ((dist >= 0) & (dist < WINDOW) & (kpos >= 0))[:, None, None]
    out = _sink_attention(qb, band(k), band(v), _t5_bias(dist, rel_bias), mask, sinks)
    return out.reshape(bsz, L, H_A * HD_A)


def _swa_decode(q, k, v, k_buf, v_buf, sinks, rel_bias):
    bsz, L = q.shape[:2]
    wb = k_buf.shape[1]
    kk = jnp.concatenate([k_buf, k], axis=1)
    vv = jnp.concatenate([v_buf, v], axis=1)
    q_pos = PAST_LEN + jnp.arange(L)
    k_pos = PAST_LEN - wb + jnp.arange(wb + L)
    dist = q_pos[:, None] - k_pos[None, :]
    mask = (dist >= 0) & (dist < WINDOW)
    out = _sink_attention(q, kk, vv, _t5_bias(dist, rel_bias), mask, sinks)
    return out.reshape(bsz, L, H_A * HD_A), kk[:, L:], vv[:, L:]


def _gmlp_mix(vg, ws, bs):
    bsz, L = vg.shape[:2]
    nc = -(-L // CHUNK_B)
    pad = nc * CHUNK_B - L
    vp = jnp.pad(vg, ((0, 0), (0, pad), (0, 0), (0, 0))).reshape(bsz, nc, CHUNK_B, GB, CG_B)
    w = ws * jnp.tril(jnp.ones((CHUNK_B, CHUNK_B), ws.dtype))
    out = jnp.einsum('gts,bcsgd->bctgd', w, vp) + bs.T[:, :, None]
    return out.reshape(bsz, nc * CHUNK_B, GB, CG_B)[:, :L]


def _causal_dwconv(xpad, w, bias):
    L = xpad.shape[1] - (CONV_W - 1)
    y = bias
    for t in range(CONV_W):
        y = y + xpad[:, t:t + L] * w[t]
    return y


def _ssd_scan(x, dt, a, bm, cm, h0):
    bsz, L = x.shape[:2]
    cl = min(SSD_CHUNK, L)
    nc = -(-L // cl)
    pad = nc * cl - L

    def chunks(t):
        t = jnp.pad(t.astype(F32), [(0, 0), (0, pad)] + [(0, 0)] * (t.ndim - 2))
        return t.reshape((bsz, nc, cl) + t.shape[2:])

    x, dt, bm, cm = chunks(x), chunks(dt), chunks(bm), chunks(cm)
    acs = jnp.cumsum(dt * a, axis=2)
    xdt = x * dt[..., None]
    causal = jnp.tril(jnp.ones((cl, cl), bool))[:, :, None, None]
    seg = acs[:, :, :, None] - acs[:, :, None, :]
    decay = jnp.exp(jnp.where(causal, seg, -jnp.inf))
    cb = jnp.einsum('bclgn,bcsgn->bclsg', cm, bm)
    y_diag = jnp.einsum('bclsgr,bcsgrp->bclgrp', cb[..., None] * decay, xdt)
    end_decay = jnp.exp(acs[:, :, -1:] - acs)
    states = jnp.einsum('bclgn,bclgrp->bcgrpn', bm, xdt * end_decay[..., None])
    chunk_decay = jnp.exp(acs[:, :, -1])

    def carry(h, inp):
        st, dec = inp
        return h * dec[..., None, None] + st, h

    h_last, h_in = lax.scan(carry, h0, (jnp.moveaxis(states, 1, 0), jnp.moveaxis(chunk_decay, 1, 0)))
    h_in = jnp.moveaxis(h_in, 0, 1)
    y_off = jnp.einsum('bclgn,bcgrpn->bclgrp', cm, h_in) * jnp.exp(acs)[..., None]
    y = (y_diag + y_off).reshape((bsz, nc * cl) + x.shape[3:])[:, :L]
    return y, h_last


def _mixer_cores(proj, p, rel_bias, win_buf, win_k=None, win_v=None, conv_st=None, ssm_st=None):
    bsz, L, _ = proj.shape
    prompt = win_k is None
    q, k, v, u, gv, z, xbc, dt_raw = jnp.split(proj, np.cumsum(IN_SIZES)[:-1].tolist(), axis=-1)
    q = q.reshape(bsz, L, KV_A, G_A, HD_A)
    k = k.reshape(bsz, L, KV_A, HD_A)
    v = v.reshape(bsz, L, KV_A, HD_A)
    if prompt:
        a_out = _swa_prompt(q, k, v, p['sinks'], rel_bias)
        new_wk, new_wv = k[:, L - win_buf:], v[:, L - win_buf:]
    else:
        a_out, new_wk, new_wv = _swa_decode(q, k, v, win_k, win_v, p['sinks'], rel_bias)

    u = jax.nn.gelu(u, approximate=False)
    gv = _layer_norm(jax.nn.gelu(gv, approximate=False), p['gmlp_ln_g'], p['gmlp_ln_b'])
    b_out = u * _gmlp_mix(gv.reshape(bsz, L, GB, CG_B), p['gmlp_ws'], p['gmlp_bs']).reshape(bsz, L, W_B)

    hist = jnp.zeros((bsz, CONV_W - 1, CONV_DIM), xbc.dtype) if prompt else conv_st
    xpad = jnp.concatenate([hist, xbc], axis=1)
    new_conv = xpad[:, xpad.shape[1] - (CONV_W - 1):]
    xbc_c = jax.nn.silu(_causal_dwconv(xpad, p['conv_w'], p['conv_b']))
    xs, bm, cm = jnp.split(xbc_c, [D_INNER, D_INNER + G_C * N_C], axis=-1)
    xs = xs.reshape(bsz, L, G_C, R_C, P_C)
    bm = bm.reshape(bsz, L, G_C, N_C)
    cm = cm.reshape(bsz, L, G_C, N_C)
    dt = jax.nn.softplus((dt_raw + p['dt_bias']).astype(F32)).reshape(bsz, L, G_C, R_C)
    a = -jnp.exp(p['a_log'].astype(F32)).reshape(G_C, R_C)
    if prompt:
        h0 = jnp.zeros((bsz, G_C, R_C, P_C, N_C), F32)
    else:
        h0 = ssm_st.astype(F32).reshape(bsz, G_C, R_C, P_C, N_C)
    y, h_last = _ssd_scan(xs, dt, a, bm, cm, h0)
    y = y + p['d_skip'].astype(F32).reshape(G_C, R_C, 1) * xs.astype(F32)
    y = (y.reshape(bsz, L, D_INNER) * jax.nn.silu(z.astype(F32))).reshape(bsz, L, G_C, D_INNER // G_C)
    y = y * lax.rsqrt(jnp.mean(y * y, axis=-1, keepdims=True) + RMS_EPS)
    c_out = y.reshape(bsz, L, D_INNER) * p['ssm_norm_g']
    new_ssm = h_last.reshape(bsz, H_C, P_C, N_C)
    return a_out, b_out, c_out, (new_wk, new_wv, new_conv, new_ssm, gv)


def _mem_attention_core(q, mk, mv):
    bsz, L, _ = q.shape
    q = q.reshape(bsz, L, XH, XHD)
    s = jnp.einsum('blhd,bmhd->bhlm', q, mk).astype(F32) * XHD ** -0.5
    w = jax.nn.softmax(s, axis=-1)
    return jnp.einsum('bhlm,bmhd->blhd', w, mv).reshape(bsz, L, XH * XHD)


def _layer_group(x, mk, mv, lw, p, rel_bias, win_buf, tm, caches=None):
    bsz, L, _ = x.shape
    T = bsz * L
    xf = x.reshape(T, D_MODEL)
    xb = xf.astype(BF16)
    proj = _matmul(xb, lw['w_in'], tm, 512)[:, :IN_DIM].reshape(bsz, L, IN_DIM)
    if caches is None:
        a_out, b_out, c_out, states = _mixer_cores(proj, p, rel_bias, win_buf)
    else:
        a_out, b_out, c_out, states = _mixer_cores(proj, p, rel_bias, win_buf, *caches)
    br = [t.reshape(T, BR_W).astype(BF16) for t in (a_out, b_out, c_out)]
    tmg = min(tm, 512)
    mixed = _gate_merge(xb, br[0], br[1], br[2], lw['w_gate'], lw['b_gate'], lw['w_branch'], tmg, 512)
    x1 = _mm_res_ln(mixed, lw['w_o'], xf, p['ln1_g'], p['ln1_b'], tmg)
    q = _matmul(x1.astype(BF16), lw['w_xq'], tmg, XH * XHD)
    o = _mem_attention_core(q.reshape(bsz, L, XH * XHD), mk, mv).reshape(T, XH * XHD)
    x2 = _mm_res_ln(o.astype(BF16), lw['w_xo'], x1, p['ln2_g'], p['ln2_b'], tmg)
    return x2, states


def kernel(x_prompt, x_sample, mem_prompt, cache_win_k, cache_win_v, state_conv, state_ssm, cache_mem_k, cache_mem_v, w_in, rel_bias, sinks, gmlp_ln_g, gmlp_ln_b, gmlp_ws, gmlp_bs, conv_w, conv_b, dt_bias, a_log, d_skip, ssm_norm_g, w_branch, w_gate, b_gate, w_o, ln1_g, ln1_b, w_xq, w_xk, w_xv, w_xo, ln2_g, ln2_b, w_router, b_router, w_e1, b_e1, w_e2, b_e2, ln3_g, ln3_b):
    win_buf = cache_win_k.shape[2]
    n_prompt, n_mem = mem_prompt.shape[0], mem_prompt.shape[1]
    bp, lp = x_prompt.shape[:2]
    bs_, ls = x_sample.shape[:2]
    tp, ts = bp * lp, bs_ * ls
    hp, hs = x_prompt, x_sample
    wk_p, wv_p, cv_p, ssm_p, mk_ps, mv_ps = [], [], [], [], [], []
    wk_s, wv_s, cv_s, ssm_s, gv_s = [], [], [], [], []
    mem_b = mem_prompt.reshape(n_prompt * n_mem, D_MODEL).astype(BF16)
    for l in range(DEPTH):
        p = dict(sinks=sinks[l], gmlp_ln_g=gmlp_ln_g[l], gmlp_ln_b=gmlp_ln_b[l],
                 gmlp_ws=gmlp_ws[l], gmlp_bs=gmlp_bs[l], conv_w=conv_w[l], conv_b=conv_b[l],
                 dt_bias=dt_bias[l], a_log=a_log[l], d_skip=d_skip[l], ssm_norm_g=ssm_norm_g[l],
                 ln1_g=ln1_g[l], ln1_b=ln1_b[l], ln2_g=ln2_g[l], ln2_b=ln2_b[l])
        lw = dict(
            w_in=jnp.pad(w_in[l].astype(BF16), ((0, 0), (0, IN_DIM_PAD - IN_DIM))),
            w_gate=w_gate[l].astype(BF16),
            b_gate=b_gate[l].reshape(1, -1),
            w_branch=w_branch[l].astype(BF16),
            w_o=w_o[l].astype(BF16),
            w_xq=w_xq[l].astype(BF16),
            w_xo=w_xo[l].astype(BF16),
            w_router=jnp.pad(w_router[l], ((0, 0), (0, 128 - N_EXPERTS))),
            b_router=jnp.pad(b_router[l], (0, 128 - N_EXPERTS)).reshape(1, -1),
            w_e1=w_e1[l],
            b_e1=b_e1[l].reshape(N_EXPERTS, 1, 2 * D_FF),
            w_e2=w_e2[l],
            b_e2=b_e2[l].reshape(N_EXPERTS, 1, D_MODEL),
        )
        w_kv = jnp.concatenate([w_xk[l], w_xv[l]], axis=1).astype(BF16)
        mkv = _matmul(mem_b, w_kv, n_mem, XH * XHD)
        mk = mkv[:, :XH * XHD].reshape(n_prompt, n_mem, XH, XHD)
        mv = mkv[:, XH * XHD:].reshape(n_prompt, n_mem, XH, XHD)
        x2_p, st_p = _layer_group(hp, mk, mv, lw, p, rel_bias, win_buf, 1024)
        x2_s, st_s = _layer_group(hs, cache_mem_k[l], cache_mem_v[l], lw, p, rel_bias, win_buf, ts,
                                  caches=(cache_win_k[l], cache_win_v[l], state_conv[l], state_ssm[l]))
        hp, hs = _moe_ln(x2_p, x2_s, lw, ln3_g[l], ln3_b[l])
        hp = hp.reshape(bp, lp, D_MODEL)
        hs = hs.reshape(bs_, ls, D_MODEL)
        wk_p.append(st_p[0]); wv_p.append(st_p[1]); cv_p.append(st_p[2]); ssm_p.append(st_p[3])
        mk_ps.append(mk); mv_ps.append(mv)
        wk_s.append(st_s[0]); wv_s.append(st_s[1]); cv_s.append(st_s[2]); ssm_s.append(st_s[3])
        gv_s.append(st_s[4])
    return (hp, hs,
            jnp.stack(wk_p), jnp.stack(wv_p), jnp.stack(cv_p), jnp.stack(ssm_p),
            jnp.stack(mk_ps), jnp.stack(mv_ps),
            jnp.stack(wk_s), jnp.stack(wv_s), jnp.stack(cv_s), jnp.stack(ssm_s), jnp.stack(gv_s))
```

```python
import functools
import math

import numpy as np
import jax
import jax.numpy as jnp
from jax import lax
from jax.experimental import pallas as pl
from jax.experimental.pallas import tpu as pltpu

D_MODEL = 2048
DEPTH = 2
PAST_LEN = 16384
WINDOW = 128
H_A = 16
KV_A = 2
HD_A = 64
G_A = H_A // KV_A
N_BUCKETS = 32
MAX_DIST = 128
CHUNK_B = 128
GB = 16
CG_B = 64
W_B = GB * CG_B
D_INNER = 1024
P_C = 64
H_C = D_INNER // P_C
G_C = 2
R_C = H_C // G_C
N_C = 128
CONV_W = 4
CONV_DIM = D_INNER + 2 * G_C * N_C
SSD_CHUNK = 128
N_BRANCH = 3
BR_W = 1024
N_MEM = 256
XH = 4
XHD = 128
N_EXPERTS = 32
TOP_K = 4
D_FF = D_MODEL
SWIGLU_ALPHA = 1.702
SWIGLU_LIMIT = 7.0
DN_ALPHA = (2 * DEPTH) ** 0.25
LN_EPS = 1e-5
RMS_EPS = 1e-5

IN_SIZES = [H_A * HD_A, KV_A * HD_A, KV_A * HD_A, W_B, W_B, D_INNER, CONV_DIM, H_C]
IN_DIM = sum(IN_SIZES)
IN_DIM_PAD = 6144
COL_Q, COL_U, COL_GV, COL_Z, COL_XBC = 0, 1024, 2048, 3072, 4096
COL_K = COL_XBC + CONV_DIM
COL_V = COL_K + KV_A * HD_A
COL_DT = COL_V + KV_A * HD_A
LANES = 128

VMEM_LIMIT = 56 * 1024 * 1024
MOE_BM = 512
MOE_FC = 256

F32 = jnp.float32
BF16 = jnp.bfloat16


def _cparams(*sem):
    return pltpu.CompilerParams(dimension_semantics=sem, vmem_limit_bytes=VMEM_LIMIT)


def _mm_kernel(x_ref, w_ref, o_ref):
    o_ref[...] = jnp.dot(x_ref[...], w_ref[...], preferred_element_type=F32).astype(o_ref.dtype)


def _matmul(x, w, tm, tn, out_dtype=F32):
    M, K = x.shape
    N = w.shape[1]
    assert M % tm == 0 and N % tn == 0
    return pl.pallas_call(
        _mm_kernel,
        grid=(N // tn, M // tm),
        in_specs=[pl.BlockSpec((tm, K), lambda j, i: (i, 0)),
                  pl.BlockSpec((K, tn), lambda j, i: (0, j))],
        out_specs=pl.BlockSpec((tm, tn), lambda j, i: (i, j)),
        out_shape=jax.ShapeDtypeStruct((M, N), out_dtype),
        compiler_params=_cparams("parallel", "parallel"),
        name="dense_matmul",
    )(x, w)


def _gate_merge_kernel(x_ref, a_ref, b_ref, c_ref, wg0_ref, wg1_ref, wg2_ref,
                       bg0_ref, bg1_ref, bg2_ref, wp_ref, o_ref):
    x = x_ref[...]
    acc = None
    for k, (br_ref, wg_ref, bg_ref) in enumerate(
            ((a_ref, wg0_ref, bg0_ref), (b_ref, wg1_ref, bg1_ref), (c_ref, wg2_ref, bg2_ref))):
        z = jnp.dot(x, wg_ref[...], preferred_element_type=F32) + bg_ref[...]
        gate = 1.0 / (1.0 + jnp.exp(-z))
        proj = jnp.dot(br_ref[...], wp_ref[k], preferred_element_type=F32)
        acc = gate * proj if acc is None else acc + gate * proj
    o_ref[...] = acc.astype(o_ref.dtype)


def _gate_merge(x, a, b, c, w_gate, b_gate, w_branch, tm, tn):
    M = x.shape[0]
    nt = D_MODEL // tn
    row = lambda j, i: (i, 0)
    in_specs = [pl.BlockSpec((tm, D_MODEL), row)] + [pl.BlockSpec((tm, BR_W), row)] * 3
    in_specs += [pl.BlockSpec((D_MODEL, tn), functools.partial(lambda j, i, k: (0, k * nt + j), k=k))
                 for k in range(N_BRANCH)]
    in_specs += [pl.BlockSpec((1, tn), functools.partial(lambda j, i, k: (0, k * nt + j), k=k))
                 for k in range(N_BRANCH)]
    in_specs += [pl.BlockSpec((N_BRANCH, BR_W, tn), lambda j, i: (0, 0, j))]
    return pl.pallas_call(
        _gate_merge_kernel,
        grid=(nt, M // tm),
        in_specs=in_specs,
        out_specs=pl.BlockSpec((tm, tn), lambda j, i: (i, j)),
        out_shape=jax.ShapeDtypeStruct((M, D_MODEL), BF16),
        compiler_params=_cparams("parallel", "parallel"),
        name="gate_merge",
    )(x, a, b, c, w_gate, w_gate, w_gate, b_gate, b_gate, b_gate, w_branch)


def _layer_norm_rows(y, g, b):
    mu = jnp.mean(y, axis=-1, keepdims=True)
    yc = y - mu
    var = jnp.mean(yc * yc, axis=-1, keepdims=True)
    return yc * lax.rsqrt(var + LN_EPS) * g + b


def _mm_res_ln_kernel(a_ref, w_ref, res_ref, g_ref, b_ref, o_ref):
    y = jnp.dot(a_ref[...], w_ref[...], preferred_element_type=F32) + DN_ALPHA * res_ref[...]
    o_ref[...] = _layer_norm_rows(y, g_ref[...], b_ref[...])


def _mm_res_ln(a, w, res, g, b, tm):
    M, K = a.shape
    row = lambda i: (i, 0)
    fixed = lambda i: (0, 0)
    return pl.pallas_call(
        _mm_res_ln_kernel,
        grid=(M // tm,),
        in_specs=[pl.BlockSpec((tm, K), row), pl.BlockSpec((K, D_MODEL), fixed),
                  pl.BlockSpec((tm, D_MODEL), row), pl.BlockSpec((1, D_MODEL), fixed),
                  pl.BlockSpec((1, D_MODEL), fixed)],
        out_specs=pl.BlockSpec((tm, D_MODEL), row),
        out_shape=jax.ShapeDtypeStruct((M, D_MODEL), F32),
        compiler_params=_cparams("parallel"),
        name="matmul_residual_layernorm",
    )(a, w, res, g.reshape(1, -1), b.reshape(1, -1))


def _router_kernel(x_ref, w_ref, b_ref, o_ref):
    acc = jnp.dot(x_ref[...].astype(BF16), w_ref[...].astype(BF16), preferred_element_type=F32)
    o_ref[...] = acc + b_ref[...]


def _router(x, w_pad, b_pad, tm):
    M = x.shape[0]
    NP = w_pad.shape[1]
    return pl.pallas_call(
        _router_kernel,
        grid=(M // tm,),
        in_specs=[pl.BlockSpec((tm, D_MODEL), lambda i: (i, 0)),
                  pl.BlockSpec((D_MODEL, NP), lambda i: (0, 0)),
                  pl.BlockSpec((1, NP), lambda i: (0, 0))],
        out_specs=pl.BlockSpec((tm, NP), lambda i: (i, 0)),
        out_shape=jax.ShapeDtypeStruct((M, NP), F32),
        compiler_params=_cparams("parallel"),
        name="router_logits",
    )(x, w_pad, b_pad)


def _bf16_row_interleave(a, b):
    a32 = lax.bitcast_convert_type(a.astype(BF16).astype(F32), jnp.uint32)
    b32 = lax.bitcast_convert_type(b.astype(BF16).astype(F32), jnp.uint32)
    word = (a32 >> 16) | (b32 & jnp.uint32(0xFFFF0000))
    return pltpu.bitcast(word, BF16)


def _moe_kernel(be_ref, bx_ref, bv_ref, x_ref, w1_ref, b1_ref, w2_ref, b2_ref, o_ref):
    i = pl.program_id(0)
    f = pl.program_id(1)
    fc = w2_ref.shape[1] // 2
    bm = x_ref.shape[0]

    @pl.when(bv_ref[i] == 1)
    def _():
        x = x_ref[...].astype(BF16)
        w1 = w1_ref[0].astype(BF16)
        h = jnp.dot(x, w1, preferred_element_type=F32) + b1_ref[0]
        lane = lax.broadcasted_iota(jnp.int32, (bm, 128), 1)
        even = (lane & 1) == 0
        acts = []
        for s in range(2 * fc // 128):
            a = h[:, 128 * s:128 * (s + 1)]
            b = h[:, 2 * fc + 128 * s:2 * fc + 128 * (s + 1)]
            glu = jnp.where(even, a, pltpu.roll(b, 1, axis=1))
            lin = jnp.where(even, pltpu.roll(a, 127, axis=1), b)
            glu = jnp.minimum(glu, SWIGLU_LIMIT)
            lin = jnp.clip(lin, -SWIGLU_LIMIT, SWIGLU_LIMIT)
            act = glu * (1.0 / (1.0 + jnp.exp(-SWIGLU_ALPHA * glu))) * (lin + 1.0)
            acts.append(act.astype(BF16))
        act = jnp.concatenate(acts, axis=1) if len(acts) > 1 else acts[0]
        w2 = w2_ref[0]
        w2q = _bf16_row_interleave(w2[:fc], w2[fc:])
        contrib = jnp.dot(act, w2q, preferred_element_type=F32)

        @pl.when(f == 0)
        def _():
            o_ref[...] = contrib + b2_ref[0]

        @pl.when(f != 0)
        def _():
            o_ref[...] += contrib


def _moe_ffn_blocks(rows, block_e, block_x, block_v, w1, b1, w2, b2, layer, bm, fc):
    d_model = rows.shape[1]
    d_ff = w2.shape[2]
    nb = rows.shape[0] // bm
    nf = d_ff // (2 * fc)
    last_f = nf - 1

    def fsel(f, bv, i):
        return jnp.where(bv[i] == 1, f, last_f)

    grid_spec = pltpu.PrefetchScalarGridSpec(
        num_scalar_prefetch=3,
        grid=(nb, nf),
        in_specs=[
            pl.BlockSpec((bm, d_model), lambda i, f, be, bx, bv: (bx[i], 0)),
            pl.BlockSpec((None, 1, d_model, 4 * fc), lambda i, f, be, bx, bv: (layer, be[i], 0, fsel(f, bv, i))),
            pl.BlockSpec((None, 1, 1, 4 * fc), lambda i, f, be, bx, bv: (layer, be[i], 0, fsel(f, bv, i))),
            pl.BlockSpec((None, 1, 2 * fc, d_model), lambda i, f, be, bx, bv: (layer, be[i], fsel(f, bv, i), 0)),
            pl.BlockSpec((None, 1, 1, d_model), lambda i, f, be, bx, bv: (layer, be[i], 0, 0)),
        ],
        out_specs=pl.BlockSpec((bm, d_model), lambda i, f, be, bx, bv: (bx[i], 0)),
    )
    return pl.pallas_call(
        _moe_kernel,
        grid_spec=grid_spec,
        out_shape=jax.ShapeDtypeStruct(rows.shape, F32),
        compiler_params=_cparams("arbitrary", "arbitrary"),
        name="moe_expert_ffn",
    )(block_e, block_x, block_v, rows, w1, b1, w2, b2)


def _combine_ln_kernel(g_ref, gate_ref, res_ref, lg_ref, lb_ref, o_ref):
    d = res_ref.shape[1]
    gate = gate_ref[...]
    y = DN_ALPHA * res_ref[...]
    ff = None
    for k in range(TOP_K):
        term = gate[:, k:k + 1] * g_ref[:, k * d:(k + 1) * d]
        ff = term if ff is None else ff + term
    o_ref[...] = _layer_norm_rows(y + ff, lg_ref[...], lb_ref[...])


def _combine_ln(gathered, gate, res, g, b, tm):
    T, d = res.shape
    row = lambda i: (i, 0)
    fixed = lambda i: (0, 0)
    return pl.pallas_call(
        _combine_ln_kernel,
        grid=(T // tm,),
        in_specs=[pl.BlockSpec((tm, TOP_K * d), row), pl.BlockSpec((tm, TOP_K), row),
                  pl.BlockSpec((tm, d), row), pl.BlockSpec((1, d), fixed), pl.BlockSpec((1, d), fixed)],
        out_specs=pl.BlockSpec((tm, d), row),
        out_shape=jax.ShapeDtypeStruct((T, d), F32),
        compiler_params=_cparams("parallel"),
        name="moe_combine_layernorm",
    )(gathered, gate, res, g.reshape(1, -1), b.reshape(1, -1))


def _moe_route(logits, bm):
    T, n_exp = logits.shape
    top_v, top_i = lax.top_k(logits, TOP_K)
    gate = jax.nn.softmax(top_v, axis=-1)
    n_assign = T * TOP_K
    nb = -(-(n_assign + n_exp * (bm - 1)) // bm)
    flat_e = top_i.reshape(-1)
    order = jnp.argsort(flat_e)
    e_sorted = flat_e[order]
    tok_sorted = (order // TOP_K).astype(jnp.int32)
    counts = jnp.bincount(flat_e, length=n_exp)
    padded = (counts + bm - 1) // bm * bm
    p_end = jnp.cumsum(padded)
    u_start = jnp.cumsum(counts) - counts
    dest = ((p_end - padded)[e_sorted] + jnp.arange(n_assign) - u_start[e_sorted]).astype(jnp.int32)
    src = jnp.zeros((nb * bm,), jnp.int32).at[dest].set(tok_sorted)
    pos = jnp.zeros((n_assign,), jnp.int32).at[order].set(dest)
    n_valid = (p_end[-1] // bm).astype(jnp.int32)
    blk = jnp.arange(nb, dtype=jnp.int32)
    block_v = (blk < n_valid).astype(jnp.int32)
    block_x = jnp.minimum(blk, n_valid - 1)
    block_e = jnp.minimum(jnp.searchsorted(p_end, block_x * bm, side='right'), n_exp - 1).astype(jnp.int32)
    return gate, src, pos, block_e, block_x, block_v


def _moe_ln(x2_p, x2_s, logits_p, logits_s, experts, layer, ln_g, ln_b):
    tp, ts = x2_p.shape[0], x2_s.shape[0]
    x2 = jnp.concatenate([x2_p, x2_s], axis=0)
    logits = jnp.concatenate([logits_p, logits_s], axis=0)[:, :N_EXPERTS]
    gate, src, pos, block_e, block_x, block_v = _moe_route(logits, MOE_BM)
    rows = jnp.take(x2, src, axis=0)
    w_e1, b_e1, w_e2, b_e2 = experts
    out_rows = _moe_ffn_blocks(rows, block_e, block_x, block_v, w_e1, b_e1, w_e2, b_e2, layer, MOE_BM, MOE_FC)
    g_p = jnp.take(out_rows, pos[:tp * TOP_K], axis=0).reshape(tp, TOP_K * D_MODEL)
    g_s = jnp.take(out_rows, pos[tp * TOP_K:], axis=0).reshape(ts, TOP_K * D_MODEL)
    hp = _combine_ln(g_p, gate[:tp], x2_p, ln_g, ln_b, 256)
    hs = _combine_ln(g_s, gate[tp:], x2_s, ln_g, ln_b, ts)
    return hp, hs


def _half_lane_variants(t):
    lo = lax.broadcasted_iota(jnp.int32, t.shape, 1) < HD_A
    zero = jnp.zeros_like(t)
    tr = pltpu.roll(t, HD_A, axis=1)
    return [[jnp.where(lo, t, zero).astype(BF16), jnp.where(lo, zero, tr).astype(BF16)],
            [jnp.where(lo, tr, zero).astype(BF16), jnp.where(lo, zero, t).astype(BF16)]]


def _swa_kernel(sinks_ref, q_ref, kp_ref, kc_ref, vp_ref, vc_ref, bias_ref, o_ref):
    j = pl.program_id(1)
    kvar = _half_lane_variants(jnp.concatenate([kp_ref[...], kc_ref[...]], axis=0))
    vvar = _half_lane_variants(jnp.concatenate([vp_ref[...], vc_ref[...]], axis=0))
    col = lax.broadcasted_iota(jnp.int32, (WINDOW, 2 * WINDOW), 1)
    no_prev = col < jnp.where(j == 0, WINDOW, 0)
    for r in range(H_A // 2):
        g = (2 * r) // G_A
        qp = q_ref[:, LANES * r:LANES * (r + 1)].astype(BF16)
        acc = None
        for par in range(2):
            h = 2 * r + par
            s = lax.dot_general(qp, kvar[g][par], (((1,), (1,)), ((), ())), preferred_element_type=F32)
            s = s * HD_A ** -0.5 + bias_ref[h]
            s = jnp.where(no_prev, -jnp.inf, s)
            sk = sinks_ref[h]
            m = jnp.maximum(jnp.max(s, axis=-1, keepdims=True), sk)
            pr = jnp.exp(s - m)
            den = jnp.sum(pr, axis=-1, keepdims=True) + jnp.exp(sk - m)
            o = jnp.dot((pr / den).astype(BF16), vvar[g][par], preferred_element_type=F32)
            acc = o if acc is None else acc + o
        o_ref[:, LANES * r:LANES * (r + 1)] = acc.astype(o_ref.dtype)


def _swa_prompt_bias(rel_bias):
    i = jnp.arange(WINDOW)[:, None]
    j = jnp.arange(2 * WINDOW)[None, :]
    dist = i + WINDOW - j
    b = jnp.moveaxis(rel_bias[_t5_bucket(dist)], -1, 0).astype(F32)
    return jnp.where((dist >= 0) & (dist < WINDOW), b, -jnp.inf)


def _swa_prompt_call(proj2d, bias, sinks, bsz, L):
    nb = L // WINDOW
    kcol, vcol = COL_K // LANES, COL_V // LANES
    cur = lambda b, j: b * nb + j
    prev = lambda b, j: b * nb + jnp.maximum(j - 1, 0)
    return pl.pallas_call(
        _swa_kernel,
        grid=(bsz, nb),
        in_specs=[
            pl.BlockSpec(memory_space=pltpu.SMEM),
            pl.BlockSpec((WINDOW, H_A * HD_A), lambda b, j: (cur(b, j), COL_Q // (H_A * HD_A))),
            pl.BlockSpec((WINDOW, LANES), lambda b, j: (prev(b, j), kcol)),
            pl.BlockSpec((WINDOW, LANES), lambda b, j: (cur(b, j), kcol)),
            pl.BlockSpec((WINDOW, LANES), lambda b, j: (prev(b, j), vcol)),
            pl.BlockSpec((WINDOW, LANES), lambda b, j: (cur(b, j), vcol)),
            pl.BlockSpec((H_A, WINDOW, 2 * WINDOW), lambda b, j: (0, 0, 0)),
        ],
        out_specs=pl.BlockSpec((WINDOW, H_A * HD_A), lambda b, j: (cur(b, j), 0)),
        out_shape=jax.ShapeDtypeStruct((bsz * L, H_A * HD_A), BF16),
        compiler_params=_cparams("parallel", "arbitrary"),
        name="swa_prompt",
    )(sinks, proj2d, proj2d, proj2d, proj2d, proj2d, bias)


def _gelu(x):
    return 0.5 * x * (1.0 + lax.erf(x * np.float32(np.sqrt(0.5))))


def _gmlp_kernel(u_ref, gv_ref, lng_ref, lnb_ref, w_ref, bias_ref, o_ref):
    u = _gelu(u_ref[...])
    gv = _layer_norm_rows(_gelu(gv_ref[...]), lng_ref[...], lnb_ref[...])
    lo = lax.broadcasted_iota(jnp.int32, (CHUNK_B, LANES), 1) < CG_B
    for r in range(GB // 2):
        vp = gv[:, LANES * r:LANES * (r + 1)]
        zero = jnp.zeros_like(vp)
        mix = jnp.dot(w_ref[2 * r], jnp.where(lo, vp, zero).astype(BF16), preferred_element_type=F32)
        mix += jnp.dot(w_ref[2 * r + 1], jnp.where(lo, zero, vp).astype(BF16), preferred_element_type=F32)
        sl = slice(LANES * r, LANES * (r + 1))
        o_ref[:, sl] = (u[:, sl] * (mix + bias_ref[:, sl])).astype(o_ref.dtype)


def _gmlp_prompt_call(proj2d, p, n_rows):
    w = (p['gmlp_ws'] * jnp.tril(jnp.ones((CHUNK_B, CHUNK_B), F32))).astype(BF16)
    bias = jnp.repeat(p['gmlp_bs'].T, CG_B, axis=1)
    fixed2 = lambda i: (0, 0)
    return pl.pallas_call(
        _gmlp_kernel,
        grid=(n_rows // CHUNK_B,),
        in_specs=[
            pl.BlockSpec((CHUNK_B, W_B), lambda i: (i, COL_U // W_B)),
            pl.BlockSpec((CHUNK_B, W_B), lambda i: (i, COL_GV // W_B)),
            pl.BlockSpec((1, W_B), fixed2), pl.BlockSpec((1, W_B), fixed2),
            pl.BlockSpec((GB, CHUNK_B, CHUNK_B), lambda i: (0, 0, 0)),
            pl.BlockSpec((CHUNK_B, W_B), fixed2),
        ],
        out_specs=pl.BlockSpec((CHUNK_B, W_B), lambda i: (i, 0)),
        out_shape=jax.ShapeDtypeStruct((n_rows, W_B), BF16),
        compiler_params=_cparams("parallel"),
        name="gmlp_prompt",
    )(proj2d, proj2d, p['gmlp_ln_g'].reshape(1, -1), p['gmlp_ln_b'].reshape(1, -1), w, bias)


def _silu(x):
    return x * (1.0 / (1.0 + jnp.exp(-x)))


def _softplus(x):
    return jnp.maximum(x, 0.0) + jnp.log1p(jnp.exp(-jnp.abs(x)))


def _causal_conv_chunk(cur, tail, w, bias):
    rows = lax.broadcasted_iota(jnp.int32, (8, cur.shape[1]), 0)
    y = jnp.broadcast_to(bias, cur.shape)
    y_head = jnp.broadcast_to(bias, (8, cur.shape[1]))
    for t in range(CONV_W):
        k = CONV_W - 1 - t
        wt = w[t:t + 1, :]
        if k == 0:
            y = y + cur * wt
            y_head = y_head + cur[:8] * wt
        else:
            sh = pltpu.roll(cur, k, axis=0)
            y = y + sh * wt
            y_head = y_head + jnp.where(rows < k, pltpu.roll(tail, k, axis=0), sh[:8]) * wt
    return jnp.concatenate([y_head, y[8:]], axis=0)


def _bf16_split3(x):
    p1 = x.astype(BF16)
    r1 = x - p1.astype(F32)
    p2 = r1.astype(BF16)
    p3 = (r1 - p2.astype(F32)).astype(BF16)
    return p1, p2, p3


def _ssd_kernel(xs_ref, bc_ref, z_ref, dt_ref, cwx_ref, cbx_ref, cwb_ref, cbb_ref, dtb_ref, a_ref,
                dsk_ref, ng_ref, y_ref, h_ref, state_ref, tailx_ref, tailb_ref):
    c = pl.program_id(1)
    C = SSD_CHUNK

    @pl.when(c == 0)
    def _():
        state_ref[...] = jnp.zeros_like(state_ref)
        tailx_ref[...] = jnp.zeros_like(tailx_ref)
        tailb_ref[...] = jnp.zeros_like(tailb_ref)

    xs_raw = xs_ref[...]
    bc_raw = bc_ref[...]
    xs = _silu(_causal_conv_chunk(xs_raw, tailx_ref[...], cwx_ref[...], cbx_ref[...]))
    bc = _silu(_causal_conv_chunk(bc_raw, tailb_ref[...], cwb_ref[...], cbb_ref[...]))
    tailx_ref[...] = xs_raw[C - 8:]
    tailb_ref[...] = bc_raw[C - 8:]

    dt = _softplus(dt_ref[...] + dtb_ref[...])
    da = dt * a_ref[...]
    row_i = lax.broadcasted_iota(jnp.int32, (C, C), 0)
    col_i = lax.broadcasted_iota(jnp.int32, (C, C), 1)
    causal = row_i >= col_i
    tril = jnp.where(causal, 1.0, 0.0).astype(BF16)
    acs = None
    for piece in _bf16_split3(da):
        t = jnp.dot(tril, piece, preferred_element_type=F32)
        acs = t if acs is None else acs + t
    acs_t = acs.T
    exp_acs = jnp.exp(acs)
    end_decay = jnp.exp(acs[C - 1:C, :] - acs)
    chunk_decay = jnp.exp(acs[C - 1:C, :])

    lo = lax.broadcasted_iota(jnp.int32, (C, LANES), 1) < P_C
    bm = [bc[:, N_C * g:N_C * (g + 1)].astype(BF16) for g in range(G_C)]
    cm = [bc[:, N_C * (G_C + g):N_C * (G_C + g + 1)].astype(BF16) for g in range(G_C)]
    cb = [lax.dot_general(cm[g], bm[g], (((1,), (1,)), ((), ())), preferred_element_type=F32)
          for g in range(G_C)]

    def per_lane_half(t, r):
        return jnp.where(lo, t[:, 2 * r:2 * r + 1], t[:, 2 * r + 1:2 * r + 2])

    ys = []
    for r in range(H_C // 2):
        g = (2 * r) // R_C
        sl = slice(LANES * r, LANES * (r + 1))
        x_pair = xs[:, sl]
        xdt = x_pair * per_lane_half(dt, r)
        zero = jnp.zeros_like(xdt)
        y_pair = None
        for par in range(2):
            h = 2 * r + par
            seg = acs[:, h:h + 1] - acs_t[h:h + 1, :]
            decay = jnp.where(causal, jnp.exp(seg), 0.0)
            m_h = (cb[g] * decay).astype(BF16)
            x_h = (jnp.where(lo, xdt, zero) if par == 0 else jnp.where(lo, zero, xdt)).astype(BF16)
            t = jnp.dot(m_h, x_h, preferred_element_type=F32)
            y_pair = t if y_pair is None else y_pair + t
        st = state_ref[sl, :]
        y_off = lax.dot_general(cm[g], st.astype(BF16), (((1,), (1,)), ((), ())), preferred_element_type=F32)
        y_pair = y_pair + y_off * per_lane_half(exp_acs, r)
        upd = lax.dot_general((xdt * per_lane_half(end_decay, r)).astype(BF16), bm[g],
                              (((0,), (0,)), ((), ())), preferred_element_type=F32)
        cd = jnp.concatenate([jnp.broadcast_to(chunk_decay[:, 2 * r:2 * r + 1], (P_C, N_C)),
                              jnp.broadcast_to(chunk_decay[:, 2 * r + 1:2 * r + 2], (P_C, N_C))], axis=0)
        state_ref[sl, :] = st * cd + upd
        ys.append(y_pair + dsk_ref[:, sl] * x_pair)
    y = jnp.concatenate(ys, axis=1) * _silu(z_ref[...])
    gw = D_INNER // G_C
    outs = []
    for g in range(G_C):
        yg = y[:, gw * g:gw * (g + 1)]
        outs.append(yg * lax.rsqrt(jnp.mean(yg * yg, axis=-1, keepdims=True) + RMS_EPS))
    y_ref[...] = (jnp.concatenate(outs, axis=1) * ng_ref[...]).astype(y_ref.dtype)

    @pl.when(c == pl.num_programs(1) - 1)
    def _():
        h_ref[...] = state_ref[...]


def _ssd_prompt_call(proj2d, p, bsz, L):
    nc = L // SSD_CHUNK
    row = lambda blk: (lambda b, c: (b * nc + c, blk))
    fixed = lambda b, c: (0, 0)
    pad_l = lambda v: jnp.pad(v.astype(F32), (0, LANES - H_C)).reshape(1, LANES)
    cw, cbias = p['conv_w'], p['conv_b'].reshape(1, -1)
    nbc = 2 * G_C * N_C
    args = (proj2d, proj2d, proj2d, proj2d,
            cw[:, :D_INNER], cbias[:, :D_INNER], cw[:, D_INNER:], cbias[:, D_INNER:],
            pad_l(p['dt_bias']), pad_l(-jnp.exp(p['a_log'].astype(F32))),
            jnp.repeat(p['d_skip'].astype(F32), P_C).reshape(1, -1), p['ssm_norm_g'].reshape(1, -1))
    return pl.pallas_call(
        _ssd_kernel,
        grid=(bsz, nc),
        in_specs=[
            pl.BlockSpec((SSD_CHUNK, D_INNER), row(COL_XBC // D_INNER)),
            pl.BlockSpec((SSD_CHUNK, nbc), row((COL_XBC + D_INNER) // nbc)),
            pl.BlockSpec((SSD_CHUNK, D_INNER), row(COL_Z // D_INNER)),
            pl.BlockSpec((SSD_CHUNK, LANES), row(COL_DT // LANES)),
            pl.BlockSpec((CONV_W, D_INNER), fixed), pl.BlockSpec((1, D_INNER), fixed),
            pl.BlockSpec((CONV_W, nbc), fixed), pl.BlockSpec((1, nbc), fixed),
            pl.BlockSpec((1, LANES), fixed), pl.BlockSpec((1, LANES), fixed),
            pl.BlockSpec((1, D_INNER), fixed), pl.BlockSpec((1, D_INNER), fixed),
        ],
        out_specs=[pl.BlockSpec((SSD_CHUNK, D_INNER), lambda b, c: (b * nc + c, 0)),
                   pl.BlockSpec((None, H_C * P_C, N_C), lambda b, c: (b, 0, 0))],
        out_shape=[jax.ShapeDtypeStruct((bsz * L, D_INNER), BF16),
                   jax.ShapeDtypeStruct((bsz, H_C * P_C, N_C), F32)],
        scratch_shapes=[pltpu.VMEM((H_C * P_C, N_C), F32), pltpu.VMEM((8, D_INNER), F32),
                        pltpu.VMEM((8, nbc), F32)],
        compiler_params=_cparams("parallel", "arbitrary"),
        name="ssd_prompt",
    )(*args)


def _bf16_round(x):
    return x.astype(BF16).astype(F32)


def _swa_decode_kernel(sinks_ref, q_ref, kn_ref, vn_ref, ck_ref, cv_ref, bias_ref, o_ref, wk_ref, wv_ref):
    W = WINDOW
    last = lax.broadcasted_iota(jnp.int32, (W, LANES), 0) == W - 1
    kw = jnp.where(last, jnp.broadcast_to(kn_ref[...], (W, LANES)), pltpu.roll(ck_ref[...], W - 1, axis=0))
    vw = jnp.where(last, jnp.broadcast_to(vn_ref[...], (W, LANES)), pltpu.roll(cv_ref[...], W - 1, axis=0))
    wk_ref[...] = kw
    wv_ref[...] = vw
    kvar = _half_lane_variants(kw)
    vvar = _half_lane_variants(vw)
    outs = []
    for r in range(H_A // 2):
        g = (2 * r) // G_A
        qp = jnp.broadcast_to(q_ref[:, LANES * r:LANES * (r + 1)], (8, LANES)).astype(BF16)
        acc = None
        for par in range(2):
            h = 2 * r + par
            s = lax.dot_general(qp, kvar[g][par], (((1,), (1,)), ((), ())), preferred_element_type=F32)
            s = s * HD_A ** -0.5 + bias_ref[h:h + 1, :]
            sk = sinks_ref[h]
            m = jnp.maximum(jnp.max(s, axis=-1, keepdims=True), sk)
            pr = jnp.exp(s - m)
            den = jnp.sum(pr, axis=-1, keepdims=True) + jnp.exp(sk - m)
            o = jnp.dot((pr / den).astype(BF16), vvar[g][par], preferred_element_type=F32)
            acc = o if acc is None else acc + o
        outs.append(acc[0:1])
    o_ref[...] = jnp.concatenate(outs, axis=1).astype(o_ref.dtype)


def _swa_decode_call(proj3, cache_k, cache_v, layer, rel_bias, sinks):
    bsz = proj3.shape[0]
    ck = cache_k.reshape(cache_k.shape[0], bsz, WINDOW, LANES)
    cv = cache_v.reshape(cache_v.shape[0], bsz, WINDOW, LANES)
    dist = WINDOW - 1 - jnp.arange(WINDOW)
    bias = rel_bias[_t5_bucket(dist)].T.astype(F32)
    tok = lambda blk: (lambda i: (i, 0, blk))
    cache = lambda i: (layer, i, 0, 0)
    return pl.pallas_call(
        _swa_decode_kernel,
        grid=(bsz,),
        in_specs=[
            pl.BlockSpec(memory_space=pltpu.SMEM),
            pl.BlockSpec((None, 1, H_A * HD_A), tok(COL_Q // (H_A * HD_A))),
            pl.BlockSpec((None, 1, LANES), tok(COL_K // LANES)),
            pl.BlockSpec((None, 1, LANES), tok(COL_V // LANES)),
            pl.BlockSpec((None, None, WINDOW, LANES), cache),
            pl.BlockSpec((None, None, WINDOW, LANES), cache),
            pl.BlockSpec((H_A, WINDOW), lambda i: (0, 0)),
        ],
        out_specs=[pl.BlockSpec((None, 1, H_A * HD_A), lambda i: (i, 0, 0)),
                   pl.BlockSpec((None, WINDOW, LANES), lambda i: (i, 0, 0)),
                   pl.BlockSpec((None, WINDOW, LANES), lambda i: (i, 0, 0))],
        out_shape=[jax.ShapeDtypeStruct((bsz, 1, H_A * HD_A), BF16),
                   jax.ShapeDtypeStruct((bsz, WINDOW, LANES), F32),
                   jax.ShapeDtypeStruct((bsz, WINDOW, LANES), F32)],
        compiler_params=_cparams("parallel"),
        name="swa_sample",
    )(sinks, proj3, proj3, proj3, ck, cv, bias)


def _gmlp_step_kernel(u_ref, gv_ref, lng_ref, lnb_ref, w0_ref, b0_ref, o_ref, gv_out_ref):
    gv = _layer_norm_rows(_gelu(gv_ref[...]), lng_ref[...], lnb_ref[...])
    gv_out_ref[...] = gv
    mix = _bf16_round(w0_ref[...]) * _bf16_round(gv) + b0_ref[...]
    o_ref[...] = (_gelu(u_ref[...]) * mix).astype(o_ref.dtype)


def _gmlp_step_call(proj2d, p):
    n = proj2d.shape[0]
    w0 = jnp.repeat(p['gmlp_ws'][:, 0, 0], CG_B).reshape(1, -1)
    b0 = jnp.repeat(p['gmlp_bs'][:, 0], CG_B).reshape(1, -1)
    fixed = lambda i: (0, 0)
    return pl.pallas_call(
        _gmlp_step_kernel,
        grid=(1,),
        in_specs=[pl.BlockSpec((n, W_B), lambda i: (0, COL_U // W_B)),
                  pl.BlockSpec((n, W_B), lambda i: (0, COL_GV // W_B)),
                  pl.BlockSpec((1, W_B), fixed), pl.BlockSpec((1, W_B), fixed),
                  pl.BlockSpec((1, W_B), fixed), pl.BlockSpec((1, W_B), fixed)],
        out_specs=[pl.BlockSpec((n, W_B), fixed), pl.BlockSpec((n, W_B), fixed)],
        out_shape=[jax.ShapeDtypeStruct((n, W_B), BF16), jax.ShapeDtypeStruct((n, W_B), F32)],
        compiler_params=_cparams("arbitrary"),
        name="gmlp_sample",
    )(proj2d, proj2d, p['gmlp_ln_g'].reshape(1, -1), p['gmlp_ln_b'].reshape(1, -1), w0, b0)


def _conv_step(st, cur, w, bias):
    y = bias
    for t in range(CONV_W - 1):
        y = y + st[t:t + 1, :] * w[t:t + 1, :]
    return y + cur * w[CONV_W - 1:CONV_W, :]


def _ssd_step_kernel(xs_ref, bc_ref, z_ref, dt_ref, stx_ref, stb_ref, h0_ref, cwx_ref, cbx_ref, cwb_ref, cbb_ref,
                     dtb_ref, a_ref, dsk_ref, ng_ref, y_ref, h_ref, ncx_ref, ncb_ref):
    xs_raw, bc_raw = xs_ref[...], bc_ref[...]
    stx, stb = stx_ref[...], stb_ref[...]
    ncx_ref[0:CONV_W - 2, :] = stx[1:CONV_W - 1]
    ncx_ref[CONV_W - 2:CONV_W - 1, :] = xs_raw
    ncb_ref[0:CONV_W - 2, :] = stb[1:CONV_W - 1]
    ncb_ref[CONV_W - 2:CONV_W - 1, :] = bc_raw
    xs = _silu(_conv_step(stx, xs_raw, cwx_ref[...], cbx_ref[...]))
    bc = _silu(_conv_step(stb, bc_raw, cwb_ref[...], cbb_ref[...]))
    dt = _softplus(dt_ref[...] + dtb_ref[...])
    decay = jnp.exp(dt * a_ref[...])
    xdt = _bf16_round(xs * dt)
    gw = D_INNER // G_C
    first_group = lax.broadcasted_iota(jnp.int32, (1, D_INNER), 1) < gw
    bm = [_bf16_round(bc[:, N_C * g:N_C * (g + 1)]) for g in range(G_C)]
    cm = [_bf16_round(bc[:, N_C * (G_C + g):N_C * (G_C + g + 1)]) for g in range(G_C)]
    cb = [_bf16_round(jnp.sum(cm[g] * bm[g], axis=1, keepdims=True)) for g in range(G_C)]
    y_diag = jnp.where(first_group, cb[0], cb[1]) * xdt
    h0 = h0_ref[...]
    y_off = jnp.concatenate([
        lax.dot_general(jnp.broadcast_to(cm[g], (8, N_C)).astype(BF16), h0[gw * g:gw * (g + 1)].astype(BF16),
                        (((1,), (1,)), ((), ())), preferred_element_type=F32)[0:1] for g in range(G_C)], axis=1)
    y = y_diag + y_off * decay
    y = (y + dsk_ref[...] * xs) * _silu(z_ref[...])
    outs = []
    for g in range(G_C):
        yg = y[:, gw * g:gw * (g + 1)]
        outs.append(yg * lax.rsqrt(jnp.mean(yg * yg, axis=-1, keepdims=True) + RMS_EPS))
    y_ref[...] = (jnp.concatenate(outs, axis=1) * ng_ref[...]).astype(y_ref.dtype)
    decay_rows = jnp.broadcast_to(decay, (LANES, D_INNER)).T
    xdt_rows = jnp.broadcast_to(xdt, (LANES, D_INNER)).T
    rows = lax.broadcasted_iota(jnp.int32, (D_INNER, N_C), 0)
    bm_rows = jnp.where(rows < gw, jnp.broadcast_to(bm[0], (D_INNER, N_C)), jnp.broadcast_to(bm[1], (D_INNER, N_C)))
    h_ref[...] = h0 * decay_rows + xdt_rows * bm_rows


def _ssd_step_call(proj3, state_conv, state_ssm, layer, p):
    bsz = proj3.shape[0]
    nbc = 2 * G_C * N_C
    ssm = state_ssm.reshape(state_ssm.shape[0], bsz, H_C * P_C, N_C)
    dt_lanes = jnp.repeat(proj3[:, :, COL_DT:COL_DT + H_C], P_C, axis=-1)
    per_lane = lambda v: jnp.repeat(v.astype(F32), P_C).reshape(1, -1)
    cw, cbias = p['conv_w'], p['conv_b'].reshape(1, -1)
    tok = lambda blk: (lambda i: (i, 0, blk))
    fixed = lambda i: (0, 0)
    return pl.pallas_call(
        _ssd_step_kernel,
        grid=(bsz,),
        in_specs=[
            pl.BlockSpec((None, 1, D_INNER), tok(COL_XBC // D_INNER)),
            pl.BlockSpec((None, 1, nbc), tok((COL_XBC + D_INNER) // nbc)),
            pl.BlockSpec((None, 1, D_INNER), tok(COL_Z // D_INNER)),
            pl.BlockSpec((None, 1, D_INNER), tok(0)),
            pl.BlockSpec((None, None, CONV_W - 1, D_INNER), lambda i: (layer, i, 0, 0)),
            pl.BlockSpec((None, None, CONV_W - 1, nbc), lambda i: (layer, i, 0, D_INNER // nbc)),
            pl.BlockSpec((None, None, H_C * P_C, N_C), lambda i: (layer, i, 0, 0)),
            pl.BlockSpec((CONV_W, D_INNER), fixed), pl.BlockSpec((1, D_INNER), fixed),
            pl.BlockSpec((CONV_W, nbc), fixed), pl.BlockSpec((1, nbc), fixed),
            pl.BlockSpec((1, D_INNER), fixed), pl.BlockSpec((1, D_INNER), fixed),
            pl.BlockSpec((1, D_INNER), fixed), pl.BlockSpec((1, D_INNER), fixed),
        ],
        out_specs=[pl.BlockSpec((None, 1, D_INNER), lambda i: (i, 0, 0)),
                   pl.BlockSpec((None, H_C * P_C, N_C), lambda i: (i, 0, 0)),
                   pl.BlockSpec((None, CONV_W - 1, D_INNER), lambda i: (i, 0, 0)),
                   pl.BlockSpec((None, CONV_W - 1, nbc), lambda i: (i, 0, 0))],
        out_shape=[jax.ShapeDtypeStruct((bsz, 1, D_INNER), BF16),
                   jax.ShapeDtypeStruct((bsz, H_C * P_C, N_C), F32),
                   jax.ShapeDtypeStruct((bsz, CONV_W - 1, D_INNER), F32),
                   jax.ShapeDtypeStruct((bsz, CONV_W - 1, nbc), F32)],
        compiler_params=_cparams("parallel"),
        name="ssd_sample",
    )(proj3, proj3, proj3, dt_lanes, state_conv, state_conv, ssm,
      cw[:, :D_INNER], cbias[:, :D_INNER], cw[:, D_INNER:], cbias[:, D_INNER:],
      per_lane(p['dt_bias']), per_lane(-jnp.exp(p['a_log'].astype(F32))), per_lane(p['d_skip']),
      p['ssm_norm_g'].reshape(1, -1))


def _xattn_kernel(x1_ref, wq_ref, mk_ref, mv_ref, wo_ref, g_ref, b_ref, wr_ref, br_ref, x2_ref, lg_ref):
    x1 = x1_ref[...]
    q = jnp.dot(x1.astype(BF16), wq_ref[...], preferred_element_type=F32)
    outs = []
    for h in range(XH):
        sl = slice(XHD * h, XHD * (h + 1))
        s = lax.dot_general(q[:, sl].astype(BF16), mk_ref[:, sl].astype(BF16),
                            (((1,), (1,)), ((), ())), preferred_element_type=F32) * XHD ** -0.5
        e = jnp.exp(s - jnp.max(s, axis=-1, keepdims=True))
        w = e / jnp.sum(e, axis=-1, keepdims=True)
        outs.append(jnp.dot(w.astype(BF16), mv_ref[:, sl].astype(BF16), preferred_element_type=F32).astype(BF16))
    o = jnp.concatenate(outs, axis=1)
    y = jnp.dot(o, wo_ref[...], preferred_element_type=F32) + DN_ALPHA * x1
    x2 = _layer_norm_rows(y, g_ref[...], b_ref[...])
    x2_ref[...] = x2
    lg_ref[...] = jnp.dot(x2.astype(BF16), wr_ref[...], preferred_element_type=F32) + br_ref[...]


def _xattn_prompt_call(x1, mkv, lw, g, b, bsz, L, tm):
    nt = L // tm
    xw = XH * XHD
    row = lambda bb, i: (bb * nt + i, 0)
    fixed = lambda bb, i: (0, 0)
    n_lg = lw['w_router'].shape[1]
    return pl.pallas_call(
        _xattn_kernel,
        grid=(bsz, nt),
        in_specs=[
            pl.BlockSpec((tm, D_MODEL), row),
            pl.BlockSpec((D_MODEL, xw), fixed),
            pl.BlockSpec((N_MEM, xw), lambda bb, i: (bb, 0)),
            pl.BlockSpec((N_MEM, xw), lambda bb, i: (bb, 1)),
            pl.BlockSpec((xw, D_MODEL), fixed),
            pl.BlockSpec((1, D_MODEL), fixed), pl.BlockSpec((1, D_MODEL), fixed),
            pl.BlockSpec((D_MODEL, n_lg), fixed), pl.BlockSpec((1, n_lg), fixed),
        ],
        out_specs=[pl.BlockSpec((tm, D_MODEL), row), pl.BlockSpec((tm, n_lg), row)],
        out_shape=[jax.ShapeDtypeStruct((bsz * L, D_MODEL), F32), jax.ShapeDtypeStruct((bsz * L, n_lg), F32)],
        compiler_params=_cparams("parallel", "parallel"),
        name="memory_attention_prompt",
    )(x1, lw['w_xq'], mkv, mkv, lw['w_xo'], g.reshape(1, -1), b.reshape(1, -1),
      lw['w_router_bf16'], lw['b_router'])


def _xattn_decode_kernel(q_ref, k_ref, v_ref, o_ref):
    q = q_ref[...].astype(BF16).astype(F32)
    k = k_ref[...].astype(BF16).astype(F32)
    v = v_ref[...].astype(BF16).astype(F32)
    prod = k * q
    outs = []
    for h in range(XH):
        sl = slice(XHD * h, XHD * (h + 1))
        s = jnp.sum(prod[:, sl], axis=1, keepdims=True) * XHD ** -0.5
        e = jnp.exp(s - jnp.max(s, axis=0, keepdims=True))
        w = (e / jnp.sum(e, axis=0, keepdims=True)).astype(BF16).astype(F32)
        outs.append(jnp.sum(w * v[:, sl], axis=0, keepdims=True))
    o_ref[...] = jnp.concatenate(outs, axis=1)


def _xattn_decode_call(q, cache_k, cache_v, layer):
    bsz, xw = q.shape
    ck = cache_k.reshape(cache_k.shape[0], bsz, N_MEM, xw)
    cv = cache_v.reshape(cache_v.shape[0], bsz, N_MEM, xw)
    out = pl.pallas_call(
        _xattn_decode_kernel,
        grid=(bsz,),
        in_specs=[pl.BlockSpec((None, 1, xw), lambda i: (i, 0, 0)),
                  pl.BlockSpec((None, None, N_MEM, xw), lambda i: (layer, i, 0, 0)),
                  pl.BlockSpec((None, None, N_MEM, xw), lambda i: (layer, i, 0, 0))],
        out_specs=pl.BlockSpec((None, 1, xw), lambda i: (i, 0, 0)),
        out_shape=jax.ShapeDtypeStruct((bsz, 1, xw), F32),
        compiler_params=_cparams("parallel"),
        name="memory_attention_sample",
    )(q.reshape(bsz, 1, xw), ck, cv)
    return out.reshape(bsz, xw)


def _t5_bucket(dist):
    n = jnp.maximum(dist, 0)
    exact = N_BUCKETS // 2
    nf = jnp.maximum(n, 1).astype(F32)
    large = exact + (jnp.log(nf / exact) / math.log(MAX_DIST / exact) * (N_BUCKETS - exact)).astype(jnp.int32)
    return jnp.where(n < exact, n, jnp.minimum(large, N_BUCKETS - 1))


def _prompt_layer(x, mkv, lw, p, swa_bias, bsz, L):
    T = bsz * L
    xb = x.astype(BF16)
    proj = _matmul(xb, lw['w_in'], 1024, 512)
    a_out = _swa_prompt_call(proj, swa_bias, p['sinks'], bsz, L)
    b_out = _gmlp_prompt_call(proj, p, T)
    c_out, h_last = _ssd_prompt_call(proj, p, bsz, L)
    mixed = _gate_merge(xb, a_out, b_out, c_out, lw['w_gate'], lw['b_gate'], lw['w_branch'], 512, 512)
    x1 = _mm_res_ln(mixed, lw['w_o'], x, p['ln1_g'], p['ln1_b'], 512)
    x2, logits = _xattn_prompt_call(x1, mkv, lw, p['ln2_g'], p['ln2_b'], bsz, L, 512)
    proj3 = proj.reshape(bsz, L, IN_DIM_PAD)
    win_k = proj3[:, L - WINDOW:, COL_K:COL_K + KV_A * HD_A].reshape(bsz, WINDOW, KV_A, HD_A)
    win_v = proj3[:, L - WINDOW:, COL_V:COL_V + KV_A * HD_A].reshape(bsz, WINDOW, KV_A, HD_A)
    conv = proj3[:, L - (CONV_W - 1):, COL_XBC:COL_XBC + CONV_DIM]
    return x2, logits, (win_k, win_v, conv, h_last.reshape(bsz, H_C, P_C, N_C))


def _sample_layer(x, layer, lw, p, rel_bias, cache_win_k, cache_win_v, state_conv, state_ssm, cache_mem_k, cache_mem_v):
    bsz = x.shape[0]
    xb = x.astype(BF16)
    proj = _matmul(xb, lw['w_in'], bsz, 512)
    proj3 = proj.reshape(bsz, 1, IN_DIM_PAD)
    a_out, win_k, win_v = _swa_decode_call(proj3, cache_win_k, cache_win_v, layer, rel_bias, p['sinks'])
    b_out, gv = _gmlp_step_call(proj, p)
    c_out, ssm, conv_x, conv_bc = _ssd_step_call(proj3, state_conv, state_ssm, layer, p)
    mixed = _gate_merge(xb, a_out.reshape(bsz, BR_W), b_out, c_out.reshape(bsz, BR_W),
                        lw['w_gate'], lw['b_gate'], lw['w_branch'], bsz, 512)
    x1 = _mm_res_ln(mixed, lw['w_o'], x, p['ln1_g'], p['ln1_b'], bsz)
    q = _matmul(x1.astype(BF16), lw['w_xq'], bsz, XH * XHD)
    o = _xattn_decode_call(q, cache_mem_k, cache_mem_v, layer)
    x2 = _mm_res_ln(o.astype(BF16), lw['w_xo'], x1, p['ln2_g'], p['ln2_b'], bsz)
    logits = _router(x2, lw['w_router'], lw['b_router'], bsz)
    states = (win_k.reshape(bsz, WINDOW, KV_A, HD_A), win_v.reshape(bsz, WINDOW, KV_A, HD_A),
              jnp.concatenate([conv_x, conv_bc], axis=-1), ssm.reshape(bsz, H_C, P_C, N_C),
              gv.reshape(bsz, 1, W_B))
    return x2, logits, states


def kernel(x_prompt, x_sample, mem_prompt, cache_win_k, cache_win_v, state_conv, state_ssm, cache_mem_k, cache_mem_v, w_in, rel_bias, sinks, gmlp_ln_g, gmlp_ln_b, gmlp_ws, gmlp_bs, conv_w, conv_b, dt_bias, a_log, d_skip, ssm_norm_g, w_branch, w_gate, b_gate, w_o, ln1_g, ln1_b, w_xq, w_xk, w_xv, w_xo, ln2_g, ln2_b, w_router, b_router, w_e1, b_e1, w_e2, b_e2, ln3_g, ln3_b):
    assert cache_win_k.shape[2] == WINDOW and x_sample.shape[1] == 1
    n_prompt, n_mem = mem_prompt.shape[0], mem_prompt.shape[1]
    bp, lp = x_prompt.shape[:2]
    bs_ = x_sample.shape[0]
    hp, hs = x_prompt.reshape(bp * lp, D_MODEL), x_sample.reshape(bs_, D_MODEL)
    wk_p, wv_p, cv_p, ssm_p, mk_ps, mv_ps = [], [], [], [], [], []
    wk_s, wv_s, cv_s, ssm_s, gv_s = [], [], [], [], []
    mem_b = mem_prompt.reshape(n_prompt * n_mem, D_MODEL).astype(BF16)
    swa_bias = _swa_prompt_bias(rel_bias)
    experts = (w_e1, b_e1.reshape(DEPTH, N_EXPERTS, 1, 2 * D_FF), w_e2, b_e2.reshape(DEPTH, N_EXPERTS, 1, D_MODEL))
    for l in range(DEPTH):
        p = dict(sinks=sinks[l], gmlp_ln_g=gmlp_ln_g[l], gmlp_ln_b=gmlp_ln_b[l],
                 gmlp_ws=gmlp_ws[l], gmlp_bs=gmlp_bs[l], conv_w=conv_w[l], conv_b=conv_b[l],
                 dt_bias=dt_bias[l], a_log=a_log[l], d_skip=d_skip[l], ssm_norm_g=ssm_norm_g[l],
                 ln1_g=ln1_g[l], ln1_b=ln1_b[l], ln2_g=ln2_g[l], ln2_b=ln2_b[l])
        wi = w_in[l]
        seg = np.cumsum([0] + IN_SIZES)
        part = lambda n: wi[:, seg[n]:seg[n + 1]]
        w_in_cols = jnp.concatenate(
            [part(0), part(3), part(4), part(5), part(6), part(1), part(2), part(7),
             jnp.zeros((D_MODEL, IN_DIM_PAD - IN_DIM), F32)], axis=1).astype(BF16)
        w_router_pad = jnp.pad(w_router[l], ((0, 0), (0, LANES - N_EXPERTS)))
        lw = dict(
            w_in=w_in_cols,
            w_gate=w_gate[l].astype(BF16),
            b_gate=b_gate[l].reshape(1, -1),
            w_branch=w_branch[l].astype(BF16),
            w_o=w_o[l].astype(BF16),
            w_xq=w_xq[l].astype(BF16),
            w_xo=w_xo[l].astype(BF16),
            w_router=w_router_pad,
            w_router_bf16=w_router_pad.astype(BF16),
            b_router=jnp.pad(b_router[l], (0, LANES - N_EXPERTS)).reshape(1, -1),
        )
        w_kv = jnp.concatenate([w_xk[l], w_xv[l]], axis=1).astype(BF16)
        mkv = _matmul(mem_b, w_kv, n_mem, XH * XHD)
        mk = mkv[:, :XH * XHD].reshape(n_prompt, n_mem, XH, XHD)
        mv = mkv[:, XH * XHD:].reshape(n_prompt, n_mem, XH, XHD)
        x2_p, lg_p, st_p = _prompt_layer(hp, mkv, lw, p, swa_bias, bp, lp)
        x2_s, lg_s, st_s = _sample_layer(hs, l, lw, p, rel_bias, cache_win_k, cache_win_v, state_conv, state_ssm,
                                         cache_mem_k, cache_mem_v)
        hp, hs = _moe_ln(x2_p, x2_s, lg_p, lg_s, experts, l, ln3_g[l], ln3_b[l])
        wk_p.append(st_p[0]); wv_p.append(st_p[1]); cv_p.append(st_p[2]); ssm_p.append(st_p[3])
        mk_ps.append(mk); mv_ps.append(mv)
        wk_s.append(st_s[0]); wv_s.append(st_s[1]); cv_s.append(st_s[2]); ssm_s.append(st_s[3])
        gv_s.append(st_s[4])
    hp = hp.reshape(bp, lp, D_MODEL)
    hs = hs.reshape(bs_, 1, D_MODEL)
    return (hp, hs,
            jnp.stack(wk_p), jnp.stack(wv_p), jnp.stack(cv_p), jnp.stack(ssm_p),
            jnp.stack(mk_ps), jnp.stack(mv_ps),
            jnp.stack(wk_s), jnp.stack(wv_s), jnp.stack(cv_s), jnp.stack(ssm_s), jnp.stack(gv_s))
```

```python
import functools
import math

import numpy as np
import jax
import jax.numpy as jnp
from jax import lax
from jax.experimental import pallas as pl
from jax.experimental.pallas import tpu as pltpu

D_MODEL = 2048
DEPTH = 2
PAST_LEN = 16384
WINDOW = 128
H_A = 16
KV_A = 2
HD_A = 64
G_A = H_A // KV_A
N_BUCKETS = 32
MAX_DIST = 128
CHUNK_B = 128
GB = 16
CG_B = 64
W_B = GB * CG_B
D_INNER = 1024
P_C = 64
H_C = D_INNER // P_C
G_C = 2
R_C = H_C // G_C
N_C = 128
CONV_W = 4
CONV_DIM = D_INNER + 2 * G_C * N_C
SSD_CHUNK = 128
N_BRANCH = 3
BR_W = 1024
N_MEM = 256
XH = 4
XHD = 128
N_EXPERTS = 32
TOP_K = 4
D_FF = D_MODEL
SWIGLU_ALPHA = 1.702
SWIGLU_LIMIT = 7.0
DN_ALPHA = (2 * DEPTH) ** 0.25
LN_EPS = 1e-5
RMS_EPS = 1e-5

IN_SIZES = [H_A * HD_A, KV_A * HD_A, KV_A * HD_A, W_B, W_B, D_INNER, CONV_DIM, H_C]
IN_DIM = sum(IN_SIZES)
IN_DIM_PAD = 6144
COL_Q, COL_U, COL_GV, COL_Z, COL_XBC = 0, 1024, 2048, 3072, 4096
COL_K = COL_XBC + CONV_DIM
COL_V = COL_K + KV_A * HD_A
COL_DT = COL_V + KV_A * HD_A
LANES = 128

VMEM_LIMIT = 56 * 1024 * 1024
MOE_BM = 512
MOE_FC = 128
MOE_UNIT = 3

F32 = jnp.float32
BF16 = jnp.bfloat16


def _cparams(*sem):
    return pltpu.CompilerParams(dimension_semantics=sem, vmem_limit_bytes=VMEM_LIMIT)


def _mm_kernel(x_ref, w_ref, o_ref):
    o_ref[...] = jnp.dot(x_ref[...], w_ref[...], preferred_element_type=F32).astype(o_ref.dtype)


def _matmul(x, w, tm, tn, out_dtype=F32):
    M, K = x.shape
    N = w.shape[1]
    assert M % tm == 0 and N % tn == 0
    return pl.pallas_call(
        _mm_kernel,
        grid=(N // tn, M // tm),
        in_specs=[pl.BlockSpec((tm, K), lambda j, i: (i, 0)),
                  pl.BlockSpec((K, tn), lambda j, i: (0, j))],
        out_specs=pl.BlockSpec((tm, tn), lambda j, i: (i, j)),
        out_shape=jax.ShapeDtypeStruct((M, N), out_dtype),
        compiler_params=_cparams("parallel", "parallel"),
        name="dense_matmul",
    )(x, w)


def _gate_merge_kernel(x_ref, a_ref, b_ref, c_ref, wg0_ref, wg1_ref, wg2_ref,
                       bg0_ref, bg1_ref, bg2_ref, wp_ref, o_ref):
    x = x_ref[...]
    acc = None
    for k, (br_ref, wg_ref, bg_ref) in enumerate(
            ((a_ref, wg0_ref, bg0_ref), (b_ref, wg1_ref, bg1_ref), (c_ref, wg2_ref, bg2_ref))):
        z = jnp.dot(x, wg_ref[...], preferred_element_type=F32) + bg_ref[...]
        gate = 1.0 / (1.0 + jnp.exp(-z))
        proj = jnp.dot(br_ref[...], wp_ref[k], preferred_element_type=F32)
        acc = gate * proj if acc is None else acc + gate * proj
    o_ref[...] = acc.astype(o_ref.dtype)


def _gate_merge(x, a, b, c, w_gate, b_gate, w_branch, tm, tn):
    M = x.shape[0]
    nt = D_MODEL // tn
    row = lambda j, i: (i, 0)
    in_specs = [pl.BlockSpec((tm, D_MODEL), row)] + [pl.BlockSpec((tm, BR_W), row)] * 3
    in_specs += [pl.BlockSpec((D_MODEL, tn), functools.partial(lambda j, i, k: (0, k * nt + j), k=k))
                 for k in range(N_BRANCH)]
    in_specs += [pl.BlockSpec((1, tn), functools.partial(lambda j, i, k: (0, k * nt + j), k=k))
                 for k in range(N_BRANCH)]
    in_specs += [pl.BlockSpec((N_BRANCH, BR_W, tn), lambda j, i: (0, 0, j))]
    return pl.pallas_call(
        _gate_merge_kernel,
        grid=(nt, M // tm),
        in_specs=in_specs,
        out_specs=pl.BlockSpec((tm, tn), lambda j, i: (i, j)),
        out_shape=jax.ShapeDtypeStruct((M, D_MODEL), BF16),
        compiler_params=_cparams("parallel", "parallel"),
        name="gate_merge",
    )(x, a, b, c, w_gate, w_gate, w_gate, b_gate, b_gate, b_gate, w_branch)


def _layer_norm_rows(y, g, b):
    mu = jnp.mean(y, axis=-1, keepdims=True)
    yc = y - mu
    var = jnp.mean(yc * yc, axis=-1, keepdims=True)
    return yc * lax.rsqrt(var + LN_EPS) * g + b


def _mm_res_ln_kernel(a_ref, w_ref, res_ref, g_ref, b_ref, o_ref):
    y = jnp.dot(a_ref[...], w_ref[...], preferred_element_type=F32) + DN_ALPHA * res_ref[...]
    o_ref[...] = _layer_norm_rows(y, g_ref[...], b_ref[...])


def _mm_res_ln(a, w, res, g, b, tm):
    M, K = a.shape
    row = lambda i: (i, 0)
    fixed = lambda i: (0, 0)
    return pl.pallas_call(
        _mm_res_ln_kernel,
        grid=(M // tm,),
        in_specs=[pl.BlockSpec((tm, K), row), pl.BlockSpec((K, D_MODEL), fixed),
                  pl.BlockSpec((tm, D_MODEL), row), pl.BlockSpec((1, D_MODEL), fixed),
                  pl.BlockSpec((1, D_MODEL), fixed)],
        out_specs=pl.BlockSpec((tm, D_MODEL), row),
        out_shape=jax.ShapeDtypeStruct((M, D_MODEL), F32),
        compiler_params=_cparams("parallel"),
        name="matmul_residual_layernorm",
    )(a, w, res, g.reshape(1, -1), b.reshape(1, -1))


def _router_kernel(x_ref, w_ref, b_ref, o_ref):
    acc = jnp.dot(x_ref[...].astype(BF16), w_ref[...].astype(BF16), preferred_element_type=F32)
    o_ref[...] = acc + b_ref[...]


def _router(x, w_pad, b_pad, tm):
    M = x.shape[0]
    NP = w_pad.shape[1]
    return pl.pallas_call(
        _router_kernel,
        grid=(M // tm,),
        in_specs=[pl.BlockSpec((tm, D_MODEL), lambda i: (i, 0)),
                  pl.BlockSpec((D_MODEL, NP), lambda i: (0, 0)),
                  pl.BlockSpec((1, NP), lambda i: (0, 0))],
        out_specs=pl.BlockSpec((tm, NP), lambda i: (i, 0)),
        out_shape=jax.ShapeDtypeStruct((M, NP), F32),
        compiler_params=_cparams("parallel"),
        name="router_logits",
    )(x, w_pad, b_pad)


def _bf16_row_interleave(a, b):
    a32 = lax.bitcast_convert_type(a.astype(BF16).astype(F32), jnp.uint32)
    b32 = lax.bitcast_convert_type(b.astype(BF16).astype(F32), jnp.uint32)
    word = (a32 >> 16) | (b32 & jnp.uint32(0xFFFF0000))
    return pltpu.bitcast(word, BF16)


STEP_VALID, STEP_FIRST, STEP_LAST, STEP_NEW_WEIGHTS = 1, 2, 4, 8


def _moe_kernel(se_ref, sf_ref, sx_ref, so_ref, sj_ref, flag_ref, x_ref, w1_ref, b1_ref, w2_ref, b2_ref, o_ref,
                acc_ref, xs_ref, w1b_ref, w2q_ref):
    s = pl.program_id(0)
    fc = w2_ref.shape[1] // 2
    bm = x_ref.shape[0]
    flags = flag_ref[s]

    @pl.when((flags & STEP_VALID) != 0)
    def _():
        j = sj_ref[s]
        first = (flags & STEP_FIRST) != 0
        last = (flags & STEP_LAST) != 0

        @pl.when((flags & STEP_NEW_WEIGHTS) != 0)
        def _():
            w1b_ref[...] = w1_ref[0].astype(BF16)
            w2 = w2_ref[0]
            w2q_ref[...] = _bf16_row_interleave(w2[:fc], w2[fc:])

        @pl.when(first)
        def _():
            xs_ref[j] = x_ref[...].astype(BF16)

        h = jnp.dot(xs_ref[j], w1b_ref[...], preferred_element_type=F32) + b1_ref[0]
        lane = lax.broadcasted_iota(jnp.int32, (bm, LANES), 1)
        even = (lane & 1) == 0
        acts = []
        for c in range(2 * fc // LANES):
            a = h[:, LANES * c:LANES * (c + 1)]
            b = h[:, 2 * fc + LANES * c:2 * fc + LANES * (c + 1)]
            glu = jnp.where(even, a, pltpu.roll(b, 1, axis=1))
            lin = jnp.where(even, pltpu.roll(a, LANES - 1, axis=1), b)
            glu = jnp.minimum(glu, SWIGLU_LIMIT)
            lin = jnp.clip(lin, -SWIGLU_LIMIT, SWIGLU_LIMIT)
            act = glu * (1.0 / (1.0 + jnp.exp(-SWIGLU_ALPHA * glu))) * (lin + 1.0)
            acts.append(act.astype(BF16))
        act = jnp.concatenate(acts, axis=1) if len(acts) > 1 else acts[0]
        contrib = jnp.dot(act, w2q_ref[...], preferred_element_type=F32)

        @pl.when(first)
        def _():
            acc_ref[j] = contrib + b2_ref[0]

        @pl.when(jnp.logical_not(first | last))
        def _():
            acc_ref[j] += contrib

        @pl.when(last)
        def _():
            o_ref[...] = acc_ref[j] + contrib


def _moe_steps(padded, p_end, n_valid, nb, bm, nf, unit):
    i32 = jnp.int32
    n_exp = padded.shape[0]
    blk = jnp.arange(nb, dtype=i32)
    valid_b = blk < n_valid
    e_b = jnp.minimum(jnp.sum(p_end[None, :] <= (blk * bm)[:, None], axis=1, dtype=i32), n_exp - 1)
    is_e = e_b[:, None] == jnp.arange(n_exp, dtype=i32)[None, :]
    look = lambda table: jnp.sum(jnp.where(is_e, table.astype(i32)[None, :], 0), axis=1, dtype=i32)
    q_b = blk - look((p_end - padded) // bm)
    j_b = q_b % unit
    b0_b = blk - j_b
    n_b = jnp.where(valid_b, jnp.minimum(unit, look(padded // bm) - (q_b - j_b)), 1)
    rep = lambda a: jnp.repeat(a, nf)
    step = jnp.arange(nb * nf, dtype=i32)
    valid, e, b0, n = rep(valid_b), rep(e_b), rep(b0_b), rep(n_b)
    r = step - nf * b0
    f = r // n
    j = r % n
    last_blk = n_valid - 1
    e_last = jnp.sum(jnp.where(blk == last_blk, e_b, 0), dtype=i32)
    se = jnp.where(valid, e, e_last)
    sf = jnp.where(valid, f, nf - 1)
    sx = jnp.where(valid, jnp.where(f == 0, b0 + j, b0 + n - 1), last_blk)
    so = jnp.where(valid, jnp.where(f == nf - 1, b0 + j, b0), last_blk)
    sj = jnp.where(valid, j, 0)
    flags = jnp.where(valid, STEP_VALID + STEP_FIRST * (f == 0) + STEP_LAST * (f == nf - 1)
                      + STEP_NEW_WEIGHTS * (j == 0), 0)
    return [a.astype(i32) for a in (se, sf, sx, so, sj, flags)]


def _moe_ffn_blocks(rows, steps, w1, b1, w2, b2, layer, bm, fc, unit):
    d_model = rows.shape[1]
    d_ff = w2.shape[2]
    nb = rows.shape[0] // bm
    nf = d_ff // (2 * fc)
    assert nf >= 2 and steps[0].shape[0] == nb * nf
    grid_spec = pltpu.PrefetchScalarGridSpec(
        num_scalar_prefetch=6,
        grid=(nb * nf,),
        in_specs=[
            pl.BlockSpec((bm, d_model), lambda s, se, sf, sx, so, sj, fl: (sx[s], 0)),
            pl.BlockSpec((None, 1, d_model, 4 * fc), lambda s, se, sf, sx, so, sj, fl: (layer, se[s], 0, sf[s])),
            pl.BlockSpec((None, 1, 1, 4 * fc), lambda s, se, sf, sx, so, sj, fl: (layer, se[s], 0, sf[s])),
            pl.BlockSpec((None, 1, 2 * fc, d_model), lambda s, se, sf, sx, so, sj, fl: (layer, se[s], sf[s], 0)),
            pl.BlockSpec((None, 1, 1, d_model), lambda s, se, sf, sx, so, sj, fl: (layer, se[s], 0, 0)),
        ],
        out_specs=pl.BlockSpec((bm, d_model), lambda s, se, sf, sx, so, sj, fl: (so[s], 0)),
        scratch_shapes=[pltpu.VMEM((unit, bm, d_model), F32), pltpu.VMEM((unit, bm, d_model), BF16),
                        pltpu.VMEM((d_model, 4 * fc), BF16), pltpu.VMEM((2 * fc, d_model), BF16)],
    )
    return pl.pallas_call(
        _moe_kernel,
        grid_spec=grid_spec,
        out_shape=jax.ShapeDtypeStruct(rows.shape, F32),
        compiler_params=_cparams("arbitrary"),
        name="moe_expert_ffn",
    )(*steps, rows, w1, b1, w2, b2)


def _combine_ln_kernel(*refs):
    g_refs = refs[:TOP_K]
    gate_ref, res_ref, lg_ref, lb_ref, o_ref = refs[TOP_K:]
    gate = gate_ref[...]
    ff = None
    for k in range(TOP_K):
        term = gate[:, k:k + 1] * g_refs[k][...]
        ff = term if ff is None else ff + term
    o_ref[...] = _layer_norm_rows(DN_ALPHA * res_ref[...] + ff, lg_ref[...], lb_ref[...])


def _combine_ln(gathered, gate, res, g, b, tm):
    T, d = res.shape
    nt = T // tm
    row = lambda i: (i, 0)
    fixed = lambda i: (0, 0)
    g_specs = [pl.BlockSpec((tm, d), functools.partial(lambda i, k: (k * nt + i, 0), k=k)) for k in range(TOP_K)]
    return pl.pallas_call(
        _combine_ln_kernel,
        grid=(nt,),
        in_specs=g_specs + [pl.BlockSpec((tm, TOP_K), row), pl.BlockSpec((tm, d), row),
                            pl.BlockSpec((1, d), fixed), pl.BlockSpec((1, d), fixed)],
        out_specs=pl.BlockSpec((tm, d), row),
        out_shape=jax.ShapeDtypeStruct((T, d), F32),
        compiler_params=_cparams("parallel"),
        name="moe_combine_layernorm",
    )(*([gathered] * TOP_K), gate, res, g.reshape(1, -1), b.reshape(1, -1))


def _moe_route(logits, bm):
    T, n_exp = logits.shape
    i32 = jnp.int32
    top_v, top_i = lax.top_k(logits, TOP_K)
    gate = jax.nn.softmax(top_v, axis=-1)
    n_assign = T * TOP_K
    nb = -(-(n_assign + n_exp * (bm - 1)) // bm)
    ids = jnp.arange(n_assign, dtype=i32)
    e_sorted, order = lax.sort((top_i.reshape(-1).astype(i32), ids), num_keys=1, is_stable=True)
    is_e = e_sorted[:, None] == jnp.arange(n_exp, dtype=i32)[None, :]
    counts = jnp.sum(is_e, axis=0, dtype=i32)
    padded = (counts + bm - 1) // bm * bm
    p_end = jnp.cumsum(padded)
    u_start = jnp.cumsum(counts) - counts
    shift = (p_end - padded - u_start).astype(i32)
    dest = ids + jnp.sum(jnp.where(is_e, shift[None, :], 0), axis=1, dtype=i32)
    _, pos = lax.sort((order, dest), num_keys=1)
    src = jnp.zeros((nb * bm,), i32).at[dest].set(order // TOP_K, indices_are_sorted=True, unique_indices=True)
    n_valid = (p_end[-1] // bm).astype(i32)
    return gate, src, pos, (padded.astype(i32), p_end.astype(i32), n_valid, nb)


def _moe_ln(x2_p, x2_s, logits_p, logits_s, experts, layer, ln_g, ln_b):
    tp, ts = x2_p.shape[0], x2_s.shape[0]
    x2 = jnp.concatenate([x2_p, x2_s], axis=0)
    logits = jnp.concatenate([logits_p, logits_s], axis=0)[:, :N_EXPERTS]
    gate, src, pos, (padded, p_end, n_valid, nb) = _moe_route(logits, MOE_BM)
    rows = jnp.take(x2, src, axis=0, mode='clip')
    w_e1, b_e1, w_e2, b_e2 = experts
    steps = _moe_steps(padded, p_end, n_valid, nb, MOE_BM, D_FF // (2 * MOE_FC), MOE_UNIT)
    out_rows = _moe_ffn_blocks(rows, steps, w_e1, b_e1, w_e2, b_e2, layer, MOE_BM, MOE_FC, MOE_UNIT)
    pos = pos.reshape(tp + ts, TOP_K)
    g_p = jnp.take(out_rows, pos[:tp].T.reshape(-1), axis=0, mode='clip')
    g_s = jnp.take(out_rows, pos[tp:].T.reshape(-1), axis=0, mode='clip')
    hp = _combine_ln(g_p, gate[:tp], x2_p, ln_g, ln_b, 256)
    hs = _combine_ln(g_s, gate[tp:], x2_s, ln_g, ln_b, ts)
    return hp, hs


def _half_lane_variants(t):
    lo = lax.broadcasted_iota(jnp.int32, t.shape, 1) < HD_A
    zero = jnp.zeros_like(t)
    tr = pltpu.roll(t, HD_A, axis=1)
    return [[jnp.where(lo, t, zero).astype(BF16), jnp.where(lo, zero, tr).astype(BF16)],
            [jnp.where(lo, tr, zero).astype(BF16), jnp.where(lo, zero, t).astype(BF16)]]


def _swa_kernel(sinks_ref, q_ref, kp_ref, kc_ref, vp_ref, vc_ref, bias_ref, o_ref):
    j = pl.program_id(1)
    kvar = _half_lane_variants(jnp.concatenate([kp_ref[...], kc_ref[...]], axis=0))
    vvar = _half_lane_variants(jnp.concatenate([vp_ref[...], vc_ref[...]], axis=0))
    col = lax.broadcasted_iota(jnp.int32, (WINDOW, 2 * WINDOW), 1)
    no_prev = col < jnp.where(j == 0, WINDOW, 0)
    for r in range(H_A // 2):
        g = (2 * r) // G_A
        qp = q_ref[:, LANES * r:LANES * (r + 1)].astype(BF16)
        acc = None
        for par in range(2):
            h = 2 * r + par
            s = lax.dot_general(qp, kvar[g][par], (((1,), (1,)), ((), ())), preferred_element_type=F32)
            s = s * HD_A ** -0.5 + bias_ref[h]
            s = jnp.where(no_prev, -jnp.inf, s)
            sk = sinks_ref[h]
            m = jnp.maximum(jnp.max(s, axis=-1, keepdims=True), sk)
            pr = jnp.exp(s - m)
            den = jnp.sum(pr, axis=-1, keepdims=True) + jnp.exp(sk - m)
            o = jnp.dot((pr / den).astype(BF16), vvar[g][par], preferred_element_type=F32)
            acc = o if acc is None else acc + o
        o_ref[:, LANES * r:LANES * (r + 1)] = acc.astype(o_ref.dtype)


def _swa_prompt_bias(rel_bias):
    i = jnp.arange(WINDOW)[:, None]
    j = jnp.arange(2 * WINDOW)[None, :]
    dist = i + WINDOW - j
    b = jnp.moveaxis(rel_bias[_t5_bucket(dist)], -1, 0).astype(F32)
    return jnp.where((dist >= 0) & (dist < WINDOW), b, -jnp.inf)


def _swa_prompt_call(proj2d, bias, sinks, bsz, L):
    nb = L // WINDOW
    kcol, vcol = COL_K // LANES, COL_V // LANES
    cur = lambda b, j: b * nb + j
    prev = lambda b, j: b * nb + jnp.maximum(j - 1, 0)
    return pl.pallas_call(
        _swa_kernel,
        grid=(bsz, nb),
        in_specs=[
            pl.BlockSpec(memory_space=pltpu.SMEM),
            pl.BlockSpec((WINDOW, H_A * HD_A), lambda b, j: (cur(b, j), COL_Q // (H_A * HD_A))),
            pl.BlockSpec((WINDOW, LANES), lambda b, j: (prev(b, j), kcol)),
            pl.BlockSpec((WINDOW, LANES), lambda b, j: (cur(b, j), kcol)),
            pl.BlockSpec((WINDOW, LANES), lambda b, j: (prev(b, j), vcol)),
            pl.BlockSpec((WINDOW, LANES), lambda b, j: (cur(b, j), vcol)),
            pl.BlockSpec((H_A, WINDOW, 2 * WINDOW), lambda b, j: (0, 0, 0)),
        ],
        out_specs=pl.BlockSpec((WINDOW, H_A * HD_A), lambda b, j: (cur(b, j), 0)),
        out_shape=jax.ShapeDtypeStruct((bsz * L, H_A * HD_A), BF16),
        compiler_params=_cparams("parallel", "arbitrary"),
        name="swa_prompt",
    )(sinks, proj2d, proj2d, proj2d, proj2d, proj2d, bias)


def _gelu(x):
    return 0.5 * x * (1.0 + lax.erf(x * np.float32(np.sqrt(0.5))))


def _gmlp_kernel(u_ref, gv_ref, lng_ref, lnb_ref, w_ref, bias_ref, o_ref):
    u = _gelu(u_ref[...])
    gv = _layer_norm_rows(_gelu(gv_ref[...]), lng_ref[...], lnb_ref[...])
    lo = lax.broadcasted_iota(jnp.int32, (CHUNK_B, LANES), 1) < CG_B
    for r in range(GB // 2):
        vp = gv[:, LANES * r:LANES * (r + 1)]
        zero = jnp.zeros_like(vp)
        mix = jnp.dot(w_ref[2 * r], jnp.where(lo, vp, zero).astype(BF16), preferred_element_type=F32)
        mix += jnp.dot(w_ref[2 * r + 1], jnp.where(lo, zero, vp).astype(BF16), preferred_element_type=F32)
        sl = slice(LANES * r, LANES * (r + 1))
        o_ref[:, sl] = (u[:, sl] * (mix + bias_ref[:, sl])).astype(o_ref.dtype)


def _gmlp_prompt_call(proj2d, p, n_rows):
    w = (p['gmlp_ws'] * jnp.tril(jnp.ones((CHUNK_B, CHUNK_B), F32))).astype(BF16)
    bias = jnp.repeat(p['gmlp_bs'].T, CG_B, axis=1)
    fixed2 = lambda i: (0, 0)
    return pl.pallas_call(
        _gmlp_kernel,
        grid=(n_rows // CHUNK_B,),
        in_specs=[
            pl.BlockSpec((CHUNK_B, W_B), lambda i: (i, COL_U // W_B)),
            pl.BlockSpec((CHUNK_B, W_B), lambda i: (i, COL_GV // W_B)),
            pl.BlockSpec((1, W_B), fixed2), pl.BlockSpec((1, W_B), fixed2),
            pl.BlockSpec((GB, CHUNK_B, CHUNK_B), lambda i: (0, 0, 0)),
            pl.BlockSpec((CHUNK_B, W_B), fixed2),
        ],
        out_specs=pl.BlockSpec((CHUNK_B, W_B), lambda i: (i, 0)),
        out_shape=jax.ShapeDtypeStruct((n_rows, W_B), BF16),
        compiler_params=_cparams("parallel"),
        name="gmlp_prompt",
    )(proj2d, proj2d, p['gmlp_ln_g'].reshape(1, -1), p['gmlp_ln_b'].reshape(1, -1), w, bias)


def _silu(x):
    return x * (1.0 / (1.0 + jnp.exp(-x)))


def _softplus(x):
    return jnp.maximum(x, 0.0) + jnp.log1p(jnp.exp(-jnp.abs(x)))


def _causal_conv_chunk(cur, tail, w, bias):
    rows = lax.broadcasted_iota(jnp.int32, (8, cur.shape[1]), 0)
    y = jnp.broadcast_to(bias, cur.shape)
    y_head = jnp.broadcast_to(bias, (8, cur.shape[1]))
    for t in range(CONV_W):
        k = CONV_W - 1 - t
        wt = w[t:t + 1, :]
        if k == 0:
            y = y + cur * wt
            y_head = y_head + cur[:8] * wt
        else:
            sh = pltpu.roll(cur, k, axis=0)
            y = y + sh * wt
            y_head = y_head + jnp.where(rows < k, pltpu.roll(tail, k, axis=0), sh[:8]) * wt
    return jnp.concatenate([y_head, y[8:]], axis=0)


def _bf16_split3(x):
    p1 = x.astype(BF16)
    r1 = x - p1.astype(F32)
    p2 = r1.astype(BF16)
    p3 = (r1 - p2.astype(F32)).astype(BF16)
    return p1, p2, p3


def _ssd_kernel(xs_ref, bc_ref, z_ref, dt_ref, cwx_ref, cbx_ref, cwb_ref, cbb_ref, dtb_ref, a_ref,
                dsk_ref, ng_ref, y_ref, h_ref, state_ref, tailx_ref, tailb_ref):
    c = pl.program_id(1)
    C = SSD_CHUNK

    @pl.when(c == 0)
    def _():
        state_ref[...] = jnp.zeros_like(state_ref)
        tailx_ref[...] = jnp.zeros_like(tailx_ref)
        tailb_ref[...] = jnp.zeros_like(tailb_ref)

    xs_raw = xs_ref[...]
    bc_raw = bc_ref[...]
    xs = _silu(_causal_conv_chunk(xs_raw, tailx_ref[...], cwx_ref[...], cbx_ref[...]))
    bc = _silu(_causal_conv_chunk(bc_raw, tailb_ref[...], cwb_ref[...], cbb_ref[...]))
    tailx_ref[...] = xs_raw[C - 8:]
    tailb_ref[...] = bc_raw[C - 8:]

    dt = _softplus(dt_ref[...] + dtb_ref[...])
    da = dt * a_ref[...]
    row_i = lax.broadcasted_iota(jnp.int32, (C, C), 0)
    col_i = lax.broadcasted_iota(jnp.int32, (C, C), 1)
    causal = row_i >= col_i
    tril = jnp.where(causal, 1.0, 0.0).astype(BF16)
    acs = None
    for piece in _bf16_split3(da):
        t = jnp.dot(tril, piece, preferred_element_type=F32)
        acs = t if acs is None else acs + t
    acs_t = acs.T
    exp_acs = jnp.exp(acs)
    end_decay = jnp.exp(acs[C - 1:C, :] - acs)
    chunk_decay = jnp.exp(acs[C - 1:C, :])

    lo = lax.broadcasted_iota(jnp.int32, (C, LANES), 1) < P_C
    bm = [bc[:, N_C * g:N_C * (g + 1)].astype(BF16) for g in range(G_C)]
    cm = [bc[:, N_C * (G_C + g):N_C * (G_C + g + 1)].astype(BF16) for g in range(G_C)]
    cb = [lax.dot_general(cm[g], bm[g], (((1,), (1,)), ((), ())), preferred_element_type=F32)
          for g in range(G_C)]

    def per_lane_half(t, r):
        return jnp.where(lo, t[:, 2 * r:2 * r + 1], t[:, 2 * r + 1:2 * r + 2])

    ys = []
    for r in range(H_C // 2):
        g = (2 * r) // R_C
        sl = slice(LANES * r, LANES * (r + 1))
        x_pair = xs[:, sl]
        xdt = x_pair * per_lane_half(dt, r)
        zero = jnp.zeros_like(xdt)
        y_pair = None
        for par in range(2):
            h = 2 * r + par
            seg = acs[:, h:h + 1] - acs_t[h:h + 1, :]
            decay = jnp.where(causal, jnp.exp(seg), 0.0)
            m_h = (cb[g] * decay).astype(BF16)
            x_h = (jnp.where(lo, xdt, zero) if par == 0 else jnp.where(lo, zero, xdt)).astype(BF16)
            t = jnp.dot(m_h, x_h, preferred_element_type=F32)
            y_pair = t if y_pair is None else y_pair + t
        st = state_ref[sl, :]
        y_off = lax.dot_general(cm[g], st.astype(BF16), (((1,), (1,)), ((), ())), preferred_element_type=F32)
        y_pair = y_pair + y_off * per_lane_half(exp_acs, r)
        upd = lax.dot_general((xdt * per_lane_half(end_decay, r)).astype(BF16), bm[g],
                              (((0,), (0,)), ((), ())), preferred_element_type=F32)
        cd = jnp.concatenate([jnp.broadcast_to(chunk_decay[:, 2 * r:2 * r + 1], (P_C, N_C)),
                              jnp.broadcast_to(chunk_decay[:, 2 * r + 1:2 * r + 2], (P_C, N_C))], axis=0)
        state_ref[sl, :] = st * cd + upd
        ys.append(y_pair + dsk_ref[:, sl] * x_pair)
    y = jnp.concatenate(ys, axis=1) * _silu(z_ref[...])
    gw = D_INNER // G_C
    outs = []
    for g in range(G_C):
        yg = y[:, gw * g:gw * (g + 1)]
        outs.append(yg * lax.rsqrt(jnp.mean(yg * yg, axis=-1, keepdims=True) + RMS_EPS))
    y_ref[...] = (jnp.concatenate(outs, axis=1) * ng_ref[...]).astype(y_ref.dtype)

    @pl.when(c == pl.num_programs(1) - 1)
    def _():
        h_ref[...] = state_ref[...]


def _ssd_prompt_call(proj2d, p, bsz, L):
    nc = L // SSD_CHUNK
    row = lambda blk: (lambda b, c: (b * nc + c, blk))
    fixed = lambda b, c: (0, 0)
    pad_l = lambda v: jnp.pad(v.astype(F32), (0, LANES - H_C)).reshape(1, LANES)
    cw, cbias = p['conv_w'], p['conv_b'].reshape(1, -1)
    nbc = 2 * G_C * N_C
    args = (proj2d, proj2d, proj2d, proj2d,
            cw[:, :D_INNER], cbias[:, :D_INNER], cw[:, D_INNER:], cbias[:, D_INNER:],
            pad_l(p['dt_bias']), pad_l(-jnp.exp(p['a_log'].astype(F32))),
            jnp.repeat(p['d_skip'].astype(F32), P_C).reshape(1, -1), p['ssm_norm_g'].reshape(1, -1))
    return pl.pallas_call(
        _ssd_kernel,
        grid=(bsz, nc),
        in_specs=[
            pl.BlockSpec((SSD_CHUNK, D_INNER), row(COL_XBC // D_INNER)),
            pl.BlockSpec((SSD_CHUNK, nbc), row((COL_XBC + D_INNER) // nbc)),
            pl.BlockSpec((SSD_CHUNK, D_INNER), row(COL_Z // D_INNER)),
            pl.BlockSpec((SSD_CHUNK, LANES), row(COL_DT // LANES)),
            pl.BlockSpec((CONV_W, D_INNER), fixed), pl.BlockSpec((1, D_INNER), fixed),
            pl.BlockSpec((CONV_W, nbc), fixed), pl.BlockSpec((1, nbc), fixed),
            pl.BlockSpec((1, LANES), fixed), pl.BlockSpec((1, LANES), fixed),
            pl.BlockSpec((1, D_INNER), fixed), pl.BlockSpec((1, D_INNER), fixed),
        ],
        out_specs=[pl.BlockSpec((SSD_CHUNK, D_INNER), lambda b, c: (b * nc + c, 0)),
                   pl.BlockSpec((None, H_C * P_C, N_C), lambda b, c: (b, 0, 0))],
        out_shape=[jax.ShapeDtypeStruct((bsz * L, D_INNER), BF16),
                   jax.ShapeDtypeStruct((bsz, H_C * P_C, N_C), F32)],
        scratch_shapes=[pltpu.VMEM((H_C * P_C, N_C), F32), pltpu.VMEM((8, D_INNER), F32),
                        pltpu.VMEM((8, nbc), F32)],
        compiler_params=_cparams("parallel", "arbitrary"),
        name="ssd_prompt",
    )(*args)


def _bf16_round(x):
    return x.astype(BF16).astype(F32)


def _swa_decode_kernel(sinks_ref, q_ref, kn_ref, vn_ref, ck_ref, cv_ref, bias_ref, o_ref, wk_ref, wv_ref):
    W = WINDOW
    last = lax.broadcasted_iota(jnp.int32, (W, LANES), 0) == W - 1
    kw = jnp.where(last, jnp.broadcast_to(kn_ref[...], (W, LANES)), pltpu.roll(ck_ref[...], W - 1, axis=0))
    vw = jnp.where(last, jnp.broadcast_to(vn_ref[...], (W, LANES)), pltpu.roll(cv_ref[...], W - 1, axis=0))
    wk_ref[...] = kw
    wv_ref[...] = vw
    kvar = _half_lane_variants(kw)
    vvar = _half_lane_variants(vw)
    outs = []
    for r in range(H_A // 2):
        g = (2 * r) // G_A
        qp = jnp.broadcast_to(q_ref[:, LANES * r:LANES * (r + 1)], (8, LANES)).astype(BF16)
        acc = None
        for par in range(2):
            h = 2 * r + par
            s = lax.dot_general(qp, kvar[g][par], (((1,), (1,)), ((), ())), preferred_element_type=F32)
            s = s * HD_A ** -0.5 + bias_ref[h:h + 1, :]
            sk = sinks_ref[h]
            m = jnp.maximum(jnp.max(s, axis=-1, keepdims=True), sk)
            pr = jnp.exp(s - m)
            den = jnp.sum(pr, axis=-1, keepdims=True) + jnp.exp(sk - m)
            o = jnp.dot((pr / den).astype(BF16), vvar[g][par], preferred_element_type=F32)
            acc = o if acc is None else acc + o
        outs.append(acc[0:1])
    o_ref[...] = jnp.concatenate(outs, axis=1).astype(o_ref.dtype)


def _swa_decode_call(proj3, cache_k, cache_v, layer, rel_bias, sinks):
    bsz = proj3.shape[0]
    ck = cache_k.reshape(cache_k.shape[0], bsz, WINDOW, LANES)
    cv = cache_v.reshape(cache_v.shape[0], bsz, WINDOW, LANES)
    dist = WINDOW - 1 - jnp.arange(WINDOW)
    bias = rel_bias[_t5_bucket(dist)].T.astype(F32)
    tok = lambda blk: (lambda i: (i, 0, blk))
    cache = lambda i: (layer, i, 0, 0)
    return pl.pallas_call(
        _swa_decode_kernel,
        grid=(bsz,),
        in_specs=[
            pl.BlockSpec(memory_space=pltpu.SMEM),
            pl.BlockSpec((None, 1, H_A * HD_A), tok(COL_Q // (H_A * HD_A))),
            pl.BlockSpec((None, 1, LANES), tok(COL_K // LANES)),
            pl.BlockSpec((None, 1, LANES), tok(COL_V // LANES)),
            pl.BlockSpec((None, None, WINDOW, LANES), cache),
            pl.BlockSpec((None, None, WINDOW, LANES), cache),
            pl.BlockSpec((H_A, WINDOW), lambda i: (0, 0)),
        ],
        out_specs=[pl.BlockSpec((None, 1, H_A * HD_A), lambda i: (i, 0, 0)),
                   pl.BlockSpec((None, WINDOW, LANES), lambda i: (i, 0, 0)),
                   pl.BlockSpec((None, WINDOW, LANES), lambda i: (i, 0, 0))],
        out_shape=[jax.ShapeDtypeStruct((bsz, 1, H_A * HD_A), BF16),
                   jax.ShapeDtypeStruct((bsz, WINDOW, LANES), F32),
                   jax.ShapeDtypeStruct((bsz, WINDOW, LANES), F32)],
        compiler_params=_cparams("parallel"),
        name="swa_sample",
    )(sinks, proj3, proj3, proj3, ck, cv, bias)


def _gmlp_step_kernel(u_ref, gv_ref, lng_ref, lnb_ref, w0_ref, b0_ref, o_ref, gv_out_ref):
    gv = _layer_norm_rows(_gelu(gv_ref[...]), lng_ref[...], lnb_ref[...])
    gv_out_ref[...] = gv
    mix = _bf16_round(w0_ref[...]) * _bf16_round(gv) + b0_ref[...]
    o_ref[...] = (_gelu(u_ref[...]) * mix).astype(o_ref.dtype)


def _gmlp_step_call(proj2d, p):
    n = proj2d.shape[0]
    w0 = jnp.repeat(p['gmlp_ws'][:, 0, 0], CG_B).reshape(1, -1)
    b0 = jnp.repeat(p['gmlp_bs'][:, 0], CG_B).reshape(1, -1)
    fixed = lambda i: (0, 0)
    return pl.pallas_call(
        _gmlp_step_kernel,
        grid=(1,),
        in_specs=[pl.BlockSpec((n, W_B), lambda i: (0, COL_U // W_B)),
                  pl.BlockSpec((n, W_B), lambda i: (0, COL_GV // W_B)),
                  pl.BlockSpec((1, W_B), fixed), pl.BlockSpec((1, W_B), fixed),
                  pl.BlockSpec((1, W_B), fixed), pl.BlockSpec((1, W_B), fixed)],
        out_specs=[pl.BlockSpec((n, W_B), fixed), pl.BlockSpec((n, W_B), fixed)],
        out_shape=[jax.ShapeDtypeStruct((n, W_B), BF16), jax.ShapeDtypeStruct((n, W_B), F32)],
        compiler_params=_cparams("arbitrary"),
        name="gmlp_sample",
    )(proj2d, proj2d, p['gmlp_ln_g'].reshape(1, -1), p['gmlp_ln_b'].reshape(1, -1), w0, b0)


def _conv_step(st, cur, w, bias):
    y = bias
    for t in range(CONV_W - 1):
        y = y + st[t:t + 1, :] * w[t:t + 1, :]
    return y + cur * w[CONV_W - 1:CONV_W, :]


def _ssd_step_kernel(xs_ref, bc_ref, z_ref, dt_ref, stx_ref, stb_ref, h0_ref, cwx_ref, cbx_ref, cwb_ref, cbb_ref,
                     dtb_ref, a_ref, dsk_ref, ng_ref, y_ref, h_ref, ncx_ref, ncb_ref):
    xs_raw, bc_raw = xs_ref[...], bc_ref[...]
    stx, stb = stx_ref[...], stb_ref[...]
    ncx_ref[0:CONV_W - 2, :] = stx[1:CONV_W - 1]
    ncx_ref[CONV_W - 2:CONV_W - 1, :] = xs_raw
    ncb_ref[0:CONV_W - 2, :] = stb[1:CONV_W - 1]
    ncb_ref[CONV_W - 2:CONV_W - 1, :] = bc_raw
    xs = _silu(_conv_step(stx, xs_raw, cwx_ref[...], cbx_ref[...]))
    bc = _silu(_conv_step(stb, bc_raw, cwb_ref[...], cbb_ref[...]))
    dt = _softplus(dt_ref[...] + dtb_ref[...])
    decay = jnp.exp(dt * a_ref[...])
    xdt = _bf16_round(xs * dt)
    gw = D_INNER // G_C
    first_group = lax.broadcasted_iota(jnp.int32, (1, D_INNER), 1) < gw
    bm = [_bf16_round(bc[:, N_C * g:N_C * (g + 1)]) for g in range(G_C)]
    cm = [_bf16_round(bc[:, N_C * (G_C + g):N_C * (G_C + g + 1)]) for g in range(G_C)]
    cb = [_bf16_round(jnp.sum(cm[g] * bm[g], axis=1, keepdims=True)) for g in range(G_C)]
    y_diag = jnp.where(first_group, cb[0], cb[1]) * xdt
    h0 = h0_ref[...]
    y_off = jnp.concatenate([
        lax.dot_general(jnp.broadcast_to(cm[g], (8, N_C)).astype(BF16), h0[gw * g:gw * (g + 1)].astype(BF16),
                        (((1,), (1,)), ((), ())), preferred_element_type=F32)[0:1] for g in range(G_C)], axis=1)
    y = y_diag + y_off * decay
    y = (y + dsk_ref[...] * xs) * _silu(z_ref[...])
    outs = []
    for g in range(G_C):
        yg = y[:, gw * g:gw * (g + 1)]
        outs.append(yg * lax.rsqrt(jnp.mean(yg * yg, axis=-1, keepdims=True) + RMS_EPS))
    y_ref[...] = (jnp.concatenate(outs, axis=1) * ng_ref[...]).astype(y_ref.dtype)
    decay_rows = jnp.broadcast_to(decay, (LANES, D_INNER)).T
    xdt_rows = jnp.broadcast_to(xdt, (LANES, D_INNER)).T
    rows = lax.broadcasted_iota(jnp.int32, (D_INNER, N_C), 0)
    bm_rows = jnp.where(rows < gw, jnp.broadcast_to(bm[0], (D_INNER, N_C)), jnp.broadcast_to(bm[1], (D_INNER, N_C)))
    h_ref[...] = h0 * decay_rows + xdt_rows * bm_rows


def _ssd_step_call(proj3, state_conv, state_ssm, layer, p):
    bsz = proj3.shape[0]
    nbc = 2 * G_C * N_C
    ssm = state_ssm.reshape(state_ssm.shape[0], bsz, H_C * P_C, N_C)
    dt_lanes = jnp.repeat(proj3[:, :, COL_DT:COL_DT + H_C], P_C, axis=-1)
    per_lane = lambda v: jnp.repeat(v.astype(F32), P_C).reshape(1, -1)
    cw, cbias = p['conv_w'], p['conv_b'].reshape(1, -1)
    tok = lambda blk: (lambda i: (i, 0, blk))
    fixed = lambda i: (0, 0)
    return pl.pallas_call(
        _ssd_step_kernel,
        grid=(bsz,),
        in_specs=[
            pl.BlockSpec((None, 1, D_INNER), tok(COL_XBC // D_INNER)),
            pl.BlockSpec((None, 1, nbc), tok((COL_XBC + D_INNER) // nbc)),
            pl.BlockSpec((None, 1, D_INNER), tok(COL_Z // D_INNER)),
            pl.BlockSpec((None, 1, D_INNER), tok(0)),
            pl.BlockSpec((None, None, CONV_W - 1, D_INNER), lambda i: (layer, i, 0, 0)),
            pl.BlockSpec((None, None, CONV_W - 1, nbc), lambda i: (layer, i, 0, D_INNER // nbc)),
            pl.BlockSpec((None, None, H_C * P_C, N_C), lambda i: (layer, i, 0, 0)),
            pl.BlockSpec((CONV_W, D_INNER), fixed), pl.BlockSpec((1, D_INNER), fixed),
            pl.BlockSpec((CONV_W, nbc), fixed), pl.BlockSpec((1, nbc), fixed),
            pl.BlockSpec((1, D_INNER), fixed), pl.BlockSpec((1, D_INNER), fixed),
            pl.BlockSpec((1, D_INNER), fixed), pl.BlockSpec((1, D_INNER), fixed),
        ],
        out_specs=[pl.BlockSpec((None, 1, D_INNER), lambda i: (i, 0, 0)),
                   pl.BlockSpec((None, H_C * P_C, N_C), lambda i: (i, 0, 0)),
                   pl.BlockSpec((None, CONV_W - 1, D_INNER), lambda i: (i, 0, 0)),
                   pl.BlockSpec((None, CONV_W - 1, nbc), lambda i: (i, 0, 0))],
        out_shape=[jax.ShapeDtypeStruct((bsz, 1, D_INNER), BF16),
                   jax.ShapeDtypeStruct((bsz, H_C * P_C, N_C), F32),
                   jax.ShapeDtypeStruct((bsz, CONV_W - 1, D_INNER), F32),
                   jax.ShapeDtypeStruct((bsz, CONV_W - 1, nbc), F32)],
        compiler_params=_cparams("parallel"),
        name="ssd_sample",
    )(proj3, proj3, proj3, dt_lanes, state_conv, state_conv, ssm,
      cw[:, :D_INNER], cbias[:, :D_INNER], cw[:, D_INNER:], cbias[:, D_INNER:],
      per_lane(p['dt_bias']), per_lane(-jnp.exp(p['a_log'].astype(F32))), per_lane(p['d_skip']),
      p['ssm_norm_g'].reshape(1, -1))


def _xattn_kernel(x1_ref, wq_ref, mk_ref, mv_ref, wo_ref, g_ref, b_ref, wr_ref, br_ref, x2_ref, lg_ref):
    x1 = x1_ref[...]
    q = jnp.dot(x1.astype(BF16), wq_ref[...], preferred_element_type=F32)
    outs = []
    for h in range(XH):
        sl = slice(XHD * h, XHD * (h + 1))
        s = lax.dot_general(q[:, sl].astype(BF16), mk_ref[:, sl].astype(BF16),
                            (((1,), (1,)), ((), ())), preferred_element_type=F32) * XHD ** -0.5
        e = jnp.exp(s - jnp.max(s, axis=-1, keepdims=True))
        w = e / jnp.sum(e, axis=-1, keepdims=True)
        outs.append(jnp.dot(w.astype(BF16), mv_ref[:, sl].astype(BF16), preferred_element_type=F32).astype(BF16))
    o = jnp.concatenate(outs, axis=1)
    y = jnp.dot(o, wo_ref[...], preferred_element_type=F32) + DN_ALPHA * x1
    x2 = _layer_norm_rows(y, g_ref[...], b_ref[...])
    x2_ref[...] = x2
    lg_ref[...] = jnp.dot(x2.astype(BF16), wr_ref[...], preferred_element_type=F32) + br_ref[...]


def _xattn_prompt_call(x1, mkv, lw, g, b, bsz, L, tm):
    nt = L // tm
    xw = XH * XHD
    row = lambda bb, i: (bb * nt + i, 0)
    fixed = lambda bb, i: (0, 0)
    n_lg = lw['w_router'].shape[1]
    return pl.pallas_call(
        _xattn_kernel,
        grid=(bsz, nt),
        in_specs=[
            pl.BlockSpec((tm, D_MODEL), row),
            pl.BlockSpec((D_MODEL, xw), fixed),
            pl.BlockSpec((N_MEM, xw), lambda bb, i: (bb, 0)),
            pl.BlockSpec((N_MEM, xw), lambda bb, i: (bb, 1)),
            pl.BlockSpec((xw, D_MODEL), fixed),
            pl.BlockSpec((1, D_MODEL), fixed), pl.BlockSpec((1, D_MODEL), fixed),
            pl.BlockSpec((D_MODEL, n_lg), fixed), pl.BlockSpec((1, n_lg), fixed),
        ],
        out_specs=[pl.BlockSpec((tm, D_MODEL), row), pl.BlockSpec((tm, n_lg), row)],
        out_shape=[jax.ShapeDtypeStruct((bsz * L, D_MODEL), F32), jax.ShapeDtypeStruct((bsz * L, n_lg), F32)],
        compiler_params=_cparams("parallel", "parallel"),
        name="memory_attention_prompt",
    )(x1, lw['w_xq'], mkv, mkv, lw['w_xo'], g.reshape(1, -1), b.reshape(1, -1),
      lw['w_router_bf16'], lw['b_router'])


def _xattn_decode_kernel(q_ref, k_ref, v_ref, o_ref):
    q = q_ref[...].astype(BF16).astype(F32)
    k = k_ref[...].astype(BF16).astype(F32)
    v = v_ref[...].astype(BF16).astype(F32)
    prod = k * q
    outs = []
    for h in range(XH):
        sl = slice(XHD * h, XHD * (h + 1))
        s = jnp.sum(prod[:, sl], axis=1, keepdims=True) * XHD ** -0.5
        e = jnp.exp(s - jnp.max(s, axis=0, keepdims=True))
        w = (e / jnp.sum(e, axis=0, keepdims=True)).astype(BF16).astype(F32)
        outs.append(jnp.sum(w * v[:, sl], axis=0, keepdims=True))
    o_ref[...] = jnp.concatenate(outs, axis=1)


def _xattn_decode_call(q, cache_k, cache_v, layer):
    bsz, xw = q.shape
    ck = cache_k.reshape(cache_k.shape[0], bsz, N_MEM, xw)
    cv = cache_v.reshape(cache_v.shape[0], bsz, N_MEM, xw)
    out = pl.pallas_call(
        _xattn_decode_kernel,
        grid=(bsz,),
        in_specs=[pl.BlockSpec((None, 1, xw), lambda i: (i, 0, 0)),
                  pl.BlockSpec((None, None, N_MEM, xw), lambda i: (layer, i, 0, 0)),
                  pl.BlockSpec((None, None, N_MEM, xw), lambda i: (layer, i, 0, 0))],
        out_specs=pl.BlockSpec((None, 1, xw), lambda i: (i, 0, 0)),
        out_shape=jax.ShapeDtypeStruct((bsz, 1, xw), F32),
        compiler_params=_cparams("parallel"),
        name="memory_attention_sample",
    )(q.reshape(bsz, 1, xw), ck, cv)
    return out.reshape(bsz, xw)


def _t5_bucket(dist):
    n = jnp.maximum(dist, 0)
    exact = N_BUCKETS // 2
    nf = jnp.maximum(n, 1).astype(F32)
    large = exact + (jnp.log(nf / exact) / math.log(MAX_DIST / exact) * (N_BUCKETS - exact)).astype(jnp.int32)
    return jnp.where(n < exact, n, jnp.minimum(large, N_BUCKETS - 1))


def _prompt_layer(x, mkv, lw, p, swa_bias, bsz, L):
    T = bsz * L
    xb = x.astype(BF16)
    proj = _matmul(xb, lw['w_in'], 1024, 512)
    a_out = _swa_prompt_call(proj, swa_bias, p['sinks'], bsz, L)
    b_out = _gmlp_prompt_call(proj, p, T)
    c_out, h_last = _ssd_prompt_call(proj, p, bsz, L)
    mixed = _gate_merge(xb, a_out, b_out, c_out, lw['w_gate'], lw['b_gate'], lw['w_branch'], 512, 512)
    x1 = _mm_res_ln(mixed, lw['w_o'], x, p['ln1_g'], p['ln1_b'], 512)
    x2, logits = _xattn_prompt_call(x1, mkv, lw, p['ln2_g'], p['ln2_b'], bsz, L, 512)
    proj3 = proj.reshape(bsz, L, IN_DIM_PAD)
    win_k = proj3[:, L - WINDOW:, COL_K:COL_K + KV_A * HD_A].reshape(bsz, WINDOW, KV_A, HD_A)
    win_v = proj3[:, L - WINDOW:, COL_V:COL_V + KV_A * HD_A].reshape(bsz, WINDOW, KV_A, HD_A)
    conv = proj3[:, L - (CONV_W - 1):, COL_XBC:COL_XBC + CONV_DIM]
    return x2, logits, (win_k, win_v, conv, h_last.reshape(bsz, H_C, P_C, N_C))


def _sample_layer(x, layer, lw, p, rel_bias, cache_win_k, cache_win_v, state_conv, state_ssm, cache_mem_k, cache_mem_v):
    bsz = x.shape[0]
    xb = x.astype(BF16)
    proj = _matmul(xb, lw['w_in'], bsz, 512)
    proj3 = proj.reshape(bsz, 1, IN_DIM_PAD)
    a_out, win_k, win_v = _swa_decode_call(proj3, cache_win_k, cache_win_v, layer, rel_bias, p['sinks'])
    b_out, gv = _gmlp_step_call(proj, p)
    c_out, ssm, conv_x, conv_bc = _ssd_step_call(proj3, state_conv, state_ssm, layer, p)
    mixed = _gate_merge(xb, a_out.reshape(bsz, BR_W), b_out, c_out.reshape(bsz, BR_W),
                        lw['w_gate'], lw['b_gate'], lw['w_branch'], bsz, 512)
    x1 = _mm_res_ln(mixed, lw['w_o'], x, p['ln1_g'], p['ln1_b'], bsz)
    q = _matmul(x1.astype(BF16), lw['w_xq'], bsz, XH * XHD)
    o = _xattn_decode_call(q, cache_mem_k, cache_mem_v, layer)
    x2 = _mm_res_ln(o.astype(BF16), lw['w_xo'], x1, p['ln2_g'], p['ln2_b'], bsz)
    logits = _router(x2, lw['w_router'], lw['b_router'], bsz)
    states = (win_k.reshape(bsz, WINDOW, KV_A, HD_A), win_v.reshape(bsz, WINDOW, KV_A, HD_A),
              jnp.concatenate([conv_x, conv_bc], axis=-1), ssm.reshape(bsz, H_C, P_C, N_C),
              gv.reshape(bsz, 1, W_B))
    return x2, logits, states


def kernel(x_prompt, x_sample, mem_prompt, cache_win_k, cache_win_v, state_conv, state_ssm, cache_mem_k, cache_mem_v, w_in, rel_bias, sinks, gmlp_ln_g, gmlp_ln_b, gmlp_ws, gmlp_bs, conv_w, conv_b, dt_bias, a_log, d_skip, ssm_norm_g, w_branch, w_gate, b_gate, w_o, ln1_g, ln1_b, w_xq, w_xk, w_xv, w_xo, ln2_g, ln2_b, w_router, b_router, w_e1, b_e1, w_e2, b_e2, ln3_g, ln3_b):
    assert cache_win_k.shape[2] == WINDOW and x_sample.shape[1] == 1
    n_prompt, n_mem = mem_prompt.shape[0], mem_prompt.shape[1]
    bp, lp = x_prompt.shape[:2]
    bs_ = x_sample.shape[0]
    hp, hs = x_prompt.reshape(bp * lp, D_MODEL), x_sample.reshape(bs_, D_MODEL)
    wk_p, wv_p, cv_p, ssm_p, mk_ps, mv_ps = [], [], [], [], [], []
    wk_s, wv_s, cv_s, ssm_s, gv_s = [], [], [], [], []
    mem_b = mem_prompt.reshape(n_prompt * n_mem, D_MODEL).astype(BF16)
    swa_bias = _swa_prompt_bias(rel_bias)
    experts = (w_e1, b_e1.reshape(DEPTH, N_EXPERTS, 1, 2 * D_FF), w_e2, b_e2.reshape(DEPTH, N_EXPERTS, 1, D_MODEL))
    for l in range(DEPTH):
        p = dict(sinks=sinks[l], gmlp_ln_g=gmlp_ln_g[l], gmlp_ln_b=gmlp_ln_b[l],
                 gmlp_ws=gmlp_ws[l], gmlp_bs=gmlp_bs[l], conv_w=conv_w[l], conv_b=conv_b[l],
                 dt_bias=dt_bias[l], a_log=a_log[l], d_skip=d_skip[l], ssm_norm_g=ssm_norm_g[l],
                 ln1_g=ln1_g[l], ln1_b=ln1_b[l], ln2_g=ln2_g[l], ln2_b=ln2_b[l])
        wi = w_in[l]
        seg = np.cumsum([0] + IN_SIZES)
        part = lambda n: wi[:, seg[n]:seg[n + 1]]
        w_in_cols = jnp.concatenate(
            [part(0), part(3), part(4), part(5), part(6), part(1), part(2), part(7),
             jnp.zeros((D_MODEL, IN_DIM_PAD - IN_DIM), F32)], axis=1).astype(BF16)
        w_router_pad = jnp.pad(w_router[l], ((0, 0), (0, LANES - N_EXPERTS)))
        lw = dict(
            w_in=w_in_cols,
            w_gate=w_gate[l].astype(BF16),
            b_gate=b_gate[l].reshape(1, -1),
            w_branch=w_branch[l].astype(BF16),
            w_o=w_o[l].astype(BF16),
            w_xq=w_xq[l].astype(BF16),
            w_xo=w_xo[l].astype(BF16),
            w_router=w_router_pad,
            w_router_bf16=w_router_pad.astype(BF16),
            b_router=jnp.pad(b_router[l], (0, LANES - N_EXPERTS)).reshape(1, -1),
        )
        w_kv = jnp.concatenate([w_xk[l], w_xv[l]], axis=1).astype(BF16)
        mkv = _matmul(mem_b, w_kv, n_mem, XH * XHD)
        mk = mkv[:, :XH * XHD].reshape(n_prompt, n_mem, XH, XHD)
        mv = mkv[:, XH * XHD:].reshape(n_prompt, n_mem, XH, XHD)
        x2_p, lg_p, st_p = _prompt_layer(hp, mkv, lw, p, swa_bias, bp, lp)
        x2_s, lg_s, st_s = _sample_layer(hs, l, lw, p, rel_bias, cache_win_k, cache_win_v, state_conv, state_ssm,
                                         cache_mem_k, cache_mem_v)
        hp, hs = _moe_ln(x2_p, x2_s, lg_p, lg_s, experts, l, ln3_g[l], ln3_b[l])
        wk_p.append(st_p[0]); wv_p.append(st_p[1]); cv_p.append(st_p[2]); ssm_p.append(st_p[3])
        mk_ps.append(mk); mv_ps.append(mv)
        wk_s.append(st_s[0]); wv_s.append(st_s[1]); cv_s.append(st_s[2]); ssm_s.append(st_s[3])
        gv_s.append(st_s[4])
    hp = hp.reshape(bp, lp, D_MODEL)
    hs = hs.reshape(bs_, 1, D_MODEL)
    return (hp, hs,
            jnp.stack(wk_p), jnp.stack(wv_p), jnp.stack(cv_p), jnp.stack(ssm_p),
            jnp.stack(mk_ps), jnp.stack(mv_ps),
            jnp.stack(wk_s), jnp.stack(wv_s), jnp.stack(cv_s), jnp.stack(ssm_s), jnp.stack(gv_s))
```

```python
import functools
import math

import numpy as np
import jax
import jax.numpy as jnp
from jax import lax
from jax.experimental import pallas as pl
from jax.experimental.pallas import tpu as pltpu

D_MODEL = 2048
DEPTH = 2
PAST_LEN = 16384
WINDOW = 128
H_A = 16
KV_A = 2
HD_A = 64
G_A = H_A // KV_A
N_BUCKETS = 32
MAX_DIST = 128
CHUNK_B = 128
GB = 16
CG_B = 64
W_B = GB * CG_B
D_INNER = 1024
P_C = 64
H_C = D_INNER // P_C
G_C = 2
R_C = H_C // G_C
N_C = 128
CONV_W = 4
CONV_DIM = D_INNER + 2 * G_C * N_C
SSD_CHUNK = 128
N_BRANCH = 3
BR_W = 1024
N_MEM = 256
XH = 4
XHD = 128
N_EXPERTS = 32
TOP_K = 4
D_FF = D_MODEL
SWIGLU_ALPHA = 1.702
SWIGLU_LIMIT = 7.0
DN_ALPHA = (2 * DEPTH) ** 0.25
LN_EPS = 1e-5
RMS_EPS = 1e-5

IN_SIZES = [H_A * HD_A, KV_A * HD_A, KV_A * HD_A, W_B, W_B, D_INNER, CONV_DIM, H_C]
IN_DIM = sum(IN_SIZES)
IN_DIM_PAD = 6144
COL_Q, COL_U, COL_GV, COL_Z, COL_XBC = 0, 1024, 2048, 3072, 4096
COL_K = COL_XBC + CONV_DIM
COL_V = COL_K + KV_A * HD_A
COL_DT = COL_V + KV_A * HD_A
LANES = 128

VMEM_LIMIT = 56 * 1024 * 1024
MOE_BM = 512
MOE_FC = 128
MOE_UNIT = 3

F32 = jnp.float32
BF16 = jnp.bfloat16


def _cparams(*sem):
    return pltpu.CompilerParams(dimension_semantics=sem, vmem_limit_bytes=VMEM_LIMIT)


def _mm_kernel(x_ref, w_ref, o_ref):
    o_ref[...] = jnp.dot(x_ref[...], w_ref[...], preferred_element_type=F32).astype(o_ref.dtype)


def _matmul(x, w, tm, tn, out_dtype=F32):
    M, K = x.shape
    N = w.shape[1]
    assert M % tm == 0 and N % tn == 0
    return pl.pallas_call(
        _mm_kernel,
        grid=(N // tn, M // tm),
        in_specs=[pl.BlockSpec((tm, K), lambda j, i: (i, 0)),
                  pl.BlockSpec((K, tn), lambda j, i: (0, j))],
        out_specs=pl.BlockSpec((tm, tn), lambda j, i: (i, j)),
        out_shape=jax.ShapeDtypeStruct((M, N), out_dtype),
        compiler_params=_cparams("parallel", "parallel"),
        name="dense_matmul",
    )(x, w)


def _gate_merge_kernel(x_ref, a_ref, b_ref, c_ref, wg0_ref, wg1_ref, wg2_ref,
                       bg0_ref, bg1_ref, bg2_ref, wp_ref, o_ref):
    x = x_ref[...]
    acc = None
    for k, (br_ref, wg_ref, bg_ref) in enumerate(
            ((a_ref, wg0_ref, bg0_ref), (b_ref, wg1_ref, bg1_ref), (c_ref, wg2_ref, bg2_ref))):
        z = jnp.dot(x, wg_ref[...], preferred_element_type=F32) + bg_ref[...]
        gate = 1.0 / (1.0 + jnp.exp(-z))
        proj = jnp.dot(br_ref[...], wp_ref[k], preferred_element_type=F32)
        acc = gate * proj if acc is None else acc + gate * proj
    o_ref[...] = acc.astype(o_ref.dtype)


def _gate_merge(x, a, b, c, w_gate, b_gate, w_branch, tm, tn):
    M = x.shape[0]
    nt = D_MODEL // tn
    row = lambda j, i: (i, 0)
    in_specs = [pl.BlockSpec((tm, D_MODEL), row)] + [pl.BlockSpec((tm, BR_W), row)] * 3
    in_specs += [pl.BlockSpec((D_MODEL, tn), functools.partial(lambda j, i, k: (0, k * nt + j), k=k))
                 for k in range(N_BRANCH)]
    in_specs += [pl.BlockSpec((1, tn), functools.partial(lambda j, i, k: (0, k * nt + j), k=k))
                 for k in range(N_BRANCH)]
    in_specs += [pl.BlockSpec((N_BRANCH, BR_W, tn), lambda j, i: (0, 0, j))]
    return pl.pallas_call(
        _gate_merge_kernel,
        grid=(nt, M // tm),
        in_specs=in_specs,
        out_specs=pl.BlockSpec((tm, tn), lambda j, i: (i, j)),
        out_shape=jax.ShapeDtypeStruct((M, D_MODEL), BF16),
        compiler_params=_cparams("parallel", "parallel"),
        name="gate_merge",
    )(x, a, b, c, w_gate, w_gate, w_gate, b_gate, b_gate, b_gate, w_branch)


def _layer_norm_rows(y, g, b):
    mu = jnp.mean(y, axis=-1, keepdims=True)
    yc = y - mu
    var = jnp.mean(yc * yc, axis=-1, keepdims=True)
    return yc * lax.rsqrt(var + LN_EPS) * g + b


def _mm_res_ln_kernel(a_ref, w_ref, res_ref, g_ref, b_ref, o_ref):
    y = jnp.dot(a_ref[...], w_ref[...], preferred_element_type=F32) + DN_ALPHA * res_ref[...]
    o_ref[...] = _layer_norm_rows(y, g_ref[...], b_ref[...])


def _mm_res_ln(a, w, res, g, b, tm):
    M, K = a.shape
    row = lambda i: (i, 0)
    fixed = lambda i: (0, 0)
    return pl.pallas_call(
        _mm_res_ln_kernel,
        grid=(M // tm,),
        in_specs=[pl.BlockSpec((tm, K), row), pl.BlockSpec((K, D_MODEL), fixed),
                  pl.BlockSpec((tm, D_MODEL), row), pl.BlockSpec((1, D_MODEL), fixed),
                  pl.BlockSpec((1, D_MODEL), fixed)],
        out_specs=pl.BlockSpec((tm, D_MODEL), row),
        out_shape=jax.ShapeDtypeStruct((M, D_MODEL), F32),
        compiler_params=_cparams("parallel"),
        name="matmul_residual_layernorm",
    )(a, w, res, g.reshape(1, -1), b.reshape(1, -1))


def _top_k_lanes(logits):
    lane_i = lax.broadcasted_iota(jnp.int32, logits.shape, 1)
    lane_f = lane_i.astype(F32)
    work = jnp.where(lane_i < N_EXPERTS, logits, -jnp.inf)
    out = jnp.zeros_like(logits)
    for k in range(TOP_K):
        m = jnp.max(work, axis=-1, keepdims=True)
        idx = jnp.min(jnp.where(work == m, lane_f, float(LANES)), axis=-1, keepdims=True)
        out = jnp.where(lane_i == k, m, out)
        out = jnp.where(lane_i == TOP_K + k, idx, out)
        work = jnp.where(lane_f == idx, -jnp.inf, work)
    return out


def _router_kernel(x_ref, w_ref, b_ref, o_ref):
    acc = jnp.dot(x_ref[...].astype(BF16), w_ref[...].astype(BF16), preferred_element_type=F32)
    o_ref[...] = _top_k_lanes(acc + b_ref[...])


def _router(x, w_pad, b_pad, tm):
    M = x.shape[0]
    NP = w_pad.shape[1]
    return pl.pallas_call(
        _router_kernel,
        grid=(M // tm,),
        in_specs=[pl.BlockSpec((tm, D_MODEL), lambda i: (i, 0)),
                  pl.BlockSpec((D_MODEL, NP), lambda i: (0, 0)),
                  pl.BlockSpec((1, NP), lambda i: (0, 0))],
        out_specs=pl.BlockSpec((tm, NP), lambda i: (i, 0)),
        out_shape=jax.ShapeDtypeStruct((M, NP), F32),
        compiler_params=_cparams("parallel"),
        name="router_logits",
    )(x, w_pad, b_pad)


def _bf16_row_interleave(a, b):
    a32 = lax.bitcast_convert_type(a.astype(BF16).astype(F32), jnp.uint32)
    b32 = lax.bitcast_convert_type(b.astype(BF16).astype(F32), jnp.uint32)
    word = (a32 >> 16) | (b32 & jnp.uint32(0xFFFF0000))
    return pltpu.bitcast(word, BF16)


STEP_VALID, STEP_FIRST, STEP_LAST, STEP_NEW_WEIGHTS = 1, 2, 4, 8


def _moe_kernel(se_ref, sf_ref, sx_ref, so_ref, sj_ref, flag_ref, x_ref, w1_ref, b1_ref, w2_ref, b2_ref, o_ref,
                acc_ref, xs_ref, w1b_ref, w2q_ref):
    s = pl.program_id(0)
    fc = w2_ref.shape[1] // 2
    bm = x_ref.shape[0]
    flags = flag_ref[s]

    @pl.when((flags & STEP_VALID) != 0)
    def _():
        j = sj_ref[s]
        first = (flags & STEP_FIRST) != 0
        last = (flags & STEP_LAST) != 0

        @pl.when((flags & STEP_NEW_WEIGHTS) != 0)
        def _():
            w1b_ref[...] = w1_ref[0].astype(BF16)
            w2 = w2_ref[0]
            w2q_ref[...] = _bf16_row_interleave(w2[:fc], w2[fc:])

        @pl.when(first)
        def _():
            xs_ref[j] = x_ref[...].astype(BF16)
            acc_ref[j] = jnp.broadcast_to(b2_ref[0], acc_ref.shape[1:])

        h = jnp.dot(xs_ref[j], w1b_ref[...], preferred_element_type=F32) + b1_ref[0]
        lane = lax.broadcasted_iota(jnp.int32, (bm, LANES), 1)
        even = (lane & 1) == 0
        acts = []
        for c in range(2 * fc // LANES):
            a = h[:, LANES * c:LANES * (c + 1)]
            b = h[:, 2 * fc + LANES * c:2 * fc + LANES * (c + 1)]
            glu = jnp.where(even, a, pltpu.roll(b, 1, axis=1))
            lin = jnp.where(even, pltpu.roll(a, LANES - 1, axis=1), b)
            glu = jnp.minimum(glu, SWIGLU_LIMIT)
            lin = jnp.clip(lin, -SWIGLU_LIMIT, SWIGLU_LIMIT)
            act = glu * (1.0 / (1.0 + jnp.exp(-SWIGLU_ALPHA * glu))) * (lin + 1.0)
            acts.append(act.astype(BF16))
        act = jnp.concatenate(acts, axis=1) if len(acts) > 1 else acts[0]
        contrib = jnp.dot(act, w2q_ref[...], preferred_element_type=F32)
        acc_ref[j] += contrib

        @pl.when(last)
        def _():
            o_ref[...] = acc_ref[j]


def _moe_steps(padded, p_end, n_valid, nb, bm, nf, unit):
    i32 = jnp.int32
    n_exp = padded.shape[0]
    blk = jnp.arange(nb, dtype=i32)
    valid_b = blk < n_valid
    e_b = jnp.minimum(jnp.sum(p_end[None, :] <= (blk * bm)[:, None], axis=1, dtype=i32), n_exp - 1)
    is_e = e_b[:, None] == jnp.arange(n_exp, dtype=i32)[None, :]
    look = lambda table: jnp.sum(jnp.where(is_e, table.astype(i32)[None, :], 0), axis=1, dtype=i32)
    q_b = blk - look((p_end - padded) // bm)
    j_b = q_b % unit
    b0_b = blk - j_b
    n_b = jnp.where(valid_b, jnp.minimum(unit, look(padded // bm) - (q_b - j_b)), 1)
    rep = lambda a: jnp.repeat(a, nf)
    step = jnp.arange(nb * nf, dtype=i32)
    valid, e, b0, n = rep(valid_b), rep(e_b), rep(b0_b), rep(n_b)
    r = step - nf * b0
    f = r // n
    j = r % n
    last_blk = n_valid - 1
    e_last = jnp.sum(jnp.where(blk == last_blk, e_b, 0), dtype=i32)
    se = jnp.where(valid, e, e_last)
    sf = jnp.where(valid, f, nf - 1)
    sx = jnp.where(valid, jnp.where(f == 0, b0 + j, b0 + n - 1), last_blk)
    so = jnp.where(valid, jnp.where(f == nf - 1, b0 + j, b0), last_blk)
    sj = jnp.where(valid, j, 0)
    flags = jnp.where(valid, STEP_VALID + STEP_FIRST * (f == 0) + STEP_LAST * (f == nf - 1)
                      + STEP_NEW_WEIGHTS * (j == 0), 0)
    return [a.astype(i32) for a in (se, sf, sx, so, sj, flags)]


def _moe_ffn_blocks(rows, steps, w1, b1, w2, b2, layer, bm, fc, unit):
    d_model = rows.shape[1]
    d_ff = w2.shape[2]
    nb = rows.shape[0] // bm
    nf = d_ff // (2 * fc)
    assert nf >= 2 and steps[0].shape[0] == nb * nf
    grid_spec = pltpu.PrefetchScalarGridSpec(
        num_scalar_prefetch=6,
        grid=(nb * nf,),
        in_specs=[
            pl.BlockSpec((bm, d_model), lambda s, se, sf, sx, so, sj, fl: (sx[s], 0)),
            pl.BlockSpec((None, 1, d_model, 4 * fc), lambda s, se, sf, sx, so, sj, fl: (layer, se[s], 0, sf[s])),
            pl.BlockSpec((None, 1, 1, 4 * fc), lambda s, se, sf, sx, so, sj, fl: (layer, se[s], 0, sf[s])),
            pl.BlockSpec((None, 1, 2 * fc, d_model), lambda s, se, sf, sx, so, sj, fl: (layer, se[s], sf[s], 0)),
            pl.BlockSpec((None, 1, 1, d_model), lambda s, se, sf, sx, so, sj, fl: (layer, se[s], 0, 0)),
        ],
        out_specs=pl.BlockSpec((bm, d_model), lambda s, se, sf, sx, so, sj, fl: (so[s], 0)),
        scratch_shapes=[pltpu.VMEM((unit, bm, d_model), F32), pltpu.VMEM((unit, bm, d_model), BF16),
                        pltpu.VMEM((d_model, 4 * fc), BF16), pltpu.VMEM((2 * fc, d_model), BF16)],
    )
    return pl.pallas_call(
        _moe_kernel,
        grid_spec=grid_spec,
        out_shape=jax.ShapeDtypeStruct(rows.shape, F32),
        compiler_params=_cparams("arbitrary"),
        name="moe_expert_ffn",
    )(*steps, rows, w1, b1, w2, b2)


def _combine_ln_kernel(*refs):
    g_refs = refs[:TOP_K]
    gate_ref, res_ref, lg_ref, lb_ref, o_ref = refs[TOP_K:]
    gate = gate_ref[...]
    ff = None
    for k in range(TOP_K):
        term = gate[:, k:k + 1] * g_refs[k][...]
        ff = term if ff is None else ff + term
    o_ref[...] = _layer_norm_rows(DN_ALPHA * res_ref[...] + ff, lg_ref[...], lb_ref[...])


def _combine_ln(gathered, gate, res, g, b, tm):
    T, d = res.shape
    nt = T // tm
    row = lambda i: (i, 0)
    fixed = lambda i: (0, 0)
    g_specs = [pl.BlockSpec((tm, d), functools.partial(lambda i, k: (k * nt + i, 0), k=k)) for k in range(TOP_K)]
    return pl.pallas_call(
        _combine_ln_kernel,
        grid=(nt,),
        in_specs=g_specs + [pl.BlockSpec((tm, TOP_K), row), pl.BlockSpec((tm, d), row),
                            pl.BlockSpec((1, d), fixed), pl.BlockSpec((1, d), fixed)],
        out_specs=pl.BlockSpec((tm, d), row),
        out_shape=jax.ShapeDtypeStruct((T, d), F32),
        compiler_params=_cparams("parallel"),
        name="moe_combine_layernorm",
    )(*([gathered] * TOP_K), gate, res, g.reshape(1, -1), b.reshape(1, -1))


def _moe_route(top_v, top_i, n_exp, bm):
    T = top_v.shape[0]
    i32 = jnp.int32
    gate = jax.nn.softmax(top_v, axis=-1)
    n_assign = T * TOP_K
    nb = -(-(n_assign + n_exp * (bm - 1)) // bm)
    ids = jnp.arange(n_assign, dtype=i32)
    e_sorted, order = lax.sort((top_i.reshape(-1).astype(i32), ids), num_keys=1, is_stable=True)
    is_e = e_sorted[:, None] == jnp.arange(n_exp, dtype=i32)[None, :]
    counts = jnp.sum(is_e, axis=0, dtype=i32)
    padded = (counts + bm - 1) // bm * bm
    p_end = jnp.cumsum(padded)
    u_start = jnp.cumsum(counts) - counts
    shift = (p_end - padded - u_start).astype(i32)
    dest = ids + jnp.sum(jnp.where(is_e, shift[None, :], 0), axis=1, dtype=i32)
    _, pos = lax.sort((order, dest), num_keys=1)
    src = jnp.zeros((nb * bm,), i32).at[dest].set(order // TOP_K, indices_are_sorted=True, unique_indices=True)
    n_valid = (p_end[-1] // bm).astype(i32)
    return gate, src, pos, (padded.astype(i32), p_end.astype(i32), n_valid, nb)


def _moe_ln(x2_p, x2_s, route_p, route_s, experts, layer, ln_g, ln_b):
    tp, ts = x2_p.shape[0], x2_s.shape[0]
    x2 = jnp.concatenate([x2_p, x2_s], axis=0)
    route = jnp.concatenate([route_p[:, :2 * TOP_K], route_s[:, :2 * TOP_K]], axis=0)
    top_v, top_i = route[:, :TOP_K], route[:, TOP_K:].astype(jnp.int32)
    gate, src, pos, (padded, p_end, n_valid, nb) = _moe_route(top_v, top_i, N_EXPERTS, MOE_BM)
    rows = jnp.take(x2, src, axis=0, mode='clip')
    w_e1, b_e1, w_e2, b_e2 = experts
    steps = _moe_steps(padded, p_end, n_valid, nb, MOE_BM, D_FF // (2 * MOE_FC), MOE_UNIT)
    out_rows = _moe_ffn_blocks(rows, steps, w_e1, b_e1, w_e2, b_e2, layer, MOE_BM, MOE_FC, MOE_UNIT)
    pos = pos.reshape(tp + ts, TOP_K)
    g_p = jnp.take(out_rows, pos[:tp].T.reshape(-1), axis=0, mode='clip')
    g_s = jnp.take(out_rows, pos[tp:].T.reshape(-1), axis=0, mode='clip')
    hp = _combine_ln(g_p, gate[:tp], x2_p, ln_g, ln_b, 256)
    hs = _combine_ln(g_s, gate[tp:], x2_s, ln_g, ln_b, ts)
    return hp, hs


def _half_lane_variants(t):
    lo = lax.broadcasted_iota(jnp.int32, t.shape, 1) < HD_A
    zero = jnp.zeros_like(t)
    tr = pltpu.roll(t, HD_A, axis=1)
    return [[jnp.where(lo, t, zero).astype(BF16), jnp.where(lo, zero, tr).astype(BF16)],
            [jnp.where(lo, tr, zero).astype(BF16), jnp.where(lo, zero, t).astype(BF16)]]


def _swa_kernel(sinks_ref, q_ref, kp_ref, kc_ref, vp_ref, vc_ref, bias_ref, o_ref):
    j = pl.program_id(1)
    kvar = _half_lane_variants(jnp.concatenate([kp_ref[...], kc_ref[...]], axis=0))
    vvar = _half_lane_variants(jnp.concatenate([vp_ref[...], vc_ref[...]], axis=0))
    col = lax.broadcasted_iota(jnp.int32, (WINDOW, 2 * WINDOW), 1)
    no_prev = col < jnp.where(j == 0, WINDOW, 0)
    for r in range(H_A // 2):
        g = (2 * r) // G_A
        qp = q_ref[:, LANES * r:LANES * (r + 1)].astype(BF16)
        acc = None
        for par in range(2):
            h = 2 * r + par
            s = lax.dot_general(qp, kvar[g][par], (((1,), (1,)), ((), ())), preferred_element_type=F32)
            s = s * HD_A ** -0.5 + bias_ref[h]
            s = jnp.where(no_prev, -jnp.inf, s)
            sk = sinks_ref[h]
            m = jnp.maximum(jnp.max(s, axis=-1, keepdims=True), sk)
            pr = jnp.exp(s - m)
            den = jnp.sum(pr, axis=-1, keepdims=True) + jnp.exp(sk - m)
            o = jnp.dot((pr / den).astype(BF16), vvar[g][par], preferred_element_type=F32)
            acc = o if acc is None else acc + o
        o_ref[:, LANES * r:LANES * (r + 1)] = acc.astype(o_ref.dtype)


def _swa_prompt_bias(rel_bias):
    i = jnp.arange(WINDOW)[:, None]
    j = jnp.arange(2 * WINDOW)[None, :]
    dist = i + WINDOW - j
    return jnp.where((dist >= 0) & (dist < WINDOW), _bucket_bias(rel_bias, dist), -jnp.inf)


def _swa_prompt_call(proj2d, bias, sinks, bsz, L):
    nb = L // WINDOW
    kcol, vcol = COL_K // LANES, COL_V // LANES
    cur = lambda b, j: b * nb + j
    prev = lambda b, j: b * nb + jnp.maximum(j - 1, 0)
    return pl.pallas_call(
        _swa_kernel,
        grid=(bsz, nb),
        in_specs=[
            pl.BlockSpec(memory_space=pltpu.SMEM),
            pl.BlockSpec((WINDOW, H_A * HD_A), lambda b, j: (cur(b, j), COL_Q // (H_A * HD_A))),
            pl.BlockSpec((WINDOW, LANES), lambda b, j: (prev(b, j), kcol)),
            pl.BlockSpec((WINDOW, LANES), lambda b, j: (cur(b, j), kcol)),
            pl.BlockSpec((WINDOW, LANES), lambda b, j: (prev(b, j), vcol)),
            pl.BlockSpec((WINDOW, LANES), lambda b, j: (cur(b, j), vcol)),
            pl.BlockSpec((H_A, WINDOW, 2 * WINDOW), lambda b, j: (0, 0, 0)),
        ],
        out_specs=pl.BlockSpec((WINDOW, H_A * HD_A), lambda b, j: (cur(b, j), 0)),
        out_shape=jax.ShapeDtypeStruct((bsz * L, H_A * HD_A), BF16),
        compiler_params=_cparams("parallel", "arbitrary"),
        name="swa_prompt",
    )(sinks, proj2d, proj2d, proj2d, proj2d, proj2d, bias)


def _gelu(x):
    return 0.5 * x * (1.0 + lax.erf(x * np.float32(np.sqrt(0.5))))


def _gmlp_kernel(u_ref, gv_ref, lng_ref, lnb_ref, w_ref, bias_ref, o_ref):
    u = _gelu(u_ref[...])
    gv = _layer_norm_rows(_gelu(gv_ref[...]), lng_ref[...], lnb_ref[...])
    lo = lax.broadcasted_iota(jnp.int32, (CHUNK_B, LANES), 1) < CG_B
    for r in range(GB // 2):
        vp = gv[:, LANES * r:LANES * (r + 1)]
        zero = jnp.zeros_like(vp)
        mix = jnp.dot(w_ref[2 * r], jnp.where(lo, vp, zero).astype(BF16), preferred_element_type=F32)
        mix += jnp.dot(w_ref[2 * r + 1], jnp.where(lo, zero, vp).astype(BF16), preferred_element_type=F32)
        sl = slice(LANES * r, LANES * (r + 1))
        o_ref[:, sl] = (u[:, sl] * (mix + bias_ref[:, sl])).astype(o_ref.dtype)


def _gmlp_prompt_call(proj2d, p, n_rows):
    w = (p['gmlp_ws'] * jnp.tril(jnp.ones((CHUNK_B, CHUNK_B), F32))).astype(BF16)
    bias = jnp.repeat(p['gmlp_bs'].T, CG_B, axis=1)
    fixed2 = lambda i: (0, 0)
    return pl.pallas_call(
        _gmlp_kernel,
        grid=(n_rows // CHUNK_B,),
        in_specs=[
            pl.BlockSpec((CHUNK_B, W_B), lambda i: (i, COL_U // W_B)),
            pl.BlockSpec((CHUNK_B, W_B), lambda i: (i, COL_GV // W_B)),
            pl.BlockSpec((1, W_B), fixed2), pl.BlockSpec((1, W_B), fixed2),
            pl.BlockSpec((GB, CHUNK_B, CHUNK_B), lambda i: (0, 0, 0)),
            pl.BlockSpec((CHUNK_B, W_B), fixed2),
        ],
        out_specs=pl.BlockSpec((CHUNK_B, W_B), lambda i: (i, 0)),
        out_shape=jax.ShapeDtypeStruct((n_rows, W_B), BF16),
        compiler_params=_cparams("parallel"),
        name="gmlp_prompt",
    )(proj2d, proj2d, p['gmlp_ln_g'].reshape(1, -1), p['gmlp_ln_b'].reshape(1, -1), w, bias)


def _silu(x):
    return x * (1.0 / (1.0 + jnp.exp(-x)))


def _softplus(x):
    return jnp.maximum(x, 0.0) + jnp.log1p(jnp.exp(-jnp.abs(x)))


def _causal_conv_chunk(cur, tail, w, bias):
    rows = lax.broadcasted_iota(jnp.int32, (8, cur.shape[1]), 0)
    y = jnp.broadcast_to(bias, cur.shape)
    y_head = jnp.broadcast_to(bias, (8, cur.shape[1]))
    for t in range(CONV_W):
        k = CONV_W - 1 - t
        wt = w[t:t + 1, :]
        if k == 0:
            y = y + cur * wt
            y_head = y_head + cur[:8] * wt
        else:
            sh = pltpu.roll(cur, k, axis=0)
            y = y + sh * wt
            y_head = y_head + jnp.where(rows < k, pltpu.roll(tail, k, axis=0), sh[:8]) * wt
    return jnp.concatenate([y_head, y[8:]], axis=0)


def _bf16_split3(x):
    p1 = x.astype(BF16)
    r1 = x - p1.astype(F32)
    p2 = r1.astype(BF16)
    p3 = (r1 - p2.astype(F32)).astype(BF16)
    return p1, p2, p3


def _ssd_kernel(xs_ref, bc_ref, z_ref, dt_ref, cwx_ref, cbx_ref, cwb_ref, cbb_ref, dtb_ref, a_ref,
                dsk_ref, ng_ref, y_ref, h_ref, state_ref, tailx_ref, tailb_ref):
    c = pl.program_id(1)
    C = SSD_CHUNK

    @pl.when(c == 0)
    def _():
        state_ref[...] = jnp.zeros_like(state_ref)
        tailx_ref[...] = jnp.zeros_like(tailx_ref)
        tailb_ref[...] = jnp.zeros_like(tailb_ref)

    xs_raw = xs_ref[...]
    bc_raw = bc_ref[...]
    xs = _silu(_causal_conv_chunk(xs_raw, tailx_ref[...], cwx_ref[...], cbx_ref[...]))
    bc = _silu(_causal_conv_chunk(bc_raw, tailb_ref[...], cwb_ref[...], cbb_ref[...]))
    tailx_ref[...] = xs_raw[C - 8:]
    tailb_ref[...] = bc_raw[C - 8:]

    dt = _softplus(dt_ref[...] + dtb_ref[...])
    da = dt * a_ref[...]
    row_i = lax.broadcasted_iota(jnp.int32, (C, C), 0)
    col_i = lax.broadcasted_iota(jnp.int32, (C, C), 1)
    causal = row_i >= col_i
    tril = jnp.where(causal, 1.0, 0.0).astype(BF16)
    acs = None
    for piece in _bf16_split3(da):
        t = jnp.dot(tril, piece, preferred_element_type=F32)
        acs = t if acs is None else acs + t
    acs_t = acs.T
    exp_acs = jnp.exp(acs)
    end_decay = jnp.exp(acs[C - 1:C, :] - acs)
    chunk_decay = jnp.exp(acs[C - 1:C, :])

    lo = lax.broadcasted_iota(jnp.int32, (C, LANES), 1) < P_C
    bm = [bc[:, N_C * g:N_C * (g + 1)].astype(BF16) for g in range(G_C)]
    cm = [bc[:, N_C * (G_C + g):N_C * (G_C + g + 1)].astype(BF16) for g in range(G_C)]
    cb = [lax.dot_general(cm[g], bm[g], (((1,), (1,)), ((), ())), preferred_element_type=F32)
          for g in range(G_C)]

    def per_lane_half(t, r):
        return jnp.where(lo, t[:, 2 * r:2 * r + 1], t[:, 2 * r + 1:2 * r + 2])

    ys = []
    for r in range(H_C // 2):
        g = (2 * r) // R_C
        sl = slice(LANES * r, LANES * (r + 1))
        x_pair = xs[:, sl]
        xdt = x_pair * per_lane_half(dt, r)
        zero = jnp.zeros_like(xdt)
        y_pair = None
        for par in range(2):
            h = 2 * r + par
            seg = acs[:, h:h + 1] - acs_t[h:h + 1, :]
            decay = jnp.where(causal, jnp.exp(seg), 0.0)
            m_h = (cb[g] * decay).astype(BF16)
            x_h = (jnp.where(lo, xdt, zero) if par == 0 else jnp.where(lo, zero, xdt)).astype(BF16)
            t = jnp.dot(m_h, x_h, preferred_element_type=F32)
            y_pair = t if y_pair is None else y_pair + t
        st = state_ref[sl, :]
        y_off = lax.dot_general(cm[g], st.astype(BF16), (((1,), (1,)), ((), ())), preferred_element_type=F32)
        y_pair = y_pair + y_off * per_lane_half(exp_acs, r)
        upd = lax.dot_general((xdt * per_lane_half(end_decay, r)).astype(BF16), bm[g],
                              (((0,), (0,)), ((), ())), preferred_element_type=F32)
        cd = jnp.concatenate([jnp.broadcast_to(chunk_decay[:, 2 * r:2 * r + 1], (P_C, N_C)),
                              jnp.broadcast_to(chunk_decay[:, 2 * r + 1:2 * r + 2], (P_C, N_C))], axis=0)
        state_ref[sl, :] = st * cd + upd
        ys.append(y_pair + dsk_ref[:, sl] * x_pair)
    y = jnp.concatenate(ys, axis=1) * _silu(z_ref[...])
    gw = D_INNER // G_C
    outs = []
    for g in range(G_C):
        yg = y[:, gw * g:gw * (g + 1)]
        outs.append(yg * lax.rsqrt(jnp.mean(yg * yg, axis=-1, keepdims=True) + RMS_EPS))
    y_ref[...] = (jnp.concatenate(outs, axis=1) * ng_ref[...]).astype(y_ref.dtype)

    @pl.when(c == pl.num_programs(1) - 1)
    def _():
        h_ref[...] = state_ref[...]


def _ssd_prompt_call(proj2d, p, bsz, L):
    nc = L // SSD_CHUNK
    row = lambda blk: (lambda b, c: (b * nc + c, blk))
    fixed = lambda b, c: (0, 0)
    pad_l = lambda v: jnp.pad(v.astype(F32), (0, LANES - H_C)).reshape(1, LANES)
    cw, cbias = p['conv_w'], p['conv_b'].reshape(1, -1)
    nbc = 2 * G_C * N_C
    args = (proj2d, proj2d, proj2d, proj2d,
            cw[:, :D_INNER], cbias[:, :D_INNER], cw[:, D_INNER:], cbias[:, D_INNER:],
            pad_l(p['dt_bias']), pad_l(-jnp.exp(p['a_log'].astype(F32))),
            jnp.repeat(p['d_skip'].astype(F32), P_C).reshape(1, -1), p['ssm_norm_g'].reshape(1, -1))
    return pl.pallas_call(
        _ssd_kernel,
        grid=(bsz, nc),
        in_specs=[
            pl.BlockSpec((SSD_CHUNK, D_INNER), row(COL_XBC // D_INNER)),
            pl.BlockSpec((SSD_CHUNK, nbc), row((COL_XBC + D_INNER) // nbc)),
            pl.BlockSpec((SSD_CHUNK, D_INNER), row(COL_Z // D_INNER)),
            pl.BlockSpec((SSD_CHUNK, LANES), row(COL_DT // LANES)),
            pl.BlockSpec((CONV_W, D_INNER), fixed), pl.BlockSpec((1, D_INNER), fixed),
            pl.BlockSpec((CONV_W, nbc), fixed), pl.BlockSpec((1, nbc), fixed),
            pl.BlockSpec((1, LANES), fixed), pl.BlockSpec((1, LANES), fixed),
            pl.BlockSpec((1, D_INNER), fixed), pl.BlockSpec((1, D_INNER), fixed),
        ],
        out_specs=[pl.BlockSpec((SSD_CHUNK, D_INNER), lambda b, c: (b * nc + c, 0)),
                   pl.BlockSpec((None, H_C * P_C, N_C), lambda b, c: (b, 0, 0))],
        out_shape=[jax.ShapeDtypeStruct((bsz * L, D_INNER), BF16),
                   jax.ShapeDtypeStruct((bsz, H_C * P_C, N_C), F32)],
        scratch_shapes=[pltpu.VMEM((H_C * P_C, N_C), F32), pltpu.VMEM((8, D_INNER), F32),
                        pltpu.VMEM((8, nbc), F32)],
        compiler_params=_cparams("parallel", "arbitrary"),
        name="ssd_prompt",
    )(*args)


def _bf16_round(x):
    return x.astype(BF16).astype(F32)


SWA_DECODE_TOKENS = 8


def _swa_decode_kernel(q_ref, kn_ref, vn_ref, ck_ref, cv_ref, bias_ref, sink_ref, o_ref, wk_ref, wv_ref):
    W = WINDOW
    last = lax.broadcasted_iota(jnp.int32, (W, LANES), 0) == W - 1
    lo = lax.broadcasted_iota(jnp.int32, (8, LANES), 1) < HD_A
    g0 = lax.broadcasted_iota(jnp.int32, (8, LANES), 0) < G_A // 2
    zero = jnp.zeros((8, LANES), F32)
    nt_dims = (((1,), (1,)), ((), ()))
    for t in range(q_ref.shape[0]):
        kw = jnp.where(last, jnp.broadcast_to(kn_ref[t], (W, LANES)), pltpu.roll(ck_ref[t], W - 1, axis=0))
        vw = jnp.where(last, jnp.broadcast_to(vn_ref[t], (W, LANES)), pltpu.roll(cv_ref[t], W - 1, axis=0))
        wk_ref[t] = kw
        wv_ref[t] = vw
        kb, vb = kw.astype(BF16), vw.astype(BF16)
        q = q_ref[t]
        qr = pltpu.roll(q, HD_A, axis=1)
        q_par = [jnp.where(g0, jnp.where(lo, q, zero), jnp.where(lo, zero, qr)),
                 jnp.where(g0, jnp.where(lo, qr, zero), jnp.where(lo, zero, q))]
        outs = []
        for par in range(2):
            s = lax.dot_general(q_par[par].astype(BF16), kb, nt_dims, preferred_element_type=F32)
            s = s * HD_A ** -0.5 + bias_ref[par]
            sk = sink_ref[par]
            m = jnp.maximum(jnp.max(s, axis=-1, keepdims=True), sk)
            pr = jnp.exp(s - m)
            den = jnp.sum(pr, axis=-1, keepdims=True) + jnp.exp(sk - m)
            outs.append(jnp.dot((pr / den).astype(BF16), vb, preferred_element_type=F32))
        o_even = jnp.where(g0, outs[0], pltpu.roll(outs[0], HD_A, axis=1))
        o_odd = jnp.where(g0, pltpu.roll(outs[1], HD_A, axis=1), outs[1])
        o_ref[t] = jnp.where(lo, o_even, o_odd).astype(o_ref.dtype)


def _swa_decode_call(proj3, cache_k, cache_v, layer, rel_bias, sinks):
    bsz = proj3.shape[0]
    nt = SWA_DECODE_TOKENS
    ck = cache_k.reshape(cache_k.shape[0], bsz, WINDOW, LANES)
    cv = cache_v.reshape(cache_v.shape[0], bsz, WINDOW, LANES)
    q8 = proj3[:, 0, COL_Q:COL_Q + H_A * HD_A].reshape(bsz, H_A // 2, LANES)
    dist = WINDOW - 1 - jnp.arange(WINDOW)
    bias = _bucket_bias(rel_bias, dist).reshape(H_A // 2, 2, WINDOW).transpose(1, 0, 2)
    sink = sinks.astype(F32).reshape(H_A // 2, 2, 1).transpose(1, 0, 2)
    new_tok = lambda blk: (lambda i: (i, 0, blk))
    cache = lambda i: (layer, i, 0, 0)
    tok3 = lambda i: (i, 0, 0)
    out, wk, wv = pl.pallas_call(
        _swa_decode_kernel,
        grid=(bsz // nt,),
        in_specs=[
            pl.BlockSpec((nt, H_A // 2, LANES), tok3),
            pl.BlockSpec((nt, 1, LANES), new_tok(COL_K // LANES)),
            pl.BlockSpec((nt, 1, LANES), new_tok(COL_V // LANES)),
            pl.BlockSpec((None, nt, WINDOW, LANES), cache),
            pl.BlockSpec((None, nt, WINDOW, LANES), cache),
            pl.BlockSpec((2, H_A // 2, WINDOW), lambda i: (0, 0, 0)),
            pl.BlockSpec((2, H_A // 2, 1), lambda i: (0, 0, 0)),
        ],
        out_specs=[pl.BlockSpec((nt, H_A // 2, LANES), tok3),
                   pl.BlockSpec((nt, WINDOW, LANES), tok3),
                   pl.BlockSpec((nt, WINDOW, LANES), tok3)],
        out_shape=[jax.ShapeDtypeStruct((bsz, H_A // 2, LANES), BF16),
                   jax.ShapeDtypeStruct((bsz, WINDOW, LANES), F32),
                   jax.ShapeDtypeStruct((bsz, WINDOW, LANES), F32)],
        compiler_params=_cparams("parallel"),
        name="swa_sample",
    )(q8, proj3, proj3, ck, cv, bias, sink)
    return out.reshape(bsz, H_A * HD_A), wk, wv


def _gmlp_step_kernel(u_ref, gv_ref, lng_ref, lnb_ref, w0_ref, b0_ref, o_ref, gv_out_ref):
    gv = _layer_norm_rows(_gelu(gv_ref[...]), lng_ref[...], lnb_ref[...])
    gv_out_ref[...] = gv
    mix = _bf16_round(w0_ref[...]) * _bf16_round(gv) + b0_ref[...]
    o_ref[...] = (_gelu(u_ref[...]) * mix).astype(o_ref.dtype)


def _gmlp_step_call(proj2d, p):
    n = proj2d.shape[0]
    w0 = jnp.repeat(p['gmlp_ws'][:, 0, 0], CG_B).reshape(1, -1)
    b0 = jnp.repeat(p['gmlp_bs'][:, 0], CG_B).reshape(1, -1)
    fixed = lambda i: (0, 0)
    return pl.pallas_call(
        _gmlp_step_kernel,
        grid=(1,),
        in_specs=[pl.BlockSpec((n, W_B), lambda i: (0, COL_U // W_B)),
                  pl.BlockSpec((n, W_B), lambda i: (0, COL_GV // W_B)),
                  pl.BlockSpec((1, W_B), fixed), pl.BlockSpec((1, W_B), fixed),
                  pl.BlockSpec((1, W_B), fixed), pl.BlockSpec((1, W_B), fixed)],
        out_specs=[pl.BlockSpec((n, W_B), fixed), pl.BlockSpec((n, W_B), fixed)],
        out_shape=[jax.ShapeDtypeStruct((n, W_B), BF16), jax.ShapeDtypeStruct((n, W_B), F32)],
        compiler_params=_cparams("arbitrary"),
        name="gmlp_sample",
    )(proj2d, proj2d, p['gmlp_ln_g'].reshape(1, -1), p['gmlp_ln_b'].reshape(1, -1), w0, b0)


def _conv_step(st, cur, w, bias):
    y = bias
    for t in range(CONV_W - 1):
        y = y + st[t:t + 1, :] * w[t:t + 1, :]
    return y + cur * w[CONV_W - 1:CONV_W, :]


def _ssd_step_kernel(xs_ref, bc_ref, z_ref, dt_ref, stx_ref, stb_ref, h0_ref, cwx_ref, cbx_ref, cwb_ref, cbb_ref,
                     dtb_ref, a_ref, dsk_ref, ng_ref, y_ref, h_ref, ncx_ref, ncb_ref):
    xs_raw, bc_raw = xs_ref[...], bc_ref[...]
    stx, stb = stx_ref[...], stb_ref[...]
    ncx_ref[0:CONV_W - 2, :] = stx[1:CONV_W - 1]
    ncx_ref[CONV_W - 2:CONV_W - 1, :] = xs_raw
    ncb_ref[0:CONV_W - 2, :] = stb[1:CONV_W - 1]
    ncb_ref[CONV_W - 2:CONV_W - 1, :] = bc_raw
    xs = _silu(_conv_step(stx, xs_raw, cwx_ref[...], cbx_ref[...]))
    bc = _silu(_conv_step(stb, bc_raw, cwb_ref[...], cbb_ref[...]))
    dt = _softplus(dt_ref[...] + dtb_ref[...])
    decay = jnp.exp(dt * a_ref[...])
    xdt = _bf16_round(xs * dt)
    gw = D_INNER // G_C
    first_group = lax.broadcasted_iota(jnp.int32, (1, D_INNER), 1) < gw
    bm = [_bf16_round(bc[:, N_C * g:N_C * (g + 1)]) for g in range(G_C)]
    cm = [_bf16_round(bc[:, N_C * (G_C + g):N_C * (G_C + g + 1)]) for g in range(G_C)]
    cb = [_bf16_round(jnp.sum(cm[g] * bm[g], axis=1, keepdims=True)) for g in range(G_C)]
    y_diag = jnp.where(first_group, cb[0], cb[1]) * xdt
    h0 = h0_ref[...]
    y_off = jnp.concatenate([
        lax.dot_general(jnp.broadcast_to(cm[g], (8, N_C)).astype(BF16), h0[gw * g:gw * (g + 1)].astype(BF16),
                        (((1,), (1,)), ((), ())), preferred_element_type=F32)[0:1] for g in range(G_C)], axis=1)
    y = y_diag + y_off * decay
    y = (y + dsk_ref[...] * xs) * _silu(z_ref[...])
    outs = []
    for g in range(G_C):
        yg = y[:, gw * g:gw * (g + 1)]
        outs.append(yg * lax.rsqrt(jnp.mean(yg * yg, axis=-1, keepdims=True) + RMS_EPS))
    y_ref[...] = (jnp.concatenate(outs, axis=1) * ng_ref[...]).astype(y_ref.dtype)
    decay_rows = jnp.broadcast_to(decay, (LANES, D_INNER)).T
    xdt_rows = jnp.broadcast_to(xdt, (LANES, D_INNER)).T
    rows = lax.broadcasted_iota(jnp.int32, (D_INNER, N_C), 0)
    bm_rows = jnp.where(rows < gw, jnp.broadcast_to(bm[0], (D_INNER, N_C)), jnp.broadcast_to(bm[1], (D_INNER, N_C)))
    h_ref[...] = h0 * decay_rows + xdt_rows * bm_rows


def _ssd_step_call(proj3, state_conv, state_ssm, layer, p):
    bsz = proj3.shape[0]
    nbc = 2 * G_C * N_C
    ssm = state_ssm.reshape(state_ssm.shape[0], bsz, H_C * P_C, N_C)
    dt_lanes = jnp.repeat(proj3[:, :, COL_DT:COL_DT + H_C], P_C, axis=-1)
    per_lane = lambda v: jnp.repeat(v.astype(F32), P_C).reshape(1, -1)
    cw, cbias = p['conv_w'], p['conv_b'].reshape(1, -1)
    tok = lambda blk: (lambda i: (i, 0, blk))
    fixed = lambda i: (0, 0)
    return pl.pallas_call(
        _ssd_step_kernel,
        grid=(bsz,),
        in_specs=[
            pl.BlockSpec((None, 1, D_INNER), tok(COL_XBC // D_INNER)),
            pl.BlockSpec((None, 1, nbc), tok((COL_XBC + D_INNER) // nbc)),
            pl.BlockSpec((None, 1, D_INNER), tok(COL_Z // D_INNER)),
            pl.BlockSpec((None, 1, D_INNER), tok(0)),
            pl.BlockSpec((None, None, CONV_W - 1, D_INNER), lambda i: (layer, i, 0, 0)),
            pl.BlockSpec((None, None, CONV_W - 1, nbc), lambda i: (layer, i, 0, D_INNER // nbc)),
            pl.BlockSpec((None, None, H_C * P_C, N_C), lambda i: (layer, i, 0, 0)),
            pl.BlockSpec((CONV_W, D_INNER), fixed), pl.BlockSpec((1, D_INNER), fixed),
            pl.BlockSpec((CONV_W, nbc), fixed), pl.BlockSpec((1, nbc), fixed),
            pl.BlockSpec((1, D_INNER), fixed), pl.BlockSpec((1, D_INNER), fixed),
            pl.BlockSpec((1, D_INNER), fixed), pl.BlockSpec((1, D_INNER), fixed),
        ],
        out_specs=[pl.BlockSpec((None, 1, D_INNER), lambda i: (i, 0, 0)),
                   pl.BlockSpec((None, H_C * P_C, N_C), lambda i: (i, 0, 0)),
                   pl.BlockSpec((None, CONV_W - 1, D_INNER), lambda i: (i, 0, 0)),
                   pl.BlockSpec((None, CONV_W - 1, nbc), lambda i: (i, 0, 0))],
        out_shape=[jax.ShapeDtypeStruct((bsz, 1, D_INNER), BF16),
                   jax.ShapeDtypeStruct((bsz, H_C * P_C, N_C), F32),
                   jax.ShapeDtypeStruct((bsz, CONV_W - 1, D_INNER), F32),
                   jax.ShapeDtypeStruct((bsz, CONV_W - 1, nbc), F32)],
        compiler_params=_cparams("parallel"),
        name="ssd_sample",
    )(proj3, proj3, proj3, dt_lanes, state_conv, state_conv, ssm,
      cw[:, :D_INNER], cbias[:, :D_INNER], cw[:, D_INNER:], cbias[:, D_INNER:],
      per_lane(p['dt_bias']), per_lane(-jnp.exp(p['a_log'].astype(F32))), per_lane(p['d_skip']),
      p['ssm_norm_g'].reshape(1, -1))


def _xattn_kernel(x1_ref, wq_ref, mk_ref, mv_ref, wo_ref, g_ref, b_ref, wr_ref, br_ref, x2_ref, lg_ref):
    x1 = x1_ref[...]
    q = jnp.dot(x1.astype(BF16), wq_ref[...], preferred_element_type=F32)
    outs = []
    for h in range(XH):
        sl = slice(XHD * h, XHD * (h + 1))
        s = lax.dot_general(q[:, sl].astype(BF16), mk_ref[:, sl].astype(BF16),
                            (((1,), (1,)), ((), ())), preferred_element_type=F32) * XHD ** -0.5
        e = jnp.exp(s - jnp.max(s, axis=-1, keepdims=True))
        w = e / jnp.sum(e, axis=-1, keepdims=True)
        outs.append(jnp.dot(w.astype(BF16), mv_ref[:, sl].astype(BF16), preferred_element_type=F32).astype(BF16))
    o = jnp.concatenate(outs, axis=1)
    y = jnp.dot(o, wo_ref[...], preferred_element_type=F32) + DN_ALPHA * x1
    x2 = _layer_norm_rows(y, g_ref[...], b_ref[...])
    x2_ref[...] = x2
    lg_ref[...] = _top_k_lanes(jnp.dot(x2.astype(BF16), wr_ref[...], preferred_element_type=F32) + br_ref[...])


def _xattn_prompt_call(x1, mkv, lw, g, b, bsz, L, tm):
    nt = L // tm
    xw = XH * XHD
    row = lambda bb, i: (bb * nt + i, 0)
    fixed = lambda bb, i: (0, 0)
    n_lg = lw['w_router'].shape[1]
    return pl.pallas_call(
        _xattn_kernel,
        grid=(bsz, nt),
        in_specs=[
            pl.BlockSpec((tm, D_MODEL), row),
            pl.BlockSpec((D_MODEL, xw), fixed),
            pl.BlockSpec((N_MEM, xw), lambda bb, i: (bb, 0)),
            pl.BlockSpec((N_MEM, xw), lambda bb, i: (bb, 1)),
            pl.BlockSpec((xw, D_MODEL), fixed),
            pl.BlockSpec((1, D_MODEL), fixed), pl.BlockSpec((1, D_MODEL), fixed),
            pl.BlockSpec((D_MODEL, n_lg), fixed), pl.BlockSpec((1, n_lg), fixed),
        ],
        out_specs=[pl.BlockSpec((tm, D_MODEL), row), pl.BlockSpec((tm, n_lg), row)],
        out_shape=[jax.ShapeDtypeStruct((bsz * L, D_MODEL), F32), jax.ShapeDtypeStruct((bsz * L, n_lg), F32)],
        compiler_params=_cparams("parallel", "parallel"),
        name="memory_attention_prompt",
    )(x1, lw['w_xq'], mkv, mkv, lw['w_xo'], g.reshape(1, -1), b.reshape(1, -1),
      lw['w_router_bf16'], lw['b_router'])


def _xattn_decode_kernel(q_ref, k_ref, v_ref, o_ref):
    q = q_ref[...].astype(BF16).astype(F32)
    k = k_ref[...].astype(BF16).astype(F32)
    v = v_ref[...].astype(BF16).astype(F32)
    prod = k * q
    outs = []
    for h in range(XH):
        sl = slice(XHD * h, XHD * (h + 1))
        s = jnp.sum(prod[:, sl], axis=1, keepdims=True) * XHD ** -0.5
        e = jnp.exp(s - jnp.max(s, axis=0, keepdims=True))
        w = (e / jnp.sum(e, axis=0, keepdims=True)).astype(BF16).astype(F32)
        outs.append(jnp.sum(w * v[:, sl], axis=0, keepdims=True))
    o_ref[...] = jnp.concatenate(outs, axis=1)


def _xattn_decode_call(q, cache_k, cache_v, layer):
    bsz, xw = q.shape
    ck = cache_k.reshape(cache_k.shape[0], bsz, N_MEM, xw)
    cv = cache_v.reshape(cache_v.shape[0], bsz, N_MEM, xw)
    out = pl.pallas_call(
        _xattn_decode_kernel,
        grid=(bsz,),
        in_specs=[pl.BlockSpec((None, 1, xw), lambda i: (i, 0, 0)),
                  pl.BlockSpec((None, None, N_MEM, xw), lambda i: (layer, i, 0, 0)),
                  pl.BlockSpec((None, None, N_MEM, xw), lambda i: (layer, i, 0, 0))],
        out_specs=pl.BlockSpec((None, 1, xw), lambda i: (i, 0, 0)),
        out_shape=jax.ShapeDtypeStruct((bsz, 1, xw), F32),
        compiler_params=_cparams("parallel"),
        name="memory_attention_sample",
    )(q.reshape(bsz, 1, xw), ck, cv)
    return out.reshape(bsz, xw)


def _t5_bucket(dist):
    n = jnp.maximum(dist, 0)
    exact = N_BUCKETS // 2
    nf = jnp.maximum(n, 1).astype(F32)
    large = exact + (jnp.log(nf / exact) / math.log(MAX_DIST / exact) * (N_BUCKETS - exact)).astype(jnp.int32)
    return jnp.where(n < exact, n, jnp.minimum(large, N_BUCKETS - 1))


def _bucket_bias(rel_bias, dist):
    onehot = (_t5_bucket(dist)[..., None] == jnp.arange(N_BUCKETS)).astype(F32)
    return jnp.einsum('...b,bh->h...', onehot, rel_bias.astype(F32), precision=lax.Precision.HIGHEST)


def _prompt_layer(x, mkv, lw, p, swa_bias, bsz, L):
    T = bsz * L
    xb = x.astype(BF16)
    proj = _matmul(xb, lw['w_in'], 1024, 512)
    a_out = _swa_prompt_call(proj, swa_bias, p['sinks'], bsz, L)
    b_out = _gmlp_prompt_call(proj, p, T)
    c_out, h_last = _ssd_prompt_call(proj, p, bsz, L)
    mixed = _gate_merge(xb, a_out, b_out, c_out, lw['w_gate'], lw['b_gate'], lw['w_branch'], 512, 512)
    x1 = _mm_res_ln(mixed, lw['w_o'], x, p['ln1_g'], p['ln1_b'], 512)
    x2, logits = _xattn_prompt_call(x1, mkv, lw, p['ln2_g'], p['ln2_b'], bsz, L, 512)
    proj3 = proj.reshape(bsz, L, IN_DIM_PAD)
    win_k = proj3[:, L - WINDOW:, COL_K:COL_K + KV_A * HD_A].reshape(bsz, WINDOW, KV_A, HD_A)
    win_v = proj3[:, L - WINDOW:, COL_V:COL_V + KV_A * HD_A].reshape(bsz, WINDOW, KV_A, HD_A)
    conv = proj3[:, L - (CONV_W - 1):, COL_XBC:COL_XBC + CONV_DIM]
    return x2, logits, (win_k, win_v, conv, h_last.reshape(bsz, H_C, P_C, N_C))


def _sample_layer(x, layer, lw, p, rel_bias, cache_win_k, cache_win_v, state_conv, state_ssm, cache_mem_k, cache_mem_v):
    bsz = x.shape[0]
    xb = x.astype(BF16)
    proj = _matmul(xb, lw['w_in'], bsz, 512)
    proj3 = proj.reshape(bsz, 1, IN_DIM_PAD)
    a_out, win_k, win_v = _swa_decode_call(proj3, cache_win_k, cache_win_v, layer, rel_bias, p['sinks'])
    b_out, gv = _gmlp_step_call(proj, p)
    c_out, ssm, conv_x, conv_bc = _ssd_step_call(proj3, state_conv, state_ssm, layer, p)
    mixed = _gate_merge(xb, a_out, b_out, c_out.reshape(bsz, BR_W),
                        lw['w_gate'], lw['b_gate'], lw['w_branch'], bsz, 512)
    x1 = _mm_res_ln(mixed, lw['w_o'], x, p['ln1_g'], p['ln1_b'], bsz)
    q = _matmul(x1.astype(BF16), lw['w_xq'], bsz, XH * XHD)
    o = _xattn_decode_call(q, cache_mem_k, cache_mem_v, layer)
    x2 = _mm_res_ln(o.astype(BF16), lw['w_xo'], x1, p['ln2_g'], p['ln2_b'], bsz)
    logits = _router(x2, lw['w_router'], lw['b_router'], bsz)
    states = (win_k.reshape(bsz, WINDOW, KV_A, HD_A), win_v.reshape(bsz, WINDOW, KV_A, HD_A),
              jnp.concatenate([conv_x, conv_bc], axis=-1), ssm.reshape(bsz, H_C, P_C, N_C),
              gv.reshape(bsz, 1, W_B))
    return x2, logits, states


def kernel(x_prompt, x_sample, mem_prompt, cache_win_k, cache_win_v, state_conv, state_ssm, cache_mem_k, cache_mem_v, w_in, rel_bias, sinks, gmlp_ln_g, gmlp_ln_b, gmlp_ws, gmlp_bs, conv_w, conv_b, dt_bias, a_log, d_skip, ssm_norm_g, w_branch, w_gate, b_gate, w_o, ln1_g, ln1_b, w_xq, w_xk, w_xv, w_xo, ln2_g, ln2_b, w_router, b_router, w_e1, b_e1, w_e2, b_e2, ln3_g, ln3_b):
    assert cache_win_k.shape[2] == WINDOW and x_sample.shape[1] == 1
    n_prompt, n_mem = mem_prompt.shape[0], mem_prompt.shape[1]
    bp, lp = x_prompt.shape[:2]
    bs_ = x_sample.shape[0]
    hp, hs = x_prompt.reshape(bp * lp, D_MODEL), x_sample.reshape(bs_, D_MODEL)
    wk_p, wv_p, cv_p, ssm_p, mk_ps, mv_ps = [], [], [], [], [], []
    wk_s, wv_s, cv_s, ssm_s, gv_s = [], [], [], [], []
    mem_b = mem_prompt.reshape(n_prompt * n_mem, D_MODEL).astype(BF16)
    swa_bias = _swa_prompt_bias(rel_bias)
    experts = (w_e1, b_e1.reshape(DEPTH, N_EXPERTS, 1, 2 * D_FF), w_e2, b_e2.reshape(DEPTH, N_EXPERTS, 1, D_MODEL))
    for l in range(DEPTH):
        p = dict(sinks=sinks[l], gmlp_ln_g=gmlp_ln_g[l], gmlp_ln_b=gmlp_ln_b[l],
                 gmlp_ws=gmlp_ws[l], gmlp_bs=gmlp_bs[l], conv_w=conv_w[l], conv_b=conv_b[l],
                 dt_bias=dt_bias[l], a_log=a_log[l], d_skip=d_skip[l], ssm_norm_g=ssm_norm_g[l],
                 ln1_g=ln1_g[l], ln1_b=ln1_b[l], ln2_g=ln2_g[l], ln2_b=ln2_b[l])
        wi = w_in[l]
        seg = np.cumsum([0] + IN_SIZES)
        part = lambda n: wi[:, seg[n]:seg[n + 1]]
        w_in_cols = jnp.concatenate(
            [part(0), part(3), part(4), part(5), part(6), part(1), part(2), part(7),
             jnp.zeros((D_MODEL, IN_DIM_PAD - IN_DIM), F32)], axis=1).astype(BF16)
        w_router_pad = jnp.pad(w_router[l], ((0, 0), (0, LANES - N_EXPERTS)))
        lw = dict(
            w_in=w_in_cols,
            w_gate=w_gate[l].astype(BF16),
            b_gate=b_gate[l].reshape(1, -1),
            w_branch=w_branch[l].astype(BF16),
            w_o=w_o[l].astype(BF16),
            w_xq=w_xq[l].astype(BF16),
            w_xo=w_xo[l].astype(BF16),
            w_router=w_router_pad,
            w_router_bf16=w_router_pad.astype(BF16),
            b_router=jnp.pad(b_router[l], (0, LANES - N_EXPERTS)).reshape(1, -1),
        )
        w_kv = jnp.concatenate([w_xk[l], w_xv[l]], axis=1).astype(BF16)
        mkv = _matmul(mem_b, w_kv, n_mem, XH * XHD)
        mk = mkv[:, :XH * XHD].reshape(n_prompt, n_mem, XH, XHD)
        mv = mkv[:, XH * XHD:].reshape(n_prompt, n_mem, XH, XHD)
        x2_p, lg_p, st_p = _prompt_layer(hp, mkv, lw, p, swa_bias, bp, lp)
        x2_s, lg_s, st_s = _sample_layer(hs, l, lw, p, rel_bias, cache_win_k, cache_win_v, state_conv, state_ssm,
                                         cache_mem_k, cache_mem_v)
        hp, hs = _moe_ln(x2_p, x2_s, lg_p, lg_s, experts, l, ln3_g[l], ln3_b[l])
        wk_p.append(st_p[0]); wv_p.append(st_p[1]); cv_p.append(st_p[2]); ssm_p.append(st_p[3])
        mk_ps.append(mk); mv_ps.append(mv)
        wk_s.append(st_s[0]); wv_s.append(st_s[1]); cv_s.append(st_s[2]); ssm_s.append(st_s[3])
        gv_s.append(st_s[4])
    hp = hp.reshape(bp, lp, D_MODEL)
    hs = hs.reshape(bs_, 1, D_MODEL)
    return (hp, hs,
            jnp.stack(wk_p), jnp.stack(wv_p), jnp.stack(cv_p), jnp.stack(ssm_p),
            jnp.stack(mk_ps), jnp.stack(mv_ps),
            jnp.stack(wk_s), jnp.stack(wv_s), jnp.stack(cv_s), jnp.stack(ssm_s), jnp.stack(gv_s))
```

```python
import functools
import math

import numpy as np
import jax
import jax.numpy as jnp
from jax import lax
from jax.experimental import pallas as pl
from jax.experimental.pallas import tpu as pltpu

D_MODEL = 2048
DEPTH = 2
PAST_LEN = 16384
WINDOW = 128
H_A = 16
KV_A = 2
HD_A = 64
G_A = H_A // KV_A
N_BUCKETS = 32
MAX_DIST = 128
CHUNK_B = 128
GB = 16
CG_B = 64
W_B = GB * CG_B
D_INNER = 1024
P_C = 64
H_C = D_INNER // P_C
G_C = 2
R_C = H_C // G_C
N_C = 128
CONV_W = 4
CONV_DIM = D_INNER + 2 * G_C * N_C
SSD_CHUNK = 128
N_BRANCH = 3
BR_W = 1024
N_MEM = 256
XH = 4
XHD = 128
N_EXPERTS = 32
TOP_K = 4
D_FF = D_MODEL
SWIGLU_ALPHA = 1.702
SWIGLU_LIMIT = 7.0
DN_ALPHA = (2 * DEPTH) ** 0.25
LN_EPS = 1e-5
RMS_EPS = 1e-5

IN_SIZES = [H_A * HD_A, KV_A * HD_A, KV_A * HD_A, W_B, W_B, D_INNER, CONV_DIM, H_C]
IN_DIM = sum(IN_SIZES)
IN_DIM_PAD = 6144
COL_Q, COL_U, COL_GV, COL_Z, COL_XBC = 0, 1024, 2048, 3072, 4096
COL_K = COL_XBC + CONV_DIM
COL_V = COL_K + KV_A * HD_A
COL_DT = COL_V + KV_A * HD_A
LANES = 128

VMEM_LIMIT = 56 * 1024 * 1024
MOE_BM = 512
MOE_FC = 128
MOE_UNIT = 3
MOE_ROW_SPLIT = 2
MOE_GATHER_ROWS = 16448

F32 = jnp.float32
BF16 = jnp.bfloat16


def _cparams(*sem):
    return pltpu.CompilerParams(dimension_semantics=sem, vmem_limit_bytes=VMEM_LIMIT)


def _mm_kernel(x_ref, w_ref, o_ref):
    o_ref[...] = jnp.dot(x_ref[...], w_ref[...], preferred_element_type=F32).astype(o_ref.dtype)


def _matmul(x, w, tm, tn, out_dtype=F32):
    M, K = x.shape
    N = w.shape[1]
    assert M % tm == 0 and N % tn == 0
    return pl.pallas_call(
        _mm_kernel,
        grid=(N // tn, M // tm),
        in_specs=[pl.BlockSpec((tm, K), lambda j, i: (i, 0)),
                  pl.BlockSpec((K, tn), lambda j, i: (0, j))],
        out_specs=pl.BlockSpec((tm, tn), lambda j, i: (i, j)),
        out_shape=jax.ShapeDtypeStruct((M, N), out_dtype),
        compiler_params=_cparams("parallel", "parallel"),
        name="dense_matmul",
    )(x, w)


def _gate_merge_kernel(x_ref, a_ref, b_ref, c_ref, wg0_ref, wg1_ref, wg2_ref,
                       bg0_ref, bg1_ref, bg2_ref, wp_ref, o_ref):
    x = x_ref[...]
    acc = None
    for k, (br_ref, wg_ref, bg_ref) in enumerate(
            ((a_ref, wg0_ref, bg0_ref), (b_ref, wg1_ref, bg1_ref), (c_ref, wg2_ref, bg2_ref))):
        z = jnp.dot(x, wg_ref[...], preferred_element_type=F32) + bg_ref[...]
        gate = 1.0 / (1.0 + jnp.exp(-z))
        proj = jnp.dot(br_ref[...], wp_ref[k], preferred_element_type=F32)
        acc = gate * proj if acc is None else acc + gate * proj
    o_ref[...] = acc.astype(o_ref.dtype)


def _gate_merge(x, a, b, c, w_gate, b_gate, w_branch, tm, tn):
    M = x.shape[0]
    nt = D_MODEL // tn
    row = lambda j, i: (i, 0)
    in_specs = [pl.BlockSpec((tm, D_MODEL), row)] + [pl.BlockSpec((tm, BR_W), row)] * 3
    in_specs += [pl.BlockSpec((D_MODEL, tn), functools.partial(lambda j, i, k: (0, k * nt + j), k=k))
                 for k in range(N_BRANCH)]
    in_specs += [pl.BlockSpec((1, tn), functools.partial(lambda j, i, k: (0, k * nt + j), k=k))
                 for k in range(N_BRANCH)]
    in_specs += [pl.BlockSpec((N_BRANCH, BR_W, tn), lambda j, i: (0, 0, j))]
    return pl.pallas_call(
        _gate_merge_kernel,
        grid=(nt, M // tm),
        in_specs=in_specs,
        out_specs=pl.BlockSpec((tm, tn), lambda j, i: (i, j)),
        out_shape=jax.ShapeDtypeStruct((M, D_MODEL), BF16),
        compiler_params=_cparams("parallel", "parallel"),
        name="gate_merge",
    )(x, a, b, c, w_gate, w_gate, w_gate, b_gate, b_gate, b_gate, w_branch)


def _layer_norm_rows(y, g, b):
    mu = jnp.mean(y, axis=-1, keepdims=True)
    yc = y - mu
    var = jnp.mean(yc * yc, axis=-1, keepdims=True)
    return yc * lax.rsqrt(var + LN_EPS) * g + b


def _mm_res_ln_kernel(a_ref, w_ref, res_ref, g_ref, b_ref, o_ref):
    y = jnp.dot(a_ref[...], w_ref[...], preferred_element_type=F32) + DN_ALPHA * res_ref[...]
    o_ref[...] = _layer_norm_rows(y, g_ref[...], b_ref[...])


def _mm_res_ln(a, w, res, g, b, tm):
    M, K = a.shape
    row = lambda i: (i, 0)
    fixed = lambda i: (0, 0)
    return pl.pallas_call(
        _mm_res_ln_kernel,
        grid=(M // tm,),
        in_specs=[pl.BlockSpec((tm, K), row), pl.BlockSpec((K, D_MODEL), fixed),
                  pl.BlockSpec((tm, D_MODEL), row), pl.BlockSpec((1, D_MODEL), fixed),
                  pl.BlockSpec((1, D_MODEL), fixed)],
        out_specs=pl.BlockSpec((tm, D_MODEL), row),
        out_shape=jax.ShapeDtypeStruct((M, D_MODEL), F32),
        compiler_params=_cparams("parallel"),
        name="matmul_residual_layernorm",
    )(a, w, res, g.reshape(1, -1), b.reshape(1, -1))


def _top_k_lanes(logits):
    lane_i = lax.broadcasted_iota(jnp.int32, logits.shape, 1)
    lane_f = lane_i.astype(F32)
    work = jnp.where(lane_i < N_EXPERTS, logits, -jnp.inf)
    out = jnp.zeros_like(logits)
    for k in range(TOP_K):
        m = jnp.max(work, axis=-1, keepdims=True)
        idx = jnp.min(jnp.where(work == m, lane_f, float(LANES)), axis=-1, keepdims=True)
        out = jnp.where(lane_i == k, m, out)
        out = jnp.where(lane_i == TOP_K + k, idx, out)
        work = jnp.where(lane_f == idx, -jnp.inf, work)
    return out


def _router_kernel(x_ref, w_ref, b_ref, o_ref):
    acc = jnp.dot(x_ref[...].astype(BF16), w_ref[...].astype(BF16), preferred_element_type=F32)
    o_ref[...] = _top_k_lanes(acc + b_ref[...])


def _router(x, w_pad, b_pad, tm):
    M = x.shape[0]
    NP = w_pad.shape[1]
    return pl.pallas_call(
        _router_kernel,
        grid=(M // tm,),
        in_specs=[pl.BlockSpec((tm, D_MODEL), lambda i: (i, 0)),
                  pl.BlockSpec((D_MODEL, NP), lambda i: (0, 0)),
                  pl.BlockSpec((1, NP), lambda i: (0, 0))],
        out_specs=pl.BlockSpec((tm, NP), lambda i: (i, 0)),
        out_shape=jax.ShapeDtypeStruct((M, NP), F32),
        compiler_params=_cparams("parallel"),
        name="router_logits",
    )(x, w_pad, b_pad)


def _bf16_row_interleave(a, b):
    a32 = lax.bitcast_convert_type(a.astype(BF16).astype(F32), jnp.uint32)
    b32 = lax.bitcast_convert_type(b.astype(BF16).astype(F32), jnp.uint32)
    word = (a32 >> 16) | (b32 & jnp.uint32(0xFFFF0000))
    return pltpu.bitcast(word, BF16)


STEP_VALID, STEP_FIRST, STEP_LAST, STEP_NEW_WEIGHTS, STEP_ZERO_OUT = 1, 2, 4, 8, 16


def _moe_kernel(se_ref, sf_ref, sx_ref, so_ref, sj_ref, flag_ref, x_ref, w1_ref, b1_ref, w2_ref, b2_ref, o_ref,
                acc_ref, xs_ref, w1b_ref, w2q_ref):
    s = pl.program_id(0)
    fc = w2_ref.shape[1] // 2
    bm = x_ref.shape[0]
    flags = flag_ref[s]

    @pl.when((flags & STEP_VALID) != 0)
    def _():
        j = sj_ref[s]
        first = (flags & STEP_FIRST) != 0
        last = (flags & STEP_LAST) != 0

        @pl.when((flags & STEP_NEW_WEIGHTS) != 0)
        def _():
            w1b_ref[...] = w1_ref[0].astype(BF16)
            w2 = w2_ref[0]
            w2q_ref[...] = _bf16_row_interleave(w2[:fc], w2[fc:])

        @pl.when(first)
        def _():
            xs_ref[j] = x_ref[...]
            acc_ref[j] = jnp.broadcast_to(b2_ref[0], acc_ref.shape[1:])

        hm = bm // MOE_ROW_SPLIT
        even = (lax.broadcasted_iota(jnp.int32, (hm, LANES), 1) & 1) == 0
        for part in range(MOE_ROW_SPLIT):
            rows = pl.ds(part * hm, hm)
            h = jnp.dot(xs_ref[j, rows, :], w1b_ref[...], preferred_element_type=F32) + b1_ref[0]
            acts = []
            for c in range(2 * fc // LANES):
                a = h[:, LANES * c:LANES * (c + 1)]
                b = h[:, 2 * fc + LANES * c:2 * fc + LANES * (c + 1)]
                glu = jnp.where(even, a, pltpu.roll(b, 1, axis=1))
                lin = jnp.where(even, pltpu.roll(a, LANES - 1, axis=1), b)
                glu = jnp.minimum(glu, SWIGLU_LIMIT)
                lin = jnp.clip(lin, -SWIGLU_LIMIT, SWIGLU_LIMIT)
                act = glu * (1.0 / (1.0 + jnp.exp(-SWIGLU_ALPHA * glu))) * (lin + 1.0)
                acts.append(act.astype(BF16))
            act = jnp.concatenate(acts, axis=1) if len(acts) > 1 else acts[0]
            acc_ref[j, rows, :] += jnp.dot(act, w2q_ref[...], preferred_element_type=F32)

        @pl.when(last)
        def _():
            o_ref[...] = acc_ref[j]

    @pl.when((flags & STEP_ZERO_OUT) != 0)
    def _():
        o_ref[...] = jnp.zeros_like(o_ref)


def _moe_steps(padded, p_end, n_valid, nb, bm, nf, unit):
    i32 = jnp.int32
    n_exp = padded.shape[0]
    blk = jnp.arange(nb, dtype=i32)
    valid_b = blk < n_valid
    e_b = jnp.minimum(jnp.sum(p_end[None, :] <= (blk * bm)[:, None], axis=1, dtype=i32), n_exp - 1)
    is_e = e_b[:, None] == jnp.arange(n_exp, dtype=i32)[None, :]
    look = lambda table: jnp.sum(jnp.where(is_e, table.astype(i32)[None, :], 0), axis=1, dtype=i32)
    q_b = blk - look((p_end - padded) // bm)
    j_b = q_b % unit
    b0_b = blk - j_b
    n_b = jnp.where(valid_b, jnp.minimum(unit, look(padded // bm) - (q_b - j_b)), 1)
    rep = lambda a: jnp.repeat(a, nf)
    step = jnp.arange(nb * nf, dtype=i32)
    valid, e, b0, n = rep(valid_b), rep(e_b), rep(b0_b), rep(n_b)
    r = step - nf * b0
    f = r // n
    j = r % n
    last_blk = n_valid - 1
    e_last = jnp.sum(jnp.where(blk == last_blk, e_b, 0), dtype=i32)
    se = jnp.where(valid, e, e_last)
    sf = jnp.where(valid, f, nf - 1)
    sx = jnp.where(valid, jnp.where(f == 0, b0 + j, b0 + n - 1), last_blk)
    so = jnp.where(valid, jnp.where(f == nf - 1, b0 + j, b0), step // nf)
    sj = jnp.where(valid, j, 0)
    flags = jnp.where(valid, STEP_VALID + STEP_FIRST * (f == 0) + STEP_LAST * (f == nf - 1)
                      + STEP_NEW_WEIGHTS * (j == 0), STEP_ZERO_OUT * (step % nf == 0))
    return [a.astype(i32) for a in (se, sf, sx, so, sj, flags)]


def _moe_ffn_blocks(rows, steps, w1, b1, w2, b2, layer, bm, fc, unit):
    d_model = rows.shape[1]
    d_ff = w2.shape[2]
    nb = rows.shape[0] // bm
    nf = d_ff // (2 * fc)
    assert nf >= 2 and steps[0].shape[0] == nb * nf
    grid_spec = pltpu.PrefetchScalarGridSpec(
        num_scalar_prefetch=6,
        grid=(nb * nf,),
        in_specs=[
            pl.BlockSpec((bm, d_model), lambda s, se, sf, sx, so, sj, fl: (sx[s], 0)),
            pl.BlockSpec((None, 1, d_model, 4 * fc), lambda s, se, sf, sx, so, sj, fl: (layer, se[s], 0, sf[s])),
            pl.BlockSpec((None, 1, 1, 4 * fc), lambda s, se, sf, sx, so, sj, fl: (layer, se[s], 0, sf[s])),
            pl.BlockSpec((None, 1, 2 * fc, d_model), lambda s, se, sf, sx, so, sj, fl: (layer, se[s], sf[s], 0)),
            pl.BlockSpec((None, 1, 1, d_model), lambda s, se, sf, sx, so, sj, fl: (layer, se[s], 0, 0)),
        ],
        out_specs=pl.BlockSpec((bm, d_model), lambda s, se, sf, sx, so, sj, fl: (so[s], 0)),
        scratch_shapes=[pltpu.VMEM((unit, bm, d_model), F32), pltpu.VMEM((unit, bm, d_model), BF16),
                        pltpu.VMEM((d_model, 4 * fc), BF16), pltpu.VMEM((2 * fc, d_model), BF16)],
    )
    return pl.pallas_call(
        _moe_kernel,
        grid_spec=grid_spec,
        out_shape=jax.ShapeDtypeStruct(rows.shape, F32),
        compiler_params=_cparams("arbitrary"),
        name="moe_expert_ffn",
    )(*steps, rows, w1, b1, w2, b2)


def _combine_ln_kernel(*refs):
    g_refs = refs[:TOP_K]
    gate_ref, res_ref, lg_ref, lb_ref, o_ref = refs[TOP_K:]
    gate = gate_ref[...]
    ff = None
    for k in range(TOP_K):
        term = gate[:, k:k + 1] * g_refs[k][...]
        ff = term if ff is None else ff + term
    o_ref[...] = _layer_norm_rows(DN_ALPHA * res_ref[...] + ff, lg_ref[...], lb_ref[...])


def _combine_ln(gathered, gate, res, g, b, tm):
    T, d = res.shape
    nt = T // tm
    row = lambda i: (i, 0)
    fixed = lambda i: (0, 0)
    g_specs = [pl.BlockSpec((tm, d), functools.partial(lambda i, k: (k * nt + i, 0), k=k)) for k in range(TOP_K)]
    return pl.pallas_call(
        _combine_ln_kernel,
        grid=(nt,),
        in_specs=g_specs + [pl.BlockSpec((tm, TOP_K), row), pl.BlockSpec((tm, d), row),
                            pl.BlockSpec((1, d), fixed), pl.BlockSpec((1, d), fixed)],
        out_specs=pl.BlockSpec((tm, d), row),
        out_shape=jax.ShapeDtypeStruct((T, d), F32),
        compiler_params=_cparams("parallel"),
        name="moe_combine_layernorm",
    )(*([gathered] * TOP_K), gate, res, g.reshape(1, -1), b.reshape(1, -1))


def _moe_route(top_v, top_i, n_exp, bm):
    T = top_v.shape[0]
    i32 = jnp.int32
    gate = jax.nn.softmax(top_v, axis=-1)
    n_assign = T * TOP_K
    nb = -(-(n_assign + n_exp * (bm - 1)) // bm)
    ids = jnp.arange(n_assign, dtype=i32)
    e_sorted, order = lax.sort((top_i.reshape(-1).astype(i32), ids), num_keys=1, is_stable=True)
    is_e = e_sorted[:, None] == jnp.arange(n_exp, dtype=i32)[None, :]
    counts = jnp.sum(is_e, axis=0, dtype=i32)
    padded = (counts + bm - 1) // bm * bm
    p_end = jnp.cumsum(padded)
    u_start = jnp.cumsum(counts) - counts
    shift = (p_end - padded - u_start).astype(i32)
    dest = ids + jnp.sum(jnp.where(is_e, shift[None, :], 0), axis=1, dtype=i32)
    _, pos = lax.sort((order, dest), num_keys=1)
    src = jnp.zeros((nb * bm,), i32).at[dest].set(order // TOP_K, indices_are_sorted=True, unique_indices=True)
    n_valid = (p_end[-1] // bm).astype(i32)
    return gate, src, pos, (padded.astype(i32), p_end.astype(i32), n_valid, nb)


def _moe_ln(x2_p, x2_s, route_p, route_s, experts, layer, ln_g, ln_b):
    tp, ts = x2_p.shape[0], x2_s.shape[0]
    x2b = jnp.concatenate([x2_p.astype(BF16), x2_s.astype(BF16),
                           jnp.zeros((MOE_GATHER_ROWS - tp - ts, D_MODEL), BF16)], axis=0)
    route =jnp.concatenate([route_p[:, :2 * TOP_K], route_s[:, :2 * TOP_K]], axis=0)
    top_v, top_i = route[:, :TOP_K], route[:, TOP_K:].astype(jnp.int32)
    gate, src, pos, (padded, p_end, n_valid, nb) = _moe_route(top_v, top_i, N_EXPERTS, MOE_BM)
    rows = jnp.take(x2b, src, axis=0, mode='clip')
    w_e1, b_e1, w_e2, b_e2 = experts
    steps = _moe_steps(padded, p_end, n_valid, nb, MOE_BM, D_FF // (2 * MOE_FC), MOE_UNIT)
    out_rows = _moe_ffn_blocks(rows, steps, w_e1, b_e1, w_e2, b_e2, layer, MOE_BM, MOE_FC, MOE_UNIT)
    pos = pos.reshape(tp + ts, TOP_K)
    g_p = jnp.take(out_rows, pos[:tp].T.reshape(-1), axis=0, mode='clip')
    g_s = jnp.take(out_rows, pos[tp:].T.reshape(-1), axis=0, mode='clip')
    hp = _combine_ln(g_p, gate[:tp], x2_p, ln_g, ln_b, 256)
    hs = _combine_ln(g_s, gate[tp:], x2_s, ln_g, ln_b, ts)
    return hp, hs


def _half_lane_variants(t):
    lo = lax.broadcasted_iota(jnp.int32, t.shape, 1) < HD_A
    zero = jnp.zeros_like(t)
    tr = pltpu.roll(t, HD_A, axis=1)
    return [[jnp.where(lo, t, zero).astype(BF16), jnp.where(lo, zero, tr).astype(BF16)],
            [jnp.where(lo, tr, zero).astype(BF16), jnp.where(lo, zero, t).astype(BF16)]]


def _swa_kernel(sinks_ref, q_ref, kp_ref, kc_ref, vp_ref, vc_ref, bias_ref, o_ref):
    j = pl.program_id(1)
    kvar = _half_lane_variants(jnp.concatenate([kp_ref[...], kc_ref[...]], axis=0))
    vvar = _half_lane_variants(jnp.concatenate([vp_ref[...], vc_ref[...]], axis=0))
    col = lax.broadcasted_iota(jnp.int32, (WINDOW, 2 * WINDOW), 1)
    no_prev = col < jnp.where(j == 0, WINDOW, 0)
    for r in range(H_A // 2):
        g = (2 * r) // G_A
        qp = q_ref[:, LANES * r:LANES * (r + 1)].astype(BF16)
        acc = None
        for par in range(2):
            h = 2 * r + par
            s = lax.dot_general(qp, kvar[g][par], (((1,), (1,)), ((), ())), preferred_element_type=F32)
            s = s * HD_A ** -0.5 + bias_ref[h]
            s = jnp.where(no_prev, -jnp.inf, s)
            sk = sinks_ref[h]
            m = jnp.maximum(jnp.max(s, axis=-1, keepdims=True), sk)
            pr = jnp.exp(s - m)
            den = jnp.sum(pr, axis=-1, keepdims=True) + jnp.exp(sk - m)
            o = jnp.dot((pr / den).astype(BF16), vvar[g][par], preferred_element_type=F32)
            acc = o if acc is None else acc + o
        o_ref[:, LANES * r:LANES * (r + 1)] = acc.astype(o_ref.dtype)


def _swa_prompt_bias(rel_bias):
    i = jnp.arange(WINDOW)[:, None]
    j = jnp.arange(2 * WINDOW)[None, :]
    dist = i + WINDOW - j
    return jnp.where((dist >= 0) & (dist < WINDOW), _bucket_bias(rel_bias, dist), -jnp.inf)


def _swa_prompt_call(proj2d, bias, sinks, bsz, L):
    nb = L // WINDOW
    kcol, vcol = COL_K // LANES, COL_V // LANES
    cur = lambda b, j: b * nb + j
    prev = lambda b, j: b * nb + jnp.maximum(j - 1, 0)
    return pl.pallas_call(
        _swa_kernel,
        grid=(bsz, nb),
        in_specs=[
            pl.BlockSpec(memory_space=pltpu.SMEM),
            pl.BlockSpec((WINDOW, H_A * HD_A), lambda b, j: (cur(b, j), COL_Q // (H_A * HD_A))),
            pl.BlockSpec((WINDOW, LANES), lambda b, j: (prev(b, j), kcol)),
            pl.BlockSpec((WINDOW, LANES), lambda b, j: (cur(b, j), kcol)),
            pl.BlockSpec((WINDOW, LANES), lambda b, j: (prev(b, j), vcol)),
            pl.BlockSpec((WINDOW, LANES), lambda b, j: (cur(b, j), vcol)),
            pl.BlockSpec((H_A, WINDOW, 2 * WINDOW), lambda b, j: (0, 0, 0)),
        ],
        out_specs=pl.BlockSpec((WINDOW, H_A * HD_A), lambda b, j: (cur(b, j), 0)),
        out_shape=jax.ShapeDtypeStruct((bsz * L, H_A * HD_A), BF16),
        compiler_params=_cparams("parallel", "arbitrary"),
        name="swa_prompt",
    )(sinks, proj2d, proj2d, proj2d, proj2d, proj2d, bias)


def _gelu(x):
    return 0.5 * x * (1.0 + lax.erf(x * np.float32(np.sqrt(0.5))))


def _gmlp_kernel(u_ref, gv_ref, lng_ref, lnb_ref, w_ref, bias_ref, o_ref):
    u = _gelu(u_ref[...])
    gv = _layer_norm_rows(_gelu(gv_ref[...]), lng_ref[...], lnb_ref[...])
    lo = lax.broadcasted_iota(jnp.int32, (CHUNK_B, LANES), 1) < CG_B
    for r in range(GB // 2):
        vp = gv[:, LANES * r:LANES * (r + 1)]
        zero = jnp.zeros_like(vp)
        mix = jnp.dot(w_ref[2 * r], jnp.where(lo, vp, zero).astype(BF16), preferred_element_type=F32)
        mix += jnp.dot(w_ref[2 * r + 1], jnp.where(lo, zero, vp).astype(BF16), preferred_element_type=F32)
        sl = slice(LANES * r, LANES * (r + 1))
        o_ref[:, sl] = (u[:, sl] * (mix + bias_ref[:, sl])).astype(o_ref.dtype)


def _gmlp_prompt_call(proj2d, p, n_rows):
    w = (p['gmlp_ws'] * jnp.tril(jnp.ones((CHUNK_B, CHUNK_B), F32))).astype(BF16)
    bias = jnp.repeat(p['gmlp_bs'].T, CG_B, axis=1)
    fixed2 = lambda i: (0, 0)
    return pl.pallas_call(
        _gmlp_kernel,
        grid=(n_rows // CHUNK_B,),
        in_specs=[
            pl.BlockSpec((CHUNK_B, W_B), lambda i: (i, COL_U // W_B)),
            pl.BlockSpec((CHUNK_B, W_B), lambda i: (i, COL_GV // W_B)),
            pl.BlockSpec((1, W_B), fixed2), pl.BlockSpec((1, W_B), fixed2),
            pl.BlockSpec((GB, CHUNK_B, CHUNK_B), lambda i: (0, 0, 0)),
            pl.BlockSpec((CHUNK_B, W_B), fixed2),
        ],
        out_specs=pl.BlockSpec((CHUNK_B, W_B), lambda i: (i, 0)),
        out_shape=jax.ShapeDtypeStruct((n_rows, W_B), BF16),
        compiler_params=_cparams("parallel"),
        name="gmlp_prompt",
    )(proj2d, proj2d, p['gmlp_ln_g'].reshape(1, -1), p['gmlp_ln_b'].reshape(1, -1), w, bias)


def _silu(x):
    return x * (1.0 / (1.0 + jnp.exp(-x)))


def _softplus(x):
    return jnp.maximum(x, 0.0) + jnp.log1p(jnp.exp(-jnp.abs(x)))


def _causal_conv_chunk(cur, tail, w, bias):
    rows = lax.broadcasted_iota(jnp.int32, (8, cur.shape[1]), 0)
    y = jnp.broadcast_to(bias, cur.shape)
    y_head = jnp.broadcast_to(bias, (8, cur.shape[1]))
    for t in range(CONV_W):
        k = CONV_W - 1 - t
        wt = w[t:t + 1, :]
        if k == 0:
            y = y + cur * wt
            y_head = y_head + cur[:8] * wt
        else:
            sh = pltpu.roll(cur, k, axis=0)
            y = y + sh * wt
            y_head = y_head + jnp.where(rows < k, pltpu.roll(tail, k, axis=0), sh[:8]) * wt
    return jnp.concatenate([y_head, y[8:]], axis=0)


def _bf16_split3(x):
    p1 = x.astype(BF16)
    r1 = x - p1.astype(F32)
    p2 = r1.astype(BF16)
    p3 = (r1 - p2.astype(F32)).astype(BF16)
    return p1, p2, p3


def _ssd_kernel(xs_ref, bc_ref, z_ref, dt_ref, cwx_ref, cbx_ref, cwb_ref, cbb_ref, dtb_ref, a_ref,
                dsk_ref, ng_ref, y_ref, h_ref, state_ref, tailx_ref, tailb_ref):
    c = pl.program_id(1)
    C = SSD_CHUNK

    @pl.when(c == 0)
    def _():
        state_ref[...] = jnp.zeros_like(state_ref)
        tailx_ref[...] = jnp.zeros_like(tailx_ref)
        tailb_ref[...] = jnp.zeros_like(tailb_ref)

    xs_raw = xs_ref[...]
    bc_raw = bc_ref[...]
    xs = _silu(_causal_conv_chunk(xs_raw, tailx_ref[...], cwx_ref[...], cbx_ref[...]))
    bc = _silu(_causal_conv_chunk(bc_raw, tailb_ref[...], cwb_ref[...], cbb_ref[...]))
    tailx_ref[...] = xs_raw[C - 8:]
    tailb_ref[...] = bc_raw[C - 8:]

    dt = _softplus(dt_ref[...] + dtb_ref[...])
    da = dt * a_ref[...]
    row_i = lax.broadcasted_iota(jnp.int32, (C, C), 0)
    col_i = lax.broadcasted_iota(jnp.int32, (C, C), 1)
    causal = row_i >= col_i
    tril = jnp.where(causal, 1.0, 0.0).astype(BF16)
    acs = None
    for piece in _bf16_split3(da):
        t = jnp.dot(tril, piece, preferred_element_type=F32)
        acs = t if acs is None else acs + t
    acs_t = acs.T
    exp_acs = jnp.exp(acs)
    end_decay = jnp.exp(acs[C - 1:C, :] - acs)
    chunk_decay = jnp.exp(acs[C - 1:C, :])

    lo = lax.broadcasted_iota(jnp.int32, (C, LANES), 1) < P_C
    bm = [bc[:, N_C * g:N_C * (g + 1)].astype(BF16) for g in range(G_C)]
    cm = [bc[:, N_C * (G_C + g):N_C * (G_C + g + 1)].astype(BF16) for g in range(G_C)]
    cb = [lax.dot_general(cm[g], bm[g], (((1,), (1,)), ((), ())), preferred_element_type=F32)
          for g in range(G_C)]

    def per_lane_half(t, r):
        return jnp.where(lo, t[:, 2 * r:2 * r + 1], t[:, 2 * r + 1:2 * r + 2])

    ys = []
    for r in range(H_C // 2):
        g = (2 * r) // R_C
        sl = slice(LANES * r, LANES * (r + 1))
        x_pair = xs[:, sl]
        xdt = x_pair * per_lane_half(dt, r)
        zero = jnp.zeros_like(xdt)
        y_pair = None
        for par in range(2):
            h = 2 * r + par
            seg = acs[:, h:h + 1] - acs_t[h:h + 1, :]
            decay = jnp.where(causal, jnp.exp(seg), 0.0)
            m_h = (cb[g] * decay).astype(BF16)
            x_h = (jnp.where(lo, xdt, zero) if par == 0 else jnp.where(lo, zero, xdt)).astype(BF16)
            t = jnp.dot(m_h, x_h, preferred_element_type=F32)
            y_pair = t if y_pair is None else y_pair + t
        st = state_ref[sl, :]
        y_off = lax.dot_general(cm[g], st.astype(BF16), (((1,), (1,)), ((), ())), preferred_element_type=F32)
        y_pair = y_pair + y_off * per_lane_half(exp_acs, r)
        upd = lax.dot_general((xdt * per_lane_half(end_decay, r)).astype(BF16), bm[g],
                              (((0,), (0,)), ((), ())), preferred_element_type=F32)
        cd = jnp.concatenate([jnp.broadcast_to(chunk_decay[:, 2 * r:2 * r + 1], (P_C, N_C)),
                              jnp.broadcast_to(chunk_decay[:, 2 * r + 1:2 * r + 2], (P_C, N_C))], axis=0)
        state_ref[sl, :] = st * cd + upd
        ys.append(y_pair + dsk_ref[:, sl] * x_pair)
    y = jnp.concatenate(ys, axis=1) * _silu(z_ref[...])
    gw = D_INNER // G_C
    outs = []
    for g in range(G_C):
        yg = y[:, gw * g:gw * (g + 1)]
        outs.append(yg * lax.rsqrt(jnp.mean(yg * yg, axis=-1, keepdims=True) + RMS_EPS))
    y_ref[...] = (jnp.concatenate(outs, axis=1) * ng_ref[...]).astype(y_ref.dtype)

    @pl.when(c == pl.num_programs(1) - 1)
    def _():
        h_ref[...] = state_ref[...]


def _ssd_prompt_call(proj2d, p, bsz, L):
    nc = L // SSD_CHUNK
    row = lambda blk: (lambda b, c: (b * nc + c, blk))
    fixed = lambda b, c: (0, 0)
    pad_l = lambda v: jnp.pad(v.astype(F32), (0, LANES - H_C)).reshape(1, LANES)
    cw, cbias = p['conv_w'], p['conv_b'].reshape(1, -1)
    nbc = 2 * G_C * N_C
    args = (proj2d, proj2d, proj2d, proj2d,
            cw[:, :D_INNER], cbias[:, :D_INNER], cw[:, D_INNER:], cbias[:, D_INNER:],
            pad_l(p['dt_bias']), pad_l(-jnp.exp(p['a_log'].astype(F32))),
            jnp.repeat(p['d_skip'].astype(F32), P_C).reshape(1, -1), p['ssm_norm_g'].reshape(1, -1))
    return pl.pallas_call(
        _ssd_kernel,
        grid=(bsz, nc),
        in_specs=[
            pl.BlockSpec((SSD_CHUNK, D_INNER), row(COL_XBC // D_INNER)),
            pl.BlockSpec((SSD_CHUNK, nbc), row((COL_XBC + D_INNER) // nbc)),
            pl.BlockSpec((SSD_CHUNK, D_INNER), row(COL_Z // D_INNER)),
            pl.BlockSpec((SSD_CHUNK, LANES), row(COL_DT // LANES)),
            pl.BlockSpec((CONV_W, D_INNER), fixed), pl.BlockSpec((1, D_INNER), fixed),
            pl.BlockSpec((CONV_W, nbc), fixed), pl.BlockSpec((1, nbc), fixed),
            pl.BlockSpec((1, LANES), fixed), pl.BlockSpec((1, LANES), fixed),
            pl.BlockSpec((1, D_INNER), fixed), pl.BlockSpec((1, D_INNER), fixed),
        ],
        out_specs=[pl.BlockSpec((SSD_CHUNK, D_INNER), lambda b, c: (b * nc + c, 0)),
                   pl.BlockSpec((None, H_C * P_C, N_C), lambda b, c: (b, 0, 0))],
        out_shape=[jax.ShapeDtypeStruct((bsz * L, D_INNER), BF16),
                   jax.ShapeDtypeStruct((bsz, H_C * P_C, N_C), F32)],
        scratch_shapes=[pltpu.VMEM((H_C * P_C, N_C), F32), pltpu.VMEM((8, D_INNER), F32),
                        pltpu.VMEM((8, nbc), F32)],
        compiler_params=_cparams("parallel", "arbitrary"),
        name="ssd_prompt",
    )(*args)


def _bf16_round(x):
    return x.astype(BF16).astype(F32)


SWA_DECODE_TOKENS = 8


def _swa_decode_kernel(q_ref, kn_ref, vn_ref, ck_ref, cv_ref, bias_ref, sink_ref, o_ref, wk_ref, wv_ref):
    W = WINDOW
    last = lax.broadcasted_iota(jnp.int32, (W, LANES), 0) == W - 1
    lo = lax.broadcasted_iota(jnp.int32, (8, LANES), 1) < HD_A
    g0 = lax.broadcasted_iota(jnp.int32, (8, LANES), 0) < G_A // 2
    zero = jnp.zeros((8, LANES), F32)
    nt_dims = (((1,), (1,)), ((), ()))
    for t in range(q_ref.shape[0]):
        kw = jnp.where(last, jnp.broadcast_to(kn_ref[t], (W, LANES)), pltpu.roll(ck_ref[t], W - 1, axis=0))
        vw = jnp.where(last, jnp.broadcast_to(vn_ref[t], (W, LANES)), pltpu.roll(cv_ref[t], W - 1, axis=0))
        wk_ref[t] = kw
        wv_ref[t] = vw
        kb, vb = kw.astype(BF16), vw.astype(BF16)
        q = q_ref[t]
        qr = pltpu.roll(q, HD_A, axis=1)
        q_par = [jnp.where(g0, jnp.where(lo, q, zero), jnp.where(lo, zero, qr)),
                 jnp.where(g0, jnp.where(lo, qr, zero), jnp.where(lo, zero, q))]
        outs = []
        for par in range(2):
            s = lax.dot_general(q_par[par].astype(BF16), kb, nt_dims, preferred_element_type=F32)
            s = s * HD_A ** -0.5 + bias_ref[par]
            sk = sink_ref[par]
            m = jnp.maximum(jnp.max(s, axis=-1, keepdims=True), sk)
            pr = jnp.exp(s - m)
            den = jnp.sum(pr, axis=-1, keepdims=True) + jnp.exp(sk - m)
            outs.append(jnp.dot((pr / den).astype(BF16), vb, preferred_element_type=F32))
        o_even = jnp.where(g0, outs[0], pltpu.roll(outs[0], HD_A, axis=1))
        o_odd = jnp.where(g0, pltpu.roll(outs[1], HD_A, axis=1), outs[1])
        o_ref[t] = jnp.where(lo, o_even, o_odd).astype(o_ref.dtype)


def _swa_decode_call(proj3, cache_k, cache_v, layer, rel_bias, sinks):
    bsz = proj3.shape[0]
    nt = SWA_DECODE_TOKENS
    ck = cache_k.reshape(cache_k.shape[0], bsz, WINDOW, LANES)
    cv = cache_v.reshape(cache_v.shape[0], bsz, WINDOW, LANES)
    q8 = proj3[:, 0, COL_Q:COL_Q + H_A * HD_A].reshape(bsz, H_A // 2, LANES)
    dist = WINDOW - 1 - jnp.arange(WINDOW)
    bias = _bucket_bias(rel_bias, dist).reshape(H_A // 2, 2, WINDOW).transpose(1, 0, 2)
    sink = sinks.astype(F32).reshape(H_A // 2, 2, 1).transpose(1, 0, 2)
    new_tok = lambda blk: (lambda i: (i, 0, blk))
    cache = lambda i: (layer, i, 0, 0)
    tok3 = lambda i: (i, 0, 0)
    out, wk, wv = pl.pallas_call(
        _swa_decode_kernel,
        grid=(bsz // nt,),
        in_specs=[
            pl.BlockSpec((nt, H_A // 2, LANES), tok3),
            pl.BlockSpec((nt, 1, LANES), new_tok(COL_K // LANES)),
            pl.BlockSpec((nt, 1, LANES), new_tok(COL_V // LANES)),
            pl.BlockSpec((None, nt, WINDOW, LANES), cache),
            pl.BlockSpec((None, nt, WINDOW, LANES), cache),
            pl.BlockSpec((2, H_A // 2, WINDOW), lambda i: (0, 0, 0)),
            pl.BlockSpec((2, H_A // 2, 1), lambda i: (0, 0, 0)),
        ],
        out_specs=[pl.BlockSpec((nt, H_A // 2, LANES), tok3),
                   pl.BlockSpec((nt, WINDOW, LANES), tok3),
                   pl.BlockSpec((nt, WINDOW, LANES), tok3)],
        out_shape=[jax.ShapeDtypeStruct((bsz, H_A // 2, LANES), BF16),
                   jax.ShapeDtypeStruct((bsz, WINDOW, LANES), F32),
                   jax.ShapeDtypeStruct((bsz, WINDOW, LANES), F32)],
        compiler_params=_cparams("parallel"),
        name="swa_sample",
    )(q8, proj3, proj3, ck, cv, bias, sink)
    return out.reshape(bsz, H_A * HD_A), wk, wv


def _gmlp_step_kernel(u_ref, gv_ref, lng_ref, lnb_ref, w0_ref, b0_ref, o_ref, gv_out_ref):
    gv = _layer_norm_rows(_gelu(gv_ref[...]), lng_ref[...], lnb_ref[...])
    gv_out_ref[...] = gv
    mix = _bf16_round(w0_ref[...]) * _bf16_round(gv) + b0_ref[...]
    o_ref[...] = (_gelu(u_ref[...]) * mix).astype(o_ref.dtype)


def _gmlp_step_call(proj2d, p):
    n = proj2d.shape[0]
    w0 = jnp.repeat(p['gmlp_ws'][:, 0, 0], CG_B).reshape(1, -1)
    b0 = jnp.repeat(p['gmlp_bs'][:, 0], CG_B).reshape(1, -1)
    fixed = lambda i: (0, 0)
    return pl.pallas_call(
        _gmlp_step_kernel,
        grid=(1,),
        in_specs=[pl.BlockSpec((n, W_B), lambda i: (0, COL_U // W_B)),
                  pl.BlockSpec((n, W_B), lambda i: (0, COL_GV // W_B)),
                  pl.BlockSpec((1, W_B), fixed), pl.BlockSpec((1, W_B), fixed),
                  pl.BlockSpec((1, W_B), fixed), pl.BlockSpec((1, W_B), fixed)],
        out_specs=[pl.BlockSpec((n, W_B), fixed), pl.BlockSpec((n, W_B), fixed)],
        out_shape=[jax.ShapeDtypeStruct((n, W_B), BF16), jax.ShapeDtypeStruct((n, W_B), F32)],
        compiler_params=_cparams("arbitrary"),
        name="gmlp_sample",
    )(proj2d, proj2d, p['gmlp_ln_g'].reshape(1, -1), p['gmlp_ln_b'].reshape(1, -1), w0, b0)


def _conv_step(st, cur, w, bias):
    y = bias
    for t in range(CONV_W - 1):
        y = y + st[t:t + 1, :] * w[t:t + 1, :]
    return y + cur * w[CONV_W - 1:CONV_W, :]


def _ssd_step_kernel(xs_ref, bc_ref, z_ref, dt_ref, stx_ref, stb_ref, h0_ref, cwx_ref, cbx_ref, cwb_ref, cbb_ref,
                     dtb_ref, a_ref, dsk_ref, ng_ref, y_ref, h_ref, ncx_ref, ncb_ref):
    xs_raw, bc_raw = xs_ref[...], bc_ref[...]
    stx, stb = stx_ref[...], stb_ref[...]
    ncx_ref[0:CONV_W - 2, :] = stx[1:CONV_W - 1]
    ncx_ref[CONV_W - 2:CONV_W - 1, :] = xs_raw
    ncb_ref[0:CONV_W - 2, :] = stb[1:CONV_W - 1]
    ncb_ref[CONV_W - 2:CONV_W - 1, :] = bc_raw
    xs = _silu(_conv_step(stx, xs_raw, cwx_ref[...], cbx_ref[...]))
    bc = _silu(_conv_step(stb, bc_raw, cwb_ref[...], cbb_ref[...]))
    dt = _softplus(dt_ref[...] + dtb_ref[...])
    decay = jnp.exp(dt * a_ref[...])
    xdt = _bf16_round(xs * dt)
    gw = D_INNER // G_C
    first_group = lax.broadcasted_iota(jnp.int32, (1, D_INNER), 1) < gw
    bm = [_bf16_round(bc[:, N_C * g:N_C * (g + 1)]) for g in range(G_C)]
    cm = [_bf16_round(bc[:, N_C * (G_C + g):N_C * (G_C + g + 1)]) for g in range(G_C)]
    cb = [_bf16_round(jnp.sum(cm[g] * bm[g], axis=1, keepdims=True)) for g in range(G_C)]
    y_diag = jnp.where(first_group, cb[0], cb[1]) * xdt
    h0 = h0_ref[...]
    y_off = jnp.concatenate([
        lax.dot_general(jnp.broadcast_to(cm[g], (8, N_C)).astype(BF16), h0[gw * g:gw * (g + 1)].astype(BF16),
                        (((1,), (1,)), ((), ())), preferred_element_type=F32)[0:1] for g in range(G_C)], axis=1)
    y = y_diag + y_off * decay
    y = (y + dsk_ref[...] * xs) * _silu(z_ref[...])
    outs = []
    for g in range(G_C):
        yg = y[:, gw * g:gw * (g + 1)]
        outs.append(yg * lax.rsqrt(jnp.mean(yg * yg, axis=-1, keepdims=True) + RMS_EPS))
    y_ref[...] = (jnp.concatenate(outs, axis=1) * ng_ref[...]).astype(y_ref.dtype)
    decay_rows = jnp.broadcast_to(decay, (LANES, D_INNER)).T
    xdt_rows = jnp.broadcast_to(xdt, (LANES, D_INNER)).T
    rows = lax.broadcasted_iota(jnp.int32, (D_INNER, N_C), 0)
    bm_rows = jnp.where(rows < gw, jnp.broadcast_to(bm[0], (D_INNER, N_C)), jnp.broadcast_to(bm[1], (D_INNER, N_C)))
    h_ref[...] = h0 * decay_rows + xdt_rows * bm_rows


def _ssd_step_call(proj3, state_conv, state_ssm, layer, p):
    bsz = proj3.shape[0]
    nbc = 2 * G_C * N_C
    ssm = state_ssm.reshape(state_ssm.shape[0], bsz, H_C * P_C, N_C)
    dt_lanes = jnp.repeat(proj3[:, :, COL_DT:COL_DT + H_C], P_C, axis=-1)
    per_lane = lambda v: jnp.repeat(v.astype(F32), P_C).reshape(1, -1)
    cw, cbias = p['conv_w'], p['conv_b'].reshape(1, -1)
    tok = lambda blk: (lambda i: (i, 0, blk))
    fixed = lambda i: (0, 0)
    return pl.pallas_call(
        _ssd_step_kernel,
        grid=(bsz,),
        in_specs=[
            pl.BlockSpec((None, 1, D_INNER), tok(COL_XBC // D_INNER)),
            pl.BlockSpec((None, 1, nbc), tok((COL_XBC + D_INNER) // nbc)),
            pl.BlockSpec((None, 1, D_INNER), tok(COL_Z // D_INNER)),
            pl.BlockSpec((None, 1, D_INNER), tok(0)),
            pl.BlockSpec((None, None, CONV_W - 1, D_INNER), lambda i: (layer, i, 0, 0)),
            pl.BlockSpec((None, None, CONV_W - 1, nbc), lambda i: (layer, i, 0, D_INNER // nbc)),
            pl.BlockSpec((None, None, H_C * P_C, N_C), lambda i: (layer, i, 0, 0)),
            pl.BlockSpec((CONV_W, D_INNER), fixed), pl.BlockSpec((1, D_INNER), fixed),
            pl.BlockSpec((CONV_W, nbc), fixed), pl.BlockSpec((1, nbc), fixed),
            pl.BlockSpec((1, D_INNER), fixed), pl.BlockSpec((1, D_INNER), fixed),
            pl.BlockSpec((1, D_INNER), fixed), pl.BlockSpec((1, D_INNER), fixed),
        ],
        out_specs=[pl.BlockSpec((None, 1, D_INNER), lambda i: (i, 0, 0)),
                   pl.BlockSpec((None, H_C * P_C, N_C), lambda i: (i, 0, 0)),
                   pl.BlockSpec((None, CONV_W - 1, D_INNER), lambda i: (i, 0, 0)),
                   pl.BlockSpec((None, CONV_W - 1, nbc), lambda i: (i, 0, 0))],
        out_shape=[jax.ShapeDtypeStruct((bsz, 1, D_INNER), BF16),
                   jax.ShapeDtypeStruct((bsz, H_C * P_C, N_C), F32),
                   jax.ShapeDtypeStruct((bsz, CONV_W - 1, D_INNER), F32),
                   jax.ShapeDtypeStruct((bsz, CONV_W - 1, nbc), F32)],
        compiler_params=_cparams("parallel"),
        name="ssd_sample",
    )(proj3, proj3, proj3, dt_lanes, state_conv, state_conv, ssm,
      cw[:, :D_INNER], cbias[:, :D_INNER], cw[:, D_INNER:], cbias[:, D_INNER:],
      per_lane(p['dt_bias']), per_lane(-jnp.exp(p['a_log'].astype(F32))), per_lane(p['d_skip']),
      p['ssm_norm_g'].reshape(1, -1))


def _xattn_kernel(x1_ref, wq_ref, mk_ref, mv_ref, wo_ref, g_ref, b_ref, wr_ref, br_ref, x2_ref, lg_ref):
    x1 = x1_ref[...]
    q = jnp.dot(x1.astype(BF16), wq_ref[...], preferred_element_type=F32)
    outs = []
    for h in range(XH):
        sl = slice(XHD * h, XHD * (h + 1))
        s = lax.dot_general(q[:, sl].astype(BF16), mk_ref[:, sl].astype(BF16),
                            (((1,), (1,)), ((), ())), preferred_element_type=F32) * XHD ** -0.5
        e = jnp.exp(s - jnp.max(s, axis=-1, keepdims=True))
        w = e / jnp.sum(e, axis=-1, keepdims=True)
        outs.append(jnp.dot(w.astype(BF16), mv_ref[:, sl].astype(BF16), preferred_element_type=F32).astype(BF16))
    o = jnp.concatenate(outs, axis=1)
    y = jnp.dot(o, wo_ref[...], preferred_element_type=F32) + DN_ALPHA * x1
    x2 = _layer_norm_rows(y, g_ref[...], b_ref[...])
    x2_ref[...] = x2
    lg_ref[...] = _top_k_lanes(jnp.dot(x2.astype(BF16), wr_ref[...], preferred_element_type=F32) + br_ref[...])


def _xattn_prompt_call(x1, mkv, lw, g, b, bsz, L, tm):
    nt = L // tm
    xw = XH * XHD
    row = lambda bb, i: (bb * nt + i, 0)
    fixed = lambda bb, i: (0, 0)
    n_lg = lw['w_router'].shape[1]
    return pl.pallas_call(
        _xattn_kernel,
        grid=(bsz, nt),
        in_specs=[
            pl.BlockSpec((tm, D_MODEL), row),
            pl.BlockSpec((D_MODEL, xw), fixed),
            pl.BlockSpec((N_MEM, xw), lambda bb, i: (bb, 0)),
            pl.BlockSpec((N_MEM, xw), lambda bb, i: (bb, 1)),
            pl.BlockSpec((xw, D_MODEL), fixed),
            pl.BlockSpec((1, D_MODEL), fixed), pl.BlockSpec((1, D_MODEL), fixed),
            pl.BlockSpec((D_MODEL, n_lg), fixed), pl.BlockSpec((1, n_lg), fixed),
        ],
        out_specs=[pl.BlockSpec((tm, D_MODEL), row), pl.BlockSpec((tm, n_lg), row)],
        out_shape=[jax.ShapeDtypeStruct((bsz * L, D_MODEL), F32), jax.ShapeDtypeStruct((bsz * L, n_lg), F32)],
        compiler_params=_cparams("parallel", "parallel"),
        name="memory_attention_prompt",
    )(x1, lw['w_xq'], mkv, mkv, lw['w_xo'], g.reshape(1, -1), b.reshape(1, -1),
      lw['w_router_bf16'], lw['b_router'])


def _xattn_decode_kernel(q_ref, k_ref, v_ref, o_ref):
    q = q_ref[...].astype(BF16).astype(F32)
    k = k_ref[...].astype(BF16).astype(F32)
    v = v_ref[...].astype(BF16).astype(F32)
    prod = k * q
    outs = []
    for h in range(XH):
        sl = slice(XHD * h, XHD * (h + 1))
        s = jnp.sum(prod[:, sl], axis=1, keepdims=True) * XHD ** -0.5
        e = jnp.exp(s - jnp.max(s, axis=0, keepdims=True))
        w = (e / jnp.sum(e, axis=0, keepdims=True)).astype(BF16).astype(F32)
        outs.append(jnp.sum(w * v[:, sl], axis=0, keepdims=True))
    o_ref[...] = jnp.concatenate(outs, axis=1)


def _xattn_decode_call(q, cache_k, cache_v, layer):
    bsz, xw = q.shape
    ck = cache_k.reshape(cache_k.shape[0], bsz, N_MEM, xw)
    cv = cache_v.reshape(cache_v.shape[0], bsz, N_MEM, xw)
    out = pl.pallas_call(
        _xattn_decode_kernel,
        grid=(bsz,),
        in_specs=[pl.BlockSpec((None, 1, xw), lambda i: (i, 0, 0)),
                  pl.BlockSpec((None, None, N_MEM, xw), lambda i: (layer, i, 0, 0)),
                  pl.BlockSpec((None, None, N_MEM, xw), lambda i: (layer, i, 0, 0))],
        out_specs=pl.BlockSpec((None, 1, xw), lambda i: (i, 0, 0)),
        out_shape=jax.ShapeDtypeStruct((bsz, 1, xw), F32),
        compiler_params=_cparams("parallel"),
        name="memory_attention_sample",
    )(q.reshape(bsz, 1, xw), ck, cv)
    return out.reshape(bsz, xw)


def _t5_bucket(dist):
    n = jnp.maximum(dist, 0)
    exact = N_BUCKETS // 2
    nf = jnp.maximum(n, 1).astype(F32)
    large = exact + (jnp.log(nf / exact) / math.log(MAX_DIST / exact) * (N_BUCKETS - exact)).astype(jnp.int32)
    return jnp.where(n < exact, n, jnp.minimum(large, N_BUCKETS - 1))


def _bucket_bias(rel_bias, dist):
    onehot = (_t5_bucket(dist)[..., None] == jnp.arange(N_BUCKETS)).astype(F32)
    return jnp.einsum('...b,bh->h...', onehot, rel_bias.astype(F32), precision=lax.Precision.HIGHEST)


def _prompt_layer(x, mkv, lw, p, swa_bias, bsz, L):
    T = bsz * L
    xb = x.astype(BF16)
    proj = _matmul(xb, lw['w_in'], 1024, 512)
    a_out = _swa_prompt_call(proj, swa_bias, p['sinks'], bsz, L)
    b_out = _gmlp_prompt_call(proj, p, T)
    c_out, h_last = _ssd_prompt_call(proj, p, bsz, L)
    mixed = _gate_merge(xb, a_out, b_out, c_out, lw['w_gate'], lw['b_gate'], lw['w_branch'], 512, 512)
    x1 = _mm_res_ln(mixed, lw['w_o'], x, p['ln1_g'], p['ln1_b'], 512)
    x2, logits = _xattn_prompt_call(x1, mkv, lw, p['ln2_g'], p['ln2_b'], bsz, L, 512)
    proj3 = proj.reshape(bsz, L, IN_DIM_PAD)
    win_k = proj3[:, L - WINDOW:, COL_K:COL_K + KV_A * HD_A].reshape(bsz, WINDOW, KV_A, HD_A)
    win_v = proj3[:, L - WINDOW:, COL_V:COL_V + KV_A * HD_A].reshape(bsz, WINDOW, KV_A, HD_A)
    conv = proj3[:, L - (CONV_W - 1):, COL_XBC:COL_XBC + CONV_DIM]
    return x2, logits, (win_k, win_v, conv, h_last.reshape(bsz, H_C, P_C, N_C))


def _sample_layer(x, layer, lw, p, rel_bias, cache_win_k, cache_win_v, state_conv, state_ssm, cache_mem_k, cache_mem_v):
    bsz = x.shape[0]
    xb = x.astype(BF16)
    proj = _matmul(xb, lw['w_in'], bsz, 512)
    proj3 = proj.reshape(bsz, 1, IN_DIM_PAD)
    a_out, win_k, win_v = _swa_decode_call(proj3, cache_win_k, cache_win_v, layer, rel_bias, p['sinks'])
    b_out, gv = _gmlp_step_call(proj, p)
    c_out, ssm, conv_x, conv_bc = _ssd_step_call(proj3, state_conv, state_ssm, layer, p)
    mixed = _gate_merge(xb, a_out, b_out, c_out.reshape(bsz, BR_W),
                        lw['w_gate'], lw['b_gate'], lw['w_branch'], bsz, 512)
    x1 = _mm_res_ln(mixed, lw['w_o'], x, p['ln1_g'], p['ln1_b'], bsz)
    q = _matmul(x1.astype(BF16), lw['w_xq'], bsz, XH * XHD)
    o = _xattn_decode_call(q, cache_mem_k, cache_mem_v, layer)
    x2 = _mm_res_ln(o.astype(BF16), lw['w_xo'], x1, p['ln2_g'], p['ln2_b'], bsz)
    logits = _router(x2, lw['w_router'], lw['b_router'], bsz)
    states = (win_k.reshape(bsz, WINDOW, KV_A, HD_A), win_v.reshape(bsz, WINDOW, KV_A, HD_A),
              jnp.concatenate([conv_x, conv_bc], axis=-1), ssm.reshape(bsz, H_C, P_C, N_C),
              gv.reshape(bsz, 1, W_B))
    return x2, logits, states


def kernel(x_prompt, x_sample, mem_prompt, cache_win_k, cache_win_v, state_conv, state_ssm, cache_mem_k, cache_mem_v, w_in, rel_bias, sinks, gmlp_ln_g, gmlp_ln_b, gmlp_ws, gmlp_bs, conv_w, conv_b, dt_bias, a_log, d_skip, ssm_norm_g, w_branch, w_gate, b_gate, w_o, ln1_g, ln1_b, w_xq, w_xk, w_xv, w_xo, ln2_g, ln2_b, w_router, b_router, w_e1, b_e1, w_e2, b_e2, ln3_g, ln3_b):
    assert cache_win_k.shape[2] == WINDOW and x_sample.shape[1] == 1
    n_prompt, n_mem = mem_prompt.shape[0], mem_prompt.shape[1]
    bp, lp = x_prompt.shape[:2]
    bs_ = x_sample.shape[0]
    hp, hs = x_prompt.reshape(bp * lp, D_MODEL), x_sample.reshape(bs_, D_MODEL)
    wk_p, wv_p, cv_p, ssm_p, mk_ps, mv_ps = [], [], [], [], [], []
    wk_s, wv_s, cv_s, ssm_s, gv_s = [], [], [], [], []
    mem_b = mem_prompt.reshape(n_prompt * n_mem, D_MODEL).astype(BF16)
    swa_bias = _swa_prompt_bias(rel_bias)
    experts = (w_e1, b_e1.reshape(DEPTH, N_EXPERTS, 1, 2 * D_FF), w_e2, b_e2.reshape(DEPTH, N_EXPERTS, 1, D_MODEL))
    for l in range(DEPTH):
        p = dict(sinks=sinks[l], gmlp_ln_g=gmlp_ln_g[l], gmlp_ln_b=gmlp_ln_b[l],
                 gmlp_ws=gmlp_ws[l], gmlp_bs=gmlp_bs[l], conv_w=conv_w[l], conv_b=conv_b[l],
                 dt_bias=dt_bias[l], a_log=a_log[l], d_skip=d_skip[l], ssm_norm_g=ssm_norm_g[l],
                 ln1_g=ln1_g[l], ln1_b=ln1_b[l], ln2_g=ln2_g[l], ln2_b=ln2_b[l])
        wi = w_in[l]
        seg = np.cumsum([0] + IN_SIZES)
        part = lambda n: wi[:, seg[n]:seg[n + 1]]
        w_in_cols = jnp.concatenate(
            [part(0), part(3), part(4), part(5), part(6), part(1), part(2), part(7),
             jnp.zeros((D_MODEL, IN_DIM_PAD - IN_DIM), F32)], axis=1).astype(BF16)
        w_router_pad = jnp.pad(w_router[l], ((0, 0), (0, LANES - N_EXPERTS)))
        lw = dict(
            w_in=w_in_cols,
            w_gate=w_gate[l].astype(BF16),
            b_gate=b_gate[l].reshape(1, -1),
            w_branch=w_branch[l].astype(BF16),
            w_o=w_o[l].astype(BF16),
            w_xq=w_xq[l].astype(BF16),
            w_xo=w_xo[l].astype(BF16),
            w_router=w_router_pad,
            w_router_bf16=w_router_pad.astype(BF16),
            b_router=jnp.pad(b_router[l], (0, LANES - N_EXPERTS)).reshape(1, -1),
        )
        w_kv = jnp.concatenate([w_xk[l], w_xv[l]], axis=1).astype(BF16)
        mkv = _matmul(mem_b, w_kv, n_mem, XH * XHD)
        mk = mkv[:, :XH * XHD].reshape(n_prompt, n_mem, XH, XHD)
        mv = mkv[:, XH * XHD:].reshape(n_prompt, n_mem, XH, XHD)
        x2_p, lg_p, st_p = _prompt_layer(hp, mkv, lw, p, swa_bias, bp, lp)
        x2_s, lg_s, st_s = _sample_layer(hs, l, lw, p, rel_bias, cache_win_k, cache_win_v, state_conv, state_ssm,
                                         cache_mem_k, cache_mem_v)
        hp, hs = _moe_ln(x2_p, x2_s, lg_p, lg_s, experts, l, ln3_g[l], ln3_b[l])
        wk_p.append(st_p[0]); wv_p.append(st_p[1]); cv_p.append(st_p[2]); ssm_p.append(st_p[3])
        mk_ps.append(mk); mv_ps.append(mv)
        wk_s.append(st_s[0]); wv_s.append(st_s[1]); cv_s.append(st_s[2]); ssm_s.append(st_s[3])
        gv_s.append(st_s[4])
    hp = hp.reshape(bp, lp, D_MODEL)
    hs = hs.reshape(bs_, 1, D_MODEL)
    return (hp, hs,
            jnp.stack(wk_p), jnp.stack(wv_p), jnp.stack(cv_p), jnp.stack(ssm_p),
            jnp.stack(mk_ps), jnp.stack(mv_ps),
            jnp.stack(wk_s), jnp.stack(wv_s), jnp.stack(cv_s), jnp.stack(ssm_s), jnp.stack(gv_s))
```

```python
import functools
import math

import numpy as np
import jax
import jax.numpy as jnp
from jax import lax
from jax.experimental import pallas as pl
from jax.experimental.pallas import tpu as pltpu

D_MODEL = 2048
DEPTH = 2
PAST_LEN = 16384
WINDOW = 128
H_A = 16
KV_A = 2
HD_A = 64
G_A = H_A // KV_A
N_BUCKETS = 32
MAX_DIST = 128
CHUNK_B = 128
GB = 16
CG_B = 64
W_B = GB * CG_B
D_INNER = 1024
P_C = 64
H_C = D_INNER // P_C
G_C = 2
R_C = H_C // G_C
N_C = 128
CONV_W = 4
CONV_DIM = D_INNER + 2 * G_C * N_C
SSD_CHUNK = 128
N_BRANCH = 3
BR_W = 1024
N_MEM = 256
XH = 4
XHD = 128
N_EXPERTS = 32
TOP_K = 4
D_FF = D_MODEL
SWIGLU_ALPHA = 1.702
SWIGLU_LIMIT = 7.0
DN_ALPHA = (2 * DEPTH) ** 0.25
LN_EPS = 1e-5
RMS_EPS = 1e-5

IN_SIZES = [H_A * HD_A, KV_A * HD_A, KV_A * HD_A, W_B, W_B, D_INNER, CONV_DIM, H_C]
IN_DIM = sum(IN_SIZES)
IN_DIM_PAD = 6144
COL_Q, COL_U, COL_GV, COL_Z, COL_XBC = 0, 1024, 2048, 3072, 4096
COL_K = COL_XBC + CONV_DIM
COL_V = COL_K + KV_A * HD_A
COL_DT = COL_V + KV_A * HD_A
LANES = 128

VMEM_LIMIT = 56 * 1024 * 1024
MOE_BM = 576
MOE_FC = 256
MOE_ROW_SPLIT = 2

F32 = jnp.float32
BF16 = jnp.bfloat16


def _cparams(*sem):
    return pltpu.CompilerParams(dimension_semantics=sem, vmem_limit_bytes=VMEM_LIMIT)


def _mm_kernel(x_ref, w_ref, o_ref):
    o_ref[...] = jnp.dot(x_ref[...], w_ref[...], preferred_element_type=F32).astype(o_ref.dtype)


def _matmul(x, w, tm, tn, out_dtype=F32):
    M, K = x.shape
    N = w.shape[1]
    assert M % tm == 0 and N % tn == 0
    return pl.pallas_call(
        _mm_kernel,
        grid=(N // tn, M // tm),
        in_specs=[pl.BlockSpec((tm, K), lambda j, i: (i, 0)),
                  pl.BlockSpec((K, tn), lambda j, i: (0, j))],
        out_specs=pl.BlockSpec((tm, tn), lambda j, i: (i, j)),
        out_shape=jax.ShapeDtypeStruct((M, N), out_dtype),
        compiler_params=_cparams("parallel", "parallel"),
        name="dense_matmul",
    )(x, w)


def _gate_merge_kernel(x_ref, a_ref, b_ref, c_ref, wg0_ref, wg1_ref, wg2_ref,
                       bg0_ref, bg1_ref, bg2_ref, wp_ref, o_ref):
    x = x_ref[...]
    acc = None
    for k, (br_ref, wg_ref, bg_ref) in enumerate(
            ((a_ref, wg0_ref, bg0_ref), (b_ref, wg1_ref, bg1_ref), (c_ref, wg2_ref, bg2_ref))):
        z = jnp.dot(x, wg_ref[...], preferred_element_type=F32) + bg_ref[...]
        gate = 1.0 / (1.0 + jnp.exp(-z))
        proj = jnp.dot(br_ref[...], wp_ref[k], preferred_element_type=F32)
        acc = gate * proj if acc is None else acc + gate * proj
    o_ref[...] = acc.astype(o_ref.dtype)


def _gate_merge(x, a, b, c, w_gate, b_gate, w_branch, tm, tn):
    M = x.shape[0]
    nt = D_MODEL // tn
    row = lambda j, i: (i, 0)
    in_specs = [pl.BlockSpec((tm, D_MODEL), row)] + [pl.BlockSpec((tm, BR_W), row)] * 3
    in_specs += [pl.BlockSpec((D_MODEL, tn), functools.partial(lambda j, i, k: (0, k * nt + j), k=k))
                 for k in range(N_BRANCH)]
    in_specs += [pl.BlockSpec((1, tn), functools.partial(lambda j, i, k: (0, k * nt + j), k=k))
                 for k in range(N_BRANCH)]
    in_specs += [pl.BlockSpec((N_BRANCH, BR_W, tn), lambda j, i: (0, 0, j))]
    return pl.pallas_call(
        _gate_merge_kernel,
        grid=(nt, M // tm),
        in_specs=in_specs,
        out_specs=pl.BlockSpec((tm, tn), lambda j, i: (i, j)),
        out_shape=jax.ShapeDtypeStruct((M, D_MODEL), BF16),
        compiler_params=_cparams("parallel", "parallel"),
        name="gate_merge",
    )(x, a, b, c, w_gate, w_gate, w_gate, b_gate, b_gate, b_gate, w_branch)


def _layer_norm_rows(y, g, b):
    mu = jnp.mean(y, axis=-1, keepdims=True)
    yc = y - mu
    var = jnp.mean(yc * yc, axis=-1, keepdims=True)
    return yc * lax.rsqrt(var + LN_EPS) * g + b


def _mm_res_ln_kernel(a_ref, w_ref, res_ref, g_ref, b_ref, o_ref):
    y = jnp.dot(a_ref[...], w_ref[...], preferred_element_type=F32) + DN_ALPHA * res_ref[...]
    o_ref[...] = _layer_norm_rows(y, g_ref[...], b_ref[...])


def _mm_res_ln(a, w, res, g, b, tm):
    M, K = a.shape
    row = lambda i: (i, 0)
    fixed = lambda i: (0, 0)
    return pl.pallas_call(
        _mm_res_ln_kernel,
        grid=(M // tm,),
        in_specs=[pl.BlockSpec((tm, K), row), pl.BlockSpec((K, D_MODEL), fixed),
                  pl.BlockSpec((tm, D_MODEL), row), pl.BlockSpec((1, D_MODEL), fixed),
                  pl.BlockSpec((1, D_MODEL), fixed)],
        out_specs=pl.BlockSpec((tm, D_MODEL), row),
        out_shape=jax.ShapeDtypeStruct((M, D_MODEL), F32),
        compiler_params=_cparams("parallel"),
        name="matmul_residual_layernorm",
    )(a, w, res, g.reshape(1, -1), b.reshape(1, -1))


def _top_k_lanes(logits):
    lane_i = lax.broadcasted_iota(jnp.int32, logits.shape, 1)
    lane_f = lane_i.astype(F32)
    work = jnp.where(lane_i < N_EXPERTS, logits, -jnp.inf)
    out = jnp.zeros_like(logits)
    for k in range(TOP_K):
        m = jnp.max(work, axis=-1, keepdims=True)
        idx = jnp.min(jnp.where(work == m, lane_f, float(LANES)), axis=-1, keepdims=True)
        out = jnp.where(lane_i == k, m, out)
        out = jnp.where(lane_i == TOP_K + k, idx, out)
        work = jnp.where(lane_f == idx, -jnp.inf, work)
    return out


def _router_kernel(x_ref, w_ref, b_ref, o_ref):
    acc = jnp.dot(x_ref[...].astype(BF16), w_ref[...].astype(BF16), preferred_element_type=F32)
    o_ref[...] = _top_k_lanes(acc + b_ref[...])


def _router(x, w_pad, b_pad, tm):
    M = x.shape[0]
    NP = w_pad.shape[1]
    return pl.pallas_call(
        _router_kernel,
        grid=(M // tm,),
        in_specs=[pl.BlockSpec((tm, D_MODEL), lambda i: (i, 0)),
                  pl.BlockSpec((D_MODEL, NP), lambda i: (0, 0)),
                  pl.BlockSpec((1, NP), lambda i: (0, 0))],
        out_specs=pl.BlockSpec((tm, NP), lambda i: (i, 0)),
        out_shape=jax.ShapeDtypeStruct((M, NP), F32),
        compiler_params=_cparams("parallel"),
        name="router_logits",
    )(x, w_pad, b_pad)


def _bf16_row_interleave(a, b):
    a32 = lax.bitcast_convert_type(a.astype(BF16).astype(F32), jnp.uint32)
    b32 = lax.bitcast_convert_type(b.astype(BF16).astype(F32), jnp.uint32)
    word = (a32 >> 16) | (b32 & jnp.uint32(0xFFFF0000))
    return pltpu.bitcast(word, BF16)


def _moe_kernel(be_ref, bx_ref, bv_ref, x_ref, w1_ref, b1_ref, w2_ref, b2_ref, o_ref):
    i = pl.program_id(0)
    f = pl.program_id(1)
    fc = w2_ref.shape[1] // 2
    bm = x_ref.shape[0]
    valid = bv_ref[i] == 1

    @pl.when(valid)
    def _():
        w1 = w1_ref[0].astype(BF16)
        w2 = w2_ref[0]
        w2q = _bf16_row_interleave(w2[:fc], w2[fc:])

        @pl.when(f == 0)
        def _():
            o_ref[...] = jnp.broadcast_to(b2_ref[0], o_ref.shape)

        hm = bm // MOE_ROW_SPLIT
        even = (lax.broadcasted_iota(jnp.int32, (hm, LANES), 1) & 1) == 0
        for part in range(MOE_ROW_SPLIT):
            rows = pl.ds(part * hm, hm)
            h = jnp.dot(x_ref[rows, :].astype(BF16), w1, preferred_element_type=F32) + b1_ref[0]
            acts = []
            for c in range(2 * fc // LANES):
                a = h[:, LANES * c:LANES * (c + 1)]
                b = h[:, 2 * fc + LANES * c:2 * fc + LANES * (c + 1)]
                glu = jnp.where(even, a, pltpu.roll(b, 1, axis=1))
                lin = jnp.where(even, pltpu.roll(a, LANES - 1, axis=1), b)
                glu = jnp.minimum(glu, SWIGLU_LIMIT)
                lin = jnp.clip(lin, -SWIGLU_LIMIT, SWIGLU_LIMIT)
                act = glu * (1.0 / (1.0 + jnp.exp(-SWIGLU_ALPHA * glu))) * (lin + 1.0)
                acts.append(act.astype(BF16))
            act = jnp.concatenate(acts, axis=1) if len(acts) > 1 else acts[0]
            o_ref[rows, :] += jnp.dot(act, w2q, preferred_element_type=F32)

    @pl.when(jnp.logical_not(valid) & (f == 0))
    def _():
        o_ref[...] = jnp.zeros_like(o_ref)


def _moe_blocks(p_end, n_valid, nb, bm):
    i32 = jnp.int32
    n_exp = p_end.shape[0]
    blk = jnp.arange(nb, dtype=i32)
    block_v = (blk < n_valid).astype(i32)
    block_x = jnp.minimum(blk, n_valid - 1)
    block_e = jnp.minimum(jnp.sum(p_end[None, :] <= (block_x * bm)[:, None], axis=1, dtype=i32), n_exp - 1)
    return block_e, block_x, block_v


def _moe_ffn_blocks(rows, block_e, block_x, block_v, w1, b1, w2, b2, layer, bm, fc):
    d_model = rows.shape[1]
    d_ff = w2.shape[2]
    nb = rows.shape[0] // bm
    nf = d_ff // (2 * fc)
    last_f = nf - 1

    def fsel(f, bv, i):
        return jnp.where(bv[i] == 1, f, last_f)

    grid_spec = pltpu.PrefetchScalarGridSpec(
        num_scalar_prefetch=3,
        grid=(nb, nf),
        in_specs=[
            pl.BlockSpec((bm, d_model), lambda i, f, be, bx, bv: (bx[i], 0)),
            pl.BlockSpec((None, 1, d_model, 4 * fc), lambda i, f, be, bx, bv: (layer, be[i], 0, fsel(f, bv, i))),
            pl.BlockSpec((None, 1, 1, 4 * fc), lambda i, f, be, bx, bv: (layer, be[i], 0, fsel(f, bv, i))),
            pl.BlockSpec((None, 1, 2 * fc, d_model), lambda i, f, be, bx, bv: (layer, be[i], fsel(f, bv, i), 0)),
            pl.BlockSpec((None, 1, 1, d_model), lambda i, f, be, bx, bv: (layer, be[i], 0, 0)),
        ],
        out_specs=pl.BlockSpec((bm, d_model), lambda i, f, be, bx, bv: (i, 0)),
    )
    return pl.pallas_call(
        _moe_kernel,
        grid_spec=grid_spec,
        out_shape=jax.ShapeDtypeStruct(rows.shape, F32),
        compiler_params=_cparams("arbitrary", "arbitrary"),
        name="moe_expert_ffn",
    )(block_e, block_x, block_v, rows, w1, b1, w2, b2)


def _combine_ln_kernel(*refs):
    g_refs = refs[:TOP_K]
    gate_ref, res_ref, lg_ref, lb_ref, o_ref = refs[TOP_K:]
    gate = gate_ref[...]
    ff = None
    for k in range(TOP_K):
        term = gate[:, k:k + 1] * g_refs[k][...]
        ff = term if ff is None else ff + term
    o_ref[...] = _layer_norm_rows(DN_ALPHA * res_ref[...] + ff, lg_ref[...], lb_ref[...])


def _combine_ln(gathered, gate, res, g, b, tm):
    T, d = res.shape
    nt = T // tm
    row = lambda i: (i, 0)
    fixed = lambda i: (0, 0)
    g_specs = [pl.BlockSpec((tm, d), functools.partial(lambda i, k: (k * nt + i, 0), k=k)) for k in range(TOP_K)]
    return pl.pallas_call(
        _combine_ln_kernel,
        grid=(nt,),
        in_specs=g_specs + [pl.BlockSpec((tm, TOP_K), row), pl.BlockSpec((tm, d), row),
                            pl.BlockSpec((1, d), fixed), pl.BlockSpec((1, d), fixed)],
        out_specs=pl.BlockSpec((tm, d), row),
        out_shape=jax.ShapeDtypeStruct((T, d), F32),
        compiler_params=_cparams("parallel"),
        name="moe_combine_layernorm",
    )(*([gathered] * TOP_K), gate, res, g.reshape(1, -1), b.reshape(1, -1))


def _moe_route(top_v, top_i, n_exp, bm):
    T = top_v.shape[0]
    i32 = jnp.int32
    gate = jax.nn.softmax(top_v, axis=-1)
    n_assign = T * TOP_K
    nb = -(-(n_assign + n_exp * (bm - 1)) // bm)
    ids = jnp.arange(n_assign, dtype=i32)
    e_sorted, order = lax.sort((top_i.reshape(-1).astype(i32), ids), num_keys=1, is_stable=True)
    is_e = e_sorted[:, None] == jnp.arange(n_exp, dtype=i32)[None, :]
    counts = jnp.sum(is_e, axis=0, dtype=i32)
    padded = (counts + bm - 1) // bm * bm
    p_end = jnp.cumsum(padded)
    u_start = jnp.cumsum(counts) - counts
    shift = (p_end - padded - u_start).astype(i32)
    dest = ids + jnp.sum(jnp.where(is_e, shift[None, :], 0), axis=1, dtype=i32)
    _, pos = lax.sort((order, dest), num_keys=1)
    src = jnp.zeros((nb * bm,), i32).at[dest].set(order // TOP_K, indices_are_sorted=True, unique_indices=True)
    n_valid = (p_end[-1] // bm).astype(i32)
    return gate, src, pos, (padded.astype(i32), p_end.astype(i32), n_valid, nb)


def _moe_ln(x2_p, x2_s, route_p, route_s, experts, layer, ln_g, ln_b):
    tp, ts = x2_p.shape[0], x2_s.shape[0]
    x2 = jnp.concatenate([x2_p, x2_s], axis=0)
    route =jnp.concatenate([route_p[:, :2 * TOP_K], route_s[:, :2 * TOP_K]], axis=0)
    top_v, top_i = route[:, :TOP_K], route[:, TOP_K:].astype(jnp.int32)
    gate, src, pos, (padded, p_end, n_valid, nb) = _moe_route(top_v, top_i, N_EXPERTS, MOE_BM)
    rows = jnp.take(x2, src, axis=0, mode='clip')
    w_e1, b_e1, w_e2, b_e2 = experts
    block_e, block_x, block_v = _moe_blocks(p_end, n_valid, nb, MOE_BM)
    out_rows = _moe_ffn_blocks(rows, block_e, block_x, block_v, w_e1, b_e1, w_e2, b_e2, layer, MOE_BM, MOE_FC)
    pos = pos.reshape(tp + ts, TOP_K)
    g_p = jnp.take(out_rows, pos[:tp].T.reshape(-1), axis=0, mode='clip')
    g_s = jnp.take(out_rows, pos[tp:].T.reshape(-1), axis=0, mode='clip')
    hp = _combine_ln(g_p, gate[:tp], x2_p, ln_g, ln_b, 256)
    hs = _combine_ln(g_s, gate[tp:], x2_s, ln_g, ln_b, ts)
    return hp, hs


def _half_lane_variants(t):
    lo = lax.broadcasted_iota(jnp.int32, t.shape, 1) < HD_A
    zero = jnp.zeros_like(t)
    tr = pltpu.roll(t, HD_A, axis=1)
    return [[jnp.where(lo, t, zero).astype(BF16), jnp.where(lo, zero, tr).astype(BF16)],
            [jnp.where(lo, tr, zero).astype(BF16), jnp.where(lo, zero, t).astype(BF16)]]


def _swa_kernel(sinks_ref, q_ref, kp_ref, kc_ref, vp_ref, vc_ref, bias_ref, o_ref):
    j = pl.program_id(1)
    kvar = _half_lane_variants(jnp.concatenate([kp_ref[...], kc_ref[...]], axis=0))
    vvar = _half_lane_variants(jnp.concatenate([vp_ref[...], vc_ref[...]], axis=0))
    col = lax.broadcasted_iota(jnp.int32, (WINDOW, 2 * WINDOW), 1)
    no_prev = col < jnp.where(j == 0, WINDOW, 0)
    for r in range(H_A // 2):
        g = (2 * r) // G_A
        qp = q_ref[:, LANES * r:LANES * (r + 1)].astype(BF16)
        acc = None
        for par in range(2):
            h = 2 * r + par
            s = lax.dot_general(qp, kvar[g][par], (((1,), (1,)), ((), ())), preferred_element_type=F32)
            s = s * HD_A ** -0.5 + bias_ref[h]
            s = jnp.where(no_prev, -jnp.inf, s)
            sk = sinks_ref[h]
            m = jnp.maximum(jnp.max(s, axis=-1, keepdims=True), sk)
            pr = jnp.exp(s - m)
            den = jnp.sum(pr, axis=-1, keepdims=True) + jnp.exp(sk - m)
            o = jnp.dot((pr / den).astype(BF16), vvar[g][par], preferred_element_type=F32)
            acc = o if acc is None else acc + o
        o_ref[:, LANES * r:LANES * (r + 1)] = acc.astype(o_ref.dtype)


def _swa_prompt_bias(rel_bias):
    i = jnp.arange(WINDOW)[:, None]
    j = jnp.arange(2 * WINDOW)[None, :]
    dist = i + WINDOW - j
    return jnp.where((dist >= 0) & (dist < WINDOW), _bucket_bias(rel_bias, dist), -jnp.inf)


def _swa_prompt_call(proj2d, bias, sinks, bsz, L):
    nb = L // WINDOW
    kcol, vcol = COL_K // LANES, COL_V // LANES
    cur = lambda b, j: b * nb + j
    prev = lambda b, j: b * nb + jnp.maximum(j - 1, 0)
    return pl.pallas_call(
        _swa_kernel,
        grid=(bsz, nb),
        in_specs=[
            pl.BlockSpec(memory_space=pltpu.SMEM),
            pl.BlockSpec((WINDOW, H_A * HD_A), lambda b, j: (cur(b, j), COL_Q // (H_A * HD_A))),
            pl.BlockSpec((WINDOW, LANES), lambda b, j: (prev(b, j), kcol)),
            pl.BlockSpec((WINDOW, LANES), lambda b, j: (cur(b, j), kcol)),
            pl.BlockSpec((WINDOW, LANES), lambda b, j: (prev(b, j), vcol)),
            pl.BlockSpec((WINDOW, LANES), lambda b, j: (cur(b, j), vcol)),
            pl.BlockSpec((H_A, WINDOW, 2 * WINDOW), lambda b, j: (0, 0, 0)),
        ],
        out_specs=pl.BlockSpec((WINDOW, H_A * HD_A), lambda b, j: (cur(b, j), 0)),
        out_shape=jax.ShapeDtypeStruct((bsz * L, H_A * HD_A), BF16),
        compiler_params=_cparams("parallel", "arbitrary"),
        name="swa_prompt",
    )(sinks, proj2d, proj2d, proj2d, proj2d, proj2d, bias)


def _gelu(x):
    return 0.5 * x * (1.0 + lax.erf(x * np.float32(np.sqrt(0.5))))


def _gmlp_kernel(u_ref, gv_ref, lng_ref, lnb_ref, w_ref, bias_ref, o_ref):
    u = _gelu(u_ref[...])
    gv = _layer_norm_rows(_gelu(gv_ref[...]), lng_ref[...], lnb_ref[...])
    lo = lax.broadcasted_iota(jnp.int32, (CHUNK_B, LANES), 1) < CG_B
    for r in range(GB // 2):
        vp = gv[:, LANES * r:LANES * (r + 1)]
        zero = jnp.zeros_like(vp)
        mix = jnp.dot(w_ref[2 * r], jnp.where(lo, vp, zero).astype(BF16), preferred_element_type=F32)
        mix += jnp.dot(w_ref[2 * r + 1], jnp.where(lo, zero, vp).astype(BF16), preferred_element_type=F32)
        sl = slice(LANES * r, LANES * (r + 1))
        o_ref[:, sl] = (u[:, sl] * (mix + bias_ref[:, sl])).astype(o_ref.dtype)


def _gmlp_prompt_call(proj2d, p, n_rows):
    w = (p['gmlp_ws'] * jnp.tril(jnp.ones((CHUNK_B, CHUNK_B), F32))).astype(BF16)
    bias = jnp.repeat(p['gmlp_bs'].T, CG_B, axis=1)
    fixed2 = lambda i: (0, 0)
    return pl.pallas_call(
        _gmlp_kernel,
        grid=(n_rows // CHUNK_B,),
        in_specs=[
            pl.BlockSpec((CHUNK_B, W_B), lambda i: (i, COL_U // W_B)),
            pl.BlockSpec((CHUNK_B, W_B), lambda i: (i, COL_GV // W_B)),
            pl.BlockSpec((1, W_B), fixed2), pl.BlockSpec((1, W_B), fixed2),
            pl.BlockSpec((GB, CHUNK_B, CHUNK_B), lambda i: (0, 0, 0)),
            pl.BlockSpec((CHUNK_B, W_B), fixed2),
        ],
        out_specs=pl.BlockSpec((CHUNK_B, W_B), lambda i: (i, 0)),
        out_shape=jax.ShapeDtypeStruct((n_rows, W_B), BF16),
        compiler_params=_cparams("parallel"),
        name="gmlp_prompt",
    )(proj2d, proj2d, p['gmlp_ln_g'].reshape(1, -1), p['gmlp_ln_b'].reshape(1, -1), w, bias)


def _silu(x):
    return x * (1.0 / (1.0 + jnp.exp(-x)))


def _softplus(x):
    return jnp.maximum(x, 0.0) + jnp.log1p(jnp.exp(-jnp.abs(x)))


def _causal_conv_chunk(cur, tail, w, bias):
    rows = lax.broadcasted_iota(jnp.int32, (8, cur.shape[1]), 0)
    y = jnp.broadcast_to(bias, cur.shape)
    y_head = jnp.broadcast_to(bias, (8, cur.shape[1]))
    for t in range(CONV_W):
        k = CONV_W - 1 - t
        wt = w[t:t + 1, :]
        if k == 0:
            y = y + cur * wt
            y_head = y_head + cur[:8] * wt
        else:
            sh = pltpu.roll(cur, k, axis=0)
            y = y + sh * wt
            y_head = y_head + jnp.where(rows < k, pltpu.roll(tail, k, axis=0), sh[:8]) * wt
    return jnp.concatenate([y_head, y[8:]], axis=0)


def _bf16_split3(x):
    p1 = x.astype(BF16)
    r1 = x - p1.astype(F32)
    p2 = r1.astype(BF16)
    p3 = (r1 - p2.astype(F32)).astype(BF16)
    return p1, p2, p3


def _ssd_kernel(xs_ref, bc_ref, z_ref, dt_ref, cwx_ref, cbx_ref, cwb_ref, cbb_ref, dtb_ref, a_ref,
                dsk_ref, ng_ref, y_ref, h_ref, state_ref, tailx_ref, tailb_ref):
    c = pl.program_id(1)
    C = SSD_CHUNK

    @pl.when(c == 0)
    def _():
        state_ref[...] = jnp.zeros_like(state_ref)
        tailx_ref[...] = jnp.zeros_like(tailx_ref)
        tailb_ref[...] = jnp.zeros_like(tailb_ref)

    xs_raw = xs_ref[...]
    bc_raw = bc_ref[...]
    xs = _silu(_causal_conv_chunk(xs_raw, tailx_ref[...], cwx_ref[...], cbx_ref[...]))
    bc = _silu(_causal_conv_chunk(bc_raw, tailb_ref[...], cwb_ref[...], cbb_ref[...]))
    tailx_ref[...] = xs_raw[C - 8:]
    tailb_ref[...] = bc_raw[C - 8:]

    dt = _softplus(dt_ref[...] + dtb_ref[...])
    da = dt * a_ref[...]
    row_i = lax.broadcasted_iota(jnp.int32, (C, C), 0)
    col_i = lax.broadcasted_iota(jnp.int32, (C, C), 1)
    causal = row_i >= col_i
    tril = jnp.where(causal, 1.0, 0.0).astype(BF16)
    acs = None
    for piece in _bf16_split3(da):
        t = jnp.dot(tril, piece, preferred_element_type=F32)
        acs = t if acs is None else acs + t
    acs_t = acs.T
    exp_acs = jnp.exp(acs)
    end_decay = jnp.exp(acs[C - 1:C, :] - acs)
    chunk_decay = jnp.exp(acs[C - 1:C, :])

    lo = lax.broadcasted_iota(jnp.int32, (C, LANES), 1) < P_C
    bm = [bc[:, N_C * g:N_C * (g + 1)].astype(BF16) for g in range(G_C)]
    cm = [bc[:, N_C * (G_C + g):N_C * (G_C + g + 1)].astype(BF16) for g in range(G_C)]
    cb = [lax.dot_general(cm[g], bm[g], (((1,), (1,)), ((), ())), preferred_element_type=F32)
          for g in range(G_C)]

    def per_lane_half(t, r):
        return jnp.where(lo, t[:, 2 * r:2 * r + 1], t[:, 2 * r + 1:2 * r + 2])

    ys = []
    for r in range(H_C // 2):
        g = (2 * r) // R_C
        sl = slice(LANES * r, LANES * (r + 1))
        x_pair = xs[:, sl]
        xdt = x_pair * per_lane_half(dt, r)
        zero = jnp.zeros_like(xdt)
        y_pair = None
        for par in range(2):
            h = 2 * r + par
            seg = acs[:, h:h + 1] - acs_t[h:h + 1, :]
            decay = jnp.where(causal, jnp.exp(seg), 0.0)
            m_h = (cb[g] * decay).astype(BF16)
            x_h = (jnp.where(lo, xdt, zero) if par == 0 else jnp.where(lo, zero, xdt)).astype(BF16)
            t = jnp.dot(m_h, x_h, preferred_element_type=F32)
            y_pair = t if y_pair is None else y_pair + t
        st = state_ref[sl, :]
        y_off = lax.dot_general(cm[g], st.astype(BF16), (((1,), (1,)), ((), ())), preferred_element_type=F32)
        y_pair = y_pair + y_off * per_lane_half(exp_acs, r)
        upd = lax.dot_general((xdt * per_lane_half(end_decay, r)).astype(BF16), bm[g],
                              (((0,), (0,)), ((), ())), preferred_element_type=F32)
        cd = jnp.concatenate([jnp.broadcast_to(chunk_decay[:, 2 * r:2 * r + 1], (P_C, N_C)),
                              jnp.broadcast_to(chunk_decay[:, 2 * r + 1:2 * r + 2], (P_C, N_C))], axis=0)
        state_ref[sl, :] = st * cd + upd
        ys.append(y_pair + dsk_ref[:, sl] * x_pair)
    y = jnp.concatenate(ys, axis=1) * _silu(z_ref[...])
    gw = D_INNER // G_C
    outs = []
    for g in range(G_C):
        yg = y[:, gw * g:gw * (g + 1)]
        outs.append(yg * lax.rsqrt(jnp.mean(yg * yg, axis=-1, keepdims=True) + RMS_EPS))
    y_ref[...] = (jnp.concatenate(outs, axis=1) * ng_ref[...]).astype(y_ref.dtype)

    @pl.when(c == pl.num_programs(1) - 1)
    def _():
        h_ref[...] = state_ref[...]


def _ssd_prompt_call(proj2d, p, bsz, L):
    nc = L // SSD_CHUNK
    row = lambda blk: (lambda b, c: (b * nc + c, blk))
    fixed = lambda b, c: (0, 0)
    pad_l = lambda v: jnp.pad(v.astype(F32), (0, LANES - H_C)).reshape(1, LANES)
    cw, cbias = p['conv_w'], p['conv_b'].reshape(1, -1)
    nbc = 2 * G_C * N_C
    args = (proj2d, proj2d, proj2d, proj2d,
            cw[:, :D_INNER], cbias[:, :D_INNER], cw[:, D_INNER:], cbias[:, D_INNER:],
            pad_l(p['dt_bias']), pad_l(-jnp.exp(p['a_log'].astype(F32))),
            jnp.repeat(p['d_skip'].astype(F32), P_C).reshape(1, -1), p['ssm_norm_g'].reshape(1, -1))
    return pl.pallas_call(
        _ssd_kernel,
        grid=(bsz, nc),
        in_specs=[
            pl.BlockSpec((SSD_CHUNK, D_INNER), row(COL_XBC // D_INNER)),
            pl.BlockSpec((SSD_CHUNK, nbc), row((COL_XBC + D_INNER) // nbc)),
            pl.BlockSpec((SSD_CHUNK, D_INNER), row(COL_Z // D_INNER)),
            pl.BlockSpec((SSD_CHUNK, LANES), row(COL_DT // LANES)),
            pl.BlockSpec((CONV_W, D_INNER), fixed), pl.BlockSpec((1, D_INNER), fixed),
            pl.BlockSpec((CONV_W, nbc), fixed), pl.BlockSpec((1, nbc), fixed),
            pl.BlockSpec((1, LANES), fixed), pl.BlockSpec((1, LANES), fixed),
            pl.BlockSpec((1, D_INNER), fixed), pl.BlockSpec((1, D_INNER), fixed),
        ],
        out_specs=[pl.BlockSpec((SSD_CHUNK, D_INNER), lambda b, c: (b * nc + c, 0)),
                   pl.BlockSpec((None, H_C * P_C, N_C), lambda b, c: (b, 0, 0))],
        out_shape=[jax.ShapeDtypeStruct((bsz * L, D_INNER), BF16),
                   jax.ShapeDtypeStruct((bsz, H_C * P_C, N_C), F32)],
        scratch_shapes=[pltpu.VMEM((H_C * P_C, N_C), F32), pltpu.VMEM((8, D_INNER), F32),
                        pltpu.VMEM((8, nbc), F32)],
        compiler_params=_cparams("parallel", "arbitrary"),
        name="ssd_prompt",
    )(*args)


def _bf16_round(x):
    return x.astype(BF16).astype(F32)


SWA_DECODE_TOKENS = 8


def _swa_decode_kernel(q_ref, kn_ref, vn_ref, ck_ref, cv_ref, bias_ref, sink_ref, o_ref, wk_ref, wv_ref):
    W = WINDOW
    last = lax.broadcasted_iota(jnp.int32, (W, LANES), 0) == W - 1
    lo = lax.broadcasted_iota(jnp.int32, (8, LANES), 1) < HD_A
    g0 = lax.broadcasted_iota(jnp.int32, (8, LANES), 0) < G_A // 2
    zero = jnp.zeros((8, LANES), F32)
    nt_dims = (((1,), (1,)), ((), ()))
    for t in range(q_ref.shape[0]):
        kw = jnp.where(last, jnp.broadcast_to(kn_ref[t], (W, LANES)), pltpu.roll(ck_ref[t], W - 1, axis=0))
        vw = jnp.where(last, jnp.broadcast_to(vn_ref[t], (W, LANES)), pltpu.roll(cv_ref[t], W - 1, axis=0))
        wk_ref[t] = kw
        wv_ref[t] = vw
        kb, vb = kw.astype(BF16), vw.astype(BF16)
        q = q_ref[t]
        qr = pltpu.roll(q, HD_A, axis=1)
        q_par = [jnp.where(g0, jnp.where(lo, q, zero), jnp.where(lo, zero, qr)),
                 jnp.where(g0, jnp.where(lo, qr, zero), jnp.where(lo, zero, q))]
        outs = []
        for par in range(2):
            s = lax.dot_general(q_par[par].astype(BF16), kb, nt_dims, preferred_element_type=F32)
            s = s * HD_A ** -0.5 + bias_ref[par]
            sk = sink_ref[par]
            m = jnp.maximum(jnp.max(s, axis=-1, keepdims=True), sk)
            pr = jnp.exp(s - m)
            den = jnp.sum(pr, axis=-1, keepdims=True) + jnp.exp(sk - m)
            outs.append(jnp.dot((pr / den).astype(BF16), vb, preferred_element_type=F32))
        o_even = jnp.where(g0, outs[0], pltpu.roll(outs[0], HD_A, axis=1))
        o_odd = jnp.where(g0, pltpu.roll(outs[1], HD_A, axis=1), outs[1])
        o_ref[t] = jnp.where(lo, o_even, o_odd).astype(o_ref.dtype)


def _swa_decode_call(proj3, cache_k, cache_v, layer, rel_bias, sinks):
    bsz = proj3.shape[0]
    nt = SWA_DECODE_TOKENS
    ck = cache_k.reshape(cache_k.shape[0], bsz, WINDOW, LANES)
    cv = cache_v.reshape(cache_v.shape[0], bsz, WINDOW, LANES)
    q8 = proj3[:, 0, COL_Q:COL_Q + H_A * HD_A].reshape(bsz, H_A // 2, LANES)
    dist = WINDOW - 1 - jnp.arange(WINDOW)
    bias = _bucket_bias(rel_bias, dist).reshape(H_A // 2, 2, WINDOW).transpose(1, 0, 2)
    sink = sinks.astype(F32).reshape(H_A // 2, 2, 1).transpose(1, 0, 2)
    new_tok = lambda blk: (lambda i: (i, 0, blk))
    cache = lambda i: (layer, i, 0, 0)
    tok3 = lambda i: (i, 0, 0)
    out, wk, wv = pl.pallas_call(
        _swa_decode_kernel,
        grid=(bsz // nt,),
        in_specs=[
            pl.BlockSpec((nt, H_A // 2, LANES), tok3),
            pl.BlockSpec((nt, 1, LANES), new_tok(COL_K // LANES)),
            pl.BlockSpec((nt, 1, LANES), new_tok(COL_V // LANES)),
            pl.BlockSpec((None, nt, WINDOW, LANES), cache),
            pl.BlockSpec((None, nt, WINDOW, LANES), cache),
            pl.BlockSpec((2, H_A // 2, WINDOW), lambda i: (0, 0, 0)),
            pl.BlockSpec((2, H_A // 2, 1), lambda i: (0, 0, 0)),
        ],
        out_specs=[pl.BlockSpec((nt, H_A // 2, LANES), tok3),
                   pl.BlockSpec((nt, WINDOW, LANES), tok3),
                   pl.BlockSpec((nt, WINDOW, LANES), tok3)],
        out_shape=[jax.ShapeDtypeStruct((bsz, H_A // 2, LANES), BF16),
                   jax.ShapeDtypeStruct((bsz, WINDOW, LANES), F32),
                   jax.ShapeDtypeStruct((bsz, WINDOW, LANES), F32)],
        compiler_params=_cparams("parallel"),
        name="swa_sample",
    )(q8, proj3, proj3, ck, cv, bias, sink)
    return out.reshape(bsz, H_A * HD_A), wk, wv


def _gmlp_step_kernel(u_ref, gv_ref, lng_ref, lnb_ref, w0_ref, b0_ref, o_ref, gv_out_ref):
    gv = _layer_norm_rows(_gelu(gv_ref[...]), lng_ref[...], lnb_ref[...])
    gv_out_ref[...] = gv
    mix = _bf16_round(w0_ref[...]) * _bf16_round(gv) + b0_ref[...]
    o_ref[...] = (_gelu(u_ref[...]) * mix).astype(o_ref.dtype)


def _gmlp_step_call(proj2d, p):
    n = proj2d.shape[0]
    w0 = jnp.repeat(p['gmlp_ws'][:, 0, 0], CG_B).reshape(1, -1)
    b0 = jnp.repeat(p['gmlp_bs'][:, 0], CG_B).reshape(1, -1)
    fixed = lambda i: (0, 0)
    return pl.pallas_call(
        _gmlp_step_kernel,
        grid=(1,),
        in_specs=[pl.BlockSpec((n, W_B), lambda i: (0, COL_U // W_B)),
                  pl.BlockSpec((n, W_B), lambda i: (0, COL_GV // W_B)),
                  pl.BlockSpec((1, W_B), fixed), pl.BlockSpec((1, W_B), fixed),
                  pl.BlockSpec((1, W_B), fixed), pl.BlockSpec((1, W_B), fixed)],
        out_specs=[pl.BlockSpec((n, W_B), fixed), pl.BlockSpec((n, W_B), fixed)],
        out_shape=[jax.ShapeDtypeStruct((n, W_B), BF16), jax.ShapeDtypeStruct((n, W_B), F32)],
        compiler_params=_cparams("arbitrary"),
        name="gmlp_sample",
    )(proj2d, proj2d, p['gmlp_ln_g'].reshape(1, -1), p['gmlp_ln_b'].reshape(1, -1), w0, b0)


def _conv_step(st, cur, w, bias):
    y = bias
    for t in range(CONV_W - 1):
        y = y + st[t:t + 1, :] * w[t:t + 1, :]
    return y + cur * w[CONV_W - 1:CONV_W, :]


def _ssd_step_kernel(xs_ref, bc_ref, z_ref, dt_ref, stx_ref, stb_ref, h0_ref, cwx_ref, cbx_ref, cwb_ref, cbb_ref,
                     dtb_ref, a_ref, dsk_ref, ng_ref, y_ref, h_ref, ncx_ref, ncb_ref):
    xs_raw, bc_raw = xs_ref[...], bc_ref[...]
    stx, stb = stx_ref[...], stb_ref[...]
    ncx_ref[0:CONV_W - 2, :] = stx[1:CONV_W - 1]
    ncx_ref[CONV_W - 2:CONV_W - 1, :] = xs_raw
    ncb_ref[0:CONV_W - 2, :] = stb[1:CONV_W - 1]
    ncb_ref[CONV_W - 2:CONV_W - 1, :] = bc_raw
    xs = _silu(_conv_step(stx, xs_raw, cwx_ref[...], cbx_ref[...]))
    bc = _silu(_conv_step(stb, bc_raw, cwb_ref[...], cbb_ref[...]))
    dt = _softplus(dt_ref[...] + dtb_ref[...])
    decay = jnp.exp(dt * a_ref[...])
    xdt = _bf16_round(xs * dt)
    gw = D_INNER // G_C
    first_group = lax.broadcasted_iota(jnp.int32, (1, D_INNER), 1) < gw
    bm = [_bf16_round(bc[:, N_C * g:N_C * (g + 1)]) for g in range(G_C)]
    cm = [_bf16_round(bc[:, N_C * (G_C + g):N_C * (G_C + g + 1)]) for g in range(G_C)]
    cb = [_bf16_round(jnp.sum(cm[g] * bm[g], axis=1, keepdims=True)) for g in range(G_C)]
    y_diag = jnp.where(first_group, cb[0], cb[1]) * xdt
    h0 = h0_ref[...]
    y_off = jnp.concatenate([
        lax.dot_general(jnp.broadcast_to(cm[g], (8, N_C)).astype(BF16), h0[gw * g:gw * (g + 1)].astype(BF16),
                        (((1,), (1,)), ((), ())), preferred_element_type=F32)[0:1] for g in range(G_C)], axis=1)
    y = y_diag + y_off * decay
    y = (y + dsk_ref[...] * xs) * _silu(z_ref[...])
    outs = []
    for g in range(G_C):
        yg = y[:, gw * g:gw * (g + 1)]
        outs.append(yg * lax.rsqrt(jnp.mean(yg * yg, axis=-1, keepdims=True) + RMS_EPS))
    y_ref[...] = (jnp.concatenate(outs, axis=1) * ng_ref[...]).astype(y_ref.dtype)
    decay_rows = jnp.broadcast_to(decay, (LANES, D_INNER)).T
    xdt_rows = jnp.broadcast_to(xdt, (LANES, D_INNER)).T
    rows = lax.broadcasted_iota(jnp.int32, (D_INNER, N_C), 0)
    bm_rows = jnp.where(rows < gw, jnp.broadcast_to(bm[0], (D_INNER, N_C)), jnp.broadcast_to(bm[1], (D_INNER, N_C)))
    h_ref[...] = h0 * decay_rows + xdt_rows * bm_rows


def _ssd_step_call(proj3, state_conv, state_ssm, layer, p):
    bsz = proj3.shape[0]
    nbc = 2 * G_C * N_C
    ssm = state_ssm.reshape(state_ssm.shape[0], bsz, H_C * P_C, N_C)
    dt_lanes = jnp.repeat(proj3[:, :, COL_DT:COL_DT + H_C], P_C, axis=-1)
    per_lane = lambda v: jnp.repeat(v.astype(F32), P_C).reshape(1, -1)
    cw, cbias = p['conv_w'], p['conv_b'].reshape(1, -1)
    tok = lambda blk: (lambda i: (i, 0, blk))
    fixed = lambda i: (0, 0)
    return pl.pallas_call(
        _ssd_step_kernel,
        grid=(bsz,),
        in_specs=[
            pl.BlockSpec((None, 1, D_INNER), tok(COL_XBC // D_INNER)),
            pl.BlockSpec((None, 1, nbc), tok((COL_XBC + D_INNER) // nbc)),
            pl.BlockSpec((None, 1, D_INNER), tok(COL_Z // D_INNER)),
            pl.BlockSpec((None, 1, D_INNER), tok(0)),
            pl.BlockSpec((None, None, CONV_W - 1, D_INNER), lambda i: (layer, i, 0, 0)),
            pl.BlockSpec((None, None, CONV_W - 1, nbc), lambda i: (layer, i, 0, D_INNER // nbc)),
            pl.BlockSpec((None, None, H_C * P_C, N_C), lambda i: (layer, i, 0, 0)),
            pl.BlockSpec((CONV_W, D_INNER), fixed), pl.BlockSpec((1, D_INNER), fixed),
            pl.BlockSpec((CONV_W, nbc), fixed), pl.BlockSpec((1, nbc), fixed),
            pl.BlockSpec((1, D_INNER), fixed), pl.BlockSpec((1, D_INNER), fixed),
            pl.BlockSpec((1, D_INNER), fixed), pl.BlockSpec((1, D_INNER), fixed),
        ],
        out_specs=[pl.BlockSpec((None, 1, D_INNER), lambda i: (i, 0, 0)),
                   pl.BlockSpec((None, H_C * P_C, N_C), lambda i: (i, 0, 0)),
                   pl.BlockSpec((None, CONV_W - 1, D_INNER), lambda i: (i, 0, 0)),
                   pl.BlockSpec((None, CONV_W - 1, nbc), lambda i: (i, 0, 0))],
        out_shape=[jax.ShapeDtypeStruct((bsz, 1, D_INNER), BF16),
                   jax.ShapeDtypeStruct((bsz, H_C * P_C, N_C), F32),
                   jax.ShapeDtypeStruct((bsz, CONV_W - 1, D_INNER), F32),
                   jax.ShapeDtypeStruct((bsz, CONV_W - 1, nbc), F32)],
        compiler_params=_cparams("parallel"),
        name="ssd_sample",
    )(proj3, proj3, proj3, dt_lanes, state_conv, state_conv, ssm,
      cw[:, :D_INNER], cbias[:, :D_INNER], cw[:, D_INNER:], cbias[:, D_INNER:],
      per_lane(p['dt_bias']), per_lane(-jnp.exp(p['a_log'].astype(F32))), per_lane(p['d_skip']),
      p['ssm_norm_g'].reshape(1, -1))


def _xattn_kernel(x1_ref, wq_ref, mk_ref, mv_ref, wo_ref, g_ref, b_ref, wr_ref, br_ref, x2_ref, lg_ref):
    x1 = x1_ref[...]
    q = jnp.dot(x1.astype(BF16), wq_ref[...], preferred_element_type=F32)
    outs = []
    for h in range(XH):
        sl = slice(XHD * h, XHD * (h + 1))
        s = lax.dot_general(q[:, sl].astype(BF16), mk_ref[:, sl].astype(BF16),
                            (((1,), (1,)), ((), ())), preferred_element_type=F32) * XHD ** -0.5
        e = jnp.exp(s - jnp.max(s, axis=-1, keepdims=True))
        w = e / jnp.sum(e, axis=-1, keepdims=True)
        outs.append(jnp.dot(w.astype(BF16), mv_ref[:, sl].astype(BF16), preferred_element_type=F32).astype(BF16))
    o = jnp.concatenate(outs, axis=1)
    y = jnp.dot(o, wo_ref[...], preferred_element_type=F32) + DN_ALPHA * x1
    x2 = _layer_norm_rows(y, g_ref[...], b_ref[...])
    x2_ref[...] = x2
    lg_ref[...] = _top_k_lanes(jnp.dot(x2.astype(BF16), wr_ref[...], preferred_element_type=F32) + br_ref[...])


def _xattn_prompt_call(x1, mkv, lw, g, b, bsz, L, tm):
    nt = L // tm
    xw = XH * XHD
    row = lambda bb, i: (bb * nt + i, 0)
    fixed = lambda bb, i: (0, 0)
    n_lg = lw['w_router'].shape[1]
    return pl.pallas_call(
        _xattn_kernel,
        grid=(bsz, nt),
        in_specs=[
            pl.BlockSpec((tm, D_MODEL), row),
            pl.BlockSpec((D_MODEL, xw), fixed),
            pl.BlockSpec((N_MEM, xw), lambda bb, i: (bb, 0)),
            pl.BlockSpec((N_MEM, xw), lambda bb, i: (bb, 1)),
            pl.BlockSpec((xw, D_MODEL), fixed),
            pl.BlockSpec((1, D_MODEL), fixed), pl.BlockSpec((1, D_MODEL), fixed),
            pl.BlockSpec((D_MODEL, n_lg), fixed), pl.BlockSpec((1, n_lg), fixed),
        ],
        out_specs=[pl.BlockSpec((tm, D_MODEL), row), pl.BlockSpec((tm, n_lg), row)],
        out_shape=[jax.ShapeDtypeStruct((bsz * L, D_MODEL), F32), jax.ShapeDtypeStruct((bsz * L, n_lg), F32)],
        compiler_params=_cparams("parallel", "parallel"),
        name="memory_attention_prompt",
    )(x1, lw['w_xq'], mkv, mkv, lw['w_xo'], g.reshape(1, -1), b.reshape(1, -1),
      lw['w_router_bf16'], lw['b_router'])


def _xattn_decode_kernel(q_ref, k_ref, v_ref, o_ref):
    q = q_ref[...].astype(BF16).astype(F32)
    k = k_ref[...].astype(BF16).astype(F32)
    v = v_ref[...].astype(BF16).astype(F32)
    prod = k * q
    outs = []
    for h in range(XH):
        sl = slice(XHD * h, XHD * (h + 1))
        s = jnp.sum(prod[:, sl], axis=1, keepdims=True) * XHD ** -0.5
        e = jnp.exp(s - jnp.max(s, axis=0, keepdims=True))
        w = (e / jnp.sum(e, axis=0, keepdims=True)).astype(BF16).astype(F32)
        outs.append(jnp.sum(w * v[:, sl], axis=0, keepdims=True))
    o_ref[...] = jnp.concatenate(outs, axis=1)


def _xattn_decode_call(q, cache_k, cache_v, layer):
    bsz, xw = q.shape
    ck = cache_k.reshape(cache_k.shape[0], bsz, N_MEM, xw)
    cv = cache_v.reshape(cache_v.shape[0], bsz, N_MEM, xw)
    out = pl.pallas_call(
        _xattn_decode_kernel,
        grid=(bsz,),
        in_specs=[pl.BlockSpec((None, 1, xw), lambda i: (i, 0, 0)),
                  pl.BlockSpec((None, None, N_MEM, xw), lambda i: (layer, i, 0, 0)),
                  pl.BlockSpec((None, None, N_MEM, xw), lambda i: (layer, i, 0, 0))],
        out_specs=pl.BlockSpec((None, 1, xw), lambda i: (i, 0, 0)),
        out_shape=jax.ShapeDtypeStruct((bsz, 1, xw), F32),
        compiler_params=_cparams("parallel"),
        name="memory_attention_sample",
    )(q.reshape(bsz, 1, xw), ck, cv)
    return out.reshape(bsz, xw)


def _t5_bucket(dist):
    n = jnp.maximum(dist, 0)
    exact = N_BUCKETS // 2
    nf = jnp.maximum(n, 1).astype(F32)
    large = exact + (jnp.log(nf / exact) / math.log(MAX_DIST / exact) * (N_BUCKETS - exact)).astype(jnp.int32)
    return jnp.where(n < exact, n, jnp.minimum(large, N_BUCKETS - 1))


def _bucket_bias(rel_bias, dist):
    onehot = (_t5_bucket(dist)[..., None] == jnp.arange(N_BUCKETS)).astype(F32)
    return jnp.einsum('...b,bh->h...', onehot, rel_bias.astype(F32), precision=lax.Precision.HIGHEST)


def _prompt_layer(x, mkv, lw, p, swa_bias, bsz, L):
    T = bsz * L
    xb = x.astype(BF16)
    proj = _matmul(xb, lw['w_in'], 1024, 512)
    a_out = _swa_prompt_call(proj, swa_bias, p['sinks'], bsz, L)
    b_out = _gmlp_prompt_call(proj, p, T)
    c_out, h_last = _ssd_prompt_call(proj, p, bsz, L)
    mixed = _gate_merge(xb, a_out, b_out, c_out, lw['w_gate'], lw['b_gate'], lw['w_branch'], 512, 512)
    x1 = _mm_res_ln(mixed, lw['w_o'], x, p['ln1_g'], p['ln1_b'], 512)
    x2, logits = _xattn_prompt_call(x1, mkv, lw, p['ln2_g'], p['ln2_b'], bsz, L, 512)
    proj3 = proj.reshape(bsz, L, IN_DIM_PAD)
    win_k = proj3[:, L - WINDOW:, COL_K:COL_K + KV_A * HD_A].reshape(bsz, WINDOW, KV_A, HD_A)
    win_v = proj3[:, L - WINDOW:, COL_V:COL_V + KV_A * HD_A].reshape(bsz, WINDOW, KV_A, HD_A)
    conv = proj3[:, L - (CONV_W - 1):, COL_XBC:COL_XBC + CONV_DIM]
    return x2, logits, (win_k, win_v, conv, h_last.reshape(bsz, H_C, P_C, N_C))


def _sample_layer(x, layer, lw, p, rel_bias, cache_win_k, cache_win_v, state_conv, state_ssm, cache_mem_k, cache_mem_v):
    bsz = x.shape[0]
    xb = x.astype(BF16)
    proj = _matmul(xb, lw['w_in'], bsz, 512)
    proj3 = proj.reshape(bsz, 1, IN_DIM_PAD)
    a_out, win_k, win_v = _swa_decode_call(proj3, cache_win_k, cache_win_v, layer, rel_bias, p['sinks'])
    b_out, gv = _gmlp_step_call(proj, p)
    c_out, ssm, conv_x, conv_bc = _ssd_step_call(proj3, state_conv, state_ssm, layer, p)
    mixed = _gate_merge(xb, a_out, b_out, c_out.reshape(bsz, BR_W),
                        lw['w_gate'], lw['b_gate'], lw['w_branch'], bsz, 512)
    x1 = _mm_res_ln(mixed, lw['w_o'], x, p['ln1_g'], p['ln1_b'], bsz)
    q = _matmul(x1.astype(BF16), lw['w_xq'], bsz, XH * XHD)
    o = _xattn_decode_call(q, cache_mem_k, cache_mem_v, layer)
    x2 = _mm_res_ln(o.astype(BF16), lw['w_xo'], x1, p['ln2_g'], p['ln2_b'], bsz)
    logits = _router(x2, lw['w_router'], lw['b_router'], bsz)
    states = (win_k.reshape(bsz, WINDOW, KV_A, HD_A), win_v.reshape(bsz, WINDOW, KV_A, HD_A),
              jnp.concatenate([conv_x, conv_bc], axis=-1), ssm.reshape(bsz, H_C, P_C, N_C),
              gv.reshape(bsz, 1, W_B))
    return x2, logits, states


def kernel(x_prompt, x_sample, mem_prompt, cache_win_k, cache_win_v, state_conv, state_ssm, cache_mem_k, cache_mem_v, w_in, rel_bias, sinks, gmlp_ln_g, gmlp_ln_b, gmlp_ws, gmlp_bs, conv_w, conv_b, dt_bias, a_log, d_skip, ssm_norm_g, w_branch, w_gate, b_gate, w_o, ln1_g, ln1_b, w_xq, w_xk, w_xv, w_xo, ln2_g, ln2_b, w_router, b_router, w_e1, b_e1, w_e2, b_e2, ln3_g, ln3_b):
    assert cache_win_k.shape[2] == WINDOW and x_sample.shape[1] == 1
    n_prompt, n_mem = mem_prompt.shape[0], mem_prompt.shape[1]
    bp, lp = x_prompt.shape[:2]
    bs_ = x_sample.shape[0]
    hp, hs = x_prompt.reshape(bp * lp, D_MODEL), x_sample.reshape(bs_, D_MODEL)
    wk_p, wv_p, cv_p, ssm_p, mk_ps, mv_ps = [], [], [], [], [], []
    wk_s, wv_s, cv_s, ssm_s, gv_s = [], [], [], [], []
    mem_b = mem_prompt.reshape(n_prompt * n_mem, D_MODEL).astype(BF16)
    swa_bias = _swa_prompt_bias(rel_bias)
    experts = (w_e1, b_e1.reshape(DEPTH, N_EXPERTS, 1, 2 * D_FF), w_e2, b_e2.reshape(DEPTH, N_EXPERTS, 1, D_MODEL))
    for l in range(DEPTH):
        p = dict(sinks=sinks[l], gmlp_ln_g=gmlp_ln_g[l], gmlp_ln_b=gmlp_ln_b[l],
                 gmlp_ws=gmlp_ws[l], gmlp_bs=gmlp_bs[l], conv_w=conv_w[l], conv_b=conv_b[l],
                 dt_bias=dt_bias[l], a_log=a_log[l], d_skip=d_skip[l], ssm_norm_g=ssm_norm_g[l],
                 ln1_g=ln1_g[l], ln1_b=ln1_b[l], ln2_g=ln2_g[l], ln2_b=ln2_b[l])
        wi = w_in[l]
        seg = np.cumsum([0] + IN_SIZES)
        part = lambda n: wi[:, seg[n]:seg[n + 1]]
        w_in_cols = jnp.concatenate(
            [part(0), part(3), part(4), part(5), part(6), part(1), part(2), part(7),
             jnp.zeros((D_MODEL, IN_DIM_PAD - IN_DIM), F32)], axis=1).astype(BF16)
        w_router_pad = jnp.pad(w_router[l], ((0, 0), (0, LANES - N_EXPERTS)))
        lw = dict(
            w_in=w_in_cols,
            w_gate=w_gate[l].astype(BF16),
            b_gate=b_gate[l].reshape(1, -1),
            w_branch=w_branch[l].astype(BF16),
            w_o=w_o[l].astype(BF16),
            w_xq=w_xq[l].astype(BF16),
            w_xo=w_xo[l].astype(BF16),
            w_router=w_router_pad,
            w_router_bf16=w_router_pad.astype(BF16),
            b_router=jnp.pad(b_router[l], (0, LANES - N_EXPERTS)).reshape(1, -1),
        )
        w_kv = jnp.concatenate([w_xk[l], w_xv[l]], axis=1).astype(BF16)
        mkv = _matmul(mem_b, w_kv, n_mem, XH * XHD)
        mk = mkv[:, :XH * XHD].reshape(n_prompt, n_mem, XH, XHD)
        mv = mkv[:, XH * XHD:].reshape(n_prompt, n_mem, XH, XHD)
        x2_p, lg_p, st_p = _prompt_layer(hp, mkv, lw, p, swa_bias, bp, lp)
        x2_s, lg_s, st_s = _sample_layer(hs, l, lw, p, rel_bias, cache_win_k, cache_win_v, state_conv, state_ssm,
                                         cache_mem_k, cache_mem_v)
        hp, hs = _moe_ln(x2_p, x2_s, lg_p, lg_s, experts, l, ln3_g[l], ln3_b[l])
        wk_p.append(st_p[0]); wv_p.append(st_p[1]); cv_p.append(st_p[2]); ssm_p.append(st_p[3])
        mk_ps.append(mk); mv_ps.append(mv)
        wk_s.append(st_s[0]); wv_s.append(st_s[1]); cv_s.append(st_s[2]); ssm_s.append(st_s[3])
        gv_s.append(st_s[4])
    hp = hp.reshape(bp, lp, D_MODEL)
    hs = hs.reshape(bs_, 1, D_MODEL)
    return (hp, hs,
            jnp.stack(wk_p), jnp.stack(wv_p), jnp.stack(cv_p), jnp.stack(ssm_p),
            jnp.stack(mk_ps), jnp.stack(mv_ps),
            jnp.stack(wk_s), jnp.stack(wv_s), jnp.stack(cv_s), jnp.stack(ssm_s), jnp.stack(gv_s))
```

```python
import functools
import math

import numpy as np
import jax
import jax.numpy as jnp
from jax import lax
from jax.experimental import pallas as pl
from jax.experimental.pallas import tpu as pltpu

D_MODEL = 2048
DEPTH = 2
PAST_LEN = 16384
WINDOW = 128
H_A = 16
KV_A = 2
HD_A = 64
G_A = H_A // KV_A
N_BUCKETS = 32
MAX_DIST = 128
CHUNK_B = 128
GB = 16
CG_B = 64
W_B = GB * CG_B
D_INNER = 1024
P_C = 64
H_C = D_INNER // P_C
G_C = 2
R_C = H_C // G_C
N_C = 128
CONV_W = 4
CONV_DIM = D_INNER + 2 * G_C * N_C
SSD_CHUNK = 128
N_BRANCH = 3
BR_W = 1024
N_MEM = 256
XH = 4
XHD = 128
N_EXPERTS = 32
TOP_K = 4
D_FF = D_MODEL
SWIGLU_ALPHA = 1.702
SWIGLU_LIMIT = 7.0
DN_ALPHA = (2 * DEPTH) ** 0.25
LN_EPS = 1e-5
RMS_EPS = 1e-5

IN_SIZES = [H_A * HD_A, KV_A * HD_A, KV_A * HD_A, W_B, W_B, D_INNER, CONV_DIM, H_C]
IN_DIM = sum(IN_SIZES)
IN_DIM_PAD = 6144
COL_Q, COL_U, COL_GV, COL_Z, COL_XBC = 0, 1024, 2048, 3072, 4096
COL_K = COL_XBC + CONV_DIM
COL_V = COL_K + KV_A * HD_A
COL_DT = COL_V + KV_A * HD_A
LANES = 128

VMEM_LIMIT = 56 * 1024 * 1024
MOE_BM = 576
MOE_FC = 256
MOE_ROW_SPLIT = 1

F32 = jnp.float32
BF16 = jnp.bfloat16


def _cparams(*sem):
    return pltpu.CompilerParams(dimension_semantics=sem, vmem_limit_bytes=VMEM_LIMIT)


def _mm_kernel(x_ref, w_ref, o_ref):
    o_ref[...] = jnp.dot(x_ref[...], w_ref[...], preferred_element_type=F32).astype(o_ref.dtype)


def _matmul(x, w, tm, tn, out_dtype=F32):
    M, K = x.shape
    N = w.shape[1]
    assert M % tm == 0 and N % tn == 0
    return pl.pallas_call(
        _mm_kernel,
        grid=(N // tn, M // tm),
        in_specs=[pl.BlockSpec((tm, K), lambda j, i: (i, 0)),
                  pl.BlockSpec((K, tn), lambda j, i: (0, j))],
        out_specs=pl.BlockSpec((tm, tn), lambda j, i: (i, j)),
        out_shape=jax.ShapeDtypeStruct((M, N), out_dtype),
        compiler_params=_cparams("parallel", "parallel"),
        name="dense_matmul",
    )(x, w)


def _gate_merge_kernel(x_ref, a_ref, b_ref, c_ref, wg0_ref, wg1_ref, wg2_ref,
                       bg0_ref, bg1_ref, bg2_ref, wp_ref, o_ref):
    x = x_ref[...]
    acc = None
    for k, (br_ref, wg_ref, bg_ref) in enumerate(
            ((a_ref, wg0_ref, bg0_ref), (b_ref, wg1_ref, bg1_ref), (c_ref, wg2_ref, bg2_ref))):
        z = jnp.dot(x, wg_ref[...], preferred_element_type=F32) + bg_ref[...]
        gate = 1.0 / (1.0 + jnp.exp(-z))
        proj = jnp.dot(br_ref[...], wp_ref[k], preferred_element_type=F32)
        acc = gate * proj if acc is None else acc + gate * proj
    o_ref[...] = acc.astype(o_ref.dtype)


def _gate_merge(x, a, b, c, w_gate, b_gate, w_branch, tm, tn):
    M = x.shape[0]
    nt = D_MODEL // tn
    row = lambda j, i: (i, 0)
    in_specs = [pl.BlockSpec((tm, D_MODEL), row)] + [pl.BlockSpec((tm, BR_W), row)] * 3
    in_specs += [pl.BlockSpec((D_MODEL, tn), functools.partial(lambda j, i, k: (0, k * nt + j), k=k))
                 for k in range(N_BRANCH)]
    in_specs += [pl.BlockSpec((1, tn), functools.partial(lambda j, i, k: (0, k * nt + j), k=k))
                 for k in range(N_BRANCH)]
    in_specs += [pl.BlockSpec((N_BRANCH, BR_W, tn), lambda j, i: (0, 0, j))]
    return pl.pallas_call(
        _gate_merge_kernel,
        grid=(nt, M // tm),
        in_specs=in_specs,
        out_specs=pl.BlockSpec((tm, tn), lambda j, i: (i, j)),
        out_shape=jax.ShapeDtypeStruct((M, D_MODEL), BF16),
        compiler_params=_cparams("parallel", "parallel"),
        name="gate_merge",
    )(x, a, b, c, w_gate, w_gate, w_gate, b_gate, b_gate, b_gate, w_branch)


def _layer_norm_rows(y, g, b):
    mu = jnp.mean(y, axis=-1, keepdims=True)
    yc = y - mu
    var = jnp.mean(yc * yc, axis=-1, keepdims=True)
    return yc * lax.rsqrt(var + LN_EPS) * g + b


def _mm_res_ln_kernel(a_ref, w_ref, res_ref, g_ref, b_ref, o_ref):
    y = jnp.dot(a_ref[...], w_ref[...], preferred_element_type=F32) + DN_ALPHA * res_ref[...]
    o_ref[...] = _layer_norm_rows(y, g_ref[...], b_ref[...])


def _mm_res_ln(a, w, res, g, b, tm):
    M, K = a.shape
    row = lambda i: (i, 0)
    fixed = lambda i: (0, 0)
    return pl.pallas_call(
        _mm_res_ln_kernel,
        grid=(M // tm,),
        in_specs=[pl.BlockSpec((tm, K), row), pl.BlockSpec((K, D_MODEL), fixed),
                  pl.BlockSpec((tm, D_MODEL), row), pl.BlockSpec((1, D_MODEL), fixed),
                  pl.BlockSpec((1, D_MODEL), fixed)],
        out_specs=pl.BlockSpec((tm, D_MODEL), row),
        out_shape=jax.ShapeDtypeStruct((M, D_MODEL), F32),
        compiler_params=_cparams("parallel"),
        name="matmul_residual_layernorm",
    )(a, w, res, g.reshape(1, -1), b.reshape(1, -1))


def _top_k_lanes(logits):
    lane_i = lax.broadcasted_iota(jnp.int32, logits.shape, 1)
    lane_f = lane_i.astype(F32)
    work = jnp.where(lane_i < N_EXPERTS, logits, -jnp.inf)
    out = jnp.zeros_like(logits)
    for k in range(TOP_K):
        m = jnp.max(work, axis=-1, keepdims=True)
        idx = jnp.min(jnp.where(work == m, lane_f, float(LANES)), axis=-1, keepdims=True)
        out = jnp.where(lane_i == k, m, out)
        out = jnp.where(lane_i == TOP_K + k, idx, out)
        work = jnp.where(lane_f == idx, -jnp.inf, work)
    return out


def _router_kernel(x_ref, w_ref, b_ref, o_ref):
    acc = jnp.dot(x_ref[...].astype(BF16), w_ref[...].astype(BF16), preferred_element_type=F32)
    o_ref[...] = _top_k_lanes(acc + b_ref[...])


def _router(x, w_pad, b_pad, tm):
    M = x.shape[0]
    NP = w_pad.shape[1]
    return pl.pallas_call(
        _router_kernel,
        grid=(M // tm,),
        in_specs=[pl.BlockSpec((tm, D_MODEL), lambda i: (i, 0)),
                  pl.BlockSpec((D_MODEL, NP), lambda i: (0, 0)),
                  pl.BlockSpec((1, NP), lambda i: (0, 0))],
        out_specs=pl.BlockSpec((tm, NP), lambda i: (i, 0)),
        out_shape=jax.ShapeDtypeStruct((M, NP), F32),
        compiler_params=_cparams("parallel"),
        name="router_logits",
    )(x, w_pad, b_pad)


def _bf16_row_interleave(a, b):
    a32 = lax.bitcast_convert_type(a.astype(BF16).astype(F32), jnp.uint32)
    b32 = lax.bitcast_convert_type(b.astype(BF16).astype(F32), jnp.uint32)
    word = (a32 >> 16) | (b32 & jnp.uint32(0xFFFF0000))
    return pltpu.bitcast(word, BF16)


def _moe_kernel(be_ref, bx_ref, bv_ref, x_ref, w1_ref, b1_ref, w2_ref, b2_ref, o_ref):
    i = pl.program_id(0)
    f = pl.program_id(1)
    fc = w2_ref.shape[1] // 2
    bm = x_ref.shape[0]
    valid = bv_ref[i] == 1

    @pl.when(valid)
    def _():
        w1 = w1_ref[0].astype(BF16)
        w2 = w2_ref[0]
        w2q = _bf16_row_interleave(w2[:fc], w2[fc:])

        @pl.when(f == 0)
        def _():
            o_ref[...] = jnp.broadcast_to(b2_ref[0], o_ref.shape)

        hm = bm // MOE_ROW_SPLIT
        even = (lax.broadcasted_iota(jnp.int32, (hm, LANES), 1) & 1) == 0
        for part in range(MOE_ROW_SPLIT):
            rows = pl.ds(part * hm, hm)
            h = jnp.dot(x_ref[rows, :].astype(BF16), w1, preferred_element_type=F32) + b1_ref[0]
            acts = []
            for c in range(2 * fc // LANES):
                a = h[:, LANES * c:LANES * (c + 1)]
                b = h[:, 2 * fc + LANES * c:2 * fc + LANES * (c + 1)]
                glu = jnp.where(even, a, pltpu.roll(b, 1, axis=1))
                lin = jnp.where(even, pltpu.roll(a, LANES - 1, axis=1), b)
                glu = jnp.minimum(glu, SWIGLU_LIMIT)
                lin = jnp.clip(lin, -SWIGLU_LIMIT, SWIGLU_LIMIT)
                act = glu * (1.0 / (1.0 + jnp.exp(-SWIGLU_ALPHA * glu))) * (lin + 1.0)
                acts.append(act.astype(BF16))
            act = jnp.concatenate(acts, axis=1) if len(acts) > 1 else acts[0]
            o_ref[rows, :] += jnp.dot(act, w2q, preferred_element_type=F32)

    @pl.when(jnp.logical_not(valid) & (f == 0))
    def _():
        o_ref[...] = jnp.zeros_like(o_ref)


def _moe_blocks(p_end, n_valid, nb, bm):
    i32 = jnp.int32
    n_exp = p_end.shape[0]
    blk = jnp.arange(nb, dtype=i32)
    block_v = (blk < n_valid).astype(i32)
    block_x = jnp.minimum(blk, n_valid - 1)
    block_e = jnp.minimum(jnp.sum(p_end[None, :] <= (block_x * bm)[:, None], axis=1, dtype=i32), n_exp - 1)
    return block_e, block_x, block_v


def _moe_ffn_blocks(rows, block_e, block_x, block_v, w1, b1, w2, b2, layer, bm, fc):
    d_model = rows.shape[1]
    d_ff = w2.shape[2]
    nb = rows.shape[0] // bm
    nf = d_ff // (2 * fc)
    last_f = nf - 1

    def fsel(f, bv, i):
        return jnp.where(bv[i] == 1, f, last_f)

    grid_spec = pltpu.PrefetchScalarGridSpec(
        num_scalar_prefetch=3,
        grid=(nb, nf),
        in_specs=[
            pl.BlockSpec((bm, d_model), lambda i, f, be, bx, bv: (bx[i], 0)),
            pl.BlockSpec((None, 1, d_model, 4 * fc), lambda i, f, be, bx, bv: (layer, be[i], 0, fsel(f, bv, i))),
            pl.BlockSpec((None, 1, 1, 4 * fc), lambda i, f, be, bx, bv: (layer, be[i], 0, fsel(f, bv, i))),
            pl.BlockSpec((None, 1, 2 * fc, d_model), lambda i, f, be, bx, bv: (layer, be[i], fsel(f, bv, i), 0)),
            pl.BlockSpec((None, 1, 1, d_model), lambda i, f, be, bx, bv: (layer, be[i], 0, 0)),
        ],
        out_specs=pl.BlockSpec((bm, d_model), lambda i, f, be, bx, bv: (i, 0)),
    )
    return pl.pallas_call(
        _moe_kernel,
        grid_spec=grid_spec,
        out_shape=jax.ShapeDtypeStruct(rows.shape, F32),
        compiler_params=_cparams("arbitrary", "arbitrary"),
        name="moe_expert_ffn",
    )(block_e, block_x, block_v, rows, w1, b1, w2, b2)


def _combine_ln_kernel(*refs):
    g_refs = refs[:TOP_K]
    gate_ref, res_ref, lg_ref, lb_ref, o_ref = refs[TOP_K:]
    gate = gate_ref[...]
    ff = None
    for k in range(TOP_K):
        term = gate[:, k:k + 1] * g_refs[k][...]
        ff = term if ff is None else ff + term
    o_ref[...] = _layer_norm_rows(DN_ALPHA * res_ref[...] + ff, lg_ref[...], lb_ref[...])


def _combine_ln(gathered, gate, res, g, b, tm):
    T, d = res.shape
    nt = T // tm
    row = lambda i: (i, 0)
    fixed = lambda i: (0, 0)
    g_specs = [pl.BlockSpec((tm, d), functools.partial(lambda i, k: (k * nt + i, 0), k=k)) for k in range(TOP_K)]
    return pl.pallas_call(
        _combine_ln_kernel,
        grid=(nt,),
        in_specs=g_specs + [pl.BlockSpec((tm, TOP_K), row), pl.BlockSpec((tm, d), row),
                            pl.BlockSpec((1, d), fixed), pl.BlockSpec((1, d), fixed)],
        out_specs=pl.BlockSpec((tm, d), row),
        out_shape=jax.ShapeDtypeStruct((T, d), F32),
        compiler_params=_cparams("parallel"),
        name="moe_combine_layernorm",
    )(*([gathered] * TOP_K), gate, res, g.reshape(1, -1), b.reshape(1, -1))


RANK_TOKENS = 128


def _rank_kernel(route_ref, rank_ref, count_ref, carry_ref):
    i = pl.program_id(0)
    n = route_ref.shape[0]

    @pl.when(i == 0)
    def _():
        carry_ref[...] = jnp.zeros_like(carry_ref)

    route = route_ref[...]
    lane = lax.broadcasted_iota(jnp.int32, (n, LANES), 1)
    lane_f = lane.astype(F32)
    hits = [lane_f == route[:, TOP_K + k:TOP_K + k + 1] for k in range(TOP_K)]
    per_token = jnp.zeros((n, LANES), F32)
    for k in range(TOP_K):
        per_token = per_token + jnp.where(hits[k], 1.0, 0.0)
    earlier = lax.broadcasted_iota(jnp.int32, (n, n), 0) > lax.broadcasted_iota(jnp.int32, (n, n), 1)
    before = jnp.dot(jnp.where(earlier, 1.0, 0.0).astype(BF16), per_token.astype(BF16),
                     preferred_element_type=F32) + carry_ref[...]
    out = jnp.zeros((n, LANES), F32)
    for k in range(TOP_K):
        r = jnp.sum(jnp.where(hits[k], before, 0.0), axis=-1, keepdims=True)
        out = jnp.where(lane == k, r, out)
    rank_ref[...] = out
    carry_ref[...] += jnp.sum(per_token, axis=0, keepdims=True)
    count_ref[...] = carry_ref[...]


def _rank_assignments(route):
    n = route.shape[0]
    return pl.pallas_call(
        _rank_kernel,
        grid=(n // RANK_TOKENS,),
        in_specs=[pl.BlockSpec((RANK_TOKENS, LANES), lambda i: (i, 0))],
        out_specs=[pl.BlockSpec((RANK_TOKENS, LANES), lambda i: (i, 0)), pl.BlockSpec((1, LANES), lambda i: (0, 0))],
        out_shape=[jax.ShapeDtypeStruct((n, LANES), F32), jax.ShapeDtypeStruct((1, LANES), F32)],
        scratch_shapes=[pltpu.VMEM((1, LANES), F32)],
        compiler_params=_cparams("arbitrary"),
        name="moe_rank_assignments",
    )(route)


def _moe_route(route, n_tok, n_exp, bm):
    i32 = jnp.int32
    ranks, counts = _rank_assignments(route)
    top_v = route[:n_tok, :TOP_K]
    top_i = route[:n_tok, TOP_K:2 * TOP_K].astype(i32)
    rank = ranks[:n_tok, :TOP_K].astype(i32)
    gate = jax.nn.softmax(top_v, axis=-1)
    n_assign = n_tok * TOP_K
    nb = -(-(n_assign + n_exp * (bm - 1)) // bm)
    counts = counts[0, :n_exp].astype(i32)
    padded = (counts + bm - 1) // bm * bm
    p_end = jnp.cumsum(padded)
    p_start = (p_end - padded).astype(i32)
    is_e = top_i[:, :, None] == jnp.arange(n_exp, dtype=i32)[None, None, :]
    pos = rank + jnp.sum(jnp.where(is_e, p_start[None, None, :], 0), axis=-1, dtype=i32)
    tok = jnp.broadcast_to(jnp.arange(n_tok, dtype=i32)[:, None], (n_tok, TOP_K))
    src = jnp.zeros((nb * bm,), i32).at[pos.reshape(-1)].set(tok.reshape(-1), unique_indices=True)
    n_valid = (p_end[-1] // bm).astype(i32)
    return gate, src, pos, (padded.astype(i32), p_end.astype(i32), n_valid, nb)


def _moe_ln(x2_p, x2_s, route_p, route_s, experts, layer, ln_g, ln_b):
    tp, ts = x2_p.shape[0], x2_s.shape[0]
    x2 = jnp.concatenate([x2_p, x2_s], axis=0)
    n_pad = -(tp + ts) % RANK_TOKENS
    route = jnp.concatenate([route_p, route_s, jnp.full((n_pad, LANES), -1.0, F32)], axis=0)
    gate, src, pos, (padded, p_end, n_valid, nb) = _moe_route(route, tp + ts, N_EXPERTS, MOE_BM)
    rows = jnp.take(x2, src, axis=0, mode='clip')
    w_e1, b_e1, w_e2, b_e2 = experts
    block_e, block_x, block_v = _moe_blocks(p_end, n_valid, nb, MOE_BM)
    out_rows = _moe_ffn_blocks(rows, block_e, block_x, block_v, w_e1, b_e1, w_e2, b_e2, layer, MOE_BM, MOE_FC)
    pos = pos.reshape(tp + ts, TOP_K)
    g_p = jnp.take(out_rows, pos[:tp].T.reshape(-1), axis=0, mode='clip')
    g_s = jnp.take(out_rows, pos[tp:].T.reshape(-1), axis=0, mode='clip')
    hp = _combine_ln(g_p, gate[:tp], x2_p, ln_g, ln_b, 256)
    hs = _combine_ln(g_s, gate[tp:], x2_s, ln_g, ln_b, ts)
    return hp, hs


def _half_lane_variants(t):
    lo = lax.broadcasted_iota(jnp.int32, t.shape, 1) < HD_A
    zero = jnp.zeros_like(t)
    tr = pltpu.roll(t, HD_A, axis=1)
    return [[jnp.where(lo, t, zero).astype(BF16), jnp.where(lo, zero, tr).astype(BF16)],
            [jnp.where(lo, tr, zero).astype(BF16), jnp.where(lo, zero, t).astype(BF16)]]


def _swa_kernel(sinks_ref, q_ref, kp_ref, kc_ref, vp_ref, vc_ref, bias_ref, o_ref):
    j = pl.program_id(1)
    kvar = _half_lane_variants(jnp.concatenate([kp_ref[...], kc_ref[...]], axis=0))
    vvar = _half_lane_variants(jnp.concatenate([vp_ref[...], vc_ref[...]], axis=0))
    col = lax.broadcasted_iota(jnp.int32, (WINDOW, 2 * WINDOW), 1)
    no_prev = col < jnp.where(j == 0, WINDOW, 0)
    for r in range(H_A // 2):
        g = (2 * r) // G_A
        qp = q_ref[:, LANES * r:LANES * (r + 1)].astype(BF16)
        acc = None
        for par in range(2):
            h = 2 * r + par
            s = lax.dot_general(qp, kvar[g][par], (((1,), (1,)), ((), ())), preferred_element_type=F32)
            s = s * HD_A ** -0.5 + bias_ref[h]
            s = jnp.where(no_prev, -jnp.inf, s)
            sk = sinks_ref[h]
            m = jnp.maximum(jnp.max(s, axis=-1, keepdims=True), sk)
            pr = jnp.exp(s - m)
            den = jnp.sum(pr, axis=-1, keepdims=True) + jnp.exp(sk - m)
            o = jnp.dot((pr / den).astype(BF16), vvar[g][par], preferred_element_type=F32)
            acc = o if acc is None else acc + o
        o_ref[:, LANES * r:LANES * (r + 1)] = acc.astype(o_ref.dtype)


def _swa_prompt_bias(rel_bias):
    i = jnp.arange(WINDOW)[:, None]
    j = jnp.arange(2 * WINDOW)[None, :]
    dist = i + WINDOW - j
    return jnp.where((dist >= 0) & (dist < WINDOW), _bucket_bias(rel_bias, dist), -jnp.inf)


def _swa_prompt_call(proj2d, bias, sinks, bsz, L):
    nb = L // WINDOW
    kcol, vcol = COL_K // LANES, COL_V // LANES
    cur = lambda b, j: b * nb + j
    prev = lambda b, j: b * nb + jnp.maximum(j - 1, 0)
    return pl.pallas_call(
        _swa_kernel,
        grid=(bsz, nb),
        in_specs=[
            pl.BlockSpec(memory_space=pltpu.SMEM),
            pl.BlockSpec((WINDOW, H_A * HD_A), lambda b, j: (cur(b, j), COL_Q // (H_A * HD_A))),
            pl.BlockSpec((WINDOW, LANES), lambda b, j: (prev(b, j), kcol)),
            pl.BlockSpec((WINDOW, LANES), lambda b, j: (cur(b, j), kcol)),
            pl.BlockSpec((WINDOW, LANES), lambda b, j: (prev(b, j), vcol)),
            pl.BlockSpec((WINDOW, LANES), lambda b, j: (cur(b, j), vcol)),
            pl.BlockSpec((H_A, WINDOW, 2 * WINDOW), lambda b, j: (0, 0, 0)),
        ],
        out_specs=pl.BlockSpec((WINDOW, H_A * HD_A), lambda b, j: (cur(b, j), 0)),
        out_shape=jax.ShapeDtypeStruct((bsz * L, H_A * HD_A), BF16),
        compiler_params=_cparams("parallel", "arbitrary"),
        name="swa_prompt",
    )(sinks, proj2d, proj2d, proj2d, proj2d, proj2d, bias)


def _gelu(x):
    return 0.5 * x * (1.0 + lax.erf(x * np.float32(np.sqrt(0.5))))


def _gmlp_kernel(u_ref, gv_ref, lng_ref, lnb_ref, w_ref, bias_ref, o_ref):
    u = _gelu(u_ref[...])
    gv = _layer_norm_rows(_gelu(gv_ref[...]), lng_ref[...], lnb_ref[...])
    lo = lax.broadcasted_iota(jnp.int32, (CHUNK_B, LANES), 1) < CG_B
    for r in range(GB // 2):
        vp = gv[:, LANES * r:LANES * (r + 1)]
        zero = jnp.zeros_like(vp)
        mix = jnp.dot(w_ref[2 * r], jnp.where(lo, vp, zero).astype(BF16), preferred_element_type=F32)
        mix += jnp.dot(w_ref[2 * r + 1], jnp.where(lo, zero, vp).astype(BF16), preferred_element_type=F32)
        sl = slice(LANES * r, LANES * (r + 1))
        o_ref[:, sl] = (u[:, sl] * (mix + bias_ref[:, sl])).astype(o_ref.dtype)


def _gmlp_prompt_call(proj2d, p, n_rows):
    w = (p['gmlp_ws'] * jnp.tril(jnp.ones((CHUNK_B, CHUNK_B), F32))).astype(BF16)
    bias = jnp.repeat(p['gmlp_bs'].T, CG_B, axis=1)
    fixed2 = lambda i: (0, 0)
    return pl.pallas_call(
        _gmlp_kernel,
        grid=(n_rows // CHUNK_B,),
        in_specs=[
            pl.BlockSpec((CHUNK_B, W_B), lambda i: (i, COL_U // W_B)),
            pl.BlockSpec((CHUNK_B, W_B), lambda i: (i, COL_GV // W_B)),
            pl.BlockSpec((1, W_B), fixed2), pl.BlockSpec((1, W_B), fixed2),
            pl.BlockSpec((GB, CHUNK_B, CHUNK_B), lambda i: (0, 0, 0)),
            pl.BlockSpec((CHUNK_B, W_B), fixed2),
        ],
        out_specs=pl.BlockSpec((CHUNK_B, W_B), lambda i: (i, 0)),
        out_shape=jax.ShapeDtypeStruct((n_rows, W_B), BF16),
        compiler_params=_cparams("parallel"),
        name="gmlp_prompt",
    )(proj2d, proj2d, p['gmlp_ln_g'].reshape(1, -1), p['gmlp_ln_b'].reshape(1, -1), w, bias)


def _silu(x):
    return x * (1.0 / (1.0 + jnp.exp(-x)))


def _softplus(x):
    return jnp.maximum(x, 0.0) + jnp.log1p(jnp.exp(-jnp.abs(x)))


def _causal_conv_chunk(cur, tail, w, bias):
    rows = lax.broadcasted_iota(jnp.int32, (8, cur.shape[1]), 0)
    y = jnp.broadcast_to(bias, cur.shape)
    y_head = jnp.broadcast_to(bias, (8, cur.shape[1]))
    for t in range(CONV_W):
        k = CONV_W - 1 - t
        wt = w[t:t + 1, :]
        if k == 0:
            y = y + cur * wt
            y_head = y_head + cur[:8] * wt
        else:
            sh = pltpu.roll(cur, k, axis=0)
            y = y + sh * wt
            y_head = y_head + jnp.where(rows < k, pltpu.roll(tail, k, axis=0), sh[:8]) * wt
    return jnp.concatenate([y_head, y[8:]], axis=0)


def _bf16_split3(x):
    p1 = x.astype(BF16)
    r1 = x - p1.astype(F32)
    p2 = r1.astype(BF16)
    p3 = (r1 - p2.astype(F32)).astype(BF16)
    return p1, p2, p3


def _ssd_kernel(xs_ref, bc_ref, z_ref, dt_ref, cwx_ref, cbx_ref, cwb_ref, cbb_ref, dtb_ref, a_ref,
                dsk_ref, ng_ref, y_ref, h_ref, state_ref, tailx_ref, tailb_ref):
    c = pl.program_id(1)
    C = SSD_CHUNK

    @pl.when(c == 0)
    def _():
        state_ref[...] = jnp.zeros_like(state_ref)
        tailx_ref[...] = jnp.zeros_like(tailx_ref)
        tailb_ref[...] = jnp.zeros_like(tailb_ref)

    xs_raw = xs_ref[...]
    bc_raw = bc_ref[...]
    xs = _silu(_causal_conv_chunk(xs_raw, tailx_ref[...], cwx_ref[...], cbx_ref[...]))
    bc = _silu(_causal_conv_chunk(bc_raw, tailb_ref[...], cwb_ref[...], cbb_ref[...]))
    tailx_ref[...] = xs_raw[C - 8:]
    tailb_ref[...] = bc_raw[C - 8:]

    dt = _softplus(dt_ref[...] + dtb_ref[...])
    da = dt * a_ref[...]
    row_i = lax.broadcasted_iota(jnp.int32, (C, C), 0)
    col_i = lax.broadcasted_iota(jnp.int32, (C, C), 1)
    causal = row_i >= col_i
    tril = jnp.where(causal, 1.0, 0.0).astype(BF16)
    acs = None
    for piece in _bf16_split3(da):
        t = jnp.dot(tril, piece, preferred_element_type=F32)
        acs = t if acs is None else acs + t
    acs_t = acs.T
    exp_acs = jnp.exp(acs)
    end_decay = jnp.exp(acs[C - 1:C, :] - acs)
    chunk_decay = jnp.exp(acs[C - 1:C, :])

    lo = lax.broadcasted_iota(jnp.int32, (C, LANES), 1) < P_C
    bm = [bc[:, N_C * g:N_C * (g + 1)].astype(BF16) for g in range(G_C)]
    cm = [bc[:, N_C * (G_C + g):N_C * (G_C + g + 1)].astype(BF16) for g in range(G_C)]
    cb = [lax.dot_general(cm[g], bm[g], (((1,), (1,)), ((), ())), preferred_element_type=F32)
          for g in range(G_C)]

    def per_lane_half(t, r):
        return jnp.where(lo, t[:, 2 * r:2 * r + 1], t[:, 2 * r + 1:2 * r + 2])

    ys = []
    for r in range(H_C // 2):
        g = (2 * r) // R_C
        sl = slice(LANES * r, LANES * (r + 1))
        x_pair = xs[:, sl]
        xdt = x_pair * per_lane_half(dt, r)
        zero = jnp.zeros_like(xdt)
        y_pair = None
        for par in range(2):
            h = 2 * r + par
            seg = acs[:, h:h + 1] - acs_t[h:h + 1, :]
            decay = jnp.where(causal, jnp.exp(seg), 0.0)
            m_h = (cb[g] * decay).astype(BF16)
            x_h = (jnp.where(lo, xdt, zero) if par == 0 else jnp.where(lo, zero, xdt)).astype(BF16)
            t = jnp.dot(m_h, x_h, preferred_element_type=F32)
            y_pair = t if y_pair is None else y_pair + t
        st = state_ref[sl, :]
        y_off = lax.dot_general(cm[g], st.astype(BF16), (((1,), (1,)), ((), ())), preferred_element_type=F32)
        y_pair = y_pair + y_off * per_lane_half(exp_acs, r)
        upd = lax.dot_general((xdt * per_lane_half(end_decay, r)).astype(BF16), bm[g],
                              (((0,), (0,)), ((), ())), preferred_element_type=F32)
        cd = jnp.concatenate([jnp.broadcast_to(chunk_decay[:, 2 * r:2 * r + 1], (P_C, N_C)),
                              jnp.broadcast_to(chunk_decay[:, 2 * r + 1:2 * r + 2], (P_C, N_C))], axis=0)
        state_ref[sl, :] = st * cd + upd
        ys.append(y_pair + dsk_ref[:, sl] * x_pair)
    y = jnp.concatenate(ys, axis=1) * _silu(z_ref[...])
    gw = D_INNER // G_C
    outs = []
    for g in range(G_C):
        yg = y[:, gw * g:gw * (g + 1)]
        outs.append(yg * lax.rsqrt(jnp.mean(yg * yg, axis=-1, keepdims=True) + RMS_EPS))
    y_ref[...] = (jnp.concatenate(outs, axis=1) * ng_ref[...]).astype(y_ref.dtype)

    @pl.when(c == pl.num_programs(1) - 1)
    def _():
        h_ref[...] = state_ref[...]


def _ssd_prompt_call(proj2d, p, bsz, L):
    nc = L // SSD_CHUNK
    row = lambda blk: (lambda b, c: (b * nc + c, blk))
    fixed = lambda b, c: (0, 0)
    pad_l = lambda v: jnp.pad(v.astype(F32), (0, LANES - H_C)).reshape(1, LANES)
    cw, cbias = p['conv_w'], p['conv_b'].reshape(1, -1)
    nbc = 2 * G_C * N_C
    args = (proj2d, proj2d, proj2d, proj2d,
            cw[:, :D_INNER], cbias[:, :D_INNER], cw[:, D_INNER:], cbias[:, D_INNER:],
            pad_l(p['dt_bias']), pad_l(-jnp.exp(p['a_log'].astype(F32))),
            jnp.repeat(p['d_skip'].astype(F32), P_C).reshape(1, -1), p['ssm_norm_g'].reshape(1, -1))
    return pl.pallas_call(
        _ssd_kernel,
        grid=(bsz, nc),
        in_specs=[
            pl.BlockSpec((SSD_CHUNK, D_INNER), row(COL_XBC // D_INNER)),
            pl.BlockSpec((SSD_CHUNK, nbc), row((COL_XBC + D_INNER) // nbc)),
            pl.BlockSpec((SSD_CHUNK, D_INNER), row(COL_Z // D_INNER)),
            pl.BlockSpec((SSD_CHUNK, LANES), row(COL_DT // LANES)),
            pl.BlockSpec((CONV_W, D_INNER), fixed), pl.BlockSpec((1, D_INNER), fixed),
            pl.BlockSpec((CONV_W, nbc), fixed), pl.BlockSpec((1, nbc), fixed),
            pl.BlockSpec((1, LANES), fixed), pl.BlockSpec((1, LANES), fixed),
            pl.BlockSpec((1, D_INNER), fixed), pl.BlockSpec((1, D_INNER), fixed),
        ],
        out_specs=[pl.BlockSpec((SSD_CHUNK, D_INNER), lambda b, c: (b * nc + c, 0)),
                   pl.BlockSpec((None, H_C * P_C, N_C), lambda b, c: (b, 0, 0))],
        out_shape=[jax.ShapeDtypeStruct((bsz * L, D_INNER), BF16),
                   jax.ShapeDtypeStruct((bsz, H_C * P_C, N_C), F32)],
        scratch_shapes=[pltpu.VMEM((H_C * P_C, N_C), F32), pltpu.VMEM((8, D_INNER), F32),
                        pltpu.VMEM((8, nbc), F32)],
        compiler_params=_cparams("parallel", "arbitrary"),
        name="ssd_prompt",
    )(*args)


def _bf16_round(x):
    return x.astype(BF16).astype(F32)


SWA_DECODE_TOKENS = 8


def _swa_decode_kernel(q_ref, kn_ref, vn_ref, ck_ref, cv_ref, bias_ref, sink_ref, o_ref, wk_ref, wv_ref):
    W = WINDOW
    last = lax.broadcasted_iota(jnp.int32, (W, LANES), 0) == W - 1
    lo = lax.broadcasted_iota(jnp.int32, (8, LANES), 1) < HD_A
    g0 = lax.broadcasted_iota(jnp.int32, (8, LANES), 0) < G_A // 2
    zero = jnp.zeros((8, LANES), F32)
    nt_dims = (((1,), (1,)), ((), ()))
    for t in range(q_ref.shape[0]):
        kw = jnp.where(last, jnp.broadcast_to(kn_ref[t], (W, LANES)), pltpu.roll(ck_ref[t], W - 1, axis=0))
        vw = jnp.where(last, jnp.broadcast_to(vn_ref[t], (W, LANES)), pltpu.roll(cv_ref[t], W - 1, axis=0))
        wk_ref[t] = kw
        wv_ref[t] = vw
        kb, vb = kw.astype(BF16), vw.astype(BF16)
        q = q_ref[t]
        qr = pltpu.roll(q, HD_A, axis=1)
        q_par = [jnp.where(g0, jnp.where(lo, q, zero), jnp.where(lo, zero, qr)),
                 jnp.where(g0, jnp.where(lo, qr, zero), jnp.where(lo, zero, q))]
        outs = []
        for par in range(2):
            s = lax.dot_general(q_par[par].astype(BF16), kb, nt_dims, preferred_element_type=F32)
            s = s * HD_A ** -0.5 + bias_ref[par]
            sk = sink_ref[par]
            m = jnp.maximum(jnp.max(s, axis=-1, keepdims=True), sk)
            pr = jnp.exp(s - m)
            den = jnp.sum(pr, axis=-1, keepdims=True) + jnp.exp(sk - m)
            outs.append(jnp.dot((pr / den).astype(BF16), vb, preferred_element_type=F32))
        o_even = jnp.where(g0, outs[0], pltpu.roll(outs[0], HD_A, axis=1))
        o_odd = jnp.where(g0, pltpu.roll(outs[1], HD_A, axis=1), outs[1])
        o_ref[t] = jnp.where(lo, o_even, o_odd).astype(o_ref.dtype)


def _swa_decode_call(proj3, cache_k, cache_v, layer, rel_bias, sinks):
    bsz = proj3.shape[0]
    nt = SWA_DECODE_TOKENS
    ck = cache_k.reshape(cache_k.shape[0], bsz, WINDOW, LANES)
    cv = cache_v.reshape(cache_v.shape[0], bsz, WINDOW, LANES)
    q8 = proj3[:, 0, COL_Q:COL_Q + H_A * HD_A].reshape(bsz, H_A // 2, LANES)
    dist = WINDOW - 1 - jnp.arange(WINDOW)
    bias = _bucket_bias(rel_bias, dist).reshape(H_A // 2, 2, WINDOW).transpose(1, 0, 2)
    sink = sinks.astype(F32).reshape(H_A // 2, 2, 1).transpose(1, 0, 2)
    new_tok = lambda blk: (lambda i: (i, 0, blk))
    cache = lambda i: (layer, i, 0, 0)
    tok3 = lambda i: (i, 0, 0)
    out, wk, wv = pl.pallas_call(
        _swa_decode_kernel,
        grid=(bsz // nt,),
        in_specs=[
            pl.BlockSpec((nt, H_A // 2, LANES), tok3),
            pl.BlockSpec((nt, 1, LANES), new_tok(COL_K // LANES)),
            pl.BlockSpec((nt, 1, LANES), new_tok(COL_V // LANES)),
            pl.BlockSpec((None, nt, WINDOW, LANES), cache),
            pl.BlockSpec((None, nt, WINDOW, LANES), cache),
            pl.BlockSpec((2, H_A // 2, WINDOW), lambda i: (0, 0, 0)),
            pl.BlockSpec((2, H_A // 2, 1), lambda i: (0, 0, 0)),
        ],
        out_specs=[pl.BlockSpec((nt, H_A // 2, LANES), tok3),
                   pl.BlockSpec((nt, WINDOW, LANES), tok3),
                   pl.BlockSpec((nt, WINDOW, LANES), tok3)],
        out_shape=[jax.ShapeDtypeStruct((bsz, H_A // 2, LANES), BF16),
                   jax.ShapeDtypeStruct((bsz, WINDOW, LANES), F32),
                   jax.ShapeDtypeStruct((bsz, WINDOW, LANES), F32)],
        compiler_params=_cparams("parallel"),
        name="swa_sample",
    )(q8, proj3, proj3, ck, cv, bias, sink)
    return out.reshape(bsz, H_A * HD_A), wk, wv


def _gmlp_step_kernel(u_ref, gv_ref, lng_ref, lnb_ref, w0_ref, b0_ref, o_ref, gv_out_ref):
    gv = _layer_norm_rows(_gelu(gv_ref[...]), lng_ref[...], lnb_ref[...])
    gv_out_ref[...] = gv
    mix = _bf16_round(w0_ref[...]) * _bf16_round(gv) + b0_ref[...]
    o_ref[...] = (_gelu(u_ref[...]) * mix).astype(o_ref.dtype)


def _gmlp_step_call(proj2d, p):
    n = proj2d.shape[0]
    w0 = jnp.repeat(p['gmlp_ws'][:, 0, 0], CG_B).reshape(1, -1)
    b0 = jnp.repeat(p['gmlp_bs'][:, 0], CG_B).reshape(1, -1)
    fixed = lambda i: (0, 0)
    return pl.pallas_call(
        _gmlp_step_kernel,
        grid=(1,),
        in_specs=[pl.BlockSpec((n, W_B), lambda i: (0, COL_U // W_B)),
                  pl.BlockSpec((n, W_B), lambda i: (0, COL_GV // W_B)),
                  pl.BlockSpec((1, W_B), fixed), pl.BlockSpec((1, W_B), fixed),
                  pl.BlockSpec((1, W_B), fixed), pl.BlockSpec((1, W_B), fixed)],
        out_specs=[pl.BlockSpec((n, W_B), fixed), pl.BlockSpec((n, W_B), fixed)],
        out_shape=[jax.ShapeDtypeStruct((n, W_B), BF16), jax.ShapeDtypeStruct((n, W_B), F32)],
        compiler_params=_cparams("arbitrary"),
        name="gmlp_sample",
    )(proj2d, proj2d, p['gmlp_ln_g'].reshape(1, -1), p['gmlp_ln_b'].reshape(1, -1), w0, b0)


def _conv_step(st, cur, w, bias):
    y = bias
    for t in range(CONV_W - 1):
        y = y + st[t:t + 1, :] * w[t:t + 1, :]
    return y + cur * w[CONV_W - 1:CONV_W, :]


def _ssd_step_kernel(xs_ref, bc_ref, z_ref, dt_ref, stx_ref, stb_ref, h0_ref, cwx_ref, cbx_ref, cwb_ref, cbb_ref,
                     dtb_ref, a_ref, dsk_ref, ng_ref, y_ref, h_ref, ncx_ref, ncb_ref):
    xs_raw, bc_raw = xs_ref[...], bc_ref[...]
    stx, stb = stx_ref[...], stb_ref[...]
    ncx_ref[0:CONV_W - 2, :] = stx[1:CONV_W - 1]
    ncx_ref[CONV_W - 2:CONV_W - 1, :] = xs_raw
    ncb_ref[0:CONV_W - 2, :] = stb[1:CONV_W - 1]
    ncb_ref[CONV_W - 2:CONV_W - 1, :] = bc_raw
    xs = _silu(_conv_step(stx, xs_raw, cwx_ref[...], cbx_ref[...]))
    bc = _silu(_conv_step(stb, bc_raw, cwb_ref[...], cbb_ref[...]))
    dt = _softplus(dt_ref[...] + dtb_ref[...])
    decay = jnp.exp(dt * a_ref[...])
    xdt = _bf16_round(xs * dt)
    gw = D_INNER // G_C
    first_group = lax.broadcasted_iota(jnp.int32, (1, D_INNER), 1) < gw
    bm = [_bf16_round(bc[:, N_C * g:N_C * (g + 1)]) for g in range(G_C)]
    cm = [_bf16_round(bc[:, N_C * (G_C + g):N_C * (G_C + g + 1)]) for g in range(G_C)]
    cb = [_bf16_round(jnp.sum(cm[g] * bm[g], axis=1, keepdims=True)) for g in range(G_C)]
    y_diag = jnp.where(first_group, cb[0], cb[1]) * xdt
    h0 = h0_ref[...]
    y_off = jnp.concatenate([
        lax.dot_general(jnp.broadcast_to(cm[g], (8, N_C)).astype(BF16), h0[gw * g:gw * (g + 1)].astype(BF16),
                        (((1,), (1,)), ((), ())), preferred_element_type=F32)[0:1] for g in range(G_C)], axis=1)
    y = y_diag + y_off * decay
    y = (y + dsk_ref[...] * xs) * _silu(z_ref[...])
    outs = []
    for g in range(G_C):
        yg = y[:, gw * g:gw * (g + 1)]
        outs.append(yg * lax.rsqrt(jnp.mean(yg * yg, axis=-1, keepdims=True) + RMS_EPS))
    y_ref[...] = (jnp.concatenate(outs, axis=1) * ng_ref[...]).astype(y_ref.dtype)
    decay_rows = jnp.broadcast_to(decay, (LANES, D_INNER)).T
    xdt_rows = jnp.broadcast_to(xdt, (LANES, D_INNER)).T
    rows = lax.broadcasted_iota(jnp.int32, (D_INNER, N_C), 0)
    bm_rows = jnp.where(rows < gw, jnp.broadcast_to(bm[0], (D_INNER, N_C)), jnp.broadcast_to(bm[1], (D_INNER, N_C)))
    h_ref[...] = h0 * decay_rows + xdt_rows * bm_rows


def _ssd_step_call(proj3, state_conv, state_ssm, layer, p):
    bsz = proj3.shape[0]
    nbc = 2 * G_C * N_C
    ssm = state_ssm.reshape(state_ssm.shape[0], bsz, H_C * P_C, N_C)
    dt_lanes = jnp.repeat(proj3[:, :, COL_DT:COL_DT + H_C], P_C, axis=-1)
    per_lane = lambda v: jnp.repeat(v.astype(F32), P_C).reshape(1, -1)
    cw, cbias = p['conv_w'], p['conv_b'].reshape(1, -1)
    tok = lambda blk: (lambda i: (i, 0, blk))
    fixed = lambda i: (0, 0)
    return pl.pallas_call(
        _ssd_step_kernel,
        grid=(bsz,),
        in_specs=[
            pl.BlockSpec((None, 1, D_INNER), tok(COL_XBC // D_INNER)),
            pl.BlockSpec((None, 1, nbc), tok((COL_XBC + D_INNER) // nbc)),
            pl.BlockSpec((None, 1, D_INNER), tok(COL_Z // D_INNER)),
            pl.BlockSpec((None, 1, D_INNER), tok(0)),
            pl.BlockSpec((None, None, CONV_W - 1, D_INNER), lambda i: (layer, i, 0, 0)),
            pl.BlockSpec((None, None, CONV_W - 1, nbc), lambda i: (layer, i, 0, D_INNER // nbc)),
            pl.BlockSpec((None, None, H_C * P_C, N_C), lambda i: (layer, i, 0, 0)),
            pl.BlockSpec((CONV_W, D_INNER), fixed), pl.BlockSpec((1, D_INNER), fixed),
            pl.BlockSpec((CONV_W, nbc), fixed), pl.BlockSpec((1, nbc), fixed),
            pl.BlockSpec((1, D_INNER), fixed), pl.BlockSpec((1, D_INNER), fixed),
            pl.BlockSpec((1, D_INNER), fixed), pl.BlockSpec((1, D_INNER), fixed),
        ],
        out_specs=[pl.BlockSpec((None, 1, D_INNER), lambda i: (i, 0, 0)),
                   pl.BlockSpec((None, H_C * P_C, N_C), lambda i: (i, 0, 0)),
                   pl.BlockSpec((None, CONV_W - 1, D_INNER), lambda i: (i, 0, 0)),
                   pl.BlockSpec((None, CONV_W - 1, nbc), lambda i: (i, 0, 0))],
        out_shape=[jax.ShapeDtypeStruct((bsz, 1, D_INNER), BF16),
                   jax.ShapeDtypeStruct((bsz, H_C * P_C, N_C), F32),
                   jax.ShapeDtypeStruct((bsz, CONV_W - 1, D_INNER), F32),
                   jax.ShapeDtypeStruct((bsz, CONV_W - 1, nbc), F32)],
        compiler_params=_cparams("parallel"),
        name="ssd_sample",
    )(proj3, proj3, proj3, dt_lanes, state_conv, state_conv, ssm,
      cw[:, :D_INNER], cbias[:, :D_INNER], cw[:, D_INNER:], cbias[:, D_INNER:],
      per_lane(p['dt_bias']), per_lane(-jnp.exp(p['a_log'].astype(F32))), per_lane(p['d_skip']),
      p['ssm_norm_g'].reshape(1, -1))


def _xattn_kernel(x1_ref, wq_ref, mk_ref, mv_ref, wo_ref, g_ref, b_ref, wr_ref, br_ref, x2_ref, lg_ref):
    x1 = x1_ref[...]
    q = jnp.dot(x1.astype(BF16), wq_ref[...], preferred_element_type=F32)
    outs = []
    for h in range(XH):
        sl = slice(XHD * h, XHD * (h + 1))
        s = lax.dot_general(q[:, sl].astype(BF16), mk_ref[:, sl].astype(BF16),
                            (((1,), (1,)), ((), ())), preferred_element_type=F32) * XHD ** -0.5
        e = jnp.exp(s - jnp.max(s, axis=-1, keepdims=True))
        w = e / jnp.sum(e, axis=-1, keepdims=True)
        outs.append(jnp.dot(w.astype(BF16), mv_ref[:, sl].astype(BF16), preferred_element_type=F32).astype(BF16))
    o = jnp.concatenate(outs, axis=1)
    y = jnp.dot(o, wo_ref[...], preferred_element_type=F32) + DN_ALPHA * x1
    x2 = _layer_norm_rows(y, g_ref[...], b_ref[...])
    x2_ref[...] = x2
    lg_ref[...] = _top_k_lanes(jnp.dot(x2.astype(BF16), wr_ref[...], preferred_element_type=F32) + br_ref[...])


def _xattn_prompt_call(x1, mkv, lw, g, b, bsz, L, tm):
    nt = L // tm
    xw = XH * XHD
    row = lambda bb, i: (bb * nt + i, 0)
    fixed = lambda bb, i: (0, 0)
    n_lg = lw['w_router'].shape[1]
    return pl.pallas_call(
        _xattn_kernel,
        grid=(bsz, nt),
        in_specs=[
            pl.BlockSpec((tm, D_MODEL), row),
            pl.BlockSpec((D_MODEL, xw), fixed),
            pl.BlockSpec((N_MEM, xw), lambda bb, i: (bb, 0)),
            pl.BlockSpec((N_MEM, xw), lambda bb, i: (bb, 1)),
            pl.BlockSpec((xw, D_MODEL), fixed),
            pl.BlockSpec((1, D_MODEL), fixed), pl.BlockSpec((1, D_MODEL), fixed),
            pl.BlockSpec((D_MODEL, n_lg), fixed), pl.BlockSpec((1, n_lg), fixed),
        ],
        out_specs=[pl.BlockSpec((tm, D_MODEL), row), pl.BlockSpec((tm, n_lg), row)],
        out_shape=[jax.ShapeDtypeStruct((bsz * L, D_MODEL), F32), jax.ShapeDtypeStruct((bsz * L, n_lg), F32)],
        compiler_params=_cparams("parallel", "parallel"),
        name="memory_attention_prompt",
    )(x1, lw['w_xq'], mkv, mkv, lw['w_xo'], g.reshape(1, -1), b.reshape(1, -1),
      lw['w_router_bf16'], lw['b_router'])


def _xattn_decode_kernel(q_ref, k_ref, v_ref, o_ref):
    q = q_ref[...].astype(BF16).astype(F32)
    k = k_ref[...].astype(BF16).astype(F32)
    v = v_ref[...].astype(BF16).astype(F32)
    prod = k * q
    outs = []
    for h in range(XH):
        sl = slice(XHD * h, XHD * (h + 1))
        s = jnp.sum(prod[:, sl], axis=1, keepdims=True) * XHD ** -0.5
        e = jnp.exp(s - jnp.max(s, axis=0, keepdims=True))
        w = (e / jnp.sum(e, axis=0, keepdims=True)).astype(BF16).astype(F32)
        outs.append(jnp.sum(w * v[:, sl], axis=0, keepdims=True))
    o_ref[...] = jnp.concatenate(outs, axis=1)


def _xattn_decode_call(q, cache_k, cache_v, layer):
    bsz, xw = q.shape
    ck = cache_k.reshape(cache_k.shape[0], bsz, N_MEM, xw)
    cv = cache_v.reshape(cache_v.shape[0], bsz, N_MEM, xw)
    out = pl.pallas_call(
        _xattn_decode_kernel,
        grid=(bsz,),
        in_specs=[pl.BlockSpec((None, 1, xw), lambda i: (i, 0, 0)),
                  pl.BlockSpec((None, None, N_MEM, xw), lambda i: (layer, i, 0, 0)),
                  pl.BlockSpec((None, None, N_MEM, xw), lambda i: (layer, i, 0, 0))],
        out_specs=pl.BlockSpec((None, 1, xw), lambda i: (i, 0, 0)),
        out_shape=jax.ShapeDtypeStruct((bsz, 1, xw), F32),
        compiler_params=_cparams("parallel"),
        name="memory_attention_sample",
    )(q.reshape(bsz, 1, xw), ck, cv)
    return out.reshape(bsz, xw)


def _t5_bucket(dist):
    n = jnp.maximum(dist, 0)
    exact = N_BUCKETS // 2
    nf = jnp.maximum(n, 1).astype(F32)
    large = exact + (jnp.log(nf / exact) / math.log(MAX_DIST / exact) * (N_BUCKETS - exact)).astype(jnp.int32)
    return jnp.where(n < exact, n, jnp.minimum(large, N_BUCKETS - 1))


def _bucket_bias(rel_bias, dist):
    onehot = (_t5_bucket(dist)[..., None] == jnp.arange(N_BUCKETS)).astype(F32)
    return jnp.einsum('...b,bh->h...', onehot, rel_bias.astype(F32), precision=lax.Precision.HIGHEST)


def _prompt_layer(x, mkv, lw, p, swa_bias, bsz, L):
    T = bsz * L
    xb = x.astype(BF16)
    proj = _matmul(xb, lw['w_in'], 1024, 512)
    a_out = _swa_prompt_call(proj, swa_bias, p['sinks'], bsz, L)
    b_out = _gmlp_prompt_call(proj, p, T)
    c_out, h_last = _ssd_prompt_call(proj, p, bsz, L)
    mixed = _gate_merge(xb, a_out, b_out, c_out, lw['w_gate'], lw['b_gate'], lw['w_branch'], 512, 512)
    x1 = _mm_res_ln(mixed, lw['w_o'], x, p['ln1_g'], p['ln1_b'], 512)
    x2, logits = _xattn_prompt_call(x1, mkv, lw, p['ln2_g'], p['ln2_b'], bsz, L, 512)
    proj3 = proj.reshape(bsz, L, IN_DIM_PAD)
    win_k = proj3[:, L - WINDOW:, COL_K:COL_K + KV_A * HD_A].reshape(bsz, WINDOW, KV_A, HD_A)
    win_v = proj3[:, L - WINDOW:, COL_V:COL_V + KV_A * HD_A].reshape(bsz, WINDOW, KV_A, HD_A)
    conv = proj3[:, L - (CONV_W - 1):, COL_XBC:COL_XBC + CONV_DIM]
    return x2, logits, (win_k, win_v, conv, h_last.reshape(bsz, H_C, P_C, N_C))


def _sample_layer(x, layer, lw, p, rel_bias, cache_win_k, cache_win_v, state_conv, state_ssm, cache_mem_k, cache_mem_v):
    bsz = x.shape[0]
    xb = x.astype(BF16)
    proj = _matmul(xb, lw['w_in'], bsz, 512)
    proj3 = proj.reshape(bsz, 1, IN_DIM_PAD)
    a_out, win_k, win_v = _swa_decode_call(proj3, cache_win_k, cache_win_v, layer, rel_bias, p['sinks'])
    b_out, gv = _gmlp_step_call(proj, p)
    c_out, ssm, conv_x, conv_bc = _ssd_step_call(proj3, state_conv, state_ssm, layer, p)
    mixed = _gate_merge(xb, a_out, b_out, c_out.reshape(bsz, BR_W),
                        lw['w_gate'], lw['b_gate'], lw['w_branch'], bsz, 512)
    x1 = _mm_res_ln(mixed, lw['w_o'], x, p['ln1_g'], p['ln1_b'], bsz)
    q = _matmul(x1.astype(BF16), lw['w_xq'], bsz, XH * XHD)
    o = _xattn_decode_call(q, cache_mem_k, cache_mem_v, layer)
    x2 = _mm_res_ln(o.astype(BF16), lw['w_xo'], x1, p['ln2_g'], p['ln2_b'], bsz)
    logits = _router(x2, lw['w_router'], lw['b_router'], bsz)
    states = (win_k.reshape(bsz, WINDOW, KV_A, HD_A), win_v.reshape(bsz, WINDOW, KV_A, HD_A),
              jnp.concatenate([conv_x, conv_bc], axis=-1), ssm.reshape(bsz, H_C, P_C, N_C),
              gv.reshape(bsz, 1, W_B))
    return x2, logits, states


def kernel(x_prompt, x_sample, mem_prompt, cache_win_k, cache_win_v, state_conv, state_ssm, cache_mem_k, cache_mem_v, w_in, rel_bias, sinks, gmlp_ln_g, gmlp_ln_b, gmlp_ws, gmlp_bs, conv_w, conv_b, dt_bias, a_log, d_skip, ssm_norm_g, w_branch, w_gate, b_gate, w_o, ln1_g, ln1_b, w_xq, w_xk, w_xv, w_xo, ln2_g, ln2_b, w_router, b_router, w_e1, b_e1, w_e2, b_e2, ln3_g, ln3_b):
    assert cache_win_k.shape[2] == WINDOW and x_sample.shape[1] == 1
    n_prompt, n_mem = mem_prompt.shape[0], mem_prompt.shape[1]
    bp, lp = x_prompt.shape[:2]
    bs_ = x_sample.shape[0]
    hp, hs = x_prompt.reshape(bp * lp, D_MODEL), x_sample.reshape(bs_, D_MODEL)
    wk_p, wv_p, cv_p, ssm_p, mk_ps, mv_ps = [], [], [], [], [], []
    wk_s, wv_s, cv_s, ssm_s, gv_s = [], [], [], [], []
    mem_b = mem_prompt.reshape(n_prompt * n_mem, D_MODEL).astype(BF16)
    swa_bias = _swa_prompt_bias(rel_bias)
    experts = (w_e1, b_e1.reshape(DEPTH, N_EXPERTS, 1, 2 * D_FF), w_e2, b_e2.reshape(DEPTH, N_EXPERTS, 1, D_MODEL))
    for l in range(DEPTH):
        p = dict(sinks=sinks[l], gmlp_ln_g=gmlp_ln_g[l], gmlp_ln_b=gmlp_ln_b[l],
                 gmlp_ws=gmlp_ws[l], gmlp_bs=gmlp_bs[l], conv_w=conv_w[l], conv_b=conv_b[l],
                 dt_bias=dt_bias[l], a_log=a_log[l], d_skip=d_skip[l], ssm_norm_g=ssm_norm_g[l],
                 ln1_g=ln1_g[l], ln1_b=ln1_b[l], ln2_g=ln2_g[l], ln2_b=ln2_b[l])
        wi = w_in[l]
        seg = np.cumsum([0] + IN_SIZES)
        part = lambda n: wi[:, seg[n]:seg[n + 1]]
        w_in_cols = jnp.concatenate(
            [part(0), part(3), part(4), part(5), part(6), part(1), part(2), part(7),
             jnp.zeros((D_MODEL, IN_DIM_PAD - IN_DIM), F32)], axis=1).astype(BF16)
        w_router_pad = jnp.pad(w_router[l], ((0, 0), (0, LANES - N_EXPERTS)))
        lw = dict(
            w_in=w_in_cols,
            w_gate=w_gate[l].astype(BF16),
            b_gate=b_gate[l].reshape(1, -1),
            w_branch=w_branch[l].astype(BF16),
            w_o=w_o[l].astype(BF16),
            w_xq=w_xq[l].astype(BF16),
            w_xo=w_xo[l].astype(BF16),
            w_router=w_router_pad,
            w_router_bf16=w_router_pad.astype(BF16),
            b_router=jnp.pad(b_router[l], (0, LANES - N_EXPERTS)).reshape(1, -1),
        )
        w_kv = jnp.concatenate([w_xk[l], w_xv[l]], axis=1).astype(BF16)
        mkv = _matmul(mem_b, w_kv, n_mem, XH * XHD)
        mk = mkv[:, :XH * XHD].reshape(n_prompt, n_mem, XH, XHD)
        mv = mkv[:, XH * XHD:].reshape(n_prompt, n_mem, XH, XHD)
        x2_p, lg_p, st_p = _prompt_layer(hp, mkv, lw, p, swa_bias, bp, lp)
        x2_s, lg_s, st_s = _sample_layer(hs, l, lw, p, rel_bias, cache_win_k, cache_win_v, state_conv, state_ssm,
                                         cache_mem_k, cache_mem_v)
        hp, hs = _moe_ln(x2_p, x2_s, lg_p, lg_s, experts, l, ln3_g[l], ln3_b[l])
        wk_p.append(st_p[0]); wv_p.append(st_p[1]); cv_p.append(st_p[2]); ssm_p.append(st_p[3])
        mk_ps.append(mk); mv_ps.append(mv)
        wk_s.append(st_s[0]); wv_s.append(st_s[1]); cv_s.append(st_s[2]); ssm_s.append(st_s[3])
        gv_s.append(st_s[4])
    hp = hp.reshape(bp, lp, D_MODEL)
    hs = hs.reshape(bs_, 1, D_MODEL)
    return (hp, hs,
            jnp.stack(wk_p), jnp.stack(wv_p), jnp.stack(cv_p), jnp.stack(ssm_p),
            jnp.stack(mk_ps), jnp.stack(mv_ps),
            jnp.stack(wk_s), jnp.stack(wv_s), jnp.stack(cv_s), jnp.stack(ssm_s), jnp.stack(gv_s))
```

```python
import functools
import math

import numpy as np
import jax
import jax.numpy as jnp
from jax import lax
from jax.experimental import pallas as pl
from jax.experimental.pallas import tpu as pltpu

D_MODEL = 2048
DEPTH = 2
PAST_LEN = 16384
WINDOW = 128
H_A = 16
KV_A = 2
HD_A = 64
G_A = H_A // KV_A
N_BUCKETS = 32
MAX_DIST = 128
CHUNK_B = 128
GB = 16
CG_B = 64
W_B = GB * CG_B
D_INNER = 1024
P_C = 64
H_C = D_INNER // P_C
G_C = 2
R_C = H_C // G_C
N_C = 128
CONV_W = 4
CONV_DIM = D_INNER + 2 * G_C * N_C
SSD_CHUNK = 128
N_BRANCH = 3
BR_W = 1024
N_MEM = 256
XH = 4
XHD = 128
N_EXPERTS = 32
TOP_K = 4
D_FF = D_MODEL
SWIGLU_ALPHA = 1.702
SWIGLU_LIMIT = 7.0
DN_ALPHA = (2 * DEPTH) ** 0.25
LN_EPS = 1e-5
RMS_EPS = 1e-5

IN_SIZES = [H_A * HD_A, KV_A * HD_A, KV_A * HD_A, W_B, W_B, D_INNER, CONV_DIM, H_C]
IN_DIM = sum(IN_SIZES)
IN_DIM_PAD = 6144
COL_Q, COL_U, COL_GV, COL_Z, COL_XBC = 0, 1024, 2048, 3072, 4096
COL_K = COL_XBC + CONV_DIM
COL_V = COL_K + KV_A * HD_A
COL_DT = COL_V + KV_A * HD_A
LANES = 128

VMEM_LIMIT = 56 * 1024 * 1024
MOE_BM = 576
MOE_FC = 256
MOE_COL_TILE = 256

F32 = jnp.float32
BF16 = jnp.bfloat16


def _cparams(*sem):
    return pltpu.CompilerParams(dimension_semantics=sem, vmem_limit_bytes=VMEM_LIMIT)


def _mm_kernel(x_ref, w_ref, o_ref):
    o_ref[...] = jnp.dot(x_ref[...], w_ref[...], preferred_element_type=F32).astype(o_ref.dtype)


def _matmul(x, w, tm, tn, out_dtype=F32):
    M, K = x.shape
    N = w.shape[1]
    assert M % tm == 0 and N % tn == 0
    return pl.pallas_call(
        _mm_kernel,
        grid=(N // tn, M // tm),
        in_specs=[pl.BlockSpec((tm, K), lambda j, i: (i, 0)),
                  pl.BlockSpec((K, tn), lambda j, i: (0, j))],
        out_specs=pl.BlockSpec((tm, tn), lambda j, i: (i, j)),
        out_shape=jax.ShapeDtypeStruct((M, N), out_dtype),
        compiler_params=_cparams("parallel", "parallel"),
        name="dense_matmul",
    )(x, w)


def _gate_merge_kernel(x_ref, a_ref, b_ref, c_ref, wg0_ref, wg1_ref, wg2_ref,
                       bg0_ref, bg1_ref, bg2_ref, wp_ref, o_ref):
    x = x_ref[...]
    acc = None
    for k, (br_ref, wg_ref, bg_ref) in enumerate(
            ((a_ref, wg0_ref, bg0_ref), (b_ref, wg1_ref, bg1_ref), (c_ref, wg2_ref, bg2_ref))):
        z = jnp.dot(x, wg_ref[...], preferred_element_type=F32) + bg_ref[...]
        gate = 1.0 / (1.0 + jnp.exp(-z))
        proj = jnp.dot(br_ref[...], wp_ref[k], preferred_element_type=F32)
        acc = gate * proj if acc is None else acc + gate * proj
    o_ref[...] = acc.astype(o_ref.dtype)


def _gate_merge(x, a, b, c, w_gate, b_gate, w_branch, tm, tn):
    M = x.shape[0]
    nt = D_MODEL // tn
    row = lambda j, i: (i, 0)
    in_specs = [pl.BlockSpec((tm, D_MODEL), row)] + [pl.BlockSpec((tm, BR_W), row)] * 3
    in_specs += [pl.BlockSpec((D_MODEL, tn), functools.partial(lambda j, i, k: (0, k * nt + j), k=k))
                 for k in range(N_BRANCH)]
    in_specs += [pl.BlockSpec((1, tn), functools.partial(lambda j, i, k: (0, k * nt + j), k=k))
                 for k in range(N_BRANCH)]
    in_specs += [pl.BlockSpec((N_BRANCH, BR_W, tn), lambda j, i: (0, 0, j))]
    return pl.pallas_call(
        _gate_merge_kernel,
        grid=(nt, M // tm),
        in_specs=in_specs,
        out_specs=pl.BlockSpec((tm, tn), lambda j, i: (i, j)),
        out_shape=jax.ShapeDtypeStruct((M, D_MODEL), BF16),
        compiler_params=_cparams("parallel", "parallel"),
        name="gate_merge",
    )(x, a, b, c, w_gate, w_gate, w_gate, b_gate, b_gate, b_gate, w_branch)


def _layer_norm_rows(y, g, b):
    mu = jnp.mean(y, axis=-1, keepdims=True)
    yc = y - mu
    var = jnp.mean(yc * yc, axis=-1, keepdims=True)
    return yc * lax.rsqrt(var + LN_EPS) * g + b


def _mm_res_ln_kernel(a_ref, w_ref, res_ref, g_ref, b_ref, o_ref):
    y = jnp.dot(a_ref[...], w_ref[...], preferred_element_type=F32) + DN_ALPHA * res_ref[...]
    o_ref[...] = _layer_norm_rows(y, g_ref[...], b_ref[...])


def _mm_res_ln(a, w, res, g, b, tm):
    M, K = a.shape
    row = lambda i: (i, 0)
    fixed = lambda i: (0, 0)
    return pl.pallas_call(
        _mm_res_ln_kernel,
        grid=(M // tm,),
        in_specs=[pl.BlockSpec((tm, K), row), pl.BlockSpec((K, D_MODEL), fixed),
                  pl.BlockSpec((tm, D_MODEL), row), pl.BlockSpec((1, D_MODEL), fixed),
                  pl.BlockSpec((1, D_MODEL), fixed)],
        out_specs=pl.BlockSpec((tm, D_MODEL), row),
        out_shape=jax.ShapeDtypeStruct((M, D_MODEL), F32),
        compiler_params=_cparams("parallel"),
        name="matmul_residual_layernorm",
    )(a, w, res, g.reshape(1, -1), b.reshape(1, -1))


def _top_k_lanes(logits):
    lane_i = lax.broadcasted_iota(jnp.int32, logits.shape, 1)
    lane_f = lane_i.astype(F32)
    work = jnp.where(lane_i < N_EXPERTS, logits, -jnp.inf)
    out = jnp.zeros_like(logits)
    for k in range(TOP_K):
        m = jnp.max(work, axis=-1, keepdims=True)
        idx = jnp.min(jnp.where(work == m, lane_f, float(LANES)), axis=-1, keepdims=True)
        out = jnp.where(lane_i == k, m, out)
        out = jnp.where(lane_i == TOP_K + k, idx, out)
        work = jnp.where(lane_f == idx, -jnp.inf, work)
    return out


def _router_kernel(x_ref, w_ref, b_ref, o_ref):
    acc = jnp.dot(x_ref[...].astype(BF16), w_ref[...].astype(BF16), preferred_element_type=F32)
    o_ref[...] = _top_k_lanes(acc + b_ref[...])


def _router(x, w_pad, b_pad, tm):
    M = x.shape[0]
    NP = w_pad.shape[1]
    return pl.pallas_call(
        _router_kernel,
        grid=(M // tm,),
        in_specs=[pl.BlockSpec((tm, D_MODEL), lambda i: (i, 0)),
                  pl.BlockSpec((D_MODEL, NP), lambda i: (0, 0)),
                  pl.BlockSpec((1, NP), lambda i: (0, 0))],
        out_specs=pl.BlockSpec((tm, NP), lambda i: (i, 0)),
        out_shape=jax.ShapeDtypeStruct((M, NP), F32),
        compiler_params=_cparams("parallel"),
        name="router_logits",
    )(x, w_pad, b_pad)


def _bf16_row_interleave(a, b):
    a32 = lax.bitcast_convert_type(a.astype(BF16).astype(F32), jnp.uint32)
    b32 = lax.bitcast_convert_type(b.astype(BF16).astype(F32), jnp.uint32)
    word = (a32 >> 16) | (b32 & jnp.uint32(0xFFFF0000))
    return pltpu.bitcast(word, BF16)


def _moe_kernel(be_ref, bx_ref, bv_ref, x_ref, w1_ref, b1_ref, w2_ref, b2_ref, o_ref):
    i = pl.program_id(0)
    f = pl.program_id(1)
    fc = w2_ref.shape[1] // 2
    bm = x_ref.shape[0]
    valid = bv_ref[i] == 1

    @pl.when(valid)
    def _():
        @pl.when(f == 0)
        def _():
            o_ref[...] = jnp.broadcast_to(b2_ref[0], o_ref.shape)

        x = x_ref[...].astype(BF16)
        tile = min(MOE_COL_TILE, 2 * fc)
        h = [jnp.dot(x, w1_ref[0, :, tile * n:tile * (n + 1)].astype(BF16), preferred_element_type=F32)
             + b1_ref[0, :, tile * n:tile * (n + 1)] for n in range(4 * fc // tile)]
        w2 = w2_ref[0]
        w2q = _bf16_row_interleave(w2[:fc], w2[fc:])
        even = (lax.broadcasted_iota(jnp.int32, (bm, LANES), 1) & 1) == 0
        per_tile = tile // LANES
        acts = []
        for c in range(2 * fc // LANES):
            sl = slice(LANES * (c % per_tile), LANES * (c % per_tile + 1))
            a = h[c // per_tile][:, sl]
            b = h[(2 * fc // LANES + c) // per_tile][:, sl]
            glu = jnp.where(even, a, pltpu.roll(b, 1, axis=1))
            lin = jnp.where(even, pltpu.roll(a, LANES - 1, axis=1), b)
            glu = jnp.minimum(glu, SWIGLU_LIMIT)
            lin = jnp.clip(lin, -SWIGLU_LIMIT, SWIGLU_LIMIT)
            act = glu * (1.0 / (1.0 + jnp.exp(-SWIGLU_ALPHA * glu))) * (lin + 1.0)
            acts.append(act.astype(BF16))
        act = jnp.concatenate(acts, axis=1) if len(acts) > 1 else acts[0]
        o_ref[...] += jnp.dot(act, w2q, preferred_element_type=F32)

    @pl.when(jnp.logical_not(valid) & (f == 0))
    def _():
        o_ref[...] = jnp.zeros_like(o_ref)


def _moe_blocks(p_end, n_valid, nb, bm):
    i32 = jnp.int32
    n_exp = p_end.shape[0]
    blk = jnp.arange(nb, dtype=i32)
    block_v = (blk < n_valid).astype(i32)
    block_x = jnp.minimum(blk, n_valid - 1)
    block_e = jnp.minimum(jnp.sum(p_end[None, :] <= (block_x * bm)[:, None], axis=1, dtype=i32), n_exp - 1)
    return block_e, block_x, block_v


def _moe_ffn_blocks(rows, block_e, block_x, block_v, w1, b1, w2, b2, layer, bm, fc):
    d_model = rows.shape[1]
    d_ff = w2.shape[2]
    nb = rows.shape[0] // bm
    nf = d_ff // (2 * fc)
    last_f = nf - 1

    def fsel(f, bv, i):
        return jnp.where(bv[i] == 1, f, last_f)

    grid_spec = pltpu.PrefetchScalarGridSpec(
        num_scalar_prefetch=3,
        grid=(nb, nf),
        in_specs=[
            pl.BlockSpec((bm, d_model), lambda i, f, be, bx, bv: (bx[i], 0)),
            pl.BlockSpec((None, 1, d_model, 4 * fc), lambda i, f, be, bx, bv: (layer, be[i], 0, fsel(f, bv, i))),
            pl.BlockSpec((None, 1, 1, 4 * fc), lambda i, f, be, bx, bv: (layer, be[i], 0, fsel(f, bv, i))),
            pl.BlockSpec((None, 1, 2 * fc, d_model), lambda i, f, be, bx, bv: (layer, be[i], fsel(f, bv, i), 0)),
            pl.BlockSpec((None, 1, 1, d_model), lambda i, f, be, bx, bv: (layer, be[i], 0, 0)),
        ],
        out_specs=pl.BlockSpec((bm, d_model), lambda i, f, be, bx, bv: (i, 0)),
    )
    return pl.pallas_call(
        _moe_kernel,
        grid_spec=grid_spec,
        out_shape=jax.ShapeDtypeStruct(rows.shape, F32),
        compiler_params=_cparams("arbitrary", "arbitrary"),
        name="moe_expert_ffn",
    )(block_e, block_x, block_v, rows, w1, b1, w2, b2)


def _combine_ln_kernel(*refs):
    g_refs = refs[:TOP_K]
    gate_ref, res_ref, lg_ref, lb_ref, o_ref = refs[TOP_K:]
    gate = gate_ref[...]
    ff = None
    for k in range(TOP_K):
        term = gate[:, k:k + 1] * g_refs[k][...]
        ff = term if ff is None else ff + term
    o_ref[...] = _layer_norm_rows(DN_ALPHA * res_ref[...] + ff, lg_ref[...], lb_ref[...])


def _combine_ln(gathered, gate, res, g, b, tm):
    T, d = res.shape
    nt = T // tm
    row = lambda i: (i, 0)
    fixed = lambda i: (0, 0)
    g_specs = [pl.BlockSpec((tm, d), functools.partial(lambda i, k: (k * nt + i, 0), k=k)) for k in range(TOP_K)]
    return pl.pallas_call(
        _combine_ln_kernel,
        grid=(nt,),
        in_specs=g_specs + [pl.BlockSpec((tm, TOP_K), row), pl.BlockSpec((tm, d), row),
                            pl.BlockSpec((1, d), fixed), pl.BlockSpec((1, d), fixed)],
        out_specs=pl.BlockSpec((tm, d), row),
        out_shape=jax.ShapeDtypeStruct((T, d), F32),
        compiler_params=_cparams("parallel"),
        name="moe_combine_layernorm",
    )(*([gathered] * TOP_K), gate, res, g.reshape(1, -1), b.reshape(1, -1))


RANK_TOKENS = 128


def _rank_kernel(route_ref, rank_ref, count_ref, carry_ref):
    i = pl.program_id(0)
    n = route_ref.shape[0]

    @pl.when(i == 0)
    def _():
        carry_ref[...] = jnp.zeros_like(carry_ref)

    route = route_ref[...]
    lane = lax.broadcasted_iota(jnp.int32, (n, LANES), 1)
    lane_f = lane.astype(F32)
    hits = [lane_f == route[:, TOP_K + k:TOP_K + k + 1] for k in range(TOP_K)]
    per_token = jnp.zeros((n, LANES), F32)
    for k in range(TOP_K):
        per_token = per_token + jnp.where(hits[k], 1.0, 0.0)
    earlier = lax.broadcasted_iota(jnp.int32, (n, n), 0) > lax.broadcasted_iota(jnp.int32, (n, n), 1)
    before = jnp.dot(jnp.where(earlier, 1.0, 0.0).astype(BF16), per_token.astype(BF16),
                     preferred_element_type=F32) + carry_ref[...]
    out = jnp.zeros((n, LANES), F32)
    for k in range(TOP_K):
        r = jnp.sum(jnp.where(hits[k], before, 0.0), axis=-1, keepdims=True)
        out = jnp.where(lane == k, r, out)
    rank_ref[...] = out
    carry_ref[...] += jnp.sum(per_token, axis=0, keepdims=True)
    count_ref[...] = carry_ref[...]


def _rank_assignments(route):
    n = route.shape[0]
    return pl.pallas_call(
        _rank_kernel,
        grid=(n // RANK_TOKENS,),
        in_specs=[pl.BlockSpec((RANK_TOKENS, LANES), lambda i: (i, 0))],
        out_specs=[pl.BlockSpec((RANK_TOKENS, LANES), lambda i: (i, 0)), pl.BlockSpec((1, LANES), lambda i: (0, 0))],
        out_shape=[jax.ShapeDtypeStruct((n, LANES), F32), jax.ShapeDtypeStruct((1, LANES), F32)],
        scratch_shapes=[pltpu.VMEM((1, LANES), F32)],
        compiler_params=_cparams("arbitrary"),
        name="moe_rank_assignments",
    )(route)


def _moe_route(route, n_tok, n_exp, bm):
    i32 = jnp.int32
    ranks, counts = _rank_assignments(route)
    top_v = route[:n_tok, :TOP_K]
    top_i = route[:n_tok, TOP_K:2 * TOP_K].astype(i32)
    rank = ranks[:n_tok, :TOP_K].astype(i32)
    gate = jax.nn.softmax(top_v, axis=-1)
    n_assign = n_tok * TOP_K
    nb = -(-(n_assign + n_exp * (bm - 1)) // bm)
    counts = counts[0, :n_exp].astype(i32)
    padded = (counts + bm - 1) // bm * bm
    p_end = jnp.cumsum(padded)
    p_start = (p_end - padded).astype(i32)
    is_e = top_i[:, :, None] == jnp.arange(n_exp, dtype=i32)[None, None, :]
    pos = rank + jnp.sum(jnp.where(is_e, p_start[None, None, :], 0), axis=-1, dtype=i32)
    tok = jnp.broadcast_to(jnp.arange(n_tok, dtype=i32)[:, None], (n_tok, TOP_K))
    src = jnp.zeros((nb * bm,), i32).at[pos.reshape(-1)].set(tok.reshape(-1), unique_indices=True)
    n_valid = (p_end[-1] // bm).astype(i32)
    return gate, src, pos, (padded.astype(i32), p_end.astype(i32), n_valid, nb)


def _moe_ln(x2_p, x2_s, route_p, route_s, experts, layer, ln_g, ln_b):
    tp, ts = x2_p.shape[0], x2_s.shape[0]
    x2 = jnp.concatenate([x2_p, x2_s], axis=0)
    n_pad = -(tp + ts) % RANK_TOKENS
    route = jnp.concatenate([route_p, route_s, jnp.full((n_pad, LANES), -1.0, F32)], axis=0)
    gate, src, pos, (padded, p_end, n_valid, nb) = _moe_route(route, tp + ts, N_EXPERTS, MOE_BM)
    rows = jnp.take(x2, src, axis=0, mode='clip')
    w_e1, b_e1, w_e2, b_e2 = experts
    block_e, block_x, block_v = _moe_blocks(p_end, n_valid, nb, MOE_BM)
    out_rows = _moe_ffn_blocks(rows, block_e, block_x, block_v, w_e1, b_e1, w_e2, b_e2, layer, MOE_BM, MOE_FC)
    pos = pos.reshape(tp + ts, TOP_K)
    g_p = jnp.take(out_rows, pos[:tp].T.reshape(-1), axis=0, mode='clip')
    g_s = jnp.take(out_rows, pos[tp:].T.reshape(-1), axis=0, mode='clip')
    hp = _combine_ln(g_p, gate[:tp], x2_p, ln_g, ln_b, 256)
    hs = _combine_ln(g_s, gate[tp:], x2_s, ln_g, ln_b, ts)
    return hp, hs


def _half_lane_variants(t):
    lo = lax.broadcasted_iota(jnp.int32, t.shape, 1) < HD_A
    zero = jnp.zeros_like(t)
    tr = pltpu.roll(t, HD_A, axis=1)
    return [[jnp.where(lo, t, zero).astype(BF16), jnp.where(lo, zero, tr).astype(BF16)],
            [jnp.where(lo, tr, zero).astype(BF16), jnp.where(lo, zero, t).astype(BF16)]]


def _swa_kernel(sinks_ref, q_ref, kp_ref, kc_ref, vp_ref, vc_ref, bias_ref, o_ref):
    j = pl.program_id(1)
    kvar = _half_lane_variants(jnp.concatenate([kp_ref[...], kc_ref[...]], axis=0))
    vvar = _half_lane_variants(jnp.concatenate([vp_ref[...], vc_ref[...]], axis=0))
    col = lax.broadcasted_iota(jnp.int32, (WINDOW, 2 * WINDOW), 1)
    no_prev = col < jnp.where(j == 0, WINDOW, 0)
    for r in range(H_A // 2):
        g = (2 * r) // G_A
        qp = q_ref[:, LANES * r:LANES * (r + 1)].astype(BF16)
        acc = None
        for par in range(2):
            h = 2 * r + par
            s = lax.dot_general(qp, kvar[g][par], (((1,), (1,)), ((), ())), preferred_element_type=F32)
            s = s * HD_A ** -0.5 + bias_ref[h]
            s = jnp.where(no_prev, -jnp.inf, s)
            sk = sinks_ref[h]
            m = jnp.maximum(jnp.max(s, axis=-1, keepdims=True), sk)
            pr = jnp.exp(s - m)
            den = jnp.sum(pr, axis=-1, keepdims=True) + jnp.exp(sk - m)
            o = jnp.dot((pr / den).astype(BF16), vvar[g][par], preferred_element_type=F32)
            acc = o if acc is None else acc + o
        o_ref[:, LANES * r:LANES * (r + 1)] = acc.astype(o_ref.dtype)


def _swa_prompt_bias(rel_bias):
    i = jnp.arange(WINDOW)[:, None]
    j = jnp.arange(2 * WINDOW)[None, :]
    dist = i + WINDOW - j
    return jnp.where((dist >= 0) & (dist < WINDOW), _bucket_bias(rel_bias, dist), -jnp.inf)


def _swa_prompt_call(proj2d, bias, sinks, bsz, L):
    nb = L // WINDOW
    kcol, vcol = COL_K // LANES, COL_V // LANES
    cur = lambda b, j: b * nb + j
    prev = lambda b, j: b * nb + jnp.maximum(j - 1, 0)
    return pl.pallas_call(
        _swa_kernel,
        grid=(bsz, nb),
        in_specs=[
            pl.BlockSpec(memory_space=pltpu.SMEM),
            pl.BlockSpec((WINDOW, H_A * HD_A), lambda b, j: (cur(b, j), COL_Q // (H_A * HD_A))),
            pl.BlockSpec((WINDOW, LANES), lambda b, j: (prev(b, j), kcol)),
            pl.BlockSpec((WINDOW, LANES), lambda b, j: (cur(b, j), kcol)),
            pl.BlockSpec((WINDOW, LANES), lambda b, j: (prev(b, j), vcol)),
            pl.BlockSpec((WINDOW, LANES), lambda b, j: (cur(b, j), vcol)),
            pl.BlockSpec((H_A, WINDOW, 2 * WINDOW), lambda b, j: (0, 0, 0)),
        ],
        out_specs=pl.BlockSpec((WINDOW, H_A * HD_A), lambda b, j: (cur(b, j), 0)),
        out_shape=jax.ShapeDtypeStruct((bsz * L, H_A * HD_A), BF16),
        compiler_params=_cparams("parallel", "arbitrary"),
        name="swa_prompt",
    )(sinks, proj2d, proj2d, proj2d, proj2d, proj2d, bias)


def _gelu(x):
    return 0.5 * x * (1.0 + lax.erf(x * np.float32(np.sqrt(0.5))))


def _gmlp_kernel(u_ref, gv_ref, lng_ref, lnb_ref, w_ref, bias_ref, o_ref):
    u = _gelu(u_ref[...])
    gv = _layer_norm_rows(_gelu(gv_ref[...]), lng_ref[...], lnb_ref[...])
    lo = lax.broadcasted_iota(jnp.int32, (CHUNK_B, LANES), 1) < CG_B
    for r in range(GB // 2):
        vp = gv[:, LANES * r:LANES * (r + 1)]
        zero = jnp.zeros_like(vp)
        mix = jnp.dot(w_ref[2 * r], jnp.where(lo, vp, zero).astype(BF16), preferred_element_type=F32)
        mix += jnp.dot(w_ref[2 * r + 1], jnp.where(lo, zero, vp).astype(BF16), preferred_element_type=F32)
        sl = slice(LANES * r, LANES * (r + 1))
        o_ref[:, sl] = (u[:, sl] * (mix + bias_ref[:, sl])).astype(o_ref.dtype)


def _gmlp_prompt_call(proj2d, p, n_rows):
    w = (p['gmlp_ws'] * jnp.tril(jnp.ones((CHUNK_B, CHUNK_B), F32))).astype(BF16)
    bias = jnp.repeat(p['gmlp_bs'].T, CG_B, axis=1)
    fixed2 = lambda i: (0, 0)
    return pl.pallas_call(
        _gmlp_kernel,
        grid=(n_rows // CHUNK_B,),
        in_specs=[
            pl.BlockSpec((CHUNK_B, W_B), lambda i: (i, COL_U // W_B)),
            pl.BlockSpec((CHUNK_B, W_B), lambda i: (i, COL_GV // W_B)),
            pl.BlockSpec((1, W_B), fixed2), pl.BlockSpec((1, W_B), fixed2),
            pl.BlockSpec((GB, CHUNK_B, CHUNK_B), lambda i: (0, 0, 0)),
            pl.BlockSpec((CHUNK_B, W_B), fixed2),
        ],
        out_specs=pl.BlockSpec((CHUNK_B, W_B), lambda i: (i, 0)),
        out_shape=jax.ShapeDtypeStruct((n_rows, W_B), BF16),
        compiler_params=_cparams("parallel"),
        name="gmlp_prompt",
    )(proj2d, proj2d, p['gmlp_ln_g'].reshape(1, -1), p['gmlp_ln_b'].reshape(1, -1), w, bias)


def _silu(x):
    return x * (1.0 / (1.0 + jnp.exp(-x)))


def _softplus(x):
    return jnp.maximum(x, 0.0) + jnp.log1p(jnp.exp(-jnp.abs(x)))


def _causal_conv_chunk(cur, tail, w, bias):
    rows = lax.broadcasted_iota(jnp.int32, (8, cur.shape[1]), 0)
    y = jnp.broadcast_to(bias, cur.shape)
    y_head = jnp.broadcast_to(bias, (8, cur.shape[1]))
    for t in range(CONV_W):
        k = CONV_W - 1 - t
        wt = w[t:t + 1, :]
        if k == 0:
            y = y + cur * wt
            y_head = y_head + cur[:8] * wt
        else:
            sh = pltpu.roll(cur, k, axis=0)
            y = y + sh * wt
            y_head = y_head + jnp.where(rows < k, pltpu.roll(tail, k, axis=0), sh[:8]) * wt
    return jnp.concatenate([y_head, y[8:]], axis=0)


def _bf16_split3(x):
    p1 = x.astype(BF16)
    r1 = x - p1.astype(F32)
    p2 = r1.astype(BF16)
    p3 = (r1 - p2.astype(F32)).astype(BF16)
    return p1, p2, p3


def _ssd_kernel(xs_ref, bc_ref, z_ref, dt_ref, cwx_ref, cbx_ref, cwb_ref, cbb_ref, dtb_ref, a_ref,
                dsk_ref, ng_ref, y_ref, h_ref, state_ref, tailx_ref, tailb_ref):
    c = pl.program_id(1)
    C = SSD_CHUNK

    @pl.when(c == 0)
    def _():
        state_ref[...] = jnp.zeros_like(state_ref)
        tailx_ref[...] = jnp.zeros_like(tailx_ref)
        tailb_ref[...] = jnp.zeros_like(tailb_ref)

    xs_raw = xs_ref[...]
    bc_raw = bc_ref[...]
    xs = _silu(_causal_conv_chunk(xs_raw, tailx_ref[...], cwx_ref[...], cbx_ref[...]))
    bc = _silu(_causal_conv_chunk(bc_raw, tailb_ref[...], cwb_ref[...], cbb_ref[...]))
    tailx_ref[...] = xs_raw[C - 8:]
    tailb_ref[...] = bc_raw[C - 8:]

    dt = _softplus(dt_ref[...] + dtb_ref[...])
    da = dt * a_ref[...]
    row_i = lax.broadcasted_iota(jnp.int32, (C, C), 0)
    col_i = lax.broadcasted_iota(jnp.int32, (C, C), 1)
    causal = row_i >= col_i
    tril = jnp.where(causal, 1.0, 0.0).astype(BF16)
    acs = None
    for piece in _bf16_split3(da):
        t = jnp.dot(tril, piece, preferred_element_type=F32)
        acs = t if acs is None else acs + t
    acs_t = acs.T
    exp_acs = jnp.exp(acs)
    end_decay = jnp.exp(acs[C - 1:C, :] - acs)
    chunk_decay = jnp.exp(acs[C - 1:C, :])

    lo = lax.broadcasted_iota(jnp.int32, (C, LANES), 1) < P_C
    bm = [bc[:, N_C * g:N_C * (g + 1)].astype(BF16) for g in range(G_C)]
    cm = [bc[:, N_C * (G_C + g):N_C * (G_C + g + 1)].astype(BF16) for g in range(G_C)]
    cb = [lax.dot_general(cm[g], bm[g], (((1,), (1,)), ((), ())), preferred_element_type=F32)
          for g in range(G_C)]

    def per_lane_half(t, r):
        return jnp.where(lo, t[:, 2 * r:2 * r + 1], t[:, 2 * r + 1:2 * r + 2])

    ys = []
    for r in range(H_C // 2):
        g = (2 * r) // R_C
        sl = slice(LANES * r, LANES * (r + 1))
        x_pair = xs[:, sl]
        xdt = x_pair * per_lane_half(dt, r)
        zero = jnp.zeros_like(xdt)
        y_pair = None
        for par in range(2):
            h = 2 * r + par
            seg = acs[:, h:h + 1] - acs_t[h:h + 1, :]
            decay = jnp.where(causal, jnp.exp(seg), 0.0)
            m_h = (cb[g] * decay).astype(BF16)
            x_h = (jnp.where(lo, xdt, zero) if par == 0 else jnp.where(lo, zero, xdt)).astype(BF16)
            t = jnp.dot(m_h, x_h, preferred_element_type=F32)
            y_pair = t if y_pair is None else y_pair + t
        st = state_ref[sl, :]
        y_off = lax.dot_general(cm[g], st.astype(BF16), (((1,), (1,)), ((), ())), preferred_element_type=F32)
        y_pair = y_pair + y_off * per_lane_half(exp_acs, r)
        upd = lax.dot_general((xdt * per_lane_half(end_decay, r)).astype(BF16), bm[g],
                              (((0,), (0,)), ((), ())), preferred_element_type=F32)
        cd = jnp.concatenate([jnp.broadcast_to(chunk_decay[:, 2 * r:2 * r + 1], (P_C, N_C)),
                              jnp.broadcast_to(chunk_decay[:, 2 * r + 1:2 * r + 2], (P_C, N_C))], axis=0)
        state_ref[sl, :] = st * cd + upd
        ys.append(y_pair + dsk_ref[:, sl] * x_pair)
    y = jnp.concatenate(ys, axis=1) * _silu(z_ref[...])
    gw = D_INNER // G_C
    outs = []
    for g in range(G_C):
        yg = y[:, gw * g:gw * (g + 1)]
        outs.append(yg * lax.rsqrt(jnp.mean(yg * yg, axis=-1, keepdims=True) + RMS_EPS))
    y_ref[...] = (jnp.concatenate(outs, axis=1) * ng_ref[...]).astype(y_ref.dtype)

    @pl.when(c == pl.num_programs(1) - 1)
    def _():
        h_ref[...] = state_ref[...]


def _ssd_prompt_call(proj2d, p, bsz, L):
    nc = L // SSD_CHUNK
    row = lambda blk: (lambda b, c: (b * nc + c, blk))
    fixed = lambda b, c: (0, 0)
    pad_l = lambda v: jnp.pad(v.astype(F32), (0, LANES - H_C)).reshape(1, LANES)
    cw, cbias = p['conv_w'], p['conv_b'].reshape(1, -1)
    nbc = 2 * G_C * N_C
    args = (proj2d, proj2d, proj2d, proj2d,
            cw[:, :D_INNER], cbias[:, :D_INNER], cw[:, D_INNER:], cbias[:, D_INNER:],
            pad_l(p['dt_bias']), pad_l(-jnp.exp(p['a_log'].astype(F32))),
            jnp.repeat(p['d_skip'].astype(F32), P_C).reshape(1, -1), p['ssm_norm_g'].reshape(1, -1))
    return pl.pallas_call(
        _ssd_kernel,
        grid=(bsz, nc),
        in_specs=[
            pl.BlockSpec((SSD_CHUNK, D_INNER), row(COL_XBC // D_INNER)),
            pl.BlockSpec((SSD_CHUNK, nbc), row((COL_XBC + D_INNER) // nbc)),
            pl.BlockSpec((SSD_CHUNK, D_INNER), row(COL_Z // D_INNER)),
            pl.BlockSpec((SSD_CHUNK, LANES), row(COL_DT // LANES)),
            pl.BlockSpec((CONV_W, D_INNER), fixed), pl.BlockSpec((1, D_INNER), fixed),
            pl.BlockSpec((CONV_W, nbc), fixed), pl.BlockSpec((1, nbc), fixed),
            pl.BlockSpec((1, LANES), fixed), pl.BlockSpec((1, LANES), fixed),
            pl.BlockSpec((1, D_INNER), fixed), pl.BlockSpec((1, D_INNER), fixed),
        ],
        out_specs=[pl.BlockSpec((SSD_CHUNK, D_INNER), lambda b, c: (b * nc + c, 0)),
                   pl.BlockSpec((None, H_C * P_C, N_C), lambda b, c: (b, 0, 0))],
        out_shape=[jax.ShapeDtypeStruct((bsz * L, D_INNER), BF16),
                   jax.ShapeDtypeStruct((bsz, H_C * P_C, N_C), F32)],
        scratch_shapes=[pltpu.VMEM((H_C * P_C, N_C), F32), pltpu.VMEM((8, D_INNER), F32),
                        pltpu.VMEM((8, nbc), F32)],
        compiler_params=_cparams("parallel", "arbitrary"),
        name="ssd_prompt",
    )(*args)


def _bf16_round(x):
    return x.astype(BF16).astype(F32)


SWA_DECODE_TOKENS = 8


def _swa_decode_kernel(q_ref, kn_ref, vn_ref, ck_ref, cv_ref, bias_ref, sink_ref, o_ref, wk_ref, wv_ref):
    W = WINDOW
    last = lax.broadcasted_iota(jnp.int32, (W, LANES), 0) == W - 1
    lo = lax.broadcasted_iota(jnp.int32, (8, LANES), 1) < HD_A
    g0 = lax.broadcasted_iota(jnp.int32, (8, LANES), 0) < G_A // 2
    zero = jnp.zeros((8, LANES), F32)
    nt_dims = (((1,), (1,)), ((), ()))
    for t in range(q_ref.shape[0]):
        kw = jnp.where(last, jnp.broadcast_to(kn_ref[t], (W, LANES)), pltpu.roll(ck_ref[t], W - 1, axis=0))
        vw = jnp.where(last, jnp.broadcast_to(vn_ref[t], (W, LANES)), pltpu.roll(cv_ref[t], W - 1, axis=0))
        wk_ref[t] = kw
        wv_ref[t] = vw
        kb, vb = kw.astype(BF16), vw.astype(BF16)
        q = q_ref[t]
        qr = pltpu.roll(q, HD_A, axis=1)
        q_par = [jnp.where(g0, jnp.where(lo, q, zero), jnp.where(lo, zero, qr)),
                 jnp.where(g0, jnp.where(lo, qr, zero), jnp.where(lo, zero, q))]
        outs = []
        for par in range(2):
            s = lax.dot_general(q_par[par].astype(BF16), kb, nt_dims, preferred_element_type=F32)
            s = s * HD_A ** -0.5 + bias_ref[par]
            sk = sink_ref[par]
            m = jnp.maximum(jnp.max(s, axis=-1, keepdims=True), sk)
            pr = jnp.exp(s - m)
            den = jnp.sum(pr, axis=-1, keepdims=True) + jnp.exp(sk - m)
            outs.append(jnp.dot((pr / den).astype(BF16), vb, preferred_element_type=F32))
        o_even = jnp.where(g0, outs[0], pltpu.roll(outs[0], HD_A, axis=1))
        o_odd = jnp.where(g0, pltpu.roll(outs[1], HD_A, axis=1), outs[1])
        o_ref[t] = jnp.where(lo, o_even, o_odd).astype(o_ref.dtype)


def _swa_decode_call(proj3, cache_k, cache_v, layer, rel_bias, sinks):
    bsz = proj3.shape[0]
    nt = SWA_DECODE_TOKENS
    ck = cache_k.reshape(cache_k.shape[0], bsz, WINDOW, LANES)
    cv = cache_v.reshape(cache_v.shape[0], bsz, WINDOW, LANES)
    q8 = proj3[:, 0, COL_Q:COL_Q + H_A * HD_A].reshape(bsz, H_A // 2, LANES)
    dist = WINDOW - 1 - jnp.arange(WINDOW)
    bias = _bucket_bias(rel_bias, dist).reshape(H_A // 2, 2, WINDOW).transpose(1, 0, 2)
    sink = sinks.astype(F32).reshape(H_A // 2, 2, 1).transpose(1, 0, 2)
    new_tok = lambda blk: (lambda i: (i, 0, blk))
    cache = lambda i: (layer, i, 0, 0)
    tok3 = lambda i: (i, 0, 0)
    out, wk, wv = pl.pallas_call(
        _swa_decode_kernel,
        grid=(bsz // nt,),
        in_specs=[
            pl.BlockSpec((nt, H_A // 2, LANES), tok3),
            pl.BlockSpec((nt, 1, LANES), new_tok(COL_K // LANES)),
            pl.BlockSpec((nt, 1, LANES), new_tok(COL_V // LANES)),
            pl.BlockSpec((None, nt, WINDOW, LANES), cache),
            pl.BlockSpec((None, nt, WINDOW, LANES), cache),
            pl.BlockSpec((2, H_A // 2, WINDOW), lambda i: (0, 0, 0)),
            pl.BlockSpec((2, H_A // 2, 1), lambda i: (0, 0, 0)),
        ],
        out_specs=[pl.BlockSpec((nt, H_A // 2, LANES), tok3),
                   pl.BlockSpec((nt, WINDOW, LANES), tok3),
                   pl.BlockSpec((nt, WINDOW, LANES), tok3)],
        out_shape=[jax.ShapeDtypeStruct((bsz, H_A // 2, LANES), BF16),
                   jax.ShapeDtypeStruct((bsz, WINDOW, LANES), F32),
                   jax.ShapeDtypeStruct((bsz, WINDOW, LANES), F32)],
        compiler_params=_cparams("parallel"),
        name="swa_sample",
    )(q8, proj3, proj3, ck, cv, bias, sink)
    return out.reshape(bsz, H_A * HD_A), wk, wv


def _gmlp_step_kernel(u_ref, gv_ref, lng_ref, lnb_ref, w0_ref, b0_ref, o_ref, gv_out_ref):
    gv = _layer_norm_rows(_gelu(gv_ref[...]), lng_ref[...], lnb_ref[...])
    gv_out_ref[...] = gv
    mix = _bf16_round(w0_ref[...]) * _bf16_round(gv) + b0_ref[...]
    o_ref[...] = (_gelu(u_ref[...]) * mix).astype(o_ref.dtype)


def _gmlp_step_call(proj2d, p):
    n = proj2d.shape[0]
    w0 = jnp.repeat(p['gmlp_ws'][:, 0, 0], CG_B).reshape(1, -1)
    b0 = jnp.repeat(p['gmlp_bs'][:, 0], CG_B).reshape(1, -1)
    fixed = lambda i: (0, 0)
    return pl.pallas_call(
        _gmlp_step_kernel,
        grid=(1,),
        in_specs=[pl.BlockSpec((n, W_B), lambda i: (0, COL_U // W_B)),
                  pl.BlockSpec((n, W_B), lambda i: (0, COL_GV // W_B)),
                  pl.BlockSpec((1, W_B), fixed), pl.BlockSpec((1, W_B), fixed),
                  pl.BlockSpec((1, W_B), fixed), pl.BlockSpec((1, W_B), fixed)],
        out_specs=[pl.BlockSpec((n, W_B), fixed), pl.BlockSpec((n, W_B), fixed)],
        out_shape=[jax.ShapeDtypeStruct((n, W_B), BF16), jax.ShapeDtypeStruct((n, W_B), F32)],
        compiler_params=_cparams("arbitrary"),
        name="gmlp_sample",
    )(proj2d, proj2d, p['gmlp_ln_g'].reshape(1, -1), p['gmlp_ln_b'].reshape(1, -1), w0, b0)


def _conv_step(st, cur, w, bias):
    y = bias
    for t in range(CONV_W - 1):
        y = y + st[t:t + 1, :] * w[t:t + 1, :]
    return y + cur * w[CONV_W - 1:CONV_W, :]


def _ssd_step_kernel(xs_ref, bc_ref, z_ref, dt_ref, stx_ref, stb_ref, h0_ref, cwx_ref, cbx_ref, cwb_ref, cbb_ref,
                     dtb_ref, a_ref, dsk_ref, ng_ref, y_ref, h_ref, ncx_ref, ncb_ref):
    xs_raw, bc_raw = xs_ref[...], bc_ref[...]
    stx, stb = stx_ref[...], stb_ref[...]
    ncx_ref[0:CONV_W - 2, :] = stx[1:CONV_W - 1]
    ncx_ref[CONV_W - 2:CONV_W - 1, :] = xs_raw
    ncb_ref[0:CONV_W - 2, :] = stb[1:CONV_W - 1]
    ncb_ref[CONV_W - 2:CONV_W - 1, :] = bc_raw
    xs = _silu(_conv_step(stx, xs_raw, cwx_ref[...], cbx_ref[...]))
    bc = _silu(_conv_step(stb, bc_raw, cwb_ref[...], cbb_ref[...]))
    dt = _softplus(dt_ref[...] + dtb_ref[...])
    decay = jnp.exp(dt * a_ref[...])
    xdt = _bf16_round(xs * dt)
    gw = D_INNER // G_C
    first_group = lax.broadcasted_iota(jnp.int32, (1, D_INNER), 1) < gw
    bm = [_bf16_round(bc[:, N_C * g:N_C * (g + 1)]) for g in range(G_C)]
    cm = [_bf16_round(bc[:, N_C * (G_C + g):N_C * (G_C + g + 1)]) for g in range(G_C)]
    cb = [_bf16_round(jnp.sum(cm[g] * bm[g], axis=1, keepdims=True)) for g in range(G_C)]
    y_diag = jnp.where(first_group, cb[0], cb[1]) * xdt
    h0 = h0_ref[...]
    y_off = jnp.concatenate([
        lax.dot_general(jnp.broadcast_to(cm[g], (8, N_C)).astype(BF16), h0[gw * g:gw * (g + 1)].astype(BF16),
                        (((1,), (1,)), ((), ())), preferred_element_type=F32)[0:1] for g in range(G_C)], axis=1)
    y = y_diag + y_off * decay
    y = (y + dsk_ref[...] * xs) * _silu(z_ref[...])
    outs = []
    for g in range(G_C):
        yg = y[:, gw * g:gw * (g + 1)]
        outs.append(yg * lax.rsqrt(jnp.mean(yg * yg, axis=-1, keepdims=True) + RMS_EPS))
    y_ref[...] = (jnp.concatenate(outs, axis=1) * ng_ref[...]).astype(y_ref.dtype)
    decay_rows = jnp.broadcast_to(decay, (LANES, D_INNER)).T
    xdt_rows = jnp.broadcast_to(xdt, (LANES, D_INNER)).T
    rows = lax.broadcasted_iota(jnp.int32, (D_INNER, N_C), 0)
    bm_rows = jnp.where(rows < gw, jnp.broadcast_to(bm[0], (D_INNER, N_C)), jnp.broadcast_to(bm[1], (D_INNER, N_C)))
    h_ref[...] = h0 * decay_rows + xdt_rows * bm_rows


def _ssd_step_call(proj3, state_conv, state_ssm, layer, p):
    bsz = proj3.shape[0]
    nbc = 2 * G_C * N_C
    ssm = state_ssm.reshape(state_ssm.shape[0], bsz, H_C * P_C, N_C)
    dt_lanes = jnp.repeat(proj3[:, :, COL_DT:COL_DT + H_C], P_C, axis=-1)
    per_lane = lambda v: jnp.repeat(v.astype(F32), P_C).reshape(1, -1)
    cw, cbias = p['conv_w'], p['conv_b'].reshape(1, -1)
    tok = lambda blk: (lambda i: (i, 0, blk))
    fixed = lambda i: (0, 0)
    return pl.pallas_call(
        _ssd_step_kernel,
        grid=(bsz,),
        in_specs=[
            pl.BlockSpec((None, 1, D_INNER), tok(COL_XBC // D_INNER)),
            pl.BlockSpec((None, 1, nbc), tok((COL_XBC + D_INNER) // nbc)),
            pl.BlockSpec((None, 1, D_INNER), tok(COL_Z // D_INNER)),
            pl.BlockSpec((None, 1, D_INNER), tok(0)),
            pl.BlockSpec((None, None, CONV_W - 1, D_INNER), lambda i: (layer, i, 0, 0)),
            pl.BlockSpec((None, None, CONV_W - 1, nbc), lambda i: (layer, i, 0, D_INNER // nbc)),
            pl.BlockSpec((None, None, H_C * P_C, N_C), lambda i: (layer, i, 0, 0)),
            pl.BlockSpec((CONV_W, D_INNER), fixed), pl.BlockSpec((1, D_INNER), fixed),
            pl.BlockSpec((CONV_W, nbc), fixed), pl.BlockSpec((1, nbc), fixed),
            pl.BlockSpec((1, D_INNER), fixed), pl.BlockSpec((1, D_INNER), fixed),
            pl.BlockSpec((1, D_INNER), fixed), pl.BlockSpec((1, D_INNER), fixed),
        ],
        out_specs=[pl.BlockSpec((None, 1, D_INNER), lambda i: (i, 0, 0)),
                   pl.BlockSpec((None, H_C * P_C, N_C), lambda i: (i, 0, 0)),
                   pl.BlockSpec((None, CONV_W - 1, D_INNER), lambda i: (i, 0, 0)),
                   pl.BlockSpec((None, CONV_W - 1, nbc), lambda i: (i, 0, 0))],
        out_shape=[jax.ShapeDtypeStruct((bsz, 1, D_INNER), BF16),
                   jax.ShapeDtypeStruct((bsz, H_C * P_C, N_C), F32),
                   jax.ShapeDtypeStruct((bsz, CONV_W - 1, D_INNER), F32),
                   jax.ShapeDtypeStruct((bsz, CONV_W - 1, nbc), F32)],
        compiler_params=_cparams("parallel"),
        name="ssd_sample",
    )(proj3, proj3, proj3, dt_lanes, state_conv, state_conv, ssm,
      cw[:, :D_INNER], cbias[:, :D_INNER], cw[:, D_INNER:], cbias[:, D_INNER:],
      per_lane(p['dt_bias']), per_lane(-jnp.exp(p['a_log'].astype(F32))), per_lane(p['d_skip']),
      p['ssm_norm_g'].reshape(1, -1))


def _xattn_kernel(x1_ref, wq_ref, mk_ref, mv_ref, wo_ref, g_ref, b_ref, wr_ref, br_ref, x2_ref, lg_ref):
    x1 = x1_ref[...]
    q = jnp.dot(x1.astype(BF16), wq_ref[...], preferred_element_type=F32)
    outs = []
    for h in range(XH):
        sl = slice(XHD * h, XHD * (h + 1))
        s = lax.dot_general(q[:, sl].astype(BF16), mk_ref[:, sl].astype(BF16),
                            (((1,), (1,)), ((), ())), preferred_element_type=F32) * XHD ** -0.5
        e = jnp.exp(s - jnp.max(s, axis=-1, keepdims=True))
        w = e / jnp.sum(e, axis=-1, keepdims=True)
        outs.append(jnp.dot(w.astype(BF16), mv_ref[:, sl].astype(BF16), preferred_element_type=F32).astype(BF16))
    o = jnp.concatenate(outs, axis=1)
    y = jnp.dot(o, wo_ref[...], preferred_element_type=F32) + DN_ALPHA * x1
    x2 = _layer_norm_rows(y, g_ref[...], b_ref[...])
    x2_ref[...] = x2
    lg_ref[...] = _top_k_lanes(jnp.dot(x2.astype(BF16), wr_ref[...], preferred_element_type=F32) + br_ref[...])


def _xattn_prompt_call(x1, mkv, lw, g, b, bsz, L, tm):
    nt = L // tm
    xw = XH * XHD
    row = lambda bb, i: (bb * nt + i, 0)
    fixed = lambda bb, i: (0, 0)
    n_lg = lw['w_router'].shape[1]
    return pl.pallas_call(
        _xattn_kernel,
        grid=(bsz, nt),
        in_specs=[
            pl.BlockSpec((tm, D_MODEL), row),
            pl.BlockSpec((D_MODEL, xw), fixed),
            pl.BlockSpec((N_MEM, xw), lambda bb, i: (bb, 0)),
            pl.BlockSpec((N_MEM, xw), lambda bb, i: (bb, 1)),
            pl.BlockSpec((xw, D_MODEL), fixed),
            pl.BlockSpec((1, D_MODEL), fixed), pl.BlockSpec((1, D_MODEL), fixed),
            pl.BlockSpec((D_MODEL, n_lg), fixed), pl.BlockSpec((1, n_lg), fixed),
        ],
        out_specs=[pl.BlockSpec((tm, D_MODEL), row), pl.BlockSpec((tm, n_lg), row)],
        out_shape=[jax.ShapeDtypeStruct((bsz * L, D_MODEL), F32), jax.ShapeDtypeStruct((bsz * L, n_lg), F32)],
        compiler_params=_cparams("parallel", "parallel"),
        name="memory_attention_prompt",
    )(x1, lw['w_xq'], mkv, mkv, lw['w_xo'], g.reshape(1, -1), b.reshape(1, -1),
      lw['w_router_bf16'], lw['b_router'])


def _xattn_decode_kernel(q_ref, k_ref, v_ref, o_ref):
    q = q_ref[...].astype(BF16).astype(F32)
    k = k_ref[...].astype(BF16).astype(F32)
    v = v_ref[...].astype(BF16).astype(F32)
    prod = k * q
    outs = []
    for h in range(XH):
        sl = slice(XHD * h, XHD * (h + 1))
        s = jnp.sum(prod[:, sl], axis=1, keepdims=True) * XHD ** -0.5
        e = jnp.exp(s - jnp.max(s, axis=0, keepdims=True))
        w = (e / jnp.sum(e, axis=0, keepdims=True)).astype(BF16).astype(F32)
        outs.append(jnp.sum(w * v[:, sl], axis=0, keepdims=True))
    o_ref[...] = jnp.concatenate(outs, axis=1)


def _xattn_decode_call(q, cache_k, cache_v, layer):
    bsz, xw = q.shape
    ck = cache_k.reshape(cache_k.shape[0], bsz, N_MEM, xw)
    cv = cache_v.reshape(cache_v.shape[0], bsz, N_MEM, xw)
    out = pl.pallas_call(
        _xattn_decode_kernel,
        grid=(bsz,),
        in_specs=[pl.BlockSpec((None, 1, xw), lambda i: (i, 0, 0)),
                  pl.BlockSpec((None, None, N_MEM, xw), lambda i: (layer, i, 0, 0)),
                  pl.BlockSpec((None, None, N_MEM, xw), lambda i: (layer, i, 0, 0))],
        out_specs=pl.BlockSpec((None, 1, xw), lambda i: (i, 0, 0)),
        out_shape=jax.ShapeDtypeStruct((bsz, 1, xw), F32),
        compiler_params=_cparams("parallel"),
        name="memory_attention_sample",
    )(q.reshape(bsz, 1, xw), ck, cv)
    return out.reshape(bsz, xw)


def _t5_bucket(dist):
    n = jnp.maximum(dist, 0)
    exact = N_BUCKETS // 2
    nf = jnp.maximum(n, 1).astype(F32)
    large = exact + (jnp.log(nf / exact) / math.log(MAX_DIST / exact) * (N_BUCKETS - exact)).astype(jnp.int32)
    return jnp.where(n < exact, n, jnp.minimum(large, N_BUCKETS - 1))


def _bucket_bias(rel_bias, dist):
    onehot = (_t5_bucket(dist)[..., None] == jnp.arange(N_BUCKETS)).astype(F32)
    return jnp.einsum('...b,bh->h...', onehot, rel_bias.astype(F32), precision=lax.Precision.HIGHEST)


def _prompt_layer(x, mkv, lw, p, swa_bias, bsz, L):
    T = bsz * L
    xb = x.astype(BF16)
    proj = _matmul(xb, lw['w_in'], 1024, 512)
    a_out = _swa_prompt_call(proj, swa_bias, p['sinks'], bsz, L)
    b_out = _gmlp_prompt_call(proj, p, T)
    c_out, h_last = _ssd_prompt_call(proj, p, bsz, L)
    mixed = _gate_merge(xb, a_out, b_out, c_out, lw['w_gate'], lw['b_gate'], lw['w_branch'], 512, 512)
    x1 = _mm_res_ln(mixed, lw['w_o'], x, p['ln1_g'], p['ln1_b'], 512)
    x2, logits = _xattn_prompt_call(x1, mkv, lw, p['ln2_g'], p['ln2_b'], bsz, L, 512)
    proj3 = proj.reshape(bsz, L, IN_DIM_PAD)
    win_k = proj3[:, L - WINDOW:, COL_K:COL_K + KV_A * HD_A].reshape(bsz, WINDOW, KV_A, HD_A)
    win_v = proj3[:, L - WINDOW:, COL_V:COL_V + KV_A * HD_A].reshape(bsz, WINDOW, KV_A, HD_A)
    conv = proj3[:, L - (CONV_W - 1):, COL_XBC:COL_XBC + CONV_DIM]
    return x2, logits, (win_k, win_v, conv, h_last.reshape(bsz, H_C, P_C, N_C))


def _sample_layer(x, layer, lw, p, rel_bias, cache_win_k, cache_win_v, state_conv, state_ssm, cache_mem_k, cache_mem_v):
    bsz = x.shape[0]
    xb = x.astype(BF16)
    proj = _matmul(xb, lw['w_in'], bsz, 512)
    proj3 = proj.reshape(bsz, 1, IN_DIM_PAD)
    a_out, win_k, win_v = _swa_decode_call(proj3, cache_win_k, cache_win_v, layer, rel_bias, p['sinks'])
    b_out, gv = _gmlp_step_call(proj, p)
    c_out, ssm, conv_x, conv_bc = _ssd_step_call(proj3, state_conv, state_ssm, layer, p)
    mixed = _gate_merge(xb, a_out, b_out, c_out.reshape(bsz, BR_W),
                        lw['w_gate'], lw['b_gate'], lw['w_branch'], bsz, 512)
    x1 = _mm_res_ln(mixed, lw['w_o'], x, p['ln1_g'], p['ln1_b'], bsz)
    q = _matmul(x1.astype(BF16), lw['w_xq'], bsz, XH * XHD)
    o = _xattn_decode_call(q, cache_mem_k, cache_mem_v, layer)
    x2 = _mm_res_ln(o.astype(BF16), lw['w_xo'], x1, p['ln2_g'], p['ln2_b'], bsz)
    logits = _router(x2, lw['w_router'], lw['b_router'], bsz)
    states = (win_k.reshape(bsz, WINDOW, KV_A, HD_A), win_v.reshape(bsz, WINDOW, KV_A, HD_A),
              jnp.concatenate([conv_x, conv_bc], axis=-1), ssm.reshape(bsz, H_C, P_C, N_C),
              gv.reshape(bsz, 1, W_B))
    return x2, logits, states


def kernel(x_prompt, x_sample, mem_prompt, cache_win_k, cache_win_v, state_conv, state_ssm, cache_mem_k, cache_mem_v, w_in, rel_bias, sinks, gmlp_ln_g, gmlp_ln_b, gmlp_ws, gmlp_bs, conv_w, conv_b, dt_bias, a_log, d_skip, ssm_norm_g, w_branch, w_gate, b_gate, w_o, ln1_g, ln1_b, w_xq, w_xk, w_xv, w_xo, ln2_g, ln2_b, w_router, b_router, w_e1, b_e1, w_e2, b_e2, ln3_g, ln3_b):
    assert cache_win_k.shape[2] == WINDOW and x_sample.shape[1] == 1
    n_prompt, n_mem = mem_prompt.shape[0], mem_prompt.shape[1]
    bp, lp = x_prompt.shape[:2]
    bs_ = x_sample.shape[0]
    hp, hs = x_prompt.reshape(bp * lp, D_MODEL), x_sample.reshape(bs_, D_MODEL)
    wk_p, wv_p, cv_p, ssm_p, mk_ps, mv_ps = [], [], [], [], [], []
    wk_s, wv_s, cv_s, ssm_s, gv_s = [], [], [], [], []
    mem_b = mem_prompt.reshape(n_prompt * n_mem, D_MODEL).astype(BF16)
    swa_bias = _swa_prompt_bias(rel_bias)
    experts = (w_e1, b_e1.reshape(DEPTH, N_EXPERTS, 1, 2 * D_FF), w_e2, b_e2.reshape(DEPTH, N_EXPERTS, 1, D_MODEL))
    for l in range(DEPTH):
        p = dict(sinks=sinks[l], gmlp_ln_g=gmlp_ln_g[l], gmlp_ln_b=gmlp_ln_b[l],
                 gmlp_ws=gmlp_ws[l], gmlp_bs=gmlp_bs[l], conv_w=conv_w[l], conv_b=conv_b[l],
                 dt_bias=dt_bias[l], a_log=a_log[l], d_skip=d_skip[l], ssm_norm_g=ssm_norm_g[l],
                 ln1_g=ln1_g[l], ln1_b=ln1_b[l], ln2_g=ln2_g[l], ln2_b=ln2_b[l])
        wi = w_in[l]
        seg = np.cumsum([0] + IN_SIZES)
        part = lambda n: wi[:, seg[n]:seg[n + 1]]
        w_in_cols = jnp.concatenate(
            [part(0), part(3), part(4), part(5), part(6), part(1), part(2), part(7),
             jnp.zeros((D_MODEL, IN_DIM_PAD - IN_DIM), F32)], axis=1).astype(BF16)
        w_router_pad = jnp.pad(w_router[l], ((0, 0), (0, LANES - N_EXPERTS)))
        lw = dict(
            w_in=w_in_cols,
            w_gate=w_gate[l].astype(BF16),
            b_gate=b_gate[l].reshape(1, -1),
            w_branch=w_branch[l].astype(BF16),
            w_o=w_o[l].astype(BF16),
            w_xq=w_xq[l].astype(BF16),
            w_xo=w_xo[l].astype(BF16),
            w_router=w_router_pad,
            w_router_bf16=w_router_pad.astype(BF16),
            b_router=jnp.pad(b_router[l], (0, LANES - N_EXPERTS)).reshape(1, -1),
        )
        w_kv = jnp.concatenate([w_xk[l], w_xv[l]], axis=1).astype(BF16)
        mkv = _matmul(mem_b, w_kv, n_mem, XH * XHD)
        mk = mkv[:, :XH * XHD].reshape(n_prompt, n_mem, XH, XHD)
        mv = mkv[:, XH * XHD:].reshape(n_prompt, n_mem, XH, XHD)
        x2_p, lg_p, st_p = _prompt_layer(hp, mkv, lw, p, swa_bias, bp, lp)
        x2_s, lg_s, st_s = _sample_layer(hs, l, lw, p, rel_bias, cache_win_k, cache_win_v, state_conv, state_ssm,
                                         cache_mem_k, cache_mem_v)
        hp, hs = _moe_ln(x2_p, x2_s, lg_p, lg_s, experts, l, ln3_g[l], ln3_b[l])
        wk_p.append(st_p[0]); wv_p.append(st_p[1]); cv_p.append(st_p[2]); ssm_p.append(st_p[3])
        mk_ps.append(mk); mv_ps.append(mv)
        wk_s.append(st_s[0]); wv_s.append(st_s[1]); cv_s.append(st_s[2]); ssm_s.append(st_s[3])
        gv_s.append(st_s[4])
    hp = hp.reshape(bp, lp, D_MODEL)
    hs = hs.reshape(bs_, 1, D_MODEL)
    return (hp, hs,
            jnp.stack(wk_p), jnp.stack(wv_p), jnp.stack(cv_p), jnp.stack(ssm_p),
            jnp.stack(mk_ps), jnp.stack(mv_ps),
            jnp.stack(wk_s), jnp.stack(wv_s), jnp.stack(cv_s), jnp.stack(ssm_s), jnp.stack(gv_s))
```

```python
import functools
import math

import numpy as np
import jax
import jax.numpy as jnp
from jax import lax
from jax.experimental import pallas as pl
from jax.experimental.pallas import tpu as pltpu

D_MODEL = 2048
DEPTH = 2
PAST_LEN = 16384
WINDOW = 128
H_A = 16
KV_A = 2
HD_A = 64
G_A = H_A // KV_A
N_BUCKETS = 32
MAX_DIST = 128
CHUNK_B = 128
GB = 16
CG_B = 64
W_B = GB * CG_B
D_INNER = 1024
P_C = 64
H_C = D_INNER // P_C
G_C = 2
R_C = H_C // G_C
N_C = 128
CONV_W = 4
CONV_DIM = D_INNER + 2 * G_C * N_C
SSD_CHUNK = 128
N_BRANCH = 3
BR_W = 1024
N_MEM = 256
XH = 4
XHD = 128
N_EXPERTS = 32
TOP_K = 4
D_FF = D_MODEL
SWIGLU_ALPHA = 1.702
SWIGLU_LIMIT = 7.0
DN_ALPHA = (2 * DEPTH) ** 0.25
LN_EPS = 1e-5
RMS_EPS = 1e-5

IN_SIZES = [H_A * HD_A, KV_A * HD_A, KV_A * HD_A, W_B, W_B, D_INNER, CONV_DIM, H_C]
IN_DIM = sum(IN_SIZES)
IN_DIM_PAD = 6144
COL_Q, COL_U, COL_GV, COL_Z, COL_XBC = 0, 1024, 2048, 3072, 4096
COL_K = COL_XBC + CONV_DIM
COL_V = COL_K + KV_A * HD_A
COL_DT = COL_V + KV_A * HD_A
LANES = 128

VMEM_LIMIT = 56 * 1024 * 1024
MOE_BM = 576
MOE_FC = 256
MOE_COL_TILE = 256

F32 = jnp.float32
BF16 = jnp.bfloat16


def _cparams(*sem):
    return pltpu.CompilerParams(dimension_semantics=sem, vmem_limit_bytes=VMEM_LIMIT)


def _mm_kernel(x_ref, w_ref, o_ref):
    o_ref[...] = jnp.dot(x_ref[...], w_ref[...], preferred_element_type=F32).astype(o_ref.dtype)


def _matmul(x, w, tm, tn, out_dtype=F32):
    M, K = x.shape
    N = w.shape[1]
    assert M % tm == 0 and N % tn == 0
    return pl.pallas_call(
        _mm_kernel,
        grid=(N // tn, M // tm),
        in_specs=[pl.BlockSpec((tm, K), lambda j, i: (i, 0)),
                  pl.BlockSpec((K, tn), lambda j, i: (0, j))],
        out_specs=pl.BlockSpec((tm, tn), lambda j, i: (i, j)),
        out_shape=jax.ShapeDtypeStruct((M, N), out_dtype),
        compiler_params=_cparams("parallel", "parallel"),
        name="dense_matmul",
    )(x, w)


def _gate_merge_kernel(x_ref, a_ref, b_ref, c_ref, wg0_ref, wg1_ref, wg2_ref,
                       bg0_ref, bg1_ref, bg2_ref, wp_ref, o_ref):
    x = x_ref[...]
    acc = None
    for k, (br_ref, wg_ref, bg_ref) in enumerate(
            ((a_ref, wg0_ref, bg0_ref), (b_ref, wg1_ref, bg1_ref), (c_ref, wg2_ref, bg2_ref))):
        z = jnp.dot(x, wg_ref[...], preferred_element_type=F32) + bg_ref[...]
        gate = 1.0 / (1.0 + jnp.exp(-z))
        proj = jnp.dot(br_ref[...], wp_ref[k], preferred_element_type=F32)
        acc = gate * proj if acc is None else acc + gate * proj
    o_ref[...] = acc.astype(o_ref.dtype)


def _gate_merge(x, a, b, c, w_gate, b_gate, w_branch, tm, tn):
    M = x.shape[0]
    nt = D_MODEL // tn
    row = lambda j, i: (i, 0)
    in_specs = [pl.BlockSpec((tm, D_MODEL), row)] + [pl.BlockSpec((tm, BR_W), row)] * 3
    in_specs += [pl.BlockSpec((D_MODEL, tn), functools.partial(lambda j, i, k: (0, k * nt + j), k=k))
                 for k in range(N_BRANCH)]
    in_specs += [pl.BlockSpec((1, tn), functools.partial(lambda j, i, k: (0, k * nt + j), k=k))
                 for k in range(N_BRANCH)]
    in_specs += [pl.BlockSpec((N_BRANCH, BR_W, tn), lambda j, i: (0, 0, j))]
    return pl.pallas_call(
        _gate_merge_kernel,
        grid=(nt, M // tm),
        in_specs=in_specs,
        out_specs=pl.BlockSpec((tm, tn), lambda j, i: (i, j)),
        out_shape=jax.ShapeDtypeStruct((M, D_MODEL), BF16),
        compiler_params=_cparams("parallel", "parallel"),
        name="gate_merge",
    )(x, a, b, c, w_gate, w_gate, w_gate, b_gate, b_gate, b_gate, w_branch)


def _layer_norm_rows(y, g, b):
    mu = jnp.mean(y, axis=-1, keepdims=True)
    yc = y - mu
    var = jnp.mean(yc * yc, axis=-1, keepdims=True)
    return yc * lax.rsqrt(var + LN_EPS) * g + b


def _mm_res_ln_kernel(a_ref, w_ref, res_ref, g_ref, b_ref, o_ref):
    y = jnp.dot(a_ref[...], w_ref[...], preferred_element_type=F32) + DN_ALPHA * res_ref[...]
    o_ref[...] = _layer_norm_rows(y, g_ref[...], b_ref[...])


def _mm_res_ln(a, w, res, g, b, tm):
    M, K = a.shape
    row = lambda i: (i, 0)
    fixed = lambda i: (0, 0)
    return pl.pallas_call(
        _mm_res_ln_kernel,
        grid=(M // tm,),
        in_specs=[pl.BlockSpec((tm, K), row), pl.BlockSpec((K, D_MODEL), fixed),
                  pl.BlockSpec((tm, D_MODEL), row), pl.BlockSpec((1, D_MODEL), fixed),
                  pl.BlockSpec((1, D_MODEL), fixed)],
        out_specs=pl.BlockSpec((tm, D_MODEL), row),
        out_shape=jax.ShapeDtypeStruct((M, D_MODEL), F32),
        compiler_params=_cparams("parallel"),
        name="matmul_residual_layernorm",
    )(a, w, res, g.reshape(1, -1), b.reshape(1, -1))


def _top_k_lanes(logits):
    lane_i = lax.broadcasted_iota(jnp.int32, logits.shape, 1)
    lane_f = lane_i.astype(F32)
    work = jnp.where(lane_i < N_EXPERTS, logits, -jnp.inf)
    out = jnp.zeros_like(logits)
    for k in range(TOP_K):
        m = jnp.max(work, axis=-1, keepdims=True)
        idx = jnp.min(jnp.where(work == m, lane_f, float(LANES)), axis=-1, keepdims=True)
        out = jnp.where(lane_i == k, m, out)
        out = jnp.where(lane_i == TOP_K + k, idx, out)
        work = jnp.where(lane_f == idx, -jnp.inf, work)
    return out


def _router_kernel(x_ref, w_ref, b_ref, o_ref):
    acc = jnp.dot(x_ref[...].astype(BF16), w_ref[...].astype(BF16), preferred_element_type=F32)
    o_ref[...] = _top_k_lanes(acc + b_ref[...])


def _router(x, w_pad, b_pad, tm):
    M = x.shape[0]
    NP = w_pad.shape[1]
    return pl.pallas_call(
        _router_kernel,
        grid=(M // tm,),
        in_specs=[pl.BlockSpec((tm, D_MODEL), lambda i: (i, 0)),
                  pl.BlockSpec((D_MODEL, NP), lambda i: (0, 0)),
                  pl.BlockSpec((1, NP), lambda i: (0, 0))],
        out_specs=pl.BlockSpec((tm, NP), lambda i: (i, 0)),
        out_shape=jax.ShapeDtypeStruct((M, NP), F32),
        compiler_params=_cparams("parallel"),
        name="router_logits",
    )(x, w_pad, b_pad)


def _bf16_row_interleave(a, b):
    a32 = lax.bitcast_convert_type(a.astype(BF16).astype(F32), jnp.uint32)
    b32 = lax.bitcast_convert_type(b.astype(BF16).astype(F32), jnp.uint32)
    word = (a32 >> 16) | (b32 & jnp.uint32(0xFFFF0000))
    return pltpu.bitcast(word, BF16)


def _moe_kernel(be_ref, bx_ref, bv_ref, x_ref, w1_ref, b1_ref, w2_ref, b2_ref, o_ref):
    i = pl.program_id(0)
    f = pl.program_id(1)
    fc = w2_ref.shape[1] // 2
    bm = x_ref.shape[0]
    valid = bv_ref[i] == 1

    @pl.when(valid)
    def _():
        @pl.when(f == 0)
        def _():
            o_ref[...] = jnp.broadcast_to(b2_ref[0], o_ref.shape)

        x = x_ref[...].astype(BF16)
        tile = min(MOE_COL_TILE, 2 * fc)
        h = [jnp.dot(x, w1_ref[0, :, tile * n:tile * (n + 1)].astype(BF16), preferred_element_type=F32)
             + b1_ref[0, :, tile * n:tile * (n + 1)] for n in range(4 * fc // tile)]
        w2 = w2_ref[0]
        w2q = _bf16_row_interleave(w2[:fc], w2[fc:])
        even = (lax.broadcasted_iota(jnp.int32, (bm, LANES), 1) & 1) == 0
        per_tile = tile // LANES
        acts = []
        for c in range(2 * fc // LANES):
            sl = slice(LANES * (c % per_tile), LANES * (c % per_tile + 1))
            a = h[c // per_tile][:, sl]
            b = h[(2 * fc // LANES + c) // per_tile][:, sl]
            glu = jnp.where(even, a, pltpu.roll(b, 1, axis=1))
            lin = jnp.where(even, pltpu.roll(a, LANES - 1, axis=1), b)
            glu = jnp.minimum(glu, SWIGLU_LIMIT)
            lin = jnp.clip(lin, -SWIGLU_LIMIT, SWIGLU_LIMIT)
            act = glu * (1.0 / (1.0 + jnp.exp(-SWIGLU_ALPHA * glu))) * (lin + 1.0)
            acts.append(act.astype(BF16))
        act = jnp.concatenate(acts, axis=1) if len(acts) > 1 else acts[0]
        o_ref[...] += jnp.dot(act, w2q, preferred_element_type=F32)

    @pl.when(jnp.logical_not(valid) & (f == 0))
    def _():
        o_ref[...] = jnp.zeros_like(o_ref)


def _moe_blocks(p_end, n_valid, nb, bm):
    i32 = jnp.int32
    n_exp = p_end.shape[0]
    blk = jnp.arange(nb, dtype=i32)
    block_v = (blk < n_valid).astype(i32)
    block_x = jnp.minimum(blk, n_valid - 1)
    block_e = jnp.minimum(jnp.sum(p_end[None, :] <= (block_x * bm)[:, None], axis=1, dtype=i32), n_exp - 1)
    return block_e, block_x, block_v


def _moe_ffn_blocks(rows, block_e, block_x, block_v, w1, b1, w2, b2, layer, bm, fc):
    d_model = rows.shape[1]
    d_ff = w2.shape[2]
    nb = rows.shape[0] // bm
    nf = d_ff // (2 * fc)
    last_f = nf - 1

    def fsel(f, bv, i):
        return jnp.where(bv[i] == 1, f, last_f)

    grid_spec = pltpu.PrefetchScalarGridSpec(
        num_scalar_prefetch=3,
        grid=(nb, nf),
        in_specs=[
            pl.BlockSpec((bm, d_model), lambda i, f, be, bx, bv: (bx[i], 0)),
            pl.BlockSpec((None, 1, d_model, 4 * fc), lambda i, f, be, bx, bv: (layer, be[i], 0, fsel(f, bv, i))),
            pl.BlockSpec((None, 1, 1, 4 * fc), lambda i, f, be, bx, bv: (layer, be[i], 0, fsel(f, bv, i))),
            pl.BlockSpec((None, 1, 2 * fc, d_model), lambda i, f, be, bx, bv: (layer, be[i], fsel(f, bv, i), 0)),
            pl.BlockSpec((None, 1, 1, d_model), lambda i, f, be, bx, bv: (layer, be[i], 0, 0)),
        ],
        out_specs=pl.BlockSpec((bm, d_model), lambda i, f, be, bx, bv: (i, 0)),
    )
    return pl.pallas_call(
        _moe_kernel,
        grid_spec=grid_spec,
        out_shape=jax.ShapeDtypeStruct(rows.shape, F32),
        compiler_params=_cparams("arbitrary", "arbitrary"),
        name="moe_expert_ffn",
    )(block_e, block_x, block_v, rows, w1, b1, w2, b2)


def _combine_ln_kernel(*refs):
    g_refs = refs[:TOP_K]
    gate_ref, res_ref, lg_ref, lb_ref, o_ref = refs[TOP_K:]
    gate = gate_ref[...]
    ff = None
    for k in range(TOP_K):
        term = gate[:, k:k + 1] * g_refs[k][...]
        ff = term if ff is None else ff + term
    o_ref[...] = _layer_norm_rows(DN_ALPHA * res_ref[...] + ff, lg_ref[...], lb_ref[...])


def _combine_ln(gathered, gate, res, g, b, tm):
    T, d = res.shape
    nt = T // tm
    row = lambda i: (i, 0)
    fixed = lambda i: (0, 0)
    g_specs = [pl.BlockSpec((tm, d), functools.partial(lambda i, k: (k * nt + i, 0), k=k)) for k in range(TOP_K)]
    return pl.pallas_call(
        _combine_ln_kernel,
        grid=(nt,),
        in_specs=g_specs + [pl.BlockSpec((tm, TOP_K), row), pl.BlockSpec((tm, d), row),
                            pl.BlockSpec((1, d), fixed), pl.BlockSpec((1, d), fixed)],
        out_specs=pl.BlockSpec((tm, d), row),
        out_shape=jax.ShapeDtypeStruct((T, d), F32),
        compiler_params=_cparams("parallel"),
        name="moe_combine_layernorm",
    )(*([gathered] * TOP_K), gate, res, g.reshape(1, -1), b.reshape(1, -1))


RANK_TOKENS = 128


def _rank_kernel(route_ref, rank_ref, count_ref, carry_ref):
    i = pl.program_id(0)
    n = route_ref.shape[0]

    @pl.when(i == 0)
    def _():
        carry_ref[...] = jnp.zeros_like(carry_ref)

    route = route_ref[...]
    lane = lax.broadcasted_iota(jnp.int32, (n, LANES), 1)
    lane_f = lane.astype(F32)
    hits = [lane_f == route[:, TOP_K + k:TOP_K + k + 1] for k in range(TOP_K)]
    per_token = jnp.zeros((n, LANES), F32)
    for k in range(TOP_K):
        per_token = per_token + jnp.where(hits[k], 1.0, 0.0)
    earlier = lax.broadcasted_iota(jnp.int32, (n, n), 0) > lax.broadcasted_iota(jnp.int32, (n, n), 1)
    before = jnp.dot(jnp.where(earlier, 1.0, 0.0).astype(BF16), per_token.astype(BF16),
                     preferred_element_type=F32) + carry_ref[...]
    out = jnp.zeros((n, LANES), F32)
    for k in range(TOP_K):
        r = jnp.sum(jnp.where(hits[k], before, 0.0), axis=-1, keepdims=True)
        out = jnp.where(lane == k, r, out)
    rank_ref[...] = out
    carry_ref[...] += jnp.sum(per_token, axis=0, keepdims=True)
    count_ref[...] = carry_ref[...]


def _rank_assignments(route):
    n = route.shape[0]
    return pl.pallas_call(
        _rank_kernel,
        grid=(n // RANK_TOKENS,),
        in_specs=[pl.BlockSpec((RANK_TOKENS, LANES), lambda i: (i, 0))],
        out_specs=[pl.BlockSpec((RANK_TOKENS, LANES), lambda i: (i, 0)), pl.BlockSpec((1, LANES), lambda i: (0, 0))],
        out_shape=[jax.ShapeDtypeStruct((n, LANES), F32), jax.ShapeDtypeStruct((1, LANES), F32)],
        scratch_shapes=[pltpu.VMEM((1, LANES), F32)],
        compiler_params=_cparams("arbitrary"),
        name="moe_rank_assignments",
    )(route)


def _moe_route(route, n_tok, n_exp, bm):
    i32 = jnp.int32
    ranks, counts = _rank_assignments(route)
    top_v = route[:n_tok, :TOP_K]
    top_i = route[:n_tok, TOP_K:2 * TOP_K].astype(i32)
    rank = ranks[:n_tok, :TOP_K].astype(i32)
    gate = jax.nn.softmax(top_v, axis=-1)
    n_assign = n_tok * TOP_K
    nb = -(-(n_assign + n_exp * (bm - 1)) // bm)
    counts = counts[0, :n_exp].astype(i32)
    padded = (counts + bm - 1) // bm * bm
    p_end = jnp.cumsum(padded)
    p_start = (p_end - padded).astype(i32)
    is_e = top_i[:, :, None] == jnp.arange(n_exp, dtype=i32)[None, None, :]
    pos = rank + jnp.sum(jnp.where(is_e, p_start[None, None, :], 0), axis=-1, dtype=i32)
    tok = jnp.broadcast_to(jnp.arange(n_tok, dtype=i32)[:, None], (n_tok, TOP_K))
    src = jnp.zeros((nb * bm,), i32).at[pos.reshape(-1)].set(tok.reshape(-1), unique_indices=True)
    n_valid = (p_end[-1] // bm).astype(i32)
    return gate, src, pos, (padded.astype(i32), p_end.astype(i32), n_valid, nb)


def _moe_ln(x2_p, x2_s, route_p, route_s, experts, layer, ln_g, ln_b):
    tp, ts = x2_p.shape[0], x2_s.shape[0]
    x2 = jnp.concatenate([x2_p, x2_s], axis=0)
    n_pad = -(tp + ts) % RANK_TOKENS
    route = jnp.concatenate([route_p, route_s, jnp.full((n_pad, LANES), -1.0, F32)], axis=0)
    gate, src, pos, (padded, p_end, n_valid, nb) = _moe_route(route, tp + ts, N_EXPERTS, MOE_BM)
    rows = jnp.take(x2, src, axis=0, mode='clip')
    w_e1, b_e1, w_e2, b_e2 = experts
    block_e, block_x, block_v = _moe_blocks(p_end, n_valid, nb, MOE_BM)
    out_rows = _moe_ffn_blocks(rows, block_e, block_x, block_v, w_e1, b_e1, w_e2, b_e2, layer, MOE_BM, MOE_FC)
    pos = pos.reshape(tp + ts, TOP_K)
    g_p = jnp.take(out_rows, pos[:tp].T.reshape(-1), axis=0, mode='clip')
    g_s = jnp.take(out_rows, pos[tp:].T.reshape(-1), axis=0, mode='clip')
    hp = _combine_ln(g_p, gate[:tp], x2_p, ln_g, ln_b, 256)
    hs = _combine_ln(g_s, gate[tp:], x2_s, ln_g, ln_b, ts)
    return hp, hs


def _half_lane_variants(t):
    lo = lax.broadcasted_iota(jnp.int32, t.shape, 1) < HD_A
    zero = jnp.zeros_like(t)
    tr = pltpu.roll(t, HD_A, axis=1)
    return [[jnp.where(lo, t, zero).astype(BF16), jnp.where(lo, zero, tr).astype(BF16)],
            [jnp.where(lo, tr, zero).astype(BF16), jnp.where(lo, zero, t).astype(BF16)]]


SWA_QBLOCKS = 2


def _swa_kernel(sinks_ref, q_ref, kp_ref, kc_ref, vp_ref, vc_ref, bias_ref, o_ref):
    j = pl.program_id(1)
    W = WINDOW
    k_all = jnp.concatenate([kp_ref[...], kc_ref[...]], axis=0)
    v_all = jnp.concatenate([vp_ref[...], vc_ref[...]], axis=0)
    col = lax.broadcasted_iota(jnp.int32, (W, 2 * W), 1)
    no_prev = col < jnp.where(j == 0, W, 0)
    for sub in range(SWA_QBLOCKS):
        kvar = _half_lane_variants(k_all[W * sub:W * (sub + 2)])
        vvar = _half_lane_variants(v_all[W * sub:W * (sub + 2)])
        rows = slice(W * sub, W * (sub + 1))
        scores = []
        for h in range(H_A):
            qp = q_ref[rows, LANES * (h // 2):LANES * (h // 2 + 1)].astype(BF16)
            scores.append(lax.dot_general(qp, kvar[h // G_A][h % 2], (((1,), (1,)), ((), ())),
                                          preferred_element_type=F32))
        probs = []
        for h in range(H_A):
            s = scores[h] * HD_A ** -0.5 + bias_ref[h]
            if sub == 0:
                s = jnp.where(no_prev, -jnp.inf, s)
            sk = sinks_ref[h]
            m = jnp.maximum(jnp.max(s, axis=-1, keepdims=True), sk)
            pr = jnp.exp(s - m)
            den = jnp.sum(pr, axis=-1, keepdims=True) + jnp.exp(sk - m)
            probs.append((pr / den).astype(BF16))
        for r in range(H_A // 2):
            g = (2 * r) // G_A
            acc = jnp.dot(probs[2 * r], vvar[g][0], preferred_element_type=F32)
            acc = acc + jnp.dot(probs[2 * r + 1], vvar[g][1], preferred_element_type=F32)
            o_ref[rows, LANES * r:LANES * (r + 1)] = acc.astype(o_ref.dtype)


def _swa_prompt_bias(rel_bias):
    i = jnp.arange(WINDOW)[:, None]
    j = jnp.arange(2 * WINDOW)[None, :]
    dist = i + WINDOW - j
    return jnp.where((dist >= 0) & (dist < WINDOW), _bucket_bias(rel_bias, dist), -jnp.inf)


def _swa_prompt_call(proj2d, bias, sinks, bsz, L):
    nq = SWA_QBLOCKS
    nb = L // (nq * WINDOW)
    kcol, vcol = COL_K // LANES, COL_V // LANES
    cur = lambda b, j: b * nb + j
    prev = lambda b, j: (b * nb + j) * nq - jnp.minimum(j, 1)
    return pl.pallas_call(
        _swa_kernel,
        grid=(bsz, nb),
        in_specs=[
            pl.BlockSpec(memory_space=pltpu.SMEM),
            pl.BlockSpec((nq * WINDOW, H_A * HD_A), lambda b, j: (cur(b, j), COL_Q // (H_A * HD_A))),
            pl.BlockSpec((WINDOW, LANES), lambda b, j: (prev(b, j), kcol)),
            pl.BlockSpec((nq * WINDOW, LANES), lambda b, j: (cur(b, j), kcol)),
            pl.BlockSpec((WINDOW, LANES), lambda b, j: (prev(b, j), vcol)),
            pl.BlockSpec((nq * WINDOW, LANES), lambda b, j: (cur(b, j), vcol)),
            pl.BlockSpec((H_A, WINDOW, 2 * WINDOW), lambda b, j: (0, 0, 0)),
        ],
        out_specs=pl.BlockSpec((nq * WINDOW, H_A * HD_A), lambda b, j: (cur(b, j), 0)),
        out_shape=jax.ShapeDtypeStruct((bsz * L, H_A * HD_A), BF16),
        compiler_params=_cparams("parallel", "arbitrary"),
        name="swa_prompt",
    )(sinks, proj2d, proj2d, proj2d, proj2d, proj2d, bias)


def _gelu(x):
    return 0.5 * x * (1.0 + lax.erf(x * np.float32(np.sqrt(0.5))))


def _gmlp_kernel(u_ref, gv_ref, lng_ref, lnb_ref, w_ref, bias_ref, o_ref):
    u = _gelu(u_ref[...])
    gv = _layer_norm_rows(_gelu(gv_ref[...]), lng_ref[...], lnb_ref[...])
    lo = lax.broadcasted_iota(jnp.int32, (CHUNK_B, LANES), 1) < CG_B
    for r in range(GB // 2):
        vp = gv[:, LANES * r:LANES * (r + 1)]
        zero = jnp.zeros_like(vp)
        mix = jnp.dot(w_ref[2 * r], jnp.where(lo, vp, zero).astype(BF16), preferred_element_type=F32)
        mix += jnp.dot(w_ref[2 * r + 1], jnp.where(lo, zero, vp).astype(BF16), preferred_element_type=F32)
        sl = slice(LANES * r, LANES * (r + 1))
        o_ref[:, sl] = (u[:, sl] * (mix + bias_ref[:, sl])).astype(o_ref.dtype)


def _gmlp_prompt_call(proj2d, p, n_rows):
    w = (p['gmlp_ws'] * jnp.tril(jnp.ones((CHUNK_B, CHUNK_B), F32))).astype(BF16)
    bias = jnp.repeat(p['gmlp_bs'].T, CG_B, axis=1)
    fixed2 = lambda i: (0, 0)
    return pl.pallas_call(
        _gmlp_kernel,
        grid=(n_rows // CHUNK_B,),
        in_specs=[
            pl.BlockSpec((CHUNK_B, W_B), lambda i: (i, COL_U // W_B)),
            pl.BlockSpec((CHUNK_B, W_B), lambda i: (i, COL_GV // W_B)),
            pl.BlockSpec((1, W_B), fixed2), pl.BlockSpec((1, W_B), fixed2),
            pl.BlockSpec((GB, CHUNK_B, CHUNK_B), lambda i: (0, 0, 0)),
            pl.BlockSpec((CHUNK_B, W_B), fixed2),
        ],
        out_specs=pl.BlockSpec((CHUNK_B, W_B), lambda i: (i, 0)),
        out_shape=jax.ShapeDtypeStruct((n_rows, W_B), BF16),
        compiler_params=_cparams("parallel"),
        name="gmlp_prompt",
    )(proj2d, proj2d, p['gmlp_ln_g'].reshape(1, -1), p['gmlp_ln_b'].reshape(1, -1), w, bias)


def _silu(x):
    return x * (1.0 / (1.0 + jnp.exp(-x)))


def _softplus(x):
    return jnp.maximum(x, 0.0) + jnp.log1p(jnp.exp(-jnp.abs(x)))


def _causal_conv_chunk(cur, tail, w, bias):
    rows = lax.broadcasted_iota(jnp.int32, (8, cur.shape[1]), 0)
    y = jnp.broadcast_to(bias, cur.shape)
    y_head = jnp.broadcast_to(bias, (8, cur.shape[1]))
    for t in range(CONV_W):
        k = CONV_W - 1 - t
        wt = w[t:t + 1, :]
        if k == 0:
            y = y + cur * wt
            y_head = y_head + cur[:8] * wt
        else:
            sh = pltpu.roll(cur, k, axis=0)
            y = y + sh * wt
            y_head = y_head + jnp.where(rows < k, pltpu.roll(tail, k, axis=0), sh[:8]) * wt
    return jnp.concatenate([y_head, y[8:]], axis=0)


def _bf16_split3(x):
    p1 = x.astype(BF16)
    r1 = x - p1.astype(F32)
    p2 = r1.astype(BF16)
    p3 = (r1 - p2.astype(F32)).astype(BF16)
    return p1, p2, p3


def _ssd_kernel(xs_ref, bc_ref, z_ref, dt_ref, cwx_ref, cbx_ref, cwb_ref, cbb_ref, dtb_ref, a_ref,
                dsk_ref, ng_ref, y_ref, h_ref, state_ref, tailx_ref, tailb_ref):
    c = pl.program_id(1)
    C = SSD_CHUNK

    @pl.when(c == 0)
    def _():
        state_ref[...] = jnp.zeros_like(state_ref)
        tailx_ref[...] = jnp.zeros_like(tailx_ref)
        tailb_ref[...] = jnp.zeros_like(tailb_ref)

    xs_raw = xs_ref[...]
    bc_raw = bc_ref[...]
    xs = _silu(_causal_conv_chunk(xs_raw, tailx_ref[...], cwx_ref[...], cbx_ref[...]))
    bc = _silu(_causal_conv_chunk(bc_raw, tailb_ref[...], cwb_ref[...], cbb_ref[...]))
    tailx_ref[...] = xs_raw[C - 8:]
    tailb_ref[...] = bc_raw[C - 8:]

    dt = _softplus(dt_ref[...] + dtb_ref[...])
    da = dt * a_ref[...]
    row_i = lax.broadcasted_iota(jnp.int32, (C, C), 0)
    col_i = lax.broadcasted_iota(jnp.int32, (C, C), 1)
    causal = row_i >= col_i
    tril = jnp.where(causal, 1.0, 0.0).astype(BF16)
    acs = None
    for piece in _bf16_split3(da):
        t = jnp.dot(tril, piece, preferred_element_type=F32)
        acs = t if acs is None else acs + t
    acs_t = acs.T
    exp_acs = jnp.exp(acs)
    end_decay = jnp.exp(acs[C - 1:C, :] - acs)
    chunk_decay = jnp.exp(acs[C - 1:C, :])

    lo = lax.broadcasted_iota(jnp.int32, (C, LANES), 1) < P_C
    bm = [bc[:, N_C * g:N_C * (g + 1)].astype(BF16) for g in range(G_C)]
    cm = [bc[:, N_C * (G_C + g):N_C * (G_C + g + 1)].astype(BF16) for g in range(G_C)]
    cb = [lax.dot_general(cm[g], bm[g], (((1,), (1,)), ((), ())), preferred_element_type=F32)
          for g in range(G_C)]

    def per_lane_half(t, r):
        return jnp.where(lo, t[:, 2 * r:2 * r + 1], t[:, 2 * r + 1:2 * r + 2])

    ys = []
    for r in range(H_C // 2):
        g = (2 * r) // R_C
        sl = slice(LANES * r, LANES * (r + 1))
        x_pair = xs[:, sl]
        xdt = x_pair * per_lane_half(dt, r)
        zero = jnp.zeros_like(xdt)
        y_pair = None
        for par in range(2):
            h = 2 * r + par
            seg = acs[:, h:h + 1] - acs_t[h:h + 1, :]
            decay = jnp.where(causal, jnp.exp(seg), 0.0)
            m_h = (cb[g] * decay).astype(BF16)
            x_h = (jnp.where(lo, xdt, zero) if par == 0 else jnp.where(lo, zero, xdt)).astype(BF16)
            t = jnp.dot(m_h, x_h, preferred_element_type=F32)
            y_pair = t if y_pair is None else y_pair + t
        st = state_ref[sl, :]
        y_off = lax.dot_general(cm[g], st.astype(BF16), (((1,), (1,)), ((), ())), preferred_element_type=F32)
        y_pair = y_pair + y_off * per_lane_half(exp_acs, r)
        upd = lax.dot_general((xdt * per_lane_half(end_decay, r)).astype(BF16), bm[g],
                              (((0,), (0,)), ((), ())), preferred_element_type=F32)
        cd = jnp.concatenate([jnp.broadcast_to(chunk_decay[:, 2 * r:2 * r + 1], (P_C, N_C)),
                              jnp.broadcast_to(chunk_decay[:, 2 * r + 1:2 * r + 2], (P_C, N_C))], axis=0)
        state_ref[sl, :] = st * cd + upd
        ys.append(y_pair + dsk_ref[:, sl] * x_pair)
    y = jnp.concatenate(ys, axis=1) * _silu(z_ref[...])
    gw = D_INNER // G_C
    outs = []
    for g in range(G_C):
        yg = y[:, gw * g:gw * (g + 1)]
        outs.append(yg * lax.rsqrt(jnp.mean(yg * yg, axis=-1, keepdims=True) + RMS_EPS))
    y_ref[...] = (jnp.concatenate(outs, axis=1) * ng_ref[...]).astype(y_ref.dtype)

    @pl.when(c == pl.num_programs(1) - 1)
    def _():
        h_ref[...] = state_ref[...]


def _ssd_prompt_call(proj2d, p, bsz, L):
    nc = L // SSD_CHUNK
    row = lambda blk: (lambda b, c: (b * nc + c, blk))
    fixed = lambda b, c: (0, 0)
    pad_l = lambda v: jnp.pad(v.astype(F32), (0, LANES - H_C)).reshape(1, LANES)
    cw, cbias = p['conv_w'], p['conv_b'].reshape(1, -1)
    nbc = 2 * G_C * N_C
    args = (proj2d, proj2d, proj2d, proj2d,
            cw[:, :D_INNER], cbias[:, :D_INNER], cw[:, D_INNER:], cbias[:, D_INNER:],
            pad_l(p['dt_bias']), pad_l(-jnp.exp(p['a_log'].astype(F32))),
            jnp.repeat(p['d_skip'].astype(F32), P_C).reshape(1, -1), p['ssm_norm_g'].reshape(1, -1))
    return pl.pallas_call(
        _ssd_kernel,
        grid=(bsz, nc),
        in_specs=[
            pl.BlockSpec((SSD_CHUNK, D_INNER), row(COL_XBC // D_INNER)),
            pl.BlockSpec((SSD_CHUNK, nbc), row((COL_XBC + D_INNER) // nbc)),
            pl.BlockSpec((SSD_CHUNK, D_INNER), row(COL_Z // D_INNER)),
            pl.BlockSpec((SSD_CHUNK, LANES), row(COL_DT // LANES)),
            pl.BlockSpec((CONV_W, D_INNER), fixed), pl.BlockSpec((1, D_INNER), fixed),
            pl.BlockSpec((CONV_W, nbc), fixed), pl.BlockSpec((1, nbc), fixed),
            pl.BlockSpec((1, LANES), fixed), pl.BlockSpec((1, LANES), fixed),
            pl.BlockSpec((1, D_INNER), fixed), pl.BlockSpec((1, D_INNER), fixed),
        ],
        out_specs=[pl.BlockSpec((SSD_CHUNK, D_INNER), lambda b, c: (b * nc + c, 0)),
                   pl.BlockSpec((None, H_C * P_C, N_C), lambda b, c: (b, 0, 0))],
        out_shape=[jax.ShapeDtypeStruct((bsz * L, D_INNER), BF16),
                   jax.ShapeDtypeStruct((bsz, H_C * P_C, N_C), F32)],
        scratch_shapes=[pltpu.VMEM((H_C * P_C, N_C), F32), pltpu.VMEM((8, D_INNER), F32),
                        pltpu.VMEM((8, nbc), F32)],
        compiler_params=_cparams("parallel", "arbitrary"),
        name="ssd_prompt",
    )(*args)


def _bf16_round(x):
    return x.astype(BF16).astype(F32)


SWA_DECODE_TOKENS = 8


def _swa_decode_kernel(q_ref, kn_ref, vn_ref, ck_ref, cv_ref, bias_ref, sink_ref, o_ref, wk_ref, wv_ref):
    W = WINDOW
    last = lax.broadcasted_iota(jnp.int32, (W, LANES), 0) == W - 1
    lo = lax.broadcasted_iota(jnp.int32, (8, LANES), 1) < HD_A
    g0 = lax.broadcasted_iota(jnp.int32, (8, LANES), 0) < G_A // 2
    zero = jnp.zeros((8, LANES), F32)
    nt_dims = (((1,), (1,)), ((), ()))
    for t in range(q_ref.shape[0]):
        kw = jnp.where(last, jnp.broadcast_to(kn_ref[t], (W, LANES)), pltpu.roll(ck_ref[t], W - 1, axis=0))
        vw = jnp.where(last, jnp.broadcast_to(vn_ref[t], (W, LANES)), pltpu.roll(cv_ref[t], W - 1, axis=0))
        wk_ref[t] = kw
        wv_ref[t] = vw
        kb, vb = kw.astype(BF16), vw.astype(BF16)
        q = q_ref[t]
        qr = pltpu.roll(q, HD_A, axis=1)
        q_par = [jnp.where(g0, jnp.where(lo, q, zero), jnp.where(lo, zero, qr)),
                 jnp.where(g0, jnp.where(lo, qr, zero), jnp.where(lo, zero, q))]
        outs = []
        for par in range(2):
            s = lax.dot_general(q_par[par].astype(BF16), kb, nt_dims, preferred_element_type=F32)
            s = s * HD_A ** -0.5 + bias_ref[par]
            sk = sink_ref[par]
            m = jnp.maximum(jnp.max(s, axis=-1, keepdims=True), sk)
            pr = jnp.exp(s - m)
            den = jnp.sum(pr, axis=-1, keepdims=True) + jnp.exp(sk - m)
            outs.append(jnp.dot((pr / den).astype(BF16), vb, preferred_element_type=F32))
        o_even = jnp.where(g0, outs[0], pltpu.roll(outs[0], HD_A, axis=1))
        o_odd = jnp.where(g0, pltpu.roll(outs[1], HD_A, axis=1), outs[1])
        o_ref[t] = jnp.where(lo, o_even, o_odd).astype(o_ref.dtype)


def _swa_decode_call(proj3, cache_k, cache_v, layer, rel_bias, sinks):
    bsz = proj3.shape[0]
    nt = SWA_DECODE_TOKENS
    ck = cache_k.reshape(cache_k.shape[0], bsz, WINDOW, LANES)
    cv = cache_v.reshape(cache_v.shape[0], bsz, WINDOW, LANES)
    q8 = proj3[:, 0, COL_Q:COL_Q + H_A * HD_A].reshape(bsz, H_A // 2, LANES)
    dist = WINDOW - 1 - jnp.arange(WINDOW)
    bias = _bucket_bias(rel_bias, dist).reshape(H_A // 2, 2, WINDOW).transpose(1, 0, 2)
    sink = sinks.astype(F32).reshape(H_A // 2, 2, 1).transpose(1, 0, 2)
    new_tok = lambda blk: (lambda i: (i, 0, blk))
    cache = lambda i: (layer, i, 0, 0)
    tok3 = lambda i: (i, 0, 0)
    out, wk, wv = pl.pallas_call(
        _swa_decode_kernel,
        grid=(bsz // nt,),
        in_specs=[
            pl.BlockSpec((nt, H_A // 2, LANES), tok3),
            pl.BlockSpec((nt, 1, LANES), new_tok(COL_K // LANES)),
            pl.BlockSpec((nt, 1, LANES), new_tok(COL_V // LANES)),
            pl.BlockSpec((None, nt, WINDOW, LANES), cache),
            pl.BlockSpec((None, nt, WINDOW, LANES), cache),
            pl.BlockSpec((2, H_A // 2, WINDOW), lambda i: (0, 0, 0)),
            pl.BlockSpec((2, H_A // 2, 1), lambda i: (0, 0, 0)),
        ],
        out_specs=[pl.BlockSpec((nt, H_A // 2, LANES), tok3),
                   pl.BlockSpec((nt, WINDOW, LANES), tok3),
                   pl.BlockSpec((nt, WINDOW, LANES), tok3)],
        out_shape=[jax.ShapeDtypeStruct((bsz, H_A // 2, LANES), BF16),
                   jax.ShapeDtypeStruct((bsz, WINDOW, LANES), F32),
                   jax.ShapeDtypeStruct((bsz, WINDOW, LANES), F32)],
        compiler_params=_cparams("parallel"),
        name="swa_sample",
    )(q8, proj3, proj3, ck, cv, bias, sink)
    return out.reshape(bsz, H_A * HD_A), wk, wv


def _gmlp_step_kernel(u_ref, gv_ref, lng_ref, lnb_ref, w0_ref, b0_ref, o_ref, gv_out_ref):
    gv = _layer_norm_rows(_gelu(gv_ref[...]), lng_ref[...], lnb_ref[...])
    gv_out_ref[...] = gv
    mix = _bf16_round(w0_ref[...]) * _bf16_round(gv) + b0_ref[...]
    o_ref[...] = (_gelu(u_ref[...]) * mix).astype(o_ref.dtype)


def _gmlp_step_call(proj2d, p):
    n = proj2d.shape[0]
    w0 = jnp.repeat(p['gmlp_ws'][:, 0, 0], CG_B).reshape(1, -1)
    b0 = jnp.repeat(p['gmlp_bs'][:, 0], CG_B).reshape(1, -1)
    fixed = lambda i: (0, 0)
    return pl.pallas_call(
        _gmlp_step_kernel,
        grid=(1,),
        in_specs=[pl.BlockSpec((n, W_B), lambda i: (0, COL_U // W_B)),
                  pl.BlockSpec((n, W_B), lambda i: (0, COL_GV // W_B)),
                  pl.BlockSpec((1, W_B), fixed), pl.BlockSpec((1, W_B), fixed),
                  pl.BlockSpec((1, W_B), fixed), pl.BlockSpec((1, W_B), fixed)],
        out_specs=[pl.BlockSpec((n, W_B), fixed), pl.BlockSpec((n, W_B), fixed)],
        out_shape=[jax.ShapeDtypeStruct((n, W_B), BF16), jax.ShapeDtypeStruct((n, W_B), F32)],
        compiler_params=_cparams("arbitrary"),
        name="gmlp_sample",
    )(proj2d, proj2d, p['gmlp_ln_g'].reshape(1, -1), p['gmlp_ln_b'].reshape(1, -1), w0, b0)


def _conv_step(st, cur, w, bias):
    y = bias
    for t in range(CONV_W - 1):
        y = y + st[t:t + 1, :] * w[t:t + 1, :]
    return y + cur * w[CONV_W - 1:CONV_W, :]


def _ssd_step_kernel(xs_ref, bc_ref, z_ref, dt_ref, stx_ref, stb_ref, h0_ref, cwx_ref, cbx_ref, cwb_ref, cbb_ref,
                     dtb_ref, a_ref, dsk_ref, ng_ref, y_ref, h_ref, ncx_ref, ncb_ref):
    xs_raw, bc_raw = xs_ref[...], bc_ref[...]
    stx, stb = stx_ref[...], stb_ref[...]
    ncx_ref[0:CONV_W - 2, :] = stx[1:CONV_W - 1]
    ncx_ref[CONV_W - 2:CONV_W - 1, :] = xs_raw
    ncb_ref[0:CONV_W - 2, :] = stb[1:CONV_W - 1]
    ncb_ref[CONV_W - 2:CONV_W - 1, :] = bc_raw
    xs = _silu(_conv_step(stx, xs_raw, cwx_ref[...], cbx_ref[...]))
    bc = _silu(_conv_step(stb, bc_raw, cwb_ref[...], cbb_ref[...]))
    dt = _softplus(dt_ref[...] + dtb_ref[...])
    decay = jnp.exp(dt * a_ref[...])
    xdt = _bf16_round(xs * dt)
    gw = D_INNER // G_C
    first_group = lax.broadcasted_iota(jnp.int32, (1, D_INNER), 1) < gw
    bm = [_bf16_round(bc[:, N_C * g:N_C * (g + 1)]) for g in range(G_C)]
    cm = [_bf16_round(bc[:, N_C * (G_C + g):N_C * (G_C + g + 1)]) for g in range(G_C)]
    cb = [_bf16_round(jnp.sum(cm[g] * bm[g], axis=1, keepdims=True)) for g in range(G_C)]
    y_diag = jnp.where(first_group, cb[0], cb[1]) * xdt
    h0 = h0_ref[...]
    y_off = jnp.concatenate([
        lax.dot_general(jnp.broadcast_to(cm[g], (8, N_C)).astype(BF16), h0[gw * g:gw * (g + 1)].astype(BF16),
                        (((1,), (1,)), ((), ())), preferred_element_type=F32)[0:1] for g in range(G_C)], axis=1)
    y = y_diag + y_off * decay
    y = (y + dsk_ref[...] * xs) * _silu(z_ref[...])
    outs = []
    for g in range(G_C):
        yg = y[:, gw * g:gw * (g + 1)]
        outs.append(yg * lax.rsqrt(jnp.mean(yg * yg, axis=-1, keepdims=True) + RMS_EPS))
    y_ref[...] = (jnp.concatenate(outs, axis=1) * ng_ref[...]).astype(y_ref.dtype)
    decay_rows = jnp.broadcast_to(decay, (LANES, D_INNER)).T
    xdt_rows = jnp.broadcast_to(xdt, (LANES, D_INNER)).T
    rows = lax.broadcasted_iota(jnp.int32, (D_INNER, N_C), 0)
    bm_rows = jnp.where(rows < gw, jnp.broadcast_to(bm[0], (D_INNER, N_C)), jnp.broadcast_to(bm[1], (D_INNER, N_C)))
    h_ref[...] = h0 * decay_rows + xdt_rows * bm_rows


def _ssd_step_call(proj3, state_conv, state_ssm, layer, p):
    bsz = proj3.shape[0]
    nbc = 2 * G_C * N_C
    ssm = state_ssm.reshape(state_ssm.shape[0], bsz, H_C * P_C, N_C)
    dt_lanes = jnp.repeat(proj3[:, :, COL_DT:COL_DT + H_C], P_C, axis=-1)
    per_lane = lambda v: jnp.repeat(v.astype(F32), P_C).reshape(1, -1)
    cw, cbias = p['conv_w'], p['conv_b'].reshape(1, -1)
    tok = lambda blk: (lambda i: (i, 0, blk))
    fixed = lambda i: (0, 0)
    return pl.pallas_call(
        _ssd_step_kernel,
        grid=(bsz,),
        in_specs=[
            pl.BlockSpec((None, 1, D_INNER), tok(COL_XBC // D_INNER)),
            pl.BlockSpec((None, 1, nbc), tok((COL_XBC + D_INNER) // nbc)),
            pl.BlockSpec((None, 1, D_INNER), tok(COL_Z // D_INNER)),
            pl.BlockSpec((None, 1, D_INNER), tok(0)),
            pl.BlockSpec((None, None, CONV_W - 1, D_INNER), lambda i: (layer, i, 0, 0)),
            pl.BlockSpec((None, None, CONV_W - 1, nbc), lambda i: (layer, i, 0, D_INNER // nbc)),
            pl.BlockSpec((None, None, H_C * P_C, N_C), lambda i: (layer, i, 0, 0)),
            pl.BlockSpec((CONV_W, D_INNER), fixed), pl.BlockSpec((1, D_INNER), fixed),
            pl.BlockSpec((CONV_W, nbc), fixed), pl.BlockSpec((1, nbc), fixed),
            pl.BlockSpec((1, D_INNER), fixed), pl.BlockSpec((1, D_INNER), fixed),
            pl.BlockSpec((1, D_INNER), fixed), pl.BlockSpec((1, D_INNER), fixed),
        ],
        out_specs=[pl.BlockSpec((None, 1, D_INNER), lambda i: (i, 0, 0)),
                   pl.BlockSpec((None, H_C * P_C, N_C), lambda i: (i, 0, 0)),
                   pl.BlockSpec((None, CONV_W - 1, D_INNER), lambda i: (i, 0, 0)),
                   pl.BlockSpec((None, CONV_W - 1, nbc), lambda i: (i, 0, 0))],
        out_shape=[jax.ShapeDtypeStruct((bsz, 1, D_INNER), BF16),
                   jax.ShapeDtypeStruct((bsz, H_C * P_C, N_C), F32),
                   jax.ShapeDtypeStruct((bsz, CONV_W - 1, D_INNER), F32),
                   jax.ShapeDtypeStruct((bsz, CONV_W - 1, nbc), F32)],
        compiler_params=_cparams("parallel"),
        name="ssd_sample",
    )(proj3, proj3, proj3, dt_lanes, state_conv, state_conv, ssm,
      cw[:, :D_INNER], cbias[:, :D_INNER], cw[:, D_INNER:], cbias[:, D_INNER:],
      per_lane(p['dt_bias']), per_lane(-jnp.exp(p['a_log'].astype(F32))), per_lane(p['d_skip']),
      p['ssm_norm_g'].reshape(1, -1))


def _xattn_kernel(x1_ref, wq_ref, mk_ref, mv_ref, wo_ref, g_ref, b_ref, wr_ref, br_ref, x2_ref, lg_ref):
    x1 = x1_ref[...]
    q = jnp.dot(x1.astype(BF16), wq_ref[...], preferred_element_type=F32)
    outs = []
    for h in range(XH):
        sl = slice(XHD * h, XHD * (h + 1))
        s = lax.dot_general(q[:, sl].astype(BF16), mk_ref[:, sl].astype(BF16),
                            (((1,), (1,)), ((), ())), preferred_element_type=F32) * XHD ** -0.5
        e = jnp.exp(s - jnp.max(s, axis=-1, keepdims=True))
        w = e / jnp.sum(e, axis=-1, keepdims=True)
        outs.append(jnp.dot(w.astype(BF16), mv_ref[:, sl].astype(BF16), preferred_element_type=F32).astype(BF16))
    o = jnp.concatenate(outs, axis=1)
    y = jnp.dot(o, wo_ref[...], preferred_element_type=F32) + DN_ALPHA * x1
    x2 = _layer_norm_rows(y, g_ref[...], b_ref[...])
    x2_ref[...] = x2
    lg_ref[...] = _top_k_lanes(jnp.dot(x2.astype(BF16), wr_ref[...], preferred_element_type=F32) + br_ref[...])


def _xattn_prompt_call(x1, mkv, lw, g, b, bsz, L, tm):
    nt = L // tm
    xw = XH * XHD
    row = lambda bb, i: (bb * nt + i, 0)
    fixed = lambda bb, i: (0, 0)
    n_lg = lw['w_router'].shape[1]
    return pl.pallas_call(
        _xattn_kernel,
        grid=(bsz, nt),
        in_specs=[
            pl.BlockSpec((tm, D_MODEL), row),
            pl.BlockSpec((D_MODEL, xw), fixed),
            pl.BlockSpec((N_MEM, xw), lambda bb, i: (bb, 0)),
            pl.BlockSpec((N_MEM, xw), lambda bb, i: (bb, 1)),
            pl.BlockSpec((xw, D_MODEL), fixed),
            pl.BlockSpec((1, D_MODEL), fixed), pl.BlockSpec((1, D_MODEL), fixed),
            pl.BlockSpec((D_MODEL, n_lg), fixed), pl.BlockSpec((1, n_lg), fixed),
        ],
        out_specs=[pl.BlockSpec((tm, D_MODEL), row), pl.BlockSpec((tm, n_lg), row)],
        out_shape=[jax.ShapeDtypeStruct((bsz * L, D_MODEL), F32), jax.ShapeDtypeStruct((bsz * L, n_lg), F32)],
        compiler_params=_cparams("parallel", "parallel"),
        name="memory_attention_prompt",
    )(x1, lw['w_xq'], mkv, mkv, lw['w_xo'], g.reshape(1, -1), b.reshape(1, -1),
      lw['w_router_bf16'], lw['b_router'])


def _xattn_decode_kernel(q_ref, k_ref, v_ref, o_ref):
    q = q_ref[...].astype(BF16).astype(F32)
    k = k_ref[...].astype(BF16).astype(F32)
    v = v_ref[...].astype(BF16).astype(F32)
    prod = k * q
    outs = []
    for h in range(XH):
        sl = slice(XHD * h, XHD * (h + 1))
        s = jnp.sum(prod[:, sl], axis=1, keepdims=True) * XHD ** -0.5
        e = jnp.exp(s - jnp.max(s, axis=0, keepdims=True))
        w = (e / jnp.sum(e, axis=0, keepdims=True)).astype(BF16).astype(F32)
        outs.append(jnp.sum(w * v[:, sl], axis=0, keepdims=True))
    o_ref[...] = jnp.concatenate(outs, axis=1)


def _xattn_decode_call(q, cache_k, cache_v, layer):
    bsz, xw = q.shape
    ck = cache_k.reshape(cache_k.shape[0], bsz, N_MEM, xw)
    cv = cache_v.reshape(cache_v.shape[0], bsz, N_MEM, xw)
    out = pl.pallas_call(
        _xattn_decode_kernel,
        grid=(bsz,),
        in_specs=[pl.BlockSpec((None, 1, xw), lambda i: (i, 0, 0)),
                  pl.BlockSpec((None, None, N_MEM, xw), lambda i: (layer, i, 0, 0)),
                  pl.BlockSpec((None, None, N_MEM, xw), lambda i: (layer, i, 0, 0))],
        out_specs=pl.BlockSpec((None, 1, xw), lambda i: (i, 0, 0)),
        out_shape=jax.ShapeDtypeStruct((bsz, 1, xw), F32),
        compiler_params=_cparams("parallel"),
        name="memory_attention_sample",
    )(q.reshape(bsz, 1, xw), ck, cv)
    return out.reshape(bsz, xw)


def _t5_bucket(dist):
    n = jnp.maximum(dist, 0)
    exact = N_BUCKETS // 2
    nf = jnp.maximum(n, 1).astype(F32)
    large = exact + (jnp.log(nf / exact) / math.log(MAX_DIST / exact) * (N_BUCKETS - exact)).astype(jnp.int32)
    return jnp.where(n < exact, n, jnp.minimum(large, N_BUCKETS - 1))


def _bucket_bias(rel_bias, dist):
    onehot = (_t5_bucket(dist)[..., None] == jnp.arange(N_BUCKETS)).astype(F32)
    return jnp.einsum('...b,bh->h...', onehot, rel_bias.astype(F32), precision=lax.Precision.HIGHEST)


def _prompt_layer(x, mkv, lw, p, swa_bias, bsz, L):
    T = bsz * L
    xb = x.astype(BF16)
    proj = _matmul(xb, lw['w_in'], 1024, 512)
    a_out = _swa_prompt_call(proj, swa_bias, p['sinks'], bsz, L)
    b_out = _gmlp_prompt_call(proj, p, T)
    c_out, h_last = _ssd_prompt_call(proj, p, bsz, L)
    mixed = _gate_merge(xb, a_out, b_out, c_out, lw['w_gate'], lw['b_gate'], lw['w_branch'], 512, 512)
    x1 = _mm_res_ln(mixed, lw['w_o'], x, p['ln1_g'], p['ln1_b'], 512)
    x2, logits = _xattn_prompt_call(x1, mkv, lw, p['ln2_g'], p['ln2_b'], bsz, L, 512)
    proj3 = proj.reshape(bsz, L, IN_DIM_PAD)
    win_k = proj3[:, L - WINDOW:, COL_K:COL_K + KV_A * HD_A].reshape(bsz, WINDOW, KV_A, HD_A)
    win_v = proj3[:, L - WINDOW:, COL_V:COL_V + KV_A * HD_A].reshape(bsz, WINDOW, KV_A, HD_A)
    conv = proj3[:, L - (CONV_W - 1):, COL_XBC:COL_XBC + CONV_DIM]
    return x2, logits, (win_k, win_v, conv, h_last.reshape(bsz, H_C, P_C, N_C))


def _sample_layer(x, layer, lw, p, rel_bias, cache_win_k, cache_win_v, state_conv, state_ssm, cache_mem_k, cache_mem_v):
    bsz = x.shape[0]
    xb = x.astype(BF16)
    proj = _matmul(xb, lw['w_in'], bsz, 512)
    proj3 = proj.reshape(bsz, 1, IN_DIM_PAD)
    a_out, win_k, win_v = _swa_decode_call(proj3, cache_win_k, cache_win_v, layer, rel_bias, p['sinks'])
    b_out, gv = _gmlp_step_call(proj, p)
    c_out, ssm, conv_x, conv_bc = _ssd_step_call(proj3, state_conv, state_ssm, layer, p)
    mixed = _gate_merge(xb, a_out, b_out, c_out.reshape(bsz, BR_W),
                        lw['w_gate'], lw['b_gate'], lw['w_branch'], bsz, 512)
    x1 = _mm_res_ln(mixed, lw['w_o'], x, p['ln1_g'], p['ln1_b'], bsz)
    q = _matmul(x1.astype(BF16), lw['w_xq'], bsz, XH * XHD)
    o = _xattn_decode_call(q, cache_mem_k, cache_mem_v, layer)
    x2 = _mm_res_ln(o.astype(BF16), lw['w_xo'], x1, p['ln2_g'], p['ln2_b'], bsz)
    logits = _router(x2, lw['w_router'], lw['b_router'], bsz)
    states = (win_k.reshape(bsz, WINDOW, KV_A, HD_A), win_v.reshape(bsz, WINDOW, KV_A, HD_A),
              jnp.concatenate([conv_x, conv_bc], axis=-1), ssm.reshape(bsz, H_C, P_C, N_C),
              gv.reshape(bsz, 1, W_B))
    return x2, logits, states


def kernel(x_prompt, x_sample, mem_prompt, cache_win_k, cache_win_v, state_conv, state_ssm, cache_mem_k, cache_mem_v, w_in, rel_bias, sinks, gmlp_ln_g, gmlp_ln_b, gmlp_ws, gmlp_bs, conv_w, conv_b, dt_bias, a_log, d_skip, ssm_norm_g, w_branch, w_gate, b_gate, w_o, ln1_g, ln1_b, w_xq, w_xk, w_xv, w_xo, ln2_g, ln2_b, w_router, b_router, w_e1, b_e1, w_e2, b_e2, ln3_g, ln3_b):
    assert cache_win_k.shape[2] == WINDOW and x_sample.shape[1] == 1
    n_prompt, n_mem = mem_prompt.shape[0], mem_prompt.shape[1]
    bp, lp = x_prompt.shape[:2]
    bs_ = x_sample.shape[0]
    hp, hs = x_prompt.reshape(bp * lp, D_MODEL), x_sample.reshape(bs_, D_MODEL)
    wk_p, wv_p, cv_p, ssm_p, mk_ps, mv_ps = [], [], [], [], [], []
    wk_s, wv_s, cv_s, ssm_s, gv_s = [], [], [], [], []
    mem_b = mem_prompt.reshape(n_prompt * n_mem, D_MODEL).astype(BF16)
    swa_bias = _swa_prompt_bias(rel_bias)
    experts = (w_e1, b_e1.reshape(DEPTH, N_EXPERTS, 1, 2 * D_FF), w_e2, b_e2.reshape(DEPTH, N_EXPERTS, 1, D_MODEL))
    for l in range(DEPTH):
        p = dict(sinks=sinks[l], gmlp_ln_g=gmlp_ln_g[l], gmlp_ln_b=gmlp_ln_b[l],
                 gmlp_ws=gmlp_ws[l], gmlp_bs=gmlp_bs[l], conv_w=conv_w[l], conv_b=conv_b[l],
                 dt_bias=dt_bias[l], a_log=a_log[l], d_skip=d_skip[l], ssm_norm_g=ssm_norm_g[l],
                 ln1_g=ln1_g[l], ln1_b=ln1_b[l], ln2_g=ln2_g[l], ln2_b=ln2_b[l])
        wi = w_in[l]
        seg = np.cumsum([0] + IN_SIZES)
        part = lambda n: wi[:, seg[n]:seg[n + 1]]
        w_in_cols = jnp.concatenate(
            [part(0), part(3), part(4), part(5), part(6), part(1), part(2), part(7),
             jnp.zeros((D_MODEL, IN_DIM_PAD - IN_DIM), F32)], axis=1).astype(BF16)
        w_router_pad = jnp.pad(w_router[l], ((0, 0), (0, LANES - N_EXPERTS)))
        lw = dict(
            w_in=w_in_cols,
            w_gate=w_gate[l].astype(BF16),
            b_gate=b_gate[l].reshape(1, -1),
            w_branch=w_branch[l].astype(BF16),
            w_o=w_o[l].astype(BF16),
            w_xq=w_xq[l].astype(BF16),
            w_xo=w_xo[l].astype(BF16),
            w_router=w_router_pad,
            w_router_bf16=w_router_pad.astype(BF16),
            b_router=jnp.pad(b_router[l], (0, LANES - N_EXPERTS)).reshape(1, -1),
        )
        w_kv = jnp.concatenate([w_xk[l], w_xv[l]], axis=1).astype(BF16)
        mkv = _matmul(mem_b, w_kv, n_mem, XH * XHD)
        mk = mkv[:, :XH * XHD].reshape(n_prompt, n_mem, XH, XHD)
        mv = mkv[:, XH * XHD:].reshape(n_prompt, n_mem, XH, XHD)
        x2_p, lg_p, st_p = _prompt_layer(hp, mkv, lw, p, swa_bias, bp, lp)
        x2_s, lg_s, st_s = _sample_layer(hs, l, lw, p, rel_bias, cache_win_k, cache_win_v, state_conv, state_ssm,
                                         cache_mem_k, cache_mem_v)
        hp, hs = _moe_ln(x2_p, x2_s, lg_p, lg_s, experts, l, ln3_g[l], ln3_b[l])
        wk_p.append(st_p[0]); wv_p.append(st_p[1]); cv_p.append(st_p[2]); ssm_p.append(st_p[3])
        mk_ps.append(mk); mv_ps.append(mv)
        wk_s.append(st_s[0]); wv_s.append(st_s[1]); cv_s.append(st_s[2]); ssm_s.append(st_s[3])
        gv_s.append(st_s[4])
    hp = hp.reshape(bp, lp, D_MODEL)
    hs = hs.reshape(bs_, 1, D_MODEL)
    return (hp, hs,
            jnp.stack(wk_p), jnp.stack(wv_p), jnp.stack(cv_p), jnp.stack(ssm_p),
            jnp.stack(mk_ps), jnp.stack(mv_ps),
            jnp.stack(wk_s), jnp.stack(wv_s), jnp.stack(cv_s), jnp.stack(ssm_s), jnp.stack(gv_s))
```

```python
import functools
import math

import numpy as np
import jax
import jax.numpy as jnp
from jax import lax
from jax.experimental import pallas as pl
from jax.experimental.pallas import tpu as pltpu

D_MODEL = 2048
DEPTH = 2
PAST_LEN = 16384
WINDOW = 128
H_A = 16
KV_A = 2
HD_A = 64
G_A = H_A // KV_A
N_BUCKETS = 32
MAX_DIST = 128
CHUNK_B = 128
GB = 16
CG_B = 64
W_B = GB * CG_B
D_INNER = 1024
P_C = 64
H_C = D_INNER // P_C
G_C = 2
R_C = H_C // G_C
N_C = 128
CONV_W = 4
CONV_DIM = D_INNER + 2 * G_C * N_C
SSD_CHUNK = 128
N_BRANCH = 3
BR_W = 1024
N_MEM = 256
XH = 4
XHD = 128
N_EXPERTS = 32
TOP_K = 4
D_FF = D_MODEL
SWIGLU_ALPHA = 1.702
SWIGLU_LIMIT = 7.0
DN_ALPHA = (2 * DEPTH) ** 0.25
LN_EPS = 1e-5
RMS_EPS = 1e-5

IN_SIZES = [H_A * HD_A, KV_A * HD_A, KV_A * HD_A, W_B, W_B, D_INNER, CONV_DIM, H_C]
IN_DIM = sum(IN_SIZES)
IN_DIM_PAD = 6144
COL_Q, COL_U, COL_GV, COL_Z, COL_XBC = 0, 1024, 2048, 3072, 4096
COL_K = COL_XBC + CONV_DIM
COL_V = COL_K + KV_A * HD_A
COL_DT = COL_V + KV_A * HD_A
LANES = 128

VMEM_LIMIT = 56 * 1024 * 1024
MOE_BM = 544
MOE_FC = 256
MOE_COL_TILE = 256

F32 = jnp.float32
BF16 = jnp.bfloat16


def _cparams(*sem):
    return pltpu.CompilerParams(dimension_semantics=sem, vmem_limit_bytes=VMEM_LIMIT)


def _mm_kernel(x_ref, w_ref, o_ref):
    o_ref[...] = jnp.dot(x_ref[...], w_ref[...], preferred_element_type=F32).astype(o_ref.dtype)


def _matmul(x, w, tm, tn, out_dtype=F32):
    M, K = x.shape
    N = w.shape[1]
    assert M % tm == 0 and N % tn == 0
    return pl.pallas_call(
        _mm_kernel,
        grid=(N // tn, M // tm),
        in_specs=[pl.BlockSpec((tm, K), lambda j, i: (i, 0)),
                  pl.BlockSpec((K, tn), lambda j, i: (0, j))],
        out_specs=pl.BlockSpec((tm, tn), lambda j, i: (i, j)),
        out_shape=jax.ShapeDtypeStruct((M, N), out_dtype),
        compiler_params=_cparams("parallel", "parallel"),
        name="dense_matmul",
    )(x, w)


def _gate_merge_kernel(x_ref, a_ref, b_ref, c_ref, wg0_ref, wg1_ref, wg2_ref,
                       bg0_ref, bg1_ref, bg2_ref, wp_ref, o_ref):
    x = x_ref[...]
    acc = None
    for k, (br_ref, wg_ref, bg_ref) in enumerate(
            ((a_ref, wg0_ref, bg0_ref), (b_ref, wg1_ref, bg1_ref), (c_ref, wg2_ref, bg2_ref))):
        z = jnp.dot(x, wg_ref[...], preferred_element_type=F32) + bg_ref[...]
        gate = 1.0 / (1.0 + jnp.exp(-z))
        proj = jnp.dot(br_ref[...], wp_ref[k], preferred_element_type=F32)
        acc = gate * proj if acc is None else acc + gate * proj
    o_ref[...] = acc.astype(o_ref.dtype)


def _gate_merge(x, a, b, c, w_gate, b_gate, w_branch, tm, tn):
    M = x.shape[0]
    nt = D_MODEL // tn
    row = lambda j, i: (i, 0)
    in_specs = [pl.BlockSpec((tm, D_MODEL), row)] + [pl.BlockSpec((tm, BR_W), row)] * 3
    in_specs += [pl.BlockSpec((D_MODEL, tn), functools.partial(lambda j, i, k: (0, k * nt + j), k=k))
                 for k in range(N_BRANCH)]
    in_specs += [pl.BlockSpec((1, tn), functools.partial(lambda j, i, k: (0, k * nt + j), k=k))
                 for k in range(N_BRANCH)]
    in_specs += [pl.BlockSpec((N_BRANCH, BR_W, tn), lambda j, i: (0, 0, j))]
    return pl.pallas_call(
        _gate_merge_kernel,
        grid=(nt, M // tm),
        in_specs=in_specs,
        out_specs=pl.BlockSpec((tm, tn), lambda j, i: (i, j)),
        out_shape=jax.ShapeDtypeStruct((M, D_MODEL), BF16),
        compiler_params=_cparams("parallel", "parallel"),
        name="gate_merge",
    )(x, a, b, c, w_gate, w_gate, w_gate, b_gate, b_gate, b_gate, w_branch)


def _layer_norm_rows(y, g, b):
    mu = jnp.mean(y, axis=-1, keepdims=True)
    yc = y - mu
    var = jnp.mean(yc * yc, axis=-1, keepdims=True)
    return yc * lax.rsqrt(var + LN_EPS) * g + b


def _mm_res_ln_kernel(a_ref, w_ref, res_ref, g_ref, b_ref, o_ref):
    y = jnp.dot(a_ref[...], w_ref[...], preferred_element_type=F32) + DN_ALPHA * res_ref[...]
    o_ref[...] = _layer_norm_rows(y, g_ref[...], b_ref[...])


def _mm_res_ln(a, w, res, g, b, tm):
    M, K = a.shape
    row = lambda i: (i, 0)
    fixed = lambda i: (0, 0)
    return pl.pallas_call(
        _mm_res_ln_kernel,
        grid=(M // tm,),
        in_specs=[pl.BlockSpec((tm, K), row), pl.BlockSpec((K, D_MODEL), fixed),
                  pl.BlockSpec((tm, D_MODEL), row), pl.BlockSpec((1, D_MODEL), fixed),
                  pl.BlockSpec((1, D_MODEL), fixed)],
        out_specs=pl.BlockSpec((tm, D_MODEL), row),
        out_shape=jax.ShapeDtypeStruct((M, D_MODEL), F32),
        compiler_params=_cparams("parallel"),
        name="matmul_residual_layernorm",
    )(a, w, res, g.reshape(1, -1), b.reshape(1, -1))


def _top_k_lanes(logits):
    lane_i = lax.broadcasted_iota(jnp.int32, logits.shape, 1)
    lane_f = lane_i.astype(F32)
    work = jnp.where(lane_i < N_EXPERTS, logits, -jnp.inf)
    out = jnp.zeros_like(logits)
    for k in range(TOP_K):
        m = jnp.max(work, axis=-1, keepdims=True)
        idx = jnp.min(jnp.where(work == m, lane_f, float(LANES)), axis=-1, keepdims=True)
        out = jnp.where(lane_i == k, m, out)
        out = jnp.where(lane_i == TOP_K + k, idx, out)
        work = jnp.where(lane_f == idx, -jnp.inf, work)
    return out


def _router_kernel(x_ref, w_ref, b_ref, o_ref):
    acc = jnp.dot(x_ref[...].astype(BF16), w_ref[...].astype(BF16), preferred_element_type=F32)
    o_ref[...] = _top_k_lanes(acc + b_ref[...])


def _router(x, w_pad, b_pad, tm):
    M = x.shape[0]
    NP = w_pad.shape[1]
    return pl.pallas_call(
        _router_kernel,
        grid=(M // tm,),
        in_specs=[pl.BlockSpec((tm, D_MODEL), lambda i: (i, 0)),
                  pl.BlockSpec((D_MODEL, NP), lambda i: (0, 0)),
                  pl.BlockSpec((1, NP), lambda i: (0, 0))],
        out_specs=pl.BlockSpec((tm, NP), lambda i: (i, 0)),
        out_shape=jax.ShapeDtypeStruct((M, NP), F32),
        compiler_params=_cparams("parallel"),
        name="router_logits",
    )(x, w_pad, b_pad)


def _bf16_row_interleave(a, b):
    a32 = lax.bitcast_convert_type(a.astype(BF16).astype(F32), jnp.uint32)
    b32 = lax.bitcast_convert_type(b.astype(BF16).astype(F32), jnp.uint32)
    word = (a32 >> 16) | (b32 & jnp.uint32(0xFFFF0000))
    return pltpu.bitcast(word, BF16)


def _moe_kernel(be_ref, bx_ref, bv_ref, x_ref, w1_ref, b1_ref, w2_ref, b2_ref, o_ref):
    i = pl.program_id(0)
    f = pl.program_id(1)
    fc = w2_ref.shape[1] // 2
    bm = x_ref.shape[0]
    valid = bv_ref[i] == 1

    @pl.when(valid)
    def _():
        @pl.when(f == 0)
        def _():
            o_ref[...] = jnp.broadcast_to(b2_ref[0], o_ref.shape)

        x = x_ref[...].astype(BF16)
        tile = min(MOE_COL_TILE, 2 * fc)
        h = [jnp.dot(x, w1_ref[0, :, tile * n:tile * (n + 1)].astype(BF16), preferred_element_type=F32)
             + b1_ref[0, :, tile * n:tile * (n + 1)] for n in range(4 * fc // tile)]
        w2 = w2_ref[0]
        w2q = _bf16_row_interleave(w2[:fc], w2[fc:])
        even = (lax.broadcasted_iota(jnp.int32, (bm, LANES), 1) & 1) == 0
        per_tile = tile // LANES
        acts = []
        for c in range(2 * fc // LANES):
            sl = slice(LANES * (c % per_tile), LANES * (c % per_tile + 1))
            a = h[c // per_tile][:, sl]
            b = h[(2 * fc // LANES + c) // per_tile][:, sl]
            glu = jnp.where(even, a, pltpu.roll(b, 1, axis=1))
            lin = jnp.where(even, pltpu.roll(a, LANES - 1, axis=1), b)
            glu = jnp.minimum(glu, SWIGLU_LIMIT)
            lin = jnp.clip(lin, -SWIGLU_LIMIT, SWIGLU_LIMIT)
            act = glu * (1.0 / (1.0 + jnp.exp(-SWIGLU_ALPHA * glu))) * (lin + 1.0)
            acts.append(act.astype(BF16))
        act = jnp.concatenate(acts, axis=1) if len(acts) > 1 else acts[0]
        o_ref[...] += jnp.dot(act, w2q, preferred_element_type=F32)

    @pl.when(jnp.logical_not(valid) & (f == 0))
    def _():
        o_ref[...] = jnp.zeros_like(o_ref)


def _moe_blocks(p_end, n_valid, nb, bm):
    i32 = jnp.int32
    n_exp = p_end.shape[0]
    blk = jnp.arange(nb, dtype=i32)
    block_v = (blk < n_valid).astype(i32)
    block_x = jnp.minimum(blk, n_valid - 1)
    block_e = jnp.minimum(jnp.sum(p_end[None, :] <= (block_x * bm)[:, None], axis=1, dtype=i32), n_exp - 1)
    return block_e, block_x, block_v


def _moe_ffn_blocks(rows, block_e, block_x, block_v, w1, b1, w2, b2, layer, bm, fc):
    d_model = rows.shape[1]
    d_ff = w2.shape[2]
    nb = rows.shape[0] // bm
    nf = d_ff // (2 * fc)
    last_f = nf - 1

    def fsel(f, bv, i):
        return jnp.where(bv[i] == 1, f, last_f)

    grid_spec = pltpu.PrefetchScalarGridSpec(
        num_scalar_prefetch=3,
        grid=(nb, nf),
        in_specs=[
            pl.BlockSpec((bm, d_model), lambda i, f, be, bx, bv: (bx[i], 0)),
            pl.BlockSpec((None, 1, d_model, 4 * fc), lambda i, f, be, bx, bv: (layer, be[i], 0, fsel(f, bv, i))),
            pl.BlockSpec((None, 1, 1, 4 * fc), lambda i, f, be, bx, bv: (layer, be[i], 0, fsel(f, bv, i))),
            pl.BlockSpec((None, 1, 2 * fc, d_model), lambda i, f, be, bx, bv: (layer, be[i], fsel(f, bv, i), 0)),
            pl.BlockSpec((None, 1, 1, d_model), lambda i, f, be, bx, bv: (layer, be[i], 0, 0)),
        ],
        out_specs=pl.BlockSpec((bm, d_model), lambda i, f, be, bx, bv: (i, 0)),
    )
    return pl.pallas_call(
        _moe_kernel,
        grid_spec=grid_spec,
        out_shape=jax.ShapeDtypeStruct(rows.shape, F32),
        compiler_params=_cparams("arbitrary", "arbitrary"),
        name="moe_expert_ffn",
    )(block_e, block_x, block_v, rows, w1, b1, w2, b2)


def _combine_ln_kernel(*refs):
    g_refs = refs[:TOP_K]
    gate_ref, res_ref, lg_ref, lb_ref, o_ref = refs[TOP_K:]
    gate = gate_ref[...]
    ff = None
    for k in range(TOP_K):
        term = gate[:, k:k + 1] * g_refs[k][...]
        ff = term if ff is None else ff + term
    o_ref[...] = _layer_norm_rows(DN_ALPHA * res_ref[...] + ff, lg_ref[...], lb_ref[...])


def _combine_ln(gathered, gate, res, g, b, tm):
    T, d = res.shape
    nt = T // tm
    row = lambda i: (i, 0)
    fixed = lambda i: (0, 0)
    g_specs = [pl.BlockSpec((tm, d), functools.partial(lambda i, k: (k * nt + i, 0), k=k)) for k in range(TOP_K)]
    return pl.pallas_call(
        _combine_ln_kernel,
        grid=(nt,),
        in_specs=g_specs + [pl.BlockSpec((tm, TOP_K), row), pl.BlockSpec((tm, d), row),
                            pl.BlockSpec((1, d), fixed), pl.BlockSpec((1, d), fixed)],
        out_specs=pl.BlockSpec((tm, d), row),
        out_shape=jax.ShapeDtypeStruct((T, d), F32),
        compiler_params=_cparams("parallel"),
        name="moe_combine_layernorm",
    )(*([gathered] * TOP_K), gate, res, g.reshape(1, -1), b.reshape(1, -1))


RANK_TOKENS = 128


def _rank_kernel(route_ref, rank_ref, count_ref, carry_ref):
    i = pl.program_id(0)
    n = route_ref.shape[0]

    @pl.when(i == 0)
    def _():
        carry_ref[...] = jnp.zeros_like(carry_ref)

    route = route_ref[...]
    lane = lax.broadcasted_iota(jnp.int32, (n, LANES), 1)
    lane_f = lane.astype(F32)
    hits = [lane_f == route[:, TOP_K + k:TOP_K + k + 1] for k in range(TOP_K)]
    per_token = jnp.zeros((n, LANES), F32)
    for k in range(TOP_K):
        per_token = per_token + jnp.where(hits[k], 1.0, 0.0)
    earlier = lax.broadcasted_iota(jnp.int32, (n, n), 0) > lax.broadcasted_iota(jnp.int32, (n, n), 1)
    before = jnp.dot(jnp.where(earlier, 1.0, 0.0).astype(BF16), per_token.astype(BF16),
                     preferred_element_type=F32) + carry_ref[...]
    out = jnp.zeros((n, LANES), F32)
    for k in range(TOP_K):
        r = jnp.sum(jnp.where(hits[k], before, 0.0), axis=-1, keepdims=True)
        out = jnp.where(lane == k, r, out)
    rank_ref[...] = out
    carry_ref[...] += jnp.sum(per_token, axis=0, keepdims=True)
    count_ref[...] = carry_ref[...]


def _rank_assignments(route):
    n = route.shape[0]
    return pl.pallas_call(
        _rank_kernel,
        grid=(n // RANK_TOKENS,),
        in_specs=[pl.BlockSpec((RANK_TOKENS, LANES), lambda i: (i, 0))],
        out_specs=[pl.BlockSpec((RANK_TOKENS, LANES), lambda i: (i, 0)), pl.BlockSpec((1, LANES), lambda i: (0, 0))],
        out_shape=[jax.ShapeDtypeStruct((n, LANES), F32), jax.ShapeDtypeStruct((1, LANES), F32)],
        scratch_shapes=[pltpu.VMEM((1, LANES), F32)],
        compiler_params=_cparams("arbitrary"),
        name="moe_rank_assignments",
    )(route)


def _moe_route(route, n_tok, n_exp, bm):
    i32 = jnp.int32
    ranks, counts = _rank_assignments(route)
    top_v = route[:n_tok, :TOP_K]
    top_i = route[:n_tok, TOP_K:2 * TOP_K].astype(i32)
    rank = ranks[:n_tok, :TOP_K].astype(i32)
    gate = jax.nn.softmax(top_v, axis=-1)
    n_assign = n_tok * TOP_K
    nb = -(-(n_assign + n_exp * (bm - 1)) // bm)
    counts = counts[0, :n_exp].astype(i32)
    padded = (counts + bm - 1) // bm * bm
    p_end = jnp.cumsum(padded)
    p_start = (p_end - padded).astype(i32)
    is_e = top_i[:, :, None] == jnp.arange(n_exp, dtype=i32)[None, None, :]
    pos = rank + jnp.sum(jnp.where(is_e, p_start[None, None, :], 0), axis=-1, dtype=i32)
    tok = jnp.broadcast_to(jnp.arange(n_tok, dtype=i32)[:, None], (n_tok, TOP_K))
    src = jnp.zeros((nb * bm,), i32).at[pos.reshape(-1)].set(tok.reshape(-1), unique_indices=True)
    n_valid = (p_end[-1] // bm).astype(i32)
    return gate, src, pos, (padded.astype(i32), p_end.astype(i32), n_valid, nb)


def _moe_ln(x2_p, x2_s, route_p, route_s, experts, layer, ln_g, ln_b):
    tp, ts = x2_p.shape[0], x2_s.shape[0]
    x2 = jnp.concatenate([x2_p, x2_s], axis=0)
    n_pad = -(tp + ts) % RANK_TOKENS
    route = jnp.concatenate([route_p, route_s, jnp.full((n_pad, LANES), -1.0, F32)], axis=0)
    gate, src, pos, (padded, p_end, n_valid, nb) = _moe_route(route, tp + ts, N_EXPERTS, MOE_BM)
    rows = jnp.take(x2, src, axis=0, mode='clip')
    w_e1, b_e1, w_e2, b_e2 = experts
    block_e, block_x, block_v = _moe_blocks(p_end, n_valid, nb, MOE_BM)
    out_rows = _moe_ffn_blocks(rows, block_e, block_x, block_v, w_e1, b_e1, w_e2, b_e2, layer, MOE_BM, MOE_FC)
    pos = pos.reshape(tp + ts, TOP_K)
    g_p = jnp.take(out_rows, pos[:tp].T.reshape(-1), axis=0, mode='clip')
    g_s = jnp.take(out_rows, pos[tp:].T.reshape(-1), axis=0, mode='clip')
    hp = _combine_ln(g_p, gate[:tp], x2_p, ln_g, ln_b, 256)
    hs = _combine_ln(g_s, gate[tp:], x2_s, ln_g, ln_b, ts)
    return hp, hs


def _half_lane_variants(t):
    lo = lax.broadcasted_iota(jnp.int32, t.shape, 1) < HD_A
    zero = jnp.zeros_like(t)
    tr = pltpu.roll(t, HD_A, axis=1)
    return [[jnp.where(lo, t, zero).astype(BF16), jnp.where(lo, zero, tr).astype(BF16)],
            [jnp.where(lo, tr, zero).astype(BF16), jnp.where(lo, zero, t).astype(BF16)]]


SWA_QBLOCKS = 2


def _swa_kernel(sinks_ref, q_ref, kp_ref, kc_ref, vp_ref, vc_ref, bias_ref, o_ref):
    j = pl.program_id(1)
    W = WINDOW
    k_all = jnp.concatenate([kp_ref[...], kc_ref[...]], axis=0)
    v_all = jnp.concatenate([vp_ref[...], vc_ref[...]], axis=0)
    col = lax.broadcasted_iota(jnp.int32, (W, 2 * W), 1)
    no_prev = col < jnp.where(j == 0, W, 0)
    for sub in range(SWA_QBLOCKS):
        kvar = _half_lane_variants(k_all[W * sub:W * (sub + 2)])
        vvar = _half_lane_variants(v_all[W * sub:W * (sub + 2)])
        rows = slice(W * sub, W * (sub + 1))
        scores = []
        for h in range(H_A):
            qp = q_ref[rows, LANES * (h // 2):LANES * (h // 2 + 1)].astype(BF16)
            scores.append(lax.dot_general(qp, kvar[h // G_A][h % 2], (((1,), (1,)), ((), ())),
                                          preferred_element_type=F32))
        probs = []
        for h in range(H_A):
            s = scores[h] * HD_A ** -0.5 + bias_ref[h]
            if sub == 0:
                s = jnp.where(no_prev, -jnp.inf, s)
            sk = sinks_ref[h]
            m = jnp.maximum(jnp.max(s, axis=-1, keepdims=True), sk)
            pr = jnp.exp(s - m)
            den = jnp.sum(pr, axis=-1, keepdims=True) + jnp.exp(sk - m)
            probs.append((pr / den).astype(BF16))
        for r in range(H_A // 2):
            g = (2 * r) // G_A
            acc = jnp.dot(probs[2 * r], vvar[g][0], preferred_element_type=F32)
            acc = acc + jnp.dot(probs[2 * r + 1], vvar[g][1], preferred_element_type=F32)
            o_ref[rows, LANES * r:LANES * (r + 1)] = acc.astype(o_ref.dtype)


def _swa_prompt_bias(rel_bias):
    i = jnp.arange(WINDOW)[:, None]
    j = jnp.arange(2 * WINDOW)[None, :]
    dist = i + WINDOW - j
    return jnp.where((dist >= 0) & (dist < WINDOW), _bucket_bias(rel_bias, dist), -jnp.inf)


def _swa_prompt_call(proj2d, bias, sinks, bsz, L):
    nq = SWA_QBLOCKS
    nb = L // (nq * WINDOW)
    kcol, vcol = COL_K // LANES, COL_V // LANES
    cur = lambda b, j: b * nb + j
    prev = lambda b, j: (b * nb + j) * nq - jnp.minimum(j, 1)
    return pl.pallas_call(
        _swa_kernel,
        grid=(bsz, nb),
        in_specs=[
            pl.BlockSpec(memory_space=pltpu.SMEM),
            pl.BlockSpec((nq * WINDOW, H_A * HD_A), lambda b, j: (cur(b, j), COL_Q // (H_A * HD_A))),
            pl.BlockSpec((WINDOW, LANES), lambda b, j: (prev(b, j), kcol)),
            pl.BlockSpec((nq * WINDOW, LANES), lambda b, j: (cur(b, j), kcol)),
            pl.BlockSpec((WINDOW, LANES), lambda b, j: (prev(b, j), vcol)),
            pl.BlockSpec((nq * WINDOW, LANES), lambda b, j: (cur(b, j), vcol)),
            pl.BlockSpec((H_A, WINDOW, 2 * WINDOW), lambda b, j: (0, 0, 0)),
        ],
        out_specs=pl.BlockSpec((nq * WINDOW, H_A * HD_A), lambda b, j: (cur(b, j), 0)),
        out_shape=jax.ShapeDtypeStruct((bsz * L, H_A * HD_A), BF16),
        compiler_params=_cparams("parallel", "arbitrary"),
        name="swa_prompt",
    )(sinks, proj2d, proj2d, proj2d, proj2d, proj2d, bias)


def _gelu(x):
    return 0.5 * x * (1.0 + lax.erf(x * np.float32(np.sqrt(0.5))))


def _gmlp_kernel(u_ref, gv_ref, lng_ref, lnb_ref, w_ref, bias_ref, o_ref):
    u = _gelu(u_ref[...])
    gv = _layer_norm_rows(_gelu(gv_ref[...]), lng_ref[...], lnb_ref[...])
    lo = lax.broadcasted_iota(jnp.int32, (CHUNK_B, LANES), 1) < CG_B
    for r in range(GB // 2):
        vp = gv[:, LANES * r:LANES * (r + 1)]
        zero = jnp.zeros_like(vp)
        mix = jnp.dot(w_ref[2 * r], jnp.where(lo, vp, zero).astype(BF16), preferred_element_type=F32)
        mix += jnp.dot(w_ref[2 * r + 1], jnp.where(lo, zero, vp).astype(BF16), preferred_element_type=F32)
        sl = slice(LANES * r, LANES * (r + 1))
        o_ref[:, sl] = (u[:, sl] * (mix + bias_ref[:, sl])).astype(o_ref.dtype)


def _gmlp_prompt_call(proj2d, p, n_rows):
    w = (p['gmlp_ws'] * jnp.tril(jnp.ones((CHUNK_B, CHUNK_B), F32))).astype(BF16)
    bias = jnp.repeat(p['gmlp_bs'].T, CG_B, axis=1)
    fixed2 = lambda i: (0, 0)
    return pl.pallas_call(
        _gmlp_kernel,
        grid=(n_rows // CHUNK_B,),
        in_specs=[
            pl.BlockSpec((CHUNK_B, W_B), lambda i: (i, COL_U // W_B)),
            pl.BlockSpec((CHUNK_B, W_B), lambda i: (i, COL_GV // W_B)),
            pl.BlockSpec((1, W_B), fixed2), pl.BlockSpec((1, W_B), fixed2),
            pl.BlockSpec((GB, CHUNK_B, CHUNK_B), lambda i: (0, 0, 0)),
            pl.BlockSpec((CHUNK_B, W_B), fixed2),
        ],
        out_specs=pl.BlockSpec((CHUNK_B, W_B), lambda i: (i, 0)),
        out_shape=jax.ShapeDtypeStruct((n_rows, W_B), BF16),
        compiler_params=_cparams("parallel"),
        name="gmlp_prompt",
    )(proj2d, proj2d, p['gmlp_ln_g'].reshape(1, -1), p['gmlp_ln_b'].reshape(1, -1), w, bias)


def _silu(x):
    return x * (1.0 / (1.0 + jnp.exp(-x)))


def _softplus(x):
    return jnp.maximum(x, 0.0) + jnp.log1p(jnp.exp(-jnp.abs(x)))


def _causal_conv_chunk(cur, tail, w, bias):
    rows = lax.broadcasted_iota(jnp.int32, (8, cur.shape[1]), 0)
    y = jnp.broadcast_to(bias, cur.shape)
    y_head = jnp.broadcast_to(bias, (8, cur.shape[1]))
    for t in range(CONV_W):
        k = CONV_W - 1 - t
        wt = w[t:t + 1, :]
        if k == 0:
            y = y + cur * wt
            y_head = y_head + cur[:8] * wt
        else:
            sh = pltpu.roll(cur, k, axis=0)
            y = y + sh * wt
            y_head = y_head + jnp.where(rows < k, pltpu.roll(tail, k, axis=0), sh[:8]) * wt
    return jnp.concatenate([y_head, y[8:]], axis=0)


def _bf16_split3(x):
    p1 = x.astype(BF16)
    r1 = x - p1.astype(F32)
    p2 = r1.astype(BF16)
    p3 = (r1 - p2.astype(F32)).astype(BF16)
    return p1, p2, p3


def _ssd_kernel(xs_ref, bc_ref, z_ref, dt_ref, cwx_ref, cbx_ref, cwb_ref, cbb_ref, dtb_ref, a_ref,
                dsk_ref, ng_ref, y_ref, h_ref, state_ref, tailx_ref, tailb_ref):
    c = pl.program_id(1)
    C = SSD_CHUNK

    @pl.when(c == 0)
    def _():
        state_ref[...] = jnp.zeros_like(state_ref)
        tailx_ref[...] = jnp.zeros_like(tailx_ref)
        tailb_ref[...] = jnp.zeros_like(tailb_ref)

    xs_raw = xs_ref[...]
    bc_raw = bc_ref[...]
    xs = _silu(_causal_conv_chunk(xs_raw, tailx_ref[...], cwx_ref[...], cbx_ref[...]))
    bc = _silu(_causal_conv_chunk(bc_raw, tailb_ref[...], cwb_ref[...], cbb_ref[...]))
    tailx_ref[...] = xs_raw[C - 8:]
    tailb_ref[...] = bc_raw[C - 8:]

    dt = _softplus(dt_ref[...] + dtb_ref[...])
    da = dt * a_ref[...]
    row_i = lax.broadcasted_iota(jnp.int32, (C, C), 0)
    col_i = lax.broadcasted_iota(jnp.int32, (C, C), 1)
    causal = row_i >= col_i
    tril = jnp.where(causal, 1.0, 0.0).astype(BF16)
    acs = None
    for piece in _bf16_split3(da):
        t = jnp.dot(tril, piece, preferred_element_type=F32)
        acs = t if acs is None else acs + t
    acs_t = acs.T
    exp_acs = jnp.exp(acs)
    end_decay = jnp.exp(acs[C - 1:C, :] - acs)
    chunk_decay = jnp.exp(acs[C - 1:C, :])

    lo = lax.broadcasted_iota(jnp.int32, (C, LANES), 1) < P_C
    bm = [bc[:, N_C * g:N_C * (g + 1)].astype(BF16) for g in range(G_C)]
    cm = [bc[:, N_C * (G_C + g):N_C * (G_C + g + 1)].astype(BF16) for g in range(G_C)]
    cb = [lax.dot_general(cm[g], bm[g], (((1,), (1,)), ((), ())), preferred_element_type=F32)
          for g in range(G_C)]

    def per_lane_half(t, r):
        return jnp.where(lo, t[:, 2 * r:2 * r + 1], t[:, 2 * r + 1:2 * r + 2])

    ys = []
    for r in range(H_C // 2):
        g = (2 * r) // R_C
        sl = slice(LANES * r, LANES * (r + 1))
        x_pair = xs[:, sl]
        xdt = x_pair * per_lane_half(dt, r)
        zero = jnp.zeros_like(xdt)
        y_pair = None
        for par in range(2):
            h = 2 * r + par
            seg = acs[:, h:h + 1] - acs_t[h:h + 1, :]
            decay = jnp.where(causal, jnp.exp(seg), 0.0)
            m_h = (cb[g] * decay).astype(BF16)
            x_h = (jnp.where(lo, xdt, zero) if par == 0 else jnp.where(lo, zero, xdt)).astype(BF16)
            t = jnp.dot(m_h, x_h, preferred_element_type=F32)
            y_pair = t if y_pair is None else y_pair + t
        st = state_ref[sl, :]
        y_off = lax.dot_general(cm[g], st.astype(BF16), (((1,), (1,)), ((), ())), preferred_element_type=F32)
        y_pair = y_pair + y_off * per_lane_half(exp_acs, r)
        upd = lax.dot_general((xdt * per_lane_half(end_decay, r)).astype(BF16), bm[g],
                              (((0,), (0,)), ((), ())), preferred_element_type=F32)
        cd = jnp.concatenate([jnp.broadcast_to(chunk_decay[:, 2 * r:2 * r + 1], (P_C, N_C)),
                              jnp.broadcast_to(chunk_decay[:, 2 * r + 1:2 * r + 2], (P_C, N_C))], axis=0)
        state_ref[sl, :] = st * cd + upd
        ys.append(y_pair + dsk_ref[:, sl] * x_pair)
    y = jnp.concatenate(ys, axis=1) * _silu(z_ref[...])
    gw = D_INNER // G_C
    outs = []
    for g in range(G_C):
        yg = y[:, gw * g:gw * (g + 1)]
        outs.append(yg * lax.rsqrt(jnp.mean(yg * yg, axis=-1, keepdims=True) + RMS_EPS))
    y_ref[...] = (jnp.concatenate(outs, axis=1) * ng_ref[...]).astype(y_ref.dtype)

    @pl.when(c == pl.num_programs(1) - 1)
    def _():
        h_ref[...] = state_ref[...]


def _ssd_prompt_call(proj2d, p, bsz, L):
    nc = L // SSD_CHUNK
    row = lambda blk: (lambda b, c: (b * nc + c, blk))
    fixed = lambda b, c: (0, 0)
    pad_l = lambda v: jnp.pad(v.astype(F32), (0, LANES - H_C)).reshape(1, LANES)
    cw, cbias = p['conv_w'], p['conv_b'].reshape(1, -1)
    nbc = 2 * G_C * N_C
    args = (proj2d, proj2d, proj2d, proj2d,
            cw[:, :D_INNER], cbias[:, :D_INNER], cw[:, D_INNER:], cbias[:, D_INNER:],
            pad_l(p['dt_bias']), pad_l(-jnp.exp(p['a_log'].astype(F32))),
            jnp.repeat(p['d_skip'].astype(F32), P_C).reshape(1, -1), p['ssm_norm_g'].reshape(1, -1))
    return pl.pallas_call(
        _ssd_kernel,
        grid=(bsz, nc),
        in_specs=[
            pl.BlockSpec((SSD_CHUNK, D_INNER), row(COL_XBC // D_INNER)),
            pl.BlockSpec((SSD_CHUNK, nbc), row((COL_XBC + D_INNER) // nbc)),
            pl.BlockSpec((SSD_CHUNK, D_INNER), row(COL_Z // D_INNER)),
            pl.BlockSpec((SSD_CHUNK, LANES), row(COL_DT // LANES)),
            pl.BlockSpec((CONV_W, D_INNER), fixed), pl.BlockSpec((1, D_INNER), fixed),
            pl.BlockSpec((CONV_W, nbc), fixed), pl.BlockSpec((1, nbc), fixed),
            pl.BlockSpec((1, LANES), fixed), pl.BlockSpec((1, LANES), fixed),
            pl.BlockSpec((1, D_INNER), fixed), pl.BlockSpec((1, D_INNER), fixed),
        ],
        out_specs=[pl.BlockSpec((SSD_CHUNK, D_INNER), lambda b, c: (b * nc + c, 0)),
                   pl.BlockSpec((None, H_C * P_C, N_C), lambda b, c: (b, 0, 0))],
        out_shape=[jax.ShapeDtypeStruct((bsz * L, D_INNER), BF16),
                   jax.ShapeDtypeStruct((bsz, H_C * P_C, N_C), F32)],
        scratch_shapes=[pltpu.VMEM((H_C * P_C, N_C), F32), pltpu.VMEM((8, D_INNER), F32),
                        pltpu.VMEM((8, nbc), F32)],
        compiler_params=_cparams("parallel", "arbitrary"),
        name="ssd_prompt",
    )(*args)


def _bf16_round(x):
    return x.astype(BF16).astype(F32)


SWA_DECODE_TOKENS = 8


def _swa_decode_kernel(q_ref, kn_ref, vn_ref, ck_ref, cv_ref, bias_ref, sink_ref, o_ref, wk_ref, wv_ref):
    W = WINDOW
    last = lax.broadcasted_iota(jnp.int32, (W, LANES), 0) == W - 1
    lo = lax.broadcasted_iota(jnp.int32, (8, LANES), 1) < HD_A
    g0 = lax.broadcasted_iota(jnp.int32, (8, LANES), 0) < G_A // 2
    zero = jnp.zeros((8, LANES), F32)
    nt_dims = (((1,), (1,)), ((), ()))
    for t in range(q_ref.shape[0]):
        kw = jnp.where(last, jnp.broadcast_to(kn_ref[t], (W, LANES)), pltpu.roll(ck_ref[t], W - 1, axis=0))
        vw = jnp.where(last, jnp.broadcast_to(vn_ref[t], (W, LANES)), pltpu.roll(cv_ref[t], W - 1, axis=0))
        wk_ref[t] = kw
        wv_ref[t] = vw
        kb, vb = kw.astype(BF16), vw.astype(BF16)
        q = q_ref[t]
        qr = pltpu.roll(q, HD_A, axis=1)
        q_par = [jnp.where(g0, jnp.where(lo, q, zero), jnp.where(lo, zero, qr)),
                 jnp.where(g0, jnp.where(lo, qr, zero), jnp.where(lo, zero, q))]
        outs = []
        for par in range(2):
            s = lax.dot_general(q_par[par].astype(BF16), kb, nt_dims, preferred_element_type=F32)
            s = s * HD_A ** -0.5 + bias_ref[par]
            sk = sink_ref[par]
            m = jnp.maximum(jnp.max(s, axis=-1, keepdims=True), sk)
            pr = jnp.exp(s - m)
            den = jnp.sum(pr, axis=-1, keepdims=True) + jnp.exp(sk - m)
            outs.append(jnp.dot((pr / den).astype(BF16), vb, preferred_element_type=F32))
        o_even = jnp.where(g0, outs[0], pltpu.roll(outs[0], HD_A, axis=1))
        o_odd = jnp.where(g0, pltpu.roll(outs[1], HD_A, axis=1), outs[1])
        o_ref[t] = jnp.where(lo, o_even, o_odd).astype(o_ref.dtype)


def _swa_decode_call(proj3, cache_k, cache_v, layer, rel_bias, sinks):
    bsz = proj3.shape[0]
    nt = SWA_DECODE_TOKENS
    ck = cache_k.reshape(cache_k.shape[0], bsz, WINDOW, LANES)
    cv = cache_v.reshape(cache_v.shape[0], bsz, WINDOW, LANES)
    q8 = proj3[:, 0, COL_Q:COL_Q + H_A * HD_A].reshape(bsz, H_A // 2, LANES)
    dist = WINDOW - 1 - jnp.arange(WINDOW)
    bias = _bucket_bias(rel_bias, dist).reshape(H_A // 2, 2, WINDOW).transpose(1, 0, 2)
    sink = sinks.astype(F32).reshape(H_A // 2, 2, 1).transpose(1, 0, 2)
    new_tok = lambda blk: (lambda i: (i, 0, blk))
    cache = lambda i: (layer, i, 0, 0)
    tok3 = lambda i: (i, 0, 0)
    out, wk, wv = pl.pallas_call(
        _swa_decode_kernel,
        grid=(bsz // nt,),
        in_specs=[
            pl.BlockSpec((nt, H_A // 2, LANES), tok3),
            pl.BlockSpec((nt, 1, LANES), new_tok(COL_K // LANES)),
            pl.BlockSpec((nt, 1, LANES), new_tok(COL_V // LANES)),
            pl.BlockSpec((None, nt, WINDOW, LANES), cache),
            pl.BlockSpec((None, nt, WINDOW, LANES), cache),
            pl.BlockSpec((2, H_A // 2, WINDOW), lambda i: (0, 0, 0)),
            pl.BlockSpec((2, H_A // 2, 1), lambda i: (0, 0, 0)),
        ],
        out_specs=[pl.BlockSpec((nt, H_A // 2, LANES), tok3),
                   pl.BlockSpec((nt, WINDOW, LANES), tok3),
                   pl.BlockSpec((nt, WINDOW, LANES), tok3)],
        out_shape=[jax.ShapeDtypeStruct((bsz, H_A // 2, LANES), BF16),
                   jax.ShapeDtypeStruct((bsz, WINDOW, LANES), F32),
                   jax.ShapeDtypeStruct((bsz, WINDOW, LANES), F32)],
        compiler_params=_cparams("parallel"),
        name="swa_sample",
    )(q8, proj3, proj3, ck, cv, bias, sink)
    return out.reshape(bsz, H_A * HD_A), wk, wv


def _gmlp_step_kernel(u_ref, gv_ref, lng_ref, lnb_ref, w0_ref, b0_ref, o_ref, gv_out_ref):
    gv = _layer_norm_rows(_gelu(gv_ref[...]), lng_ref[...], lnb_ref[...])
    gv_out_ref[...] = gv
    mix = _bf16_round(w0_ref[...]) * _bf16_round(gv) + b0_ref[...]
    o_ref[...] = (_gelu(u_ref[...]) * mix).astype(o_ref.dtype)


def _gmlp_step_call(proj2d, p):
    n = proj2d.shape[0]
    w0 = jnp.repeat(p['gmlp_ws'][:, 0, 0], CG_B).reshape(1, -1)
    b0 = jnp.repeat(p['gmlp_bs'][:, 0], CG_B).reshape(1, -1)
    fixed = lambda i: (0, 0)
    return pl.pallas_call(
        _gmlp_step_kernel,
        grid=(1,),
        in_specs=[pl.BlockSpec((n, W_B), lambda i: (0, COL_U // W_B)),
                  pl.BlockSpec((n, W_B), lambda i: (0, COL_GV // W_B)),
                  pl.BlockSpec((1, W_B), fixed), pl.BlockSpec((1, W_B), fixed),
                  pl.BlockSpec((1, W_B), fixed), pl.BlockSpec((1, W_B), fixed)],
        out_specs=[pl.BlockSpec((n, W_B), fixed), pl.BlockSpec((n, W_B), fixed)],
        out_shape=[jax.ShapeDtypeStruct((n, W_B), BF16), jax.ShapeDtypeStruct((n, W_B), F32)],
        compiler_params=_cparams("arbitrary"),
        name="gmlp_sample",
    )(proj2d, proj2d, p['gmlp_ln_g'].reshape(1, -1), p['gmlp_ln_b'].reshape(1, -1), w0, b0)


def _conv_step(st, cur, w, bias):
    y = bias
    for t in range(CONV_W - 1):
        y = y + st[t:t + 1, :] * w[t:t + 1, :]
    return y + cur * w[CONV_W - 1:CONV_W, :]


def _ssd_step_kernel(xs_ref, bc_ref, z_ref, dt_ref, stx_ref, stb_ref, h0_ref, cwx_ref, cbx_ref, cwb_ref, cbb_ref,
                     dtb_ref, a_ref, dsk_ref, ng_ref, y_ref, h_ref, ncx_ref, ncb_ref):
    xs_raw, bc_raw = xs_ref[...], bc_ref[...]
    stx, stb = stx_ref[...], stb_ref[...]
    ncx_ref[0:CONV_W - 2, :] = stx[1:CONV_W - 1]
    ncx_ref[CONV_W - 2:CONV_W - 1, :] = xs_raw
    ncb_ref[0:CONV_W - 2, :] = stb[1:CONV_W - 1]
    ncb_ref[CONV_W - 2:CONV_W - 1, :] = bc_raw
    xs = _silu(_conv_step(stx, xs_raw, cwx_ref[...], cbx_ref[...]))
    bc = _silu(_conv_step(stb, bc_raw, cwb_ref[...], cbb_ref[...]))
    dt = _softplus(dt_ref[...] + dtb_ref[...])
    decay = jnp.exp(dt * a_ref[...])
    xdt = _bf16_round(xs * dt)
    gw = D_INNER // G_C
    first_group = lax.broadcasted_iota(jnp.int32, (1, D_INNER), 1) < gw
    bm = [_bf16_round(bc[:, N_C * g:N_C * (g + 1)]) for g in range(G_C)]
    cm = [_bf16_round(bc[:, N_C * (G_C + g):N_C * (G_C + g + 1)]) for g in range(G_C)]
    cb = [_bf16_round(jnp.sum(cm[g] * bm[g], axis=1, keepdims=True)) for g in range(G_C)]
    y_diag = jnp.where(first_group, cb[0], cb[1]) * xdt
    h0 = h0_ref[...]
    y_off = jnp.concatenate([
        lax.dot_general(jnp.broadcast_to(cm[g], (8, N_C)).astype(BF16), h0[gw * g:gw * (g + 1)].astype(BF16),
                        (((1,), (1,)), ((), ())), preferred_element_type=F32)[0:1] for g in range(G_C)], axis=1)
    y = y_diag + y_off * decay
    y = (y + dsk_ref[...] * xs) * _silu(z_ref[...])
    outs = []
    for g in range(G_C):
        yg = y[:, gw * g:gw * (g + 1)]
        outs.append(yg * lax.rsqrt(jnp.mean(yg * yg, axis=-1, keepdims=True) + RMS_EPS))
    y_ref[...] = (jnp.concatenate(outs, axis=1) * ng_ref[...]).astype(y_ref.dtype)
    decay_rows = jnp.broadcast_to(decay, (LANES, D_INNER)).T
    xdt_rows = jnp.broadcast_to(xdt, (LANES, D_INNER)).T
    rows = lax.broadcasted_iota(jnp.int32, (D_INNER, N_C), 0)
    bm_rows = jnp.where(rows < gw, jnp.broadcast_to(bm[0], (D_INNER, N_C)), jnp.broadcast_to(bm[1], (D_INNER, N_C)))
    h_ref[...] = h0 * decay_rows + xdt_rows * bm_rows


def _ssd_step_call(proj3, state_conv, state_ssm, layer, p):
    bsz = proj3.shape[0]
    nbc = 2 * G_C * N_C
    ssm = state_ssm.reshape(state_ssm.shape[0], bsz, H_C * P_C, N_C)
    dt_lanes = jnp.repeat(proj3[:, :, COL_DT:COL_DT + H_C], P_C, axis=-1)
    per_lane = lambda v: jnp.repeat(v.astype(F32), P_C).reshape(1, -1)
    cw, cbias = p['conv_w'], p['conv_b'].reshape(1, -1)
    tok = lambda blk: (lambda i: (i, 0, blk))
    fixed = lambda i: (0, 0)
    return pl.pallas_call(
        _ssd_step_kernel,
        grid=(bsz,),
        in_specs=[
            pl.BlockSpec((None, 1, D_INNER), tok(COL_XBC // D_INNER)),
            pl.BlockSpec((None, 1, nbc), tok((COL_XBC + D_INNER) // nbc)),
            pl.BlockSpec((None, 1, D_INNER), tok(COL_Z // D_INNER)),
            pl.BlockSpec((None, 1, D_INNER), tok(0)),
            pl.BlockSpec((None, None, CONV_W - 1, D_INNER), lambda i: (layer, i, 0, 0)),
            pl.BlockSpec((None, None, CONV_W - 1, nbc), lambda i: (layer, i, 0, D_INNER // nbc)),
            pl.BlockSpec((None, None, H_C * P_C, N_C), lambda i: (layer, i, 0, 0)),
            pl.BlockSpec((CONV_W, D_INNER), fixed), pl.BlockSpec((1, D_INNER), fixed),
            pl.BlockSpec((CONV_W, nbc), fixed), pl.BlockSpec((1, nbc), fixed),
            pl.BlockSpec((1, D_INNER), fixed), pl.BlockSpec((1, D_INNER), fixed),
            pl.BlockSpec((1, D_INNER), fixed), pl.BlockSpec((1, D_INNER), fixed),
        ],
        out_specs=[pl.BlockSpec((None, 1, D_INNER), lambda i: (i, 0, 0)),
                   pl.BlockSpec((None, H_C * P_C, N_C), lambda i: (i, 0, 0)),
                   pl.BlockSpec((None, CONV_W - 1, D_INNER), lambda i: (i, 0, 0)),
                   pl.BlockSpec((None, CONV_W - 1, nbc), lambda i: (i, 0, 0))],
        out_shape=[jax.ShapeDtypeStruct((bsz, 1, D_INNER), BF16),
                   jax.ShapeDtypeStruct((bsz, H_C * P_C, N_C), F32),
                   jax.ShapeDtypeStruct((bsz, CONV_W - 1, D_INNER), F32),
                   jax.ShapeDtypeStruct((bsz, CONV_W - 1, nbc), F32)],
        compiler_params=_cparams("parallel"),
        name="ssd_sample",
    )(proj3, proj3, proj3, dt_lanes, state_conv, state_conv, ssm,
      cw[:, :D_INNER], cbias[:, :D_INNER], cw[:, D_INNER:], cbias[:, D_INNER:],
      per_lane(p['dt_bias']), per_lane(-jnp.exp(p['a_log'].astype(F32))), per_lane(p['d_skip']),
      p['ssm_norm_g'].reshape(1, -1))


def _xattn_kernel(x1_ref, wq_ref, mk_ref, mv_ref, wo_ref, g_ref, b_ref, wr_ref, br_ref, x2_ref, lg_ref):
    x1 = x1_ref[...]
    q = jnp.dot(x1.astype(BF16), wq_ref[...], preferred_element_type=F32)
    outs = []
    for h in range(XH):
        sl = slice(XHD * h, XHD * (h + 1))
        s = lax.dot_general(q[:, sl].astype(BF16), mk_ref[:, sl].astype(BF16),
                            (((1,), (1,)), ((), ())), preferred_element_type=F32) * XHD ** -0.5
        e = jnp.exp(s - jnp.max(s, axis=-1, keepdims=True))
        w = e / jnp.sum(e, axis=-1, keepdims=True)
        outs.append(jnp.dot(w.astype(BF16), mv_ref[:, sl].astype(BF16), preferred_element_type=F32).astype(BF16))
    o = jnp.concatenate(outs, axis=1)
    y = jnp.dot(o, wo_ref[...], preferred_element_type=F32) + DN_ALPHA * x1
    x2 = _layer_norm_rows(y, g_ref[...], b_ref[...])
    x2_ref[...] = x2
    lg_ref[...] = _top_k_lanes(jnp.dot(x2.astype(BF16), wr_ref[...], preferred_element_type=F32) + br_ref[...])


def _xattn_prompt_call(x1, mkv, lw, g, b, bsz, L, tm):
    nt = L // tm
    xw = XH * XHD
    row = lambda bb, i: (bb * nt + i, 0)
    fixed = lambda bb, i: (0, 0)
    n_lg = lw['w_router'].shape[1]
    return pl.pallas_call(
        _xattn_kernel,
        grid=(bsz, nt),
        in_specs=[
            pl.BlockSpec((tm, D_MODEL), row),
            pl.BlockSpec((D_MODEL, xw), fixed),
            pl.BlockSpec((N_MEM, xw), lambda bb, i: (bb, 0)),
            pl.BlockSpec((N_MEM, xw), lambda bb, i: (bb, 1)),
            pl.BlockSpec((xw, D_MODEL), fixed),
            pl.BlockSpec((1, D_MODEL), fixed), pl.BlockSpec((1, D_MODEL), fixed),
            pl.BlockSpec((D_MODEL, n_lg), fixed), pl.BlockSpec((1, n_lg), fixed),
        ],
        out_specs=[pl.BlockSpec((tm, D_MODEL), row), pl.BlockSpec((tm, n_lg), row)],
        out_shape=[jax.ShapeDtypeStruct((bsz * L, D_MODEL), F32), jax.ShapeDtypeStruct((bsz * L, n_lg), F32)],
        compiler_params=_cparams("parallel", "parallel"),
        name="memory_attention_prompt",
    )(x1, lw['w_xq'], mkv, mkv, lw['w_xo'], g.reshape(1, -1), b.reshape(1, -1),
      lw['w_router_bf16'], lw['b_router'])


def _xattn_decode_kernel(q_ref, k_ref, v_ref, o_ref):
    q = q_ref[...].astype(BF16).astype(F32)
    k = k_ref[...].astype(BF16).astype(F32)
    v = v_ref[...].astype(BF16).astype(F32)
    prod = k * q
    outs = []
    for h in range(XH):
        sl = slice(XHD * h, XHD * (h + 1))
        s = jnp.sum(prod[:, sl], axis=1, keepdims=True) * XHD ** -0.5
        e = jnp.exp(s - jnp.max(s, axis=0, keepdims=True))
        w = (e / jnp.sum(e, axis=0, keepdims=True)).astype(BF16).astype(F32)
        outs.append(jnp.sum(w * v[:, sl], axis=0, keepdims=True))
    o_ref[...] = jnp.concatenate(outs, axis=1)


def _xattn_decode_call(q, cache_k, cache_v, layer):
    bsz, xw = q.shape
    ck = cache_k.reshape(cache_k.shape[0], bsz, N_MEM, xw)
    cv = cache_v.reshape(cache_v.shape[0], bsz, N_MEM, xw)
    out = pl.pallas_call(
        _xattn_decode_kernel,
        grid=(bsz,),
        in_specs=[pl.BlockSpec((None, 1, xw), lambda i: (i, 0, 0)),
                  pl.BlockSpec((None, None, N_MEM, xw), lambda i: (layer, i, 0, 0)),
                  pl.BlockSpec((None, None, N_MEM, xw), lambda i: (layer, i, 0, 0))],
        out_specs=pl.BlockSpec((None, 1, xw), lambda i: (i, 0, 0)),
        out_shape=jax.ShapeDtypeStruct((bsz, 1, xw), F32),
        compiler_params=_cparams("parallel"),
        name="memory_attention_sample",
    )(q.reshape(bsz, 1, xw), ck, cv)
    return out.reshape(bsz, xw)


def _t5_bucket(dist):
    n = jnp.maximum(dist, 0)
    exact = N_BUCKETS // 2
    nf = jnp.maximum(n, 1).astype(F32)
    large = exact + (jnp.log(nf / exact) / math.log(MAX_DIST / exact) * (N_BUCKETS - exact)).astype(jnp.int32)
    return jnp.where(n < exact, n, jnp.minimum(large, N_BUCKETS - 1))


def _bucket_bias(rel_bias, dist):
    onehot = (_t5_bucket(dist)[..., None] == jnp.arange(N_BUCKETS)).astype(F32)
    return jnp.einsum('...b,bh->h...', onehot, rel_bias.astype(F32), precision=lax.Precision.HIGHEST)


def _prompt_layer(x, mkv, lw, p, swa_bias, bsz, L):
    T = bsz * L
    xb = x.astype(BF16)
    proj = _matmul(xb, lw['w_in'], 1024, 512)
    a_out = _swa_prompt_call(proj, swa_bias, p['sinks'], bsz, L)
    b_out = _gmlp_prompt_call(proj, p, T)
    c_out, h_last = _ssd_prompt_call(proj, p, bsz, L)
    mixed = _gate_merge(xb, a_out, b_out, c_out, lw['w_gate'], lw['b_gate'], lw['w_branch'], 512, 512)
    x1 = _mm_res_ln(mixed, lw['w_o'], x, p['ln1_g'], p['ln1_b'], 512)
    x2, logits = _xattn_prompt_call(x1, mkv, lw, p['ln2_g'], p['ln2_b'], bsz, L, 512)
    proj3 = proj.reshape(bsz, L, IN_DIM_PAD)
    win_k = proj3[:, L - WINDOW:, COL_K:COL_K + KV_A * HD_A].reshape(bsz, WINDOW, KV_A, HD_A)
    win_v = proj3[:, L - WINDOW:, COL_V:COL_V + KV_A * HD_A].reshape(bsz, WINDOW, KV_A, HD_A)
    conv = proj3[:, L - (CONV_W - 1):, COL_XBC:COL_XBC + CONV_DIM]
    return x2, logits, (win_k, win_v, conv, h_last.reshape(bsz, H_C, P_C, N_C))


def _sample_layer(x, layer, lw, p, rel_bias, cache_win_k, cache_win_v, state_conv, state_ssm, cache_mem_k, cache_mem_v):
    bsz = x.shape[0]
    xb = x.astype(BF16)
    proj = _matmul(xb, lw['w_in'], bsz, 512)
    proj3 = proj.reshape(bsz, 1, IN_DIM_PAD)
    a_out, win_k, win_v = _swa_decode_call(proj3, cache_win_k, cache_win_v, layer, rel_bias, p['sinks'])
    b_out, gv = _gmlp_step_call(proj, p)
    c_out, ssm, conv_x, conv_bc = _ssd_step_call(proj3, state_conv, state_ssm, layer, p)
    mixed = _gate_merge(xb, a_out, b_out, c_out.reshape(bsz, BR_W),
                        lw['w_gate'], lw['b_gate'], lw['w_branch'], bsz, 512)
    x1 = _mm_res_ln(mixed, lw['w_o'], x, p['ln1_g'], p['ln1_b'], bsz)
    q = _matmul(x1.astype(BF16), lw['w_xq'], bsz, XH * XHD)
    o = _xattn_decode_call(q, cache_mem_k, cache_mem_v, layer)
    x2 = _mm_res_ln(o.astype(BF16), lw['w_xo'], x1, p['ln2_g'], p['ln2_b'], bsz)
    logits = _router(x2, lw['w_router'], lw['b_router'], bsz)
    states = (win_k.reshape(bsz, WINDOW, KV_A, HD_A), win_v.reshape(bsz, WINDOW, KV_A, HD_A),
              jnp.concatenate([conv_x, conv_bc], axis=-1), ssm.reshape(bsz, H_C, P_C, N_C),
              gv.reshape(bsz, 1, W_B))
    return x2, logits, states


def kernel(x_prompt, x_sample, mem_prompt, cache_win_k, cache_win_v, state_conv, state_ssm, cache_mem_k, cache_mem_v, w_in, rel_bias, sinks, gmlp_ln_g, gmlp_ln_b, gmlp_ws, gmlp_bs, conv_w, conv_b, dt_bias, a_log, d_skip, ssm_norm_g, w_branch, w_gate, b_gate, w_o, ln1_g, ln1_b, w_xq, w_xk, w_xv, w_xo, ln2_g, ln2_b, w_router, b_router, w_e1, b_e1, w_e2, b_e2, ln3_g, ln3_b):
    assert cache_win_k.shape[2] == WINDOW and x_sample.shape[1] == 1
    n_prompt, n_mem = mem_prompt.shape[0], mem_prompt.shape[1]
    bp, lp = x_prompt.shape[:2]
    bs_ = x_sample.shape[0]
    hp, hs = x_prompt.reshape(bp * lp, D_MODEL), x_sample.reshape(bs_, D_MODEL)
    wk_p, wv_p, cv_p, ssm_p, mk_ps, mv_ps = [], [], [], [], [], []
    wk_s, wv_s, cv_s, ssm_s, gv_s = [], [], [], [], []
    mem_b = mem_prompt.reshape(n_prompt * n_mem, D_MODEL).astype(BF16)
    swa_bias = _swa_prompt_bias(rel_bias)
    experts = (w_e1, b_e1.reshape(DEPTH, N_EXPERTS, 1, 2 * D_FF), w_e2, b_e2.reshape(DEPTH, N_EXPERTS, 1, D_MODEL))
    for l in range(DEPTH):
        p = dict(sinks=sinks[l], gmlp_ln_g=gmlp_ln_g[l], gmlp_ln_b=gmlp_ln_b[l],
                 gmlp_ws=gmlp_ws[l], gmlp_bs=gmlp_bs[l], conv_w=conv_w[l], conv_b=conv_b[l],
                 dt_bias=dt_bias[l], a_log=a_log[l], d_skip=d_skip[l], ssm_norm_g=ssm_norm_g[l],
                 ln1_g=ln1_g[l], ln1_b=ln1_b[l], ln2_g=ln2_g[l], ln2_b=ln2_b[l])
        wi = w_in[l]
        seg = np.cumsum([0] + IN_SIZES)
        part = lambda n: wi[:, seg[n]:seg[n + 1]]
        w_in_cols = jnp.concatenate(
            [part(0), part(3), part(4), part(5), part(6), part(1), part(2), part(7),
             jnp.zeros((D_MODEL, IN_DIM_PAD - IN_DIM), F32)], axis=1).astype(BF16)
        w_router_pad = jnp.pad(w_router[l], ((0, 0), (0, LANES - N_EXPERTS)))
        lw = dict(
            w_in=w_in_cols,
            w_gate=w_gate[l].astype(BF16),
            b_gate=b_gate[l].reshape(1, -1),
            w_branch=w_branch[l].astype(BF16),
            w_o=w_o[l].astype(BF16),
            w_xq=w_xq[l].astype(BF16),
            w_xo=w_xo[l].astype(BF16),
            w_router=w_router_pad,
            w_router_bf16=w_router_pad.astype(BF16),
            b_router=jnp.pad(b_router[l], (0, LANES - N_EXPERTS)).reshape(1, -1),
        )
        w_kv = jnp.concatenate([w_xk[l], w_xv[l]], axis=1).astype(BF16)
        mkv = _matmul(mem_b, w_kv, n_mem, XH * XHD)
        mk = mkv[:, :XH * XHD].reshape(n_prompt, n_mem, XH, XHD)
        mv = mkv[:, XH * XHD:].reshape(n_prompt, n_mem, XH, XHD)
        x2_p, lg_p, st_p = _prompt_layer(hp, mkv, lw, p, swa_bias, bp, lp)
        x2_s, lg_s, st_s = _sample_layer(hs, l, lw, p, rel_bias, cache_win_k, cache_win_v, state_conv, state_ssm,
                                         cache_mem_k, cache_mem_v)
        hp, hs = _moe_ln(x2_p, x2_s, lg_p, lg_s, experts, l, ln3_g[l], ln3_b[l])
        wk_p.append(st_p[0]); wv_p.append(st_p[1]); cv_p.append(st_p[2]); ssm_p.append(st_p[3])
        mk_ps.append(mk); mv_ps.append(mv)
        wk_s.append(st_s[0]); wv_s.append(st_s[1]); cv_s.append(st_s[2]); ssm_s.append(st_s[3])
        gv_s.append(st_s[4])
    hp = hp.reshape(bp, lp, D_MODEL)
    hs = hs.reshape(bs_, 1, D_MODEL)
    return (hp, hs,
            jnp.stack(wk_p), jnp.stack(wv_p), jnp.stack(cv_p), jnp.stack(ssm_p),
            jnp.stack(mk_ps), jnp.stack(mv_ps),
            jnp.stack(wk_s), jnp.stack(wv_s), jnp.stack(cv_s), jnp.stack(ssm_s), jnp.stack(gv_s))
```
